```python
import math
import jax, jax.numpy as jnp
from jax import lax
import numpy as np

D_MODEL = 4096
BATCH = 4
SEQ = 2048
DEPTH = 2
DEC_BATCH = 8
DEC_SEQ = 1
PAST_LEN = 16384
PAGE_SIZE = 128

N_EVEN = (DEPTH + 1) // 2
N_ODD = DEPTH // 2
HEAD_SLOTS = 32
A_HEADS = 16
A_GROUPS = 2
A_HPG = A_HEADS // A_GROUPS
A_DK = 192
A_DV = 128
CMP_BLK = 64
N_SEL = 16
WINDOW = 512
B_HEADS = 16
B_GROUPS = 2
B_HPG = B_HEADS // B_GROUPS
B_DH = 128
IDX_HEADS = 8
IDX_DIM = 64
IDX_TOPK = 256
C_HEADS = 32
C_GROUPS = 8
C_HPG = C_HEADS // C_GROUPS
C_DH = 128
MOBA_BLK = 256
MOBA_TOP = 3
MOBA_QCHUNK = 32
N_MEM = 256
X_HEADS = 4
X_DH = 128
D_FF = 14336
N_EXPERTS = 8
TOP_K = 2
MOE_MAX_ROWS = 512
N_BUCKETS = 32
REL_EXACT = 16
REL_MAX_DIST = 1024
Q_BLOCK = 128
EPS = 1e-6
NEG = -1e30
FORCE = 1e9
EVEN_SPLITS = (A_HEADS * A_DK, 3 * A_GROUPS * A_DK, 3 * A_GROUPS * A_DV, 3 * A_HEADS,
               B_HEADS * B_DH, B_GROUPS * B_DH, B_GROUPS * B_DH, IDX_HEADS * IDX_DIM, IDX_DIM, IDX_HEADS)
ODD_SPLITS = (C_HEADS * C_DH, C_GROUPS * C_DH, C_GROUPS * C_DH)

kernel_name = 'hybrid_nsa_dsa_moba_decoder_step'


def rms_norm(x, g):
    xf = x.astype(jnp.float32)
    y = xf * lax.rsqrt(jnp.mean(xf * xf, axis=-1, keepdims=True) + EPS)
    return (y * g.astype(jnp.float32)).astype(x.dtype)


def split_cols(z, sizes):
    cuts = [int(c) for c in np.cumsum(sizes)[:-1]]
    return jnp.split(z, cuts, axis=-1)


def masked_softmax(logits, mask):
    p = jax.nn.softmax(jnp.where(mask, logits, NEG), axis=-1)
    return jnp.where(mask, p, 0.0)


def t5_bucket(dist):
    n = jnp.maximum(dist, 0)
    nf = jnp.maximum(n, 1).astype(jnp.float32)
    large = REL_EXACT + (jnp.log(nf / REL_EXACT) / math.log(REL_MAX_DIST / REL_EXACT)
                         * (N_BUCKETS - REL_EXACT)).astype(jnp.int32)
    return jnp.where(n < REL_EXACT, n, jnp.minimum(large, N_BUCKETS - 1))


def gather_paged(pool, page_table, new_rows, pos, *extra):
    past_len = page_table.shape[1] * PAGE_SIZE
    b = jnp.arange(pos.shape[0]).reshape((-1,) + (1,) * (pos.ndim - 1))
    pc = jnp.clip(pos, 0, past_len - 1)
    phys = page_table[b, pc // PAGE_SIZE]
    old = pool[(phys, pc % PAGE_SIZE) + extra]
    new = new_rows[(b, jnp.clip(pos - past_len, 0, new_rows.shape[1] - 1)) + extra]
    is_new = (pos >= past_len).reshape(pos.shape + (1,) * (old.ndim - pos.ndim))
    return jnp.where(is_new, new, old)


def nsa_compress(rows, pe, w1, w2):
    B, L, G, d = rows.shape
    nc = L // CMP_BLK
    blk = rows[:, : nc * CMP_BLK].reshape(B, nc, CMP_BLK, G, d) + pe[None, None, :, None, :]
    flat = blk.transpose(0, 1, 3, 2, 4).reshape(B, nc, G, CMP_BLK * d)
    return jax.nn.silu(flat @ w1) @ w2


def nsa_cmp_branch(q, q_pos, kc, vc):
    nc = kc.shape[1]
    logits = jnp.einsum('bqghd,bjgd->bqghj', q, kc, preferred_element_type=jnp.float32) * A_DK ** -0.5
    vis = (jnp.arange(nc) + 1) * CMP_BLK <= (q_pos + 1)[:, None]
    p = masked_softmax(logits, vis[None, :, None, None, :])
    o = jnp.einsum('bqghj,bjgd->bqghd', p.astype(vc.dtype), vc)
    return o, p.sum(axis=3)


def nsa_pick_blocks(imp, q_pos, n_blocks):
    score = jnp.pad(imp, ((0, 0), (0, 0), (0, 0), (0, n_blocks - imp.shape[-1])))
    j = jnp.arange(n_blocks)[None, :]
    own = (q_pos // CMP_BLK)[:, None]
    forced = (j == 0) | (j == own) | (j == own - 1)
    score = jnp.where(forced[None, :, None, :], FORCE, score)
    score = jnp.where((j <= own)[None, :, None, :], score, NEG)
    top_s, idx = lax.top_k(score, min(N_SEL, n_blocks))
    return idx, top_s > 0.5 * NEG


def nsa_sel_attend(q, q_pos, idx, valid, ksel, vsel, tab_a):
    B, Q, G, HPG, _ = q.shape
    kpos = idx[..., None] * CMP_BLK + jnp.arange(CMP_BLK)
    dist = q_pos[None, :, None, None, None] - kpos
    mask = valid[..., None] & (dist >= 0)
    tab2 = tab_a.reshape(N_BUCKETS, G, HPG).transpose(1, 0, 2)
    bias = tab2[jnp.arange(G).reshape(1, 1, G, 1, 1), t5_bucket(dist)]
    logits = (jnp.einsum('bqghd,bqgnsd->bqghns', q, ksel, preferred_element_type=jnp.float32) * A_DK ** -0.5
              + jnp.moveaxis(bias, -1, 3).astype(jnp.float32))
    shp = logits.shape
    p = masked_softmax(logits.reshape(B, Q, G, HPG, -1), mask.reshape(B, Q, G, 1, -1)).reshape(shp)
    return jnp.einsum('bqghns,bqgnsd->bqghd', p.astype(vsel.dtype), vsel)


def window_attend(q, q_pos, k, v, k_pos, tab_a):
    B, Q, G, HPG, _ = q.shape
    dist = q_pos[:, None] - k_pos[None, :]
    mask = (dist >= 0) & (dist < WINDOW) & (k_pos >= 0)[None, :]
    bias = tab_a[t5_bucket(dist)].reshape(Q, -1, G, HPG).transpose(0, 2, 3, 1).astype(jnp.float32)
    logits = jnp.einsum('bqghd,bsgd->bqghs', q, k, preferred_element_type=jnp.float32) * A_DK ** -0.5 + bias[None]
    p = masked_softmax(logits, mask[None, :, None, None, :])
    return jnp.einsum('bqghs,bsgd->bqghd', p.astype(v.dtype), v)


def window_prompt(q, k, v, tab_a):
    B, T = q.shape[:2]
    pad = ((0, 0), (WINDOW, 0), (0, 0), (0, 0))
    kp, vp = jnp.pad(k, pad), jnp.pad(v, pad)
    span = Q_BLOCK + WINDOW

    def step(i):
        t0 = i * Q_BLOCK
        return window_attend(lax.dynamic_slice_in_dim(q, t0, Q_BLOCK, 1), t0 + jnp.arange(Q_BLOCK),
                             lax.dynamic_slice_in_dim(kp, t0, span, 1), lax.dynamic_slice_in_dim(vp, t0, span, 1),
                             t0 - WINDOW + jnp.arange(span), tab_a)
    o = lax.map(step, jnp.arange(T // Q_BLOCK))
    return jnp.moveaxis(o, 0, 1).reshape(B, T, *o.shape[3:])


def nsa_combine(gates, o_c, o_s, o_w):
    g = gates[..., None].astype(o_c.dtype)
    o = g[:, :, 0] * o_c + g[:, :, 1] * o_s + g[:, :, 2] * o_w
    return o.reshape(o.shape[0], o.shape[1], -1)


def nsa_prompt(q, k3, v3, gates, pe_k, pe_v, wk1, wk2, wv1, wv2, tab_a):
    B, T, G, HPG, DK = q.shape
    pos = jnp.arange(T)
    kc = nsa_compress(k3[:, :, 0], pe_k, wk1, wk2)
    vc = nsa_compress(v3[:, :, 0], pe_v, wv1, wv2)
    o_c, imp = nsa_cmp_branch(q, pos, kc, vc)
    nsb = T // CMP_BLK
    idx, valid = nsa_pick_blocks(imp, pos, nsb)
    n_sel = idx.shape[-1]
    ks = k3[:, :, 1].reshape(B, nsb, CMP_BLK, G, DK).transpose(0, 1, 3, 2, 4)
    vs = v3[:, :, 1].reshape(B, nsb, CMP_BLK, G, A_DV).transpose(0, 1, 3, 2, 4)
    gi = jnp.arange(G).reshape(1, G, 1)

    def sel_step(args):
        b, t0 = args
        ii = lax.dynamic_slice(idx, (b, t0, 0, 0), (1, Q_BLOCK, G, n_sel))
        vv = lax.dynamic_slice(valid, (b, t0, 0, 0), (1, Q_BLOCK, G, n_sel))
        qq = lax.dynamic_slice(q, (b, t0, 0, 0, 0), (1, Q_BLOCK, G, HPG, DK))
        return nsa_sel_attend(qq, t0 + jnp.arange(Q_BLOCK), ii, vv, ks[b, ii[0], gi][None],
                              vs[b, ii[0], gi][None], tab_a)[0]
    nqb = T // Q_BLOCK
    bs = jnp.repeat(jnp.arange(B), nqb)
    t0s = jnp.tile(jnp.arange(nqb) * Q_BLOCK, B)
    o_s = lax.map(sel_step, (bs, t0s)).reshape(B, T, G, HPG, A_DV)
    o_w = window_prompt(q, k3[:, :, 2], v3[:, :, 2], tab_a)
    return nsa_combine(gates, o_c, o_s, o_w)


def nsa_sample(q, k3, v3, gates, cache_k, cache_v, win_k, win_v, page_table,
               pe_k, pe_v, wk1, wk2, wv1, wv2, tab_a):
    B, S, G, HPG, DK = q.shape
    P = page_table.shape[1] * PAGE_SIZE
    L = P + S
    pos = P + jnp.arange(S)
    rows_k = jnp.concatenate([cache_k[page_table, :, 0].reshape(B, P, G, DK), k3[:, :, 0]], axis=1)
    rows_v = jnp.concatenate([cache_v[page_table, :, 0].reshape(B, P, G, A_DV), v3[:, :, 0]], axis=1)
    kc = nsa_compress(rows_k, pe_k, wk1, wk2)
    vc = nsa_compress(rows_v, pe_v, wv1, wv2)
    o_c, imp = nsa_cmp_branch(q, pos, kc, vc)
    idx, valid = nsa_pick_blocks(imp, pos, -(-L // CMP_BLK))
    kpos = idx[..., None] * CMP_BLK + jnp.arange(CMP_BLK)
    gi = jnp.arange(G).reshape(1, 1, G, 1, 1)
    ksel = gather_paged(cache_k, page_table, k3, kpos, 1, gi)
    vsel = gather_paged(cache_v, page_table, v3, kpos, 1, gi)
    o_s = nsa_sel_attend(q, pos, idx, valid, ksel, vsel, tab_a)
    wb = win_k.shape[1]
    kw = jnp.concatenate([win_k, k3[:, :, 2]], axis=1)
    vw = jnp.concatenate([win_v, v3[:, :, 2]], axis=1)
    o_w = window_attend(q, pos, kw, vw, P - wb + jnp.arange(wb + S), tab_a)
    return nsa_combine(gates, o_c, o_s, o_w)


def dsa_pick(qi, wi, ki, q_pos, topk):
    L = ki.shape[1]
    dots = jnp.einsum('bqhd,bld->bqhl', qi, ki, preferred_element_type=jnp.float32) * IDX_DIM ** -0.5
    score = jnp.einsum('bqhl,bqh->bql', jax.nn.relu(dots), wi.astype(jnp.float32) * IDX_HEADS ** -0.5)
    score = jnp.where(jnp.arange(L)[None, None, :] <= q_pos[None, :, None], score, NEG)
    _, idx = lax.top_k(score, topk)
    return idx, idx <= q_pos[None, :, None]


def dsa_attend(q, q_pos, idx, valid, ksel, vsel, tab_b):
    B, Q, G, HPG, DH = q.shape
    dist = q_pos[None, :, None] - idx
    bias = tab_b[t5_bucket(dist)].reshape(B, Q, -1, G, HPG).transpose(0, 1, 3, 4, 2).astype(jnp.float32)
    logits = jnp.einsum('bqghd,bqkgd->bqghk', q, ksel, preferred_element_type=jnp.float32) * DH ** -0.5 + bias
    p = masked_softmax(logits, valid[:, :, None, None, :])
    o = jnp.einsum('bqghk,bqkgd->bqghd', p.astype(vsel.dtype), vsel)
    return o.reshape(B, Q, -1)


def dsa_prompt(q, k, v, qi, ki, wi, tab_b):
    B, T = q.shape[:2]
    topk = min(IDX_TOPK, T // 4)
    bidx = jnp.arange(B)[:, None, None]

    def step(i):
        t0 = i * Q_BLOCK
        q_pos = t0 + jnp.arange(Q_BLOCK)
        sl = lambda a: lax.dynamic_slice_in_dim(a, t0, Q_BLOCK, 1)
        idx, valid = dsa_pick(sl(qi), sl(wi), ki, q_pos, topk)
        return dsa_attend(sl(q), q_pos, idx, valid, k[bidx, idx], v[bidx, idx], tab_b)
    o = lax.map(step, jnp.arange(T // Q_BLOCK))
    return jnp.moveaxis(o, 0, 1).reshape(B, T, -1)


def dsa_sample(q, k, v, qi, ki, wi, cache_k, cache_v, cache_kidx, page_table, tab_b):
    B, S = q.shape[:2]
    P = page_table.shape[1] * PAGE_SIZE
    L = P + S
    q_pos = P + jnp.arange(S)
    ki_all = jnp.concatenate([cache_kidx[page_table].reshape(B, P, IDX_DIM), ki], axis=1)
    idx, valid = dsa_pick(qi, wi, ki_all, q_pos, min(IDX_TOPK, L // 4))
    return dsa_attend(q, q_pos, idx, valid, gather_paged(cache_k, page_table, k, idx),
                      gather_paged(cache_v, page_table, v, idx), tab_b)


def moba_pick(q, q_pos, kmean):
    B, Q, H, DH = q.shape
    nf = kmean.shape[1]
    s = jnp.einsum('bqghd,bjgd->bqghj', q.reshape(B, Q, C_GROUPS, C_HPG, DH).astype(jnp.float32),
                   kmean).reshape(B, Q, H, nf)
    ncand = max(nf, MOBA_TOP)
    s = jnp.pad(s, ((0, 0), (0, 0), (0, 0), (0, ncand - nf)), constant_values=NEG)
    past = jnp.arange(ncand)[None, :] < (q_pos // MOBA_BLK)[:, None]
    s = jnp.where(past[None, :, None, :], s, NEG)
    top_s, idx = lax.top_k(s, MOBA_TOP)
    return idx, top_s > 0.5 * NEG


def moba_attend(q, q_pos, idx, valid, ksel, vsel, own_pos, kown, vown, tab_c):
    B, Q, H, DH = q.shape
    scale = DH ** -0.5
    sel_dist = q_pos[None, :, None, None, None] - (idx[..., None] * MOBA_BLK + jnp.arange(MOBA_BLK))
    hidx = jnp.arange(H).reshape(1, 1, H, 1, 1)
    ls = (jnp.einsum('bqhd,bqhrsd->bqhrs', q, ksel, preferred_element_type=jnp.float32) * scale
          + tab_c.T[hidx, t5_bucket(sel_dist)].astype(jnp.float32))
    ls = jnp.where(valid[..., None], ls, NEG).reshape(B, Q, H, MOBA_TOP * MOBA_BLK)
    own_dist = q_pos[:, None] - own_pos
    lo = jnp.einsum('bqghd,bqgsd->bqghs', q.reshape(B, Q, C_GROUPS, C_HPG, DH), kown,
                    preferred_element_type=jnp.float32).reshape(B, Q, H, MOBA_BLK) * scale
    lo = lo + tab_c[t5_bucket(own_dist)].transpose(0, 2, 1)[None].astype(jnp.float32)
    lo = jnp.where((own_dist >= 0)[None, :, None, :], lo, NEG)
    p = jax.nn.softmax(jnp.concatenate([ls, lo], axis=-1), axis=-1)
    ps = p[..., : MOBA_TOP * MOBA_BLK].reshape(B, Q, H, MOBA_TOP, MOBA_BLK).astype(vsel.dtype)
    po = p[..., MOBA_TOP * MOBA_BLK:].reshape(B, Q, C_GROUPS, C_HPG, MOBA_BLK).astype(vown.dtype)
    o = (jnp.einsum('bqhrs,bqhrsd->bqhd', ps, vsel)
         + jnp.einsum('bqghs,bqgsd->bqghd', po, vown).reshape(B, Q, H, DH))
    return o


def moba_prompt(q, k, v, tab_c):
    B, T, H, DH = q.shape
    nb = -(-T // MOBA_BLK)
    pad = ((0, 0), (0, nb * MOBA_BLK - T), (0, 0), (0, 0))
    to_blocks = lambda a: jnp.pad(a, pad).reshape(B, nb, MOBA_BLK, C_GROUPS, DH).transpose(0, 1, 3, 2, 4)
    kb, vb = to_blocks(k), to_blocks(v)
    nf = T // MOBA_BLK
    kmean = k[:, : nf * MOBA_BLK].reshape(B, nf, MOBA_BLK, C_GROUPS, DH).astype(jnp.float32).mean(axis=2)
    idx, valid = moba_pick(q, jnp.arange(T), kmean)
    gh = (jnp.arange(H) // C_HPG)[None, :, None]
    own_shape = (1, MOBA_QCHUNK, C_GROUPS, MOBA_BLK, DH)

    def step(args):
        b, t0 = args
        qq = lax.dynamic_slice(q, (b, t0, 0, 0), (1, MOBA_QCHUNK, H, DH))
        ii = lax.dynamic_slice(idx, (b, t0, 0, 0), (1, MOBA_QCHUNK, H, MOBA_TOP))
        vv = lax.dynamic_slice(valid, (b, t0, 0, 0), (1, MOBA_QCHUNK, H, MOBA_TOP))
        own = t0 // MOBA_BLK
        own_pos = jnp.broadcast_to(own * MOBA_BLK + jnp.arange(MOBA_BLK), (MOBA_QCHUNK, MOBA_BLK))
        return moba_attend(qq, t0 + jnp.arange(MOBA_QCHUNK), ii, vv, kb[b, ii[0], gh][None], vb[b, ii[0], gh][None],
                           own_pos, jnp.broadcast_to(kb[b, own], own_shape),
                           jnp.broadcast_to(vb[b, own], own_shape), tab_c)[0]
    nqc = T // MOBA_QCHUNK
    bs = jnp.repeat(jnp.arange(B), nqc)
    t0s = jnp.tile(jnp.arange(nqc) * MOBA_QCHUNK, B)
    return lax.map(step, (bs, t0s)).reshape(B, T, H * DH)


def moba_sample(q, k, v, cache_k, cache_v, page_table, tab_c):
    B, S, H, DH = q.shape
    P = page_table.shape[1] * PAGE_SIZE
    L = P + S
    q_pos = P + jnp.arange(S)
    k_all = jnp.concatenate([cache_k[page_table].reshape(B, P, C_GROUPS, DH), k], axis=1)
    nf = L // MOBA_BLK
    kmean = k_all[:, : nf * MOBA_BLK].reshape(B, nf, MOBA_BLK, C_GROUPS, DH).astype(jnp.float32).mean(axis=2)
    idx, valid = moba_pick(q, q_pos, kmean)
    gh = (jnp.arange(H) // C_HPG).reshape(1, 1, H, 1, 1)
    sel_pos = idx[..., None] * MOBA_BLK + jnp.arange(MOBA_BLK)
    own_pos = (q_pos // MOBA_BLK)[:, None] * MOBA_BLK + jnp.arange(MOBA_BLK)
    opos = jnp.broadcast_to(own_pos[None, :, None, :], (B, S, C_GROUPS, MOBA_BLK))
    gi = jnp.arange(C_GROUPS).reshape(1, 1, C_GROUPS, 1)
    o = moba_attend(q, q_pos, idx, valid,
                    gather_paged(cache_k, page_table, k, sel_pos, gh), gather_paged(cache_v, page_table, v, sel_pos, gh),
                    own_pos, gather_paged(cache_k, page_table, k, opos, gi), gather_paged(cache_v, page_table, v, opos, gi),
                    tab_c)
    return o.reshape(B, S, H * DH)


def mem_kv(mem, g, wk, wv):
    m = rms_norm(mem, g)
    B = mem.shape[0]
    return (m @ wk).reshape(B, N_MEM, X_HEADS, X_DH), (m @ wv).reshape(B, N_MEM, X_HEADS, X_DH)


def cross_attend(h, mk, mv, wq, wo):
    B, T = h.shape[:2]
    q = (h @ wq).reshape(B, T, X_HEADS, X_DH)
    logits = jnp.einsum('bthd,bmhd->bthm', q, mk, preferred_element_type=jnp.float32) * X_DH ** -0.5
    p = jax.nn.softmax(logits, axis=-1)
    o = jnp.einsum('bthm,bmhd->bthd', p.astype(mv.dtype), mv).reshape(B, T, X_HEADS * X_DH)
    return o @ wo


def swiglu(h, w1, w3, w2):
    return (jax.nn.silu(h @ w1) * (h @ w3)) @ w2


def moe_swiglu(x2, w_router, b_router, w1, w3, w2):
    N, D = x2.shape
    logits = jnp.dot(x2, w_router, preferred_element_type=jnp.float32) + b_router.astype(jnp.float32)
    top_l, top_e = lax.top_k(logits, TOP_K)
    gate = jax.nn.softmax(top_l, axis=-1)
    A = N * TOP_K
    e_flat = top_e.reshape(A)
    order = jnp.argsort(e_flat)
    tok_sorted = order // TOP_K
    gate_sorted = gate.reshape(A)[order]
    counts = jnp.bincount(e_flat, length=N_EXPERTS)
    starts = jnp.cumsum(counts) - counts
    blk = max(8, min(MOE_MAX_ROWS, A // N_EXPERTS))
    rows = jnp.pad(x2[tok_sorted], ((0, blk), (0, 0)))
    ys = jnp.zeros((A + blk, D), x2.dtype)
    for e in range(N_EXPERTS):
        def body(j, acc, e=e):
            r0 = starts[e] + j * blk
            yb = swiglu(lax.dynamic_slice_in_dim(rows, r0, blk, 0), w1[e], w3[e], w2[e])
            keep = (j * blk + jnp.arange(blk) < counts[e])[:, None]
            cur = lax.dynamic_slice_in_dim(acc, r0, blk, 0)
            return lax.dynamic_update_slice_in_dim(acc, jnp.where(keep, yb, cur), r0, 0)
        ys = lax.fori_loop(0, (counts[e] + blk - 1) // blk, body, ys)
    contrib = ys[:A] * gate_sorted[:, None].astype(x2.dtype)
    return jnp.zeros_like(x2).at[tok_sorted].add(contrib)


def even_split(z):
    B, T = z.shape[:2]
    qa, ka, va, ga, qb, kb, vb, qi, ki, wi = split_cols(z, EVEN_SPLITS)
    return (qa.reshape(B, T, A_GROUPS, A_HPG, A_DK),
            ka.reshape(B, T, 3, A_GROUPS, A_DK),
            va.reshape(B, T, 3, A_GROUPS, A_DV),
            jax.nn.sigmoid(ga.astype(jnp.float32)).reshape(B, T, 3, A_GROUPS, A_HPG),
            qb.reshape(B, T, B_GROUPS, B_HPG, B_DH),
            kb.reshape(B, T, B_GROUPS, B_DH),
            vb.reshape(B, T, B_GROUPS, B_DH),
            qi.reshape(B, T, IDX_HEADS, IDX_DIM), ki, wi)


def odd_split(z):
    B, T = z.shape[:2]
    q, k, v = split_cols(z, ODD_SPLITS)
    return (q.reshape(B, T, C_HEADS, C_DH), k.reshape(B, T, C_GROUPS, C_DH), v.reshape(B, T, C_GROUPS, C_DH))


def setup_inputs(seed: int = 0) -> dict:
    keys = iter(jax.random.split(jax.random.key(seed), 64))

    def nrm(shape, scale=1.0):
        return jax.random.normal(next(keys), shape, jnp.float32) * scale

    def gain(shape):
        return 1.0 + nrm(shape, 0.05)
    n_pages = PAST_LEN // PAGE_SIZE
    n_pool = (DEC_BATCH * n_pages * 5 + 3) // 4
    wbuf = min(WINDOW, PAST_LEN)
    perm = jax.random.permutation(next(keys), n_pool)
    page_table = perm[: DEC_BATCH * n_pages].reshape(DEC_BATCH, n_pages).astype(jnp.int32)
    even_w, odd_w = sum(EVEN_SPLITS), sum(ODD_SPLITS)
    mix_e, mix_o, xw = A_HEADS * A_DV + B_HEADS * B_DH, C_HEADS * C_DH, X_HEADS * X_DH
    return {
        'x_prompt': nrm((BATCH, SEQ, D_MODEL)),
        'x_sample': nrm((DEC_BATCH, DEC_SEQ, D_MODEL)),
        'mem_prompt': nrm((BATCH, N_MEM, D_MODEL)),
        'cache_a_k': nrm((N_EVEN, n_pool, PAGE_SIZE, 2, A_GROUPS, A_DK)),
        'cache_a_v': nrm((N_EVEN, n_pool, PAGE_SIZE, 2, A_GROUPS, A_DV)),
        'state_a_win_k': nrm((N_EVEN, DEC_BATCH, wbuf, A_GROUPS, A_DK)),
        'state_a_win_v': nrm((N_EVEN, DEC_BATCH, wbuf, A_GROUPS, A_DV)),
        'cache_b_k': nrm((N_EVEN, n_pool, PAGE_SIZE, B_GROUPS, B_DH)),
        'cache_b_v': nrm((N_EVEN, n_pool, PAGE_SIZE, B_GROUPS, B_DH)),
        'cache_b_kidx': nrm((N_EVEN, n_pool, PAGE_SIZE, IDX_DIM)),
        'cache_c_k': nrm((N_ODD, n_pool, PAGE_SIZE, C_GROUPS, C_DH)),
        'cache_c_v': nrm((N_ODD, n_pool, PAGE_SIZE, C_GROUPS, C_DH)),
        'cache_mem_k': nrm((DEPTH, DEC_BATCH, N_MEM, X_HEADS, X_DH)),
        'cache_mem_v': nrm((DEPTH, DEC_BATCH, N_MEM, X_HEADS, X_DH)),
        'page_table': page_table,
        'rel_bias': nrm((N_BUCKETS, HEAD_SLOTS), 0.3),
        'norm_mix': gain((DEPTH, D_MODEL)),
        'norm_mem': gain((DEPTH, D_MODEL)),
        'norm_cross': gain((DEPTH, D_MODEL)),
        'norm_ffn': gain((DEPTH, D_MODEL)),
        'norm_final': gain((D_MODEL,)),
        'w_cross_q': nrm((DEPTH, D_MODEL, xw), D_MODEL ** -0.5),
        'w_cross_k': nrm((DEPTH, D_MODEL, xw), D_MODEL ** -0.5),
        'w_cross_v': nrm((DEPTH, D_MODEL, xw), D_MODEL ** -0.5),
        'w_cross_o': nrm((DEPTH, xw, D_MODEL), xw ** -0.5),
        'w_in_even': nrm((N_EVEN, D_MODEL, even_w), D_MODEL ** -0.5),
        'w_out_even': nrm((N_EVEN, mix_e, D_MODEL), mix_e ** -0.5),
        'nsa_pe_k': nrm((N_EVEN, CMP_BLK, A_DK), 0.1),
        'nsa_pe_v': nrm((N_EVEN, CMP_BLK, A_DV), 0.1),
        'nsa_phi_k1': nrm((N_EVEN, CMP_BLK * A_DK, A_DK), (CMP_BLK * A_DK) ** -0.5),
        'nsa_phi_k2': nrm((N_EVEN, A_DK, A_DK), A_DK ** -0.5),
        'nsa_phi_v1': nrm((N_EVEN, CMP_BLK * A_DV, A_DV), (CMP_BLK * A_DV) ** -0.5),
        'nsa_phi_v2': nrm((N_EVEN, A_DV, A_DV), A_DV ** -0.5),
        'w_ffn1': nrm((N_EVEN, D_MODEL, D_FF), D_MODEL ** -0.5),
        'w_ffn3': nrm((N_EVEN, D_MODEL, D_FF), D_MODEL ** -0.5),
        'w_ffn2': nrm((N_EVEN, D_FF, D_MODEL), D_FF ** -0.5),
        'w_in_odd': nrm((N_ODD, D_MODEL, odd_w), D_MODEL ** -0.5),
        'w_out_odd': nrm((N_ODD, mix_o, D_MODEL), mix_o ** -0.5),
        'w_router': nrm((N_ODD, D_MODEL, N_EXPERTS), D_MODEL ** -0.5),
        'b_router': nrm((N_ODD, N_EXPERTS), 0.01),
        'w_exp1': nrm((N_ODD, N_EXPERTS, D_MODEL, D_FF), D_MODEL ** -0.5),
        'w_exp3': nrm((N_ODD, N_EXPERTS, D_MODEL, D_FF), D_MODEL ** -0.5),
        'w_exp2': nrm((N_ODD, N_EXPERTS, D_FF, D_MODEL), D_FF ** -0.5),
    }


def reference(x_prompt, x_sample, mem_prompt, cache_a_k, cache_a_v, state_a_win_k, state_a_win_v,
              cache_b_k, cache_b_v, cache_b_kidx, cache_c_k, cache_c_v, cache_mem_k, cache_mem_v, page_table,
              rel_bias, norm_mix, norm_mem, norm_cross, norm_ffn, norm_final,
              w_cross_q, w_cross_k, w_cross_v, w_cross_o, w_in_even, w_out_even,
              nsa_pe_k, nsa_pe_v, nsa_phi_k1, nsa_phi_k2, nsa_phi_v1, nsa_phi_v2,
              w_ffn1, w_ffn3, w_ffn2, w_in_odd, w_out_odd, w_router, b_router, w_exp1, w_exp3, w_exp2):
    xp, xs = x_prompt, x_sample
    B, T, D = xp.shape
    Bs, S = xs.shape[:2]
    tab_a = rel_bias[:, :A_HEADS]
    tab_b = rel_bias[:, A_HEADS:A_HEADS + B_HEADS]
    tab_c = rel_bias[:, :C_HEADS]
    names = ('a_k_p', 'a_v_p', 'aw_k_p', 'aw_v_p', 'b_k_p', 'b_v_p', 'b_i_p', 'c_k_p', 'c_v_p', 'm_k_p', 'm_v_p',
             'a_k_s', 'a_v_s', 'aw_k_s', 'aw_v_s', 'b_k_s', 'b_v_s', 'b_i_s', 'c_k_s', 'c_v_s')
    new = {n: [] for n in names}
    for layer in range(DEPTH):
        li = layer // 2
        hp = rms_norm(xp, norm_mix[layer])
        hs = rms_norm(xs, norm_mix[layer])
        if layer % 2 == 0:
            phi = (nsa_pe_k[li], nsa_pe_v[li], nsa_phi_k1[li], nsa_phi_k2[li], nsa_phi_v1[li], nsa_phi_v2[li])
            qa, ka, va, ga, qb, kb, vb, qi, ki, wi = even_split(hp @ w_in_even[li])
            o_a = nsa_prompt(qa, ka, va, ga, *phi, tab_a)
            o_b = dsa_prompt(qb, kb, vb, qi, ki, wi, tab_b)
            xp = xp + jnp.concatenate([o_a, o_b], axis=-1) @ w_out_even[li]
            wk = min(WINDOW, T)
            new['a_k_p'].append(ka[:, :, :2]); new['a_v_p'].append(va[:, :, :2])
            new['aw_k_p'].append(ka[:, T - wk:, 2]); new['aw_v_p'].append(va[:, T - wk:, 2])
            new['b_k_p'].append(kb); new['b_v_p'].append(vb); new['b_i_p'].append(ki)
            qa, ka, va, ga, qb, kb, vb, qi, ki, wi = even_split(hs @ w_in_even[li])
            o_a = nsa_sample(qa, ka, va, ga, cache_a_k[li], cache_a_v[li], state_a_win_k[li], state_a_win_v[li],
                             page_table, *phi, tab_a)
            o_b = dsa_sample(qb, kb, vb, qi, ki, wi, cache_b_k[li], cache_b_v[li], cache_b_kidx[li], page_table, tab_b)
            xs = xs + jnp.concatenate([o_a, o_b], axis=-1) @ w_out_even[li]
            new['a_k_s'].append(ka[:, :, :2]); new['a_v_s'].append(va[:, :, :2])
            new['aw_k_s'].append(ka[:, :, 2]); new['aw_v_s'].append(va[:, :, 2])
            new['b_k_s'].append(kb); new['b_v_s'].append(vb); new['b_i_s'].append(ki)
        else:
            q, k, v = odd_split(hp @ w_in_odd[li])
            xp = xp + moba_prompt(q, k, v, tab_c) @ w_out_odd[li]
            new['c_k_p'].append(k); new['c_v_p'].append(v)
            q, k, v = odd_split(hs @ w_in_odd[li])
            xs = xs + moba_sample(q, k, v, cache_c_k[li], cache_c_v[li], page_table, tab_c) @ w_out_odd[li]
            new['c_k_s'].append(k); new['c_v_s'].append(v)
        mk, mv = mem_kv(mem_prompt, norm_mem[layer], w_cross_k[layer], w_cross_v[layer])
        new['m_k_p'].append(mk); new['m_v_p'].append(mv)
        xp = xp + cross_attend(rms_norm(xp, norm_cross[layer]), mk, mv, w_cross_q[layer], w_cross_o[layer])
        xs = xs + cross_attend(rms_norm(xs, norm_cross[layer]), cache_mem_k[layer], cache_mem_v[layer],
                               w_cross_q[layer], w_cross_o[layer])
        hp = rms_norm(xp, norm_ffn[layer]).reshape(B * T, D)
        hs = rms_norm(xs, norm_ffn[layer]).reshape(Bs * S, D)
        if layer % 2 == 0:
            fp = swiglu(hp, w_ffn1[li], w_ffn3[li], w_ffn2[li])
            fs = swiglu(hs, w_ffn1[li], w_ffn3[li], w_ffn2[li])
        else:
            fp = moe_swiglu(hp, w_router[li], b_router[li], w_exp1[li], w_exp3[li], w_exp2[li])
            fs = moe_swiglu(hs, w_router[li], b_router[li], w_exp1[li], w_exp3[li], w_exp2[li])
        xp = xp + fp.reshape(B, T, D)
        xs = xs + fs.reshape(Bs, S, D)
    y_prompt = rms_norm(xp, norm_final)
    y_sample = rms_norm(xs, norm_final)
    return (y_prompt, y_sample,
            jnp.stack(new['a_k_p']), jnp.stack(new['a_v_p']), jnp.stack(new['aw_k_p']), jnp.stack(new['aw_v_p']),
            jnp.stack(new['b_k_p']), jnp.stack(new['b_v_p']), jnp.stack(new['b_i_p']),
            jnp.stack(new['c_k_p']), jnp.stack(new['c_v_p']), jnp.stack(new['m_k_p']), jnp.stack(new['m_v_p']),
            jnp.stack(new['a_k_s']), jnp.stack(new['a_v_s']), jnp.stack(new['aw_k_s']), jnp.stack(new['aw_v_s']),
            jnp.stack(new['b_k_s']), jnp.stack(new['b_v_s']), jnp.stack(new['b_i_s']),
            jnp.stack(new['c_k_s']), jnp.stack(new['c_v_s']))
```

```python
import functools
import math
import jax, jax.numpy as jnp
from jax import lax
import numpy as np
from jax.experimental import pallas as pl
from jax.experimental.pallas import tpu as pltpu

D_MODEL = 4096
BATCH = 4
SEQ = 2048
DEPTH = 2
DEC_BATCH = 8
DEC_SEQ = 1
PAST_LEN = 16384
PAGE_SIZE = 128

N_EVEN = (DEPTH + 1) // 2
N_ODD = DEPTH // 2
HEAD_SLOTS = 32
A_HEADS = 16
A_GROUPS = 2
A_HPG = A_HEADS // A_GROUPS
A_DK = 192
A_DV = 128
CMP_BLK = 64
N_SEL = 16
WINDOW = 512
B_HEADS = 16
B_GROUPS = 2
B_HPG = B_HEADS // B_GROUPS
B_DH = 128
IDX_HEADS = 8
IDX_DIM = 64
IDX_TOPK = 256
C_HEADS = 32
C_GROUPS = 8
C_HPG = C_HEADS // C_GROUPS
C_DH = 128
MOBA_BLK = 256
MOBA_TOP = 3
MOBA_QCHUNK = 32
N_MEM = 256
X_HEADS = 4
X_DH = 128
D_FF = 14336
N_EXPERTS = 8
TOP_K = 2
MOE_MAX_ROWS = 512
N_BUCKETS = 32
REL_EXACT = 16
REL_MAX_DIST = 1024
Q_BLOCK = 128
EPS = 1e-6
NEG = -1e30
FORCE = 1e9
EVEN_SPLITS = (A_HEADS * A_DK, 3 * A_GROUPS * A_DK, 3 * A_GROUPS * A_DV, 3 * A_HEADS,
               B_HEADS * B_DH, B_GROUPS * B_DH, B_GROUPS * B_DH, IDX_HEADS * IDX_DIM, IDX_DIM, IDX_HEADS)
ODD_SPLITS = (C_HEADS * C_DH, C_GROUPS * C_DH, C_GROUPS * C_DH)


def _rmsnorm_body(x_ref, g_ref, o_ref):
    x = x_ref[...]
    y = x * lax.rsqrt(jnp.mean(x * x, axis=-1, keepdims=True) + EPS)
    o_ref[...] = (y * g_ref[...]).astype(o_ref.dtype)


def rms_norm(x, g, out_dtype=None):
    out_dtype = out_dtype or x.dtype
    shape = x.shape
    d = shape[-1]
    x2 = x.reshape(-1, d)
    rows = x2.shape[0]
    tr = min(rows, 256)
    out = pl.pallas_call(
        _rmsnorm_body,
        grid=(rows // tr,),
        in_specs=[pl.BlockSpec((tr, d), lambda i: (i, 0)),
                  pl.BlockSpec((1, d), lambda i: (0, 0))],
        out_specs=pl.BlockSpec((tr, d), lambda i: (i, 0)),
        out_shape=jax.ShapeDtypeStruct((rows, d), out_dtype),
    )(x2, g.reshape(1, d).astype(jnp.float32))
    return out.reshape(shape)


def split_cols(z, sizes):
    cuts = [int(c) for c in np.cumsum(sizes)[:-1]]
    return jnp.split(z, cuts, axis=-1)


def masked_softmax(logits, mask):
    p = jax.nn.softmax(jnp.where(mask, logits, NEG), axis=-1)
    return jnp.where(mask, p, 0.0)


def t5_bucket(dist):
    n = jnp.maximum(dist, 0)
    nf = jnp.maximum(n, 1).astype(jnp.float32)
    large = REL_EXACT + (jnp.log(nf / REL_EXACT) / math.log(REL_MAX_DIST / REL_EXACT)
                         * (N_BUCKETS - REL_EXACT)).astype(jnp.int32)
    return jnp.where(n < REL_EXACT, n, jnp.minimum(large, N_BUCKETS - 1))


def gather_paged(pool, page_table, new_rows, pos, *extra):
    past_len = page_table.shape[1] * PAGE_SIZE
    b = jnp.arange(pos.shape[0]).reshape((-1,) + (1,) * (pos.ndim - 1))
    pc = jnp.clip(pos, 0, past_len - 1)
    phys = page_table[b, pc // PAGE_SIZE]
    old = pool[(phys, pc % PAGE_SIZE) + extra]
    new = new_rows[(b, jnp.clip(pos - past_len, 0, new_rows.shape[1] - 1)) + extra]
    is_new = (pos >= past_len).reshape(pos.shape + (1,) * (old.ndim - pos.ndim))
    return jnp.where(is_new, new, old)


def nsa_compress(rows, pe, w1, w2):
    B, L, G, d = rows.shape
    nc = L // CMP_BLK
    blk = rows[:, : nc * CMP_BLK].reshape(B, nc, CMP_BLK, G, d) + pe[None, None, :, None, :]
    flat = blk.transpose(0, 1, 3, 2, 4).reshape(B, nc, G, CMP_BLK * d)
    return jax.nn.silu(flat @ w1) @ w2


def nsa_cmp_branch(q, q_pos, kc, vc):
    nc = kc.shape[1]
    logits = jnp.einsum('bqghd,bjgd->bqghj', q, kc, preferred_element_type=jnp.float32) * A_DK ** -0.5
    vis = (jnp.arange(nc) + 1) * CMP_BLK <= (q_pos + 1)[:, None]
    p = masked_softmax(logits, vis[None, :, None, None, :])
    o = jnp.einsum('bqghj,bjgd->bqghd', p.astype(vc.dtype), vc)
    return o, p.sum(axis=3)


def nsa_pick_blocks(imp, q_pos, n_blocks):
    score = jnp.pad(imp, ((0, 0), (0, 0), (0, 0), (0, n_blocks - imp.shape[-1])))
    j = jnp.arange(n_blocks)[None, :]
    own = (q_pos // CMP_BLK)[:, None]
    forced = (j == 0) | (j == own) | (j == own - 1)
    score = jnp.where(forced[None, :, None, :], FORCE, score)
    score = jnp.where((j <= own)[None, :, None, :], score, NEG)
    top_s, idx = lax.top_k(score, min(N_SEL, n_blocks))
    return idx, top_s > 0.5 * NEG


def nsa_sel_attend(q, q_pos, idx, valid, ksel, vsel, tab_a):
    B, Q, G, HPG, _ = q.shape
    kpos = idx[..., None] * CMP_BLK + jnp.arange(CMP_BLK)
    dist = q_pos[None, :, None, None, None] - kpos
    mask = valid[..., None] & (dist >= 0)
    tab2 = tab_a.reshape(N_BUCKETS, G, HPG).transpose(1, 0, 2)
    bias = tab2[jnp.arange(G).reshape(1, 1, G, 1, 1), t5_bucket(dist)]
    logits = (jnp.einsum('bqghd,bqgnsd->bqghns', q, ksel, preferred_element_type=jnp.float32) * A_DK ** -0.5
              + jnp.moveaxis(bias, -1, 3).astype(jnp.float32))
    shp = logits.shape
    p = masked_softmax(logits.reshape(B, Q, G, HPG, -1), mask.reshape(B, Q, G, 1, -1)).reshape(shp)
    return jnp.einsum('bqghns,bqgnsd->bqghd', p.astype(vsel.dtype), vsel)


def window_attend(q, q_pos, k, v, k_pos, tab_a):
    B, Q, G, HPG, _ = q.shape
    dist = q_pos[:, None] - k_pos[None, :]
    mask = (dist >= 0) & (dist < WINDOW) & (k_pos >= 0)[None, :]
    bias = tab_a[t5_bucket(dist)].reshape(Q, -1, G, HPG).transpose(0, 2, 3, 1).astype(jnp.float32)
    logits = jnp.einsum('bqghd,bsgd->bqghs', q, k, preferred_element_type=jnp.float32) * A_DK ** -0.5 + bias[None]
    p = masked_softmax(logits, mask[None, :, None, None, :])
    return jnp.einsum('bqghs,bsgd->bqghd', p.astype(v.dtype), v)


def window_prompt(q, k, v, tab_a):
    B, T = q.shape[:2]
    pad = ((0, 0), (WINDOW, 0), (0, 0), (0, 0))
    kp, vp = jnp.pad(k, pad), jnp.pad(v, pad)
    span = Q_BLOCK + WINDOW

    def step(i):
        t0 = i * Q_BLOCK
        return window_attend(lax.dynamic_slice_in_dim(q, t0, Q_BLOCK, 1), t0 + jnp.arange(Q_BLOCK),
                             lax.dynamic_slice_in_dim(kp, t0, span, 1), lax.dynamic_slice_in_dim(vp, t0, span, 1),
                             t0 - WINDOW + jnp.arange(span), tab_a)
    o = lax.map(step, jnp.arange(T // Q_BLOCK))
    return jnp.moveaxis(o, 0, 1).reshape(B, T, *o.shape[3:])


def nsa_combine(gates, o_c, o_s, o_w):
    g = gates[..., None].astype(o_c.dtype)
    o = g[:, :, 0] * o_c + g[:, :, 1] * o_s + g[:, :, 2] * o_w
    return o.reshape(o.shape[0], o.shape[1], -1)


def nsa_prompt(q, k3, v3, gates, pe_k, pe_v, wk1, wk2, wv1, wv2, tab_a):
    B, T, G, HPG, DK = q.shape
    pos = jnp.arange(T)
    kc = nsa_compress(k3[:, :, 0], pe_k, wk1, wk2)
    vc = nsa_compress(v3[:, :, 0], pe_v, wv1, wv2)
    o_c, imp = nsa_cmp_branch(q, pos, kc, vc)
    nsb = T // CMP_BLK
    idx, valid = nsa_pick_blocks(imp, pos, nsb)
    n_sel = idx.shape[-1]
    ks = k3[:, :, 1].reshape(B, nsb, CMP_BLK, G, DK).transpose(0, 1, 3, 2, 4)
    vs = v3[:, :, 1].reshape(B, nsb, CMP_BLK, G, A_DV).transpose(0, 1, 3, 2, 4)
    gi = jnp.arange(G).reshape(1, G, 1)

    def sel_step(args):
        b, t0 = args
        ii = lax.dynamic_slice(idx, (b, t0, 0, 0), (1, Q_BLOCK, G, n_sel))
        vv = lax.dynamic_slice(valid, (b, t0, 0, 0), (1, Q_BLOCK, G, n_sel))
        qq = lax.dynamic_slice(q, (b, t0, 0, 0, 0), (1, Q_BLOCK, G, HPG, DK))
        return nsa_sel_attend(qq, t0 + jnp.arange(Q_BLOCK), ii, vv, ks[b, ii[0], gi][None],
                              vs[b, ii[0], gi][None], tab_a)[0]
    nqb = T // Q_BLOCK
    bs = jnp.repeat(jnp.arange(B), nqb)
    t0s = jnp.tile(jnp.arange(nqb) * Q_BLOCK, B)
    o_s = lax.map(sel_step, (bs, t0s)).reshape(B, T, G, HPG, A_DV)
    o_w = window_prompt(q, k3[:, :, 2], v3[:, :, 2], tab_a)
    return nsa_combine(gates, o_c, o_s, o_w)


def nsa_sample(q, k3, v3, gates, cache_k, cache_v, win_k, win_v, page_table,
               pe_k, pe_v, wk1, wk2, wv1, wv2, tab_a):
    B, S, G, HPG, DK = q.shape
    P = page_table.shape[1] * PAGE_SIZE
    L = P + S
    pos = P + jnp.arange(S)
    rows_k = jnp.concatenate([cache_k[page_table, :, 0].reshape(B, P, G, DK), k3[:, :, 0]], axis=1)
    rows_v = jnp.concatenate([cache_v[page_table, :, 0].reshape(B, P, G, A_DV), v3[:, :, 0]], axis=1)
    kc = nsa_compress(rows_k, pe_k, wk1, wk2)
    vc = nsa_compress(rows_v, pe_v, wv1, wv2)
    o_c, imp = nsa_cmp_branch(q, pos, kc, vc)
    idx, valid = nsa_pick_blocks(imp, pos, -(-L // CMP_BLK))
    kpos = idx[..., None] * CMP_BLK + jnp.arange(CMP_BLK)
    gi = jnp.arange(G).reshape(1, 1, G, 1, 1)
    ksel = gather_paged(cache_k, page_table, k3, kpos, 1, gi)
    vsel = gather_paged(cache_v, page_table, v3, kpos, 1, gi)
    o_s = nsa_sel_attend(q, pos, idx, valid, ksel, vsel, tab_a)
    wb = win_k.shape[1]
    kw = jnp.concatenate([win_k, k3[:, :, 2]], axis=1)
    vw = jnp.concatenate([win_v, v3[:, :, 2]], axis=1)
    o_w = window_attend(q, pos, kw, vw, P - wb + jnp.arange(wb + S), tab_a)
    return nsa_combine(gates, o_c, o_s, o_w)


def dsa_pick(qi, wi, ki, q_pos, topk):
    L = ki.shape[1]
    dots = jnp.einsum('bqhd,bld->bqhl', qi, ki, preferred_element_type=jnp.float32) * IDX_DIM ** -0.5
    score = jnp.einsum('bqhl,bqh->bql', jax.nn.relu(dots), wi.astype(jnp.float32) * IDX_HEADS ** -0.5)
    score = jnp.where(jnp.arange(L)[None, None, :] <= q_pos[None, :, None], score, NEG)
    _, idx = lax.top_k(score, topk)
    return idx, idx <= q_pos[None, :, None]


def dsa_attend(q, q_pos, idx, valid, ksel, vsel, tab_b):
    B, Q, G, HPG, DH = q.shape
    dist = q_pos[None, :, None] - idx
    bias = tab_b[t5_bucket(dist)].reshape(B, Q, -1, G, HPG).transpose(0, 1, 3, 4, 2).astype(jnp.float32)
    logits = jnp.einsum('bqghd,bqkgd->bqghk', q, ksel, preferred_element_type=jnp.float32) * DH ** -0.5 + bias
    p = masked_softmax(logits, valid[:, :, None, None, :])
    o = jnp.einsum('bqghk,bqkgd->bqghd', p.astype(vsel.dtype), vsel)
    return o.reshape(B, Q, -1)


def dsa_prompt(q, k, v, qi, ki, wi, tab_b):
    B, T = q.shape[:2]
    topk = min(IDX_TOPK, T // 4)
    bidx = jnp.arange(B)[:, None, None]

    def step(i):
        t0 = i * Q_BLOCK
        q_pos = t0 + jnp.arange(Q_BLOCK)
        sl = lambda a: lax.dynamic_slice_in_dim(a, t0, Q_BLOCK, 1)
        idx, valid = dsa_pick(sl(qi), sl(wi), ki, q_pos, topk)
        return dsa_attend(sl(q), q_pos, idx, valid, k[bidx, idx], v[bidx, idx], tab_b)
    o = lax.map(step, jnp.arange(T // Q_BLOCK))
    return jnp.moveaxis(o, 0, 1).reshape(B, T, -1)


def dsa_sample(q, k, v, qi, ki, wi, cache_k, cache_v, cache_kidx, page_table, tab_b):
    B, S = q.shape[:2]
    P = page_table.shape[1] * PAGE_SIZE
    L = P + S
    q_pos = P + jnp.arange(S)
    ki_all = jnp.concatenate([cache_kidx[page_table].reshape(B, P, IDX_DIM), ki], axis=1)
    idx, valid = dsa_pick(qi, wi, ki_all, q_pos, min(IDX_TOPK, L // 4))
    return dsa_attend(q, q_pos, idx, valid, gather_paged(cache_k, page_table, k, idx),
                      gather_paged(cache_v, page_table, v, idx), tab_b)


def moba_pick(q, q_pos, kmean):
    B, Q, H, DH = q.shape
    nf = kmean.shape[1]
    s = jnp.einsum('bqghd,bjgd->bqghj', q.reshape(B, Q, C_GROUPS, C_HPG, DH).astype(jnp.float32),
                   kmean).reshape(B, Q, H, nf)
    ncand = max(nf, MOBA_TOP)
    s = jnp.pad(s, ((0, 0), (0, 0), (0, 0), (0, ncand - nf)), constant_values=NEG)
    past = jnp.arange(ncand)[None, :] < (q_pos // MOBA_BLK)[:, None]
    s = jnp.where(past[None, :, None, :], s, NEG)
    top_s, idx = lax.top_k(s, MOBA_TOP)
    return idx, top_s > 0.5 * NEG


def moba_attend(q, q_pos, idx, valid, ksel, vsel, own_pos, kown, vown, tab_c):
    B, Q, H, DH = q.shape
    scale = DH ** -0.5
    sel_dist = q_pos[None, :, None, None, None] - (idx[..., None] * MOBA_BLK + jnp.arange(MOBA_BLK))
    hidx = jnp.arange(H).reshape(1, 1, H, 1, 1)
    ls = (jnp.einsum('bqhd,bqhrsd->bqhrs', q, ksel, preferred_element_type=jnp.float32) * scale
          + tab_c.T[hidx, t5_bucket(sel_dist)].astype(jnp.float32))
    ls = jnp.where(valid[..., None], ls, NEG).reshape(B, Q, H, MOBA_TOP * MOBA_BLK)
    own_dist = q_pos[:, None] - own_pos
    lo = jnp.einsum('bqghd,bqgsd->bqghs', q.reshape(B, Q, C_GROUPS, C_HPG, DH), kown,
                    preferred_element_type=jnp.float32).reshape(B, Q, H, MOBA_BLK) * scale
    lo = lo + tab_c[t5_bucket(own_dist)].transpose(0, 2, 1)[None].astype(jnp.float32)
    lo = jnp.where((own_dist >= 0)[None, :, None, :], lo, NEG)
    p = jax.nn.softmax(jnp.concatenate([ls, lo], axis=-1), axis=-1)
    ps = p[..., : MOBA_TOP * MOBA_BLK].reshape(B, Q, H, MOBA_TOP, MOBA_BLK).astype(vsel.dtype)
    po = p[..., MOBA_TOP * MOBA_BLK:].reshape(B, Q, C_GROUPS, C_HPG, MOBA_BLK).astype(vown.dtype)
    o = (jnp.einsum('bqhrs,bqhrsd->bqhd', ps, vsel)
         + jnp.einsum('bqghs,bqgsd->bqghd', po, vown).reshape(B, Q, H, DH))
    return o


def moba_prompt(q, k, v, tab_c):
    B, T, H, DH = q.shape
    nb = -(-T // MOBA_BLK)
    pad = ((0, 0), (0, nb * MOBA_BLK - T), (0, 0), (0, 0))
    to_blocks = lambda a: jnp.pad(a, pad).reshape(B, nb, MOBA_BLK, C_GROUPS, DH).transpose(0, 1, 3, 2, 4)
    kb, vb = to_blocks(k), to_blocks(v)
    nf = T // MOBA_BLK
    kmean = k[:, : nf * MOBA_BLK].reshape(B, nf, MOBA_BLK, C_GROUPS, DH).astype(jnp.float32).mean(axis=2)
    idx, valid = moba_pick(q, jnp.arange(T), kmean)
    gh = (jnp.arange(H) // C_HPG)[None, :, None]
    own_shape = (1, MOBA_QCHUNK, C_GROUPS, MOBA_BLK, DH)

    def step(args):
        b, t0 = args
        qq = lax.dynamic_slice(q, (b, t0, 0, 0), (1, MOBA_QCHUNK, H, DH))
        ii = lax.dynamic_slice(idx, (b, t0, 0, 0), (1, MOBA_QCHUNK, H, MOBA_TOP))
        vv = lax.dynamic_slice(valid, (b, t0, 0, 0), (1, MOBA_QCHUNK, H, MOBA_TOP))
        own = t0 // MOBA_BLK
        own_pos = jnp.broadcast_to(own * MOBA_BLK + jnp.arange(MOBA_BLK), (MOBA_QCHUNK, MOBA_BLK))
        return moba_attend(qq, t0 + jnp.arange(MOBA_QCHUNK), ii, vv, kb[b, ii[0], gh][None], vb[b, ii[0], gh][None],
                           own_pos, jnp.broadcast_to(kb[b, own], own_shape),
                           jnp.broadcast_to(vb[b, own], own_shape), tab_c)[0]
    nqc = T // MOBA_QCHUNK
    bs = jnp.repeat(jnp.arange(B), nqc)
    t0s = jnp.tile(jnp.arange(nqc) * MOBA_QCHUNK, B)
    return lax.map(step, (bs, t0s)).reshape(B, T, H * DH)


def moba_sample(q, k, v, cache_k, cache_v, page_table, tab_c):
    B, S, H, DH = q.shape
    P = page_table.shape[1] * PAGE_SIZE
    L = P + S
    q_pos = P + jnp.arange(S)
    k_all = jnp.concatenate([cache_k[page_table].reshape(B, P, C_GROUPS, DH), k], axis=1)
    nf = L // MOBA_BLK
    kmean = k_all[:, : nf * MOBA_BLK].reshape(B, nf, MOBA_BLK, C_GROUPS, DH).astype(jnp.float32).mean(axis=2)
    idx, valid = moba_pick(q, q_pos, kmean)
    gh = (jnp.arange(H) // C_HPG).reshape(1, 1, H, 1, 1)
    sel_pos = idx[..., None] * MOBA_BLK + jnp.arange(MOBA_BLK)
    own_pos = (q_pos // MOBA_BLK)[:, None] * MOBA_BLK + jnp.arange(MOBA_BLK)
    opos = jnp.broadcast_to(own_pos[None, :, None, :], (B, S, C_GROUPS, MOBA_BLK))
    gi = jnp.arange(C_GROUPS).reshape(1, 1, C_GROUPS, 1)
    o = moba_attend(q, q_pos, idx, valid,
                    gather_paged(cache_k, page_table, k, sel_pos, gh), gather_paged(cache_v, page_table, v, sel_pos, gh),
                    own_pos, gather_paged(cache_k, page_table, k, opos, gi), gather_paged(cache_v, page_table, v, opos, gi),
                    tab_c)
    return o.reshape(B, S, H * DH)


def mem_kv(mem, g, wk, wv):
    m = rms_norm(mem, g)
    B = mem.shape[0]
    return (m @ wk).reshape(B, N_MEM, X_HEADS, X_DH), (m @ wv).reshape(B, N_MEM, X_HEADS, X_DH)


def cross_attend(h, mk, mv, wq, wo):
    B, T = h.shape[:2]
    q = (h @ wq).reshape(B, T, X_HEADS, X_DH)
    logits = jnp.einsum('bthd,bmhd->bthm', q, mk, preferred_element_type=jnp.float32) * X_DH ** -0.5
    p = jax.nn.softmax(logits, axis=-1)
    o = jnp.einsum('bthm,bmhd->bthd', p.astype(mv.dtype), mv).reshape(B, T, X_HEADS * X_DH)
    return o @ wo


def swiglu(h, w1, w3, w2):
    return (jax.nn.silu(h @ w1) * (h @ w3)) @ w2


def moe_swiglu(x2, w_router, b_router, w1, w3, w2):
    N, D = x2.shape
    logits = jnp.dot(x2, w_router, preferred_element_type=jnp.float32) + b_router.astype(jnp.float32)
    top_l, top_e = lax.top_k(logits, TOP_K)
    gate = jax.nn.softmax(top_l, axis=-1)
    A = N * TOP_K
    e_flat = top_e.reshape(A)
    order = jnp.argsort(e_flat)
    tok_sorted = order // TOP_K
    gate_sorted = gate.reshape(A)[order]
    counts = jnp.bincount(e_flat, length=N_EXPERTS)
    starts = jnp.cumsum(counts) - counts
    blk = max(8, min(MOE_MAX_ROWS, A // N_EXPERTS))
    rows = jnp.pad(x2[tok_sorted], ((0, blk), (0, 0)))
    ys = jnp.zeros((A + blk, D), x2.dtype)
    for e in range(N_EXPERTS):
        def body(j, acc, e=e):
            r0 = starts[e] + j * blk
            yb = swiglu(lax.dynamic_slice_in_dim(rows, r0, blk, 0), w1[e], w3[e], w2[e])
            keep = (j * blk + jnp.arange(blk) < counts[e])[:, None]
            cur = lax.dynamic_slice_in_dim(acc, r0, blk, 0)
            return lax.dynamic_update_slice_in_dim(acc, jnp.where(keep, yb, cur), r0, 0)
        ys = lax.fori_loop(0, (counts[e] + blk - 1) // blk, body, ys)
    contrib = ys[:A] * gate_sorted[:, None].astype(x2.dtype)
    return jnp.zeros_like(x2).at[tok_sorted].add(contrib)


def even_split(z):
    B, T = z.shape[:2]
    qa, ka, va, ga, qb, kb, vb, qi, ki, wi = split_cols(z, EVEN_SPLITS)
    return (qa.reshape(B, T, A_GROUPS, A_HPG, A_DK),
            ka.reshape(B, T, 3, A_GROUPS, A_DK),
            va.reshape(B, T, 3, A_GROUPS, A_DV),
            jax.nn.sigmoid(ga.astype(jnp.float32)).reshape(B, T, 3, A_GROUPS, A_HPG),
            qb.reshape(B, T, B_GROUPS, B_HPG, B_DH),
            kb.reshape(B, T, B_GROUPS, B_DH),
            vb.reshape(B, T, B_GROUPS, B_DH),
            qi.reshape(B, T, IDX_HEADS, IDX_DIM), ki, wi)


def odd_split(z):
    B, T = z.shape[:2]
    q, k, v = split_cols(z, ODD_SPLITS)
    return (q.reshape(B, T, C_HEADS, C_DH), k.reshape(B, T, C_GROUPS, C_DH), v.reshape(B, T, C_GROUPS, C_DH))


def kernel(x_prompt, x_sample, mem_prompt, cache_a_k, cache_a_v, state_a_win_k, state_a_win_v,
           cache_b_k, cache_b_v, cache_b_kidx, cache_c_k, cache_c_v, cache_mem_k, cache_mem_v, page_table,
           rel_bias, norm_mix, norm_mem, norm_cross, norm_ffn, norm_final,
           w_cross_q, w_cross_k, w_cross_v, w_cross_o, w_in_even, w_out_even,
           nsa_pe_k, nsa_pe_v, nsa_phi_k1, nsa_phi_k2, nsa_phi_v1, nsa_phi_v2,
           w_ffn1, w_ffn3, w_ffn2, w_in_odd, w_out_odd, w_router, b_router, w_exp1, w_exp3, w_exp2):
    xp, xs = x_prompt, x_sample
    B, T, D = xp.shape
    Bs, S = xs.shape[:2]
    tab_a = rel_bias[:, :A_HEADS]
    tab_b = rel_bias[:, A_HEADS:A_HEADS + B_HEADS]
    tab_c = rel_bias[:, :C_HEADS]
    names = ('a_k_p', 'a_v_p', 'aw_k_p', 'aw_v_p', 'b_k_p', 'b_v_p', 'b_i_p', 'c_k_p', 'c_v_p', 'm_k_p', 'm_v_p',
             'a_k_s', 'a_v_s', 'aw_k_s', 'aw_v_s', 'b_k_s', 'b_v_s', 'b_i_s', 'c_k_s', 'c_v_s')
    new = {n: [] for n in names}
    for layer in range(DEPTH):
        li = layer // 2
        hp = rms_norm(xp, norm_mix[layer])
        hs = rms_norm(xs, norm_mix[layer])
        if layer % 2 == 0:
            phi = (nsa_pe_k[li], nsa_pe_v[li], nsa_phi_k1[li], nsa_phi_k2[li], nsa_phi_v1[li], nsa_phi_v2[li])
            qa, ka, va, ga, qb, kb, vb, qi, ki, wi = even_split(hp @ w_in_even[li])
            o_a = nsa_prompt(qa, ka, va, ga, *phi, tab_a)
            o_b = dsa_prompt(qb, kb, vb, qi, ki, wi, tab_b)
            xp = xp + jnp.concatenate([o_a, o_b], axis=-1) @ w_out_even[li]
            wk = min(WINDOW, T)
            new['a_k_p'].append(ka[:, :, :2]); new['a_v_p'].append(va[:, :, :2])
            new['aw_k_p'].append(ka[:, T - wk:, 2]); new['aw_v_p'].append(va[:, T - wk:, 2])
            new['b_k_p'].append(kb); new['b_v_p'].append(vb); new['b_i_p'].append(ki)
            qa, ka, va, ga, qb, kb, vb, qi, ki, wi = even_split(hs @ w_in_even[li])
            o_a = nsa_sample(qa, ka, va, ga, cache_a_k[li], cache_a_v[li], state_a_win_k[li], state_a_win_v[li],
                             page_table, *phi, tab_a)
            o_b = dsa_sample(qb, kb, vb, qi, ki, wi, cache_b_k[li], cache_b_v[li], cache_b_kidx[li], page_table, tab_b)
            xs = xs + jnp.concatenate([o_a, o_b], axis=-1) @ w_out_even[li]
            new['a_k_s'].append(ka[:, :, :2]); new['a_v_s'].append(va[:, :, :2])
            new['aw_k_s'].append(ka[:, :, 2]); new['aw_v_s'].append(va[:, :, 2])
            new['b_k_s'].append(kb); new['b_v_s'].append(vb); new['b_i_s'].append(ki)
        else:
            q, k, v = odd_split(hp @ w_in_odd[li])
            xp = xp + moba_prompt(q, k, v, tab_c) @ w_out_odd[li]
            new['c_k_p'].append(k); new['c_v_p'].append(v)
            q, k, v = odd_split(hs @ w_in_odd[li])
            xs = xs + moba_sample(q, k, v, cache_c_k[li], cache_c_v[li], page_table, tab_c) @ w_out_odd[li]
            new['c_k_s'].append(k); new['c_v_s'].append(v)
        mk, mv = mem_kv(mem_prompt, norm_mem[layer], w_cross_k[layer], w_cross_v[layer])
        new['m_k_p'].append(mk); new['m_v_p'].append(mv)
        xp = xp + cross_attend(rms_norm(xp, norm_cross[layer]), mk, mv, w_cross_q[layer], w_cross_o[layer])
        xs = xs + cross_attend(rms_norm(xs, norm_cross[layer]), cache_mem_k[layer], cache_mem_v[layer],
                               w_cross_q[layer], w_cross_o[layer])
        hp = rms_norm(xp, norm_ffn[layer]).reshape(B * T, D)
        hs = rms_norm(xs, norm_ffn[layer]).reshape(Bs * S, D)
        if layer % 2 == 0:
            fp = swiglu(hp, w_ffn1[li], w_ffn3[li], w_ffn2[li])
            fs = swiglu(hs, w_ffn1[li], w_ffn3[li], w_ffn2[li])
        else:
            fp = moe_swiglu(hp, w_router[li], b_router[li], w_exp1[li], w_exp3[li], w_exp2[li])
            fs = moe_swiglu(hs, w_router[li], b_router[li], w_exp1[li], w_exp3[li], w_exp2[li])
        xp = xp + fp.reshape(B, T, D)
        xs = xs + fs.reshape(Bs, S, D)
    y_prompt = rms_norm(xp, norm_final)
    y_sample = rms_norm(xs, norm_final)
    return (y_prompt, y_sample,
            jnp.stack(new['a_k_p']), jnp.stack(new['a_v_p']), jnp.stack(new['aw_k_p']), jnp.stack(new['aw_v_p']),
            jnp.stack(new['b_k_p']), jnp.stack(new['b_v_p']), jnp.stack(new['b_i_p']),
            jnp.stack(new['c_k_p']), jnp.stack(new['c_v_p']), jnp.stack(new['m_k_p']), jnp.stack(new['m_v_p']),
            jnp.stack(new['a_k_s']), jnp.stack(new['a_v_s']), jnp.stack(new['aw_k_s']), jnp.stack(new['aw_v_s']),
            jnp.stack(new['b_k_s']), jnp.stack(new['b_v_s']), jnp.stack(new['b_i_s']),
            jnp.stack(new['c_k_s']), jnp.stack(new['c_v_s']))
```

```python
import functools
import math
import jax, jax.numpy as jnp
from jax import lax
import numpy as np
from jax.experimental import pallas as pl
from jax.experimental.pallas import tpu as pltpu

D_MODEL = 4096
BATCH = 4
SEQ = 2048
DEPTH = 2
DEC_BATCH = 8
DEC_SEQ = 1
PAST_LEN = 16384
PAGE_SIZE = 128

N_EVEN = (DEPTH + 1) // 2
N_ODD = DEPTH // 2
HEAD_SLOTS = 32
A_HEADS = 16
A_GROUPS = 2
A_HPG = A_HEADS // A_GROUPS
A_DK = 192
A_DV = 128
CMP_BLK = 64
N_SEL = 16
WINDOW = 512
B_HEADS = 16
B_GROUPS = 2
B_HPG = B_HEADS // B_GROUPS
B_DH = 128
IDX_HEADS = 8
IDX_DIM = 64
IDX_TOPK = 256
C_HEADS = 32
C_GROUPS = 8
C_HPG = C_HEADS // C_GROUPS
C_DH = 128
MOBA_BLK = 256
MOBA_TOP = 3
MOBA_QCHUNK = 32
N_MEM = 256
X_HEADS = 4
X_DH = 128
D_FF = 14336
N_EXPERTS = 8
TOP_K = 2
MOE_MAX_ROWS = 512
N_BUCKETS = 32
REL_EXACT = 16
REL_MAX_DIST = 1024
Q_BLOCK = 128
EPS = 1e-6
NEG = -1e30
FORCE = 1e9
EVEN_SPLITS = (A_HEADS * A_DK, 3 * A_GROUPS * A_DK, 3 * A_GROUPS * A_DV, 3 * A_HEADS,
               B_HEADS * B_DH, B_GROUPS * B_DH, B_GROUPS * B_DH, IDX_HEADS * IDX_DIM, IDX_DIM, IDX_HEADS)
ODD_SPLITS = (C_HEADS * C_DH, C_GROUPS * C_DH, C_GROUPS * C_DH)


def _rmsnorm_body(x_ref, g_ref, o_ref):
    x = x_ref[...]
    y = x * lax.rsqrt(jnp.mean(x * x, axis=-1, keepdims=True) + EPS)
    o_ref[...] = (y * g_ref[...]).astype(o_ref.dtype)


def rms_norm(x, g, out_dtype=None):
    out_dtype = out_dtype or x.dtype
    shape = x.shape
    d = shape[-1]
    x2 = x.reshape(-1, d)
    rows = x2.shape[0]
    tr = min(rows, 256)
    out = pl.pallas_call(
        _rmsnorm_body,
        grid=(rows // tr,),
        in_specs=[pl.BlockSpec((tr, d), lambda i: (i, 0)),
                  pl.BlockSpec((1, d), lambda i: (0, 0))],
        out_specs=pl.BlockSpec((tr, d), lambda i: (i, 0)),
        out_shape=jax.ShapeDtypeStruct((rows, d), out_dtype),
    )(x2, g.reshape(1, d).astype(jnp.float32))
    return out.reshape(shape)


TQ = 256
TK = 256
N_OFF = -(-(REL_MAX_DIST + TK - 1) // TK) + 1
LANES = 128
VMEM_LIMIT = 48 * 1024 * 1024
_NT = (((1,), (1,)), ((), ()))
BF16 = jnp.bfloat16
F32 = jnp.float32


def bias_by_distance(tab, n_dist):
    return tab[t5_bucket(jnp.arange(n_dist))].T.astype(F32)


def toeplitz_bias_tiles(tab):
    H = tab.shape[1]
    bd = bias_by_distance(tab, N_OFF * TK + TQ)
    epad = jnp.concatenate([jnp.broadcast_to(bd[:, :1], (H, TK - 1)), bd], axis=1)
    w = TQ + TK - 1
    rows = []
    for o in range(N_OFF):
        erev = epad[:, o * TK: o * TK + w][:, ::-1]
        z = jnp.concatenate([erev, erev[:, :1]], axis=1)
        rows.append(jnp.roll(z, -(TQ - 1), axis=1))
    x = jnp.stack(rows, axis=1)
    y = jnp.tile(x, (1, 1, TQ))[:, :, : TQ * w].reshape(H, N_OFF, TQ, w)
    return y[..., :TK]


def _flash_body(mode, hpg, dv, scale, *refs):
    if mode == 'window':
        q_ref, k_ref, v_ref, b_ref, o_ref, m_ref, l_ref, acc_ref = refs
        x_ref = None
    else:
        q_ref, k_ref, v_ref, b_ref, x_ref, o_ref, m_ref, l_ref, acc_ref = refs
    qi = pl.program_id(2)
    ki = pl.program_id(3)

    @pl.when(ki == 0)
    def _():
        m_ref[...] = jnp.full(m_ref.shape, NEG, F32)
        l_ref[...] = jnp.zeros(l_ref.shape, F32)
        acc_ref[...] = jnp.zeros(acc_ref.shape, F32)

    active = ki <= qi
    if mode == 'window':
        active = active & (qi - ki <= WINDOW // TK)

    @pl.when(active)
    def _():
        k = k_ref[0, 0].astype(BF16)
        v = v_ref[0, 0].astype(BF16)
        dist = (qi - ki) * TK + (lax.broadcasted_iota(jnp.int32, (TQ, TK), 0)
                                 - lax.broadcasted_iota(jnp.int32, (TQ, TK), 1))
        if mode == 'key':
            shared_mask = x_ref[0, 0] > 0
        elif mode == 'window':
            shared_mask = (dist >= 0) & (dist < WINDOW)
        else:
            own_mask = jnp.where(dist >= 0, 1.0, 0.0)
            blk = lax.broadcasted_iota(jnp.int32, (TQ, x_ref.shape[-1]), 1)
        for h in range(hpg):
            if mode == 'moba':
                flag = jnp.sum(jnp.where(blk == ki, x_ref[0, h], 0.0), axis=1, keepdims=True)
                mask = jnp.where(ki == qi, own_mask, jnp.broadcast_to(flag, (TQ, TK))) > 0.5
            else:
                mask = shared_mask
            q = q_ref[0, 0, h].astype(BF16)
            s = lax.dot_general(q, k, _NT, preferred_element_type=F32) * scale + b_ref[h, 0]
            s = jnp.where(mask, s, NEG)
            m_prev = m_ref[h][:, :1]
            l_prev = l_ref[h][:, :1]
            m_new = jnp.maximum(m_prev, jnp.max(s, axis=1, keepdims=True))
            p = jnp.where(mask, jnp.exp(s - m_new), 0.0)
            alpha = jnp.exp(m_prev - m_new)
            l_new = alpha * l_prev + jnp.sum(p, axis=1, keepdims=True)
            acc_ref[h] = alpha * acc_ref[h] + jnp.dot(p.astype(BF16), v, preferred_element_type=F32)
            m_ref[h] = jnp.broadcast_to(m_new, (TQ, LANES))
            l_ref[h] = jnp.broadcast_to(l_new, (TQ, LANES))

    @pl.when(ki == qi)
    def _():
        for h in range(hpg):
            l = l_ref[h][:, :1]
            o_ref[0, :, h * dv:(h + 1) * dv] = jnp.where(l > 0.0, acc_ref[h] / jnp.where(l > 0.0, l, 1.0), 0.0)


def flash_attention(mode, q, k, v, bias_tiles, extra, scale):
    B, G, HPG, T, dk = q.shape
    dv = v.shape[-1]
    nq, nk = T // TQ, T // TK

    def kv_idx(b, g, qi, ki):
        lo = jnp.maximum(qi - WINDOW // TK, 0) if mode == 'window' else 0
        return (b, g, jnp.clip(ki, lo, qi), 0)

    in_specs = [
        pl.BlockSpec((1, 1, HPG, TQ, dk), lambda b, g, qi, ki: (b, g, 0, qi, 0)),
        pl.BlockSpec((1, 1, TK, dk), kv_idx),
        pl.BlockSpec((1, 1, TK, dv), kv_idx),
        pl.BlockSpec((HPG, 1, TQ, TK), lambda b, g, qi, ki: (g, jnp.clip(qi - ki, 0, N_OFF - 1), 0, 0)),
    ]
    args = [q, k, v, bias_tiles]
    if mode == 'key':
        gm = extra.shape[1]
        in_specs.append(pl.BlockSpec((1, 1, TQ, TK),
                                     lambda b, g, qi, ki: (b, g if gm > 1 else 0, qi, jnp.minimum(ki, qi))))
        args.append(extra)
    elif mode == 'moba':
        in_specs.append(pl.BlockSpec((1, HPG, TQ, extra.shape[-1]), lambda b, g, qi, ki: (b, g, qi, 0)))
        args.append(extra)
    return pl.pallas_call(
        functools.partial(_flash_body, mode, HPG, dv, scale),
        grid=(B, G, nq, nk),
        in_specs=in_specs,
        out_specs=pl.BlockSpec((1, TQ, HPG * dv), lambda b, g, qi, ki: (b, qi, g)),
        out_shape=jax.ShapeDtypeStruct((B, T, G * HPG * dv), F32),
        scratch_shapes=[pltpu.VMEM((HPG, TQ, LANES), F32), pltpu.VMEM((HPG, TQ, LANES), F32),
                        pltpu.VMEM((HPG, TQ, dv), F32)],
        compiler_params=pltpu.CompilerParams(
            dimension_semantics=("parallel", "parallel", "parallel", "arbitrary"),
            vmem_limit_bytes=VMEM_LIMIT),
        name=f"flash_{mode}",
    )(*args)


def _nsa_cmp_body(q_ref, kc_ref, vc_ref, o_ref, mask_ref):
    qi = pl.program_id(2)
    nc = kc_ref.shape[2]
    T = mask_ref.shape[-1]
    kc = kc_ref[0, 0].astype(BF16)
    vc = vc_ref[0, 0].astype(BF16)
    t = qi * TQ + lax.broadcasted_iota(jnp.int32, (TQ, nc), 0)
    j = lax.broadcasted_iota(jnp.int32, (TQ, nc), 1)
    vis = (j + 1) * CMP_BLK <= t + 1
    imp = jnp.zeros((TQ, nc), F32)
    for h in range(A_HPG):
        q = q_ref[0, 0, h].astype(BF16)
        s = lax.dot_general(q, kc, _NT, preferred_element_type=F32) * (A_DK ** -0.5)
        s = jnp.where(vis, s, NEG)
        e = jnp.where(vis, jnp.exp(s - jnp.max(s, axis=1, keepdims=True)), 0.0)
        l = jnp.sum(e, axis=1, keepdims=True)
        p = jnp.where(l > 0.0, e / jnp.where(l > 0.0, l, 1.0), 0.0)
        o_ref[0, :, h * A_DV:(h + 1) * A_DV] = jnp.dot(p.astype(BF16), vc, preferred_element_type=F32)
        imp = imp + p
    own = t // CMP_BLK
    forced = (j == 0) | (j == own) | (j == own - 1)
    score = jnp.where(forced, FORCE, imp)
    score = jnp.where(j <= own, score, NEG)
    rank = jnp.zeros((TQ, nc), F32)
    for i in range(nc):
        si = score[:, i:i + 1]
        rank = rank + jnp.where((si > score) | ((si == score) & (i < j)), 1.0, 0.0)
    sel = jnp.where((rank < float(N_SEL)) & (j <= own), 1.0, 0.0).astype(BF16)
    expand = jnp.where(lax.broadcasted_iota(jnp.int32, (nc, T), 1) // CMP_BLK
                       == lax.broadcasted_iota(jnp.int32, (nc, T), 0), 1.0, 0.0).astype(BF16)
    keys = jnp.dot(sel, expand, preferred_element_type=F32)
    causal = (qi * TQ + lax.broadcasted_iota(jnp.int32, (TQ, T), 0)) >= lax.broadcasted_iota(jnp.int32, (TQ, T), 1)
    mask_ref[0, 0] = jnp.where((keys > 0.5) & causal, 1.0, 0.0).astype(BF16)


def nsa_cmp_select(q, kc, vc):
    B, G, HPG, T, dk = q.shape
    nc = kc.shape[2]
    return pl.pallas_call(
        _nsa_cmp_body,
        grid=(B, G, T // TQ),
        in_specs=[pl.BlockSpec((1, 1, HPG, TQ, dk), lambda b, g, qi: (b, g, 0, qi, 0)),
                  pl.BlockSpec((1, 1, nc, dk), lambda b, g, qi: (b, g, 0, 0)),
                  pl.BlockSpec((1, 1, nc, A_DV), lambda b, g, qi: (b, g, 0, 0))],
        out_specs=[pl.BlockSpec((1, TQ, HPG * A_DV), lambda b, g, qi: (b, qi, g)),
                   pl.BlockSpec((1, 1, TQ, T), lambda b, g, qi: (b, g, qi, 0))],
        out_shape=[jax.ShapeDtypeStruct((B, T, G * HPG * A_DV), F32),
                   jax.ShapeDtypeStruct((B, G, T, T), BF16)],
        compiler_params=pltpu.CompilerParams(
            dimension_semantics=("parallel", "parallel", "parallel"), vmem_limit_bytes=VMEM_LIMIT),
        name="nsa_cmp_select",
    )(q, kc, vc)


def _count(cond):
    return jnp.sum(jnp.where(cond, 1.0, 0.0), axis=1, keepdims=True)


def _dsa_select_body(topk, qi_ref, ki_ref, w_ref, mask_ref):
    qt = pl.program_id(1)
    T = ki_ref.shape[1]
    kidx = ki_ref[0].astype(BF16)
    w = w_ref[0] * (IDX_HEADS ** -0.5)
    score = jnp.zeros((TQ, T), F32)
    for h in range(IDX_HEADS):
        d = lax.dot_general(qi_ref[0, h].astype(BF16), kidx, _NT, preferred_element_type=F32) * (IDX_DIM ** -0.5)
        score = score + jnp.maximum(d, 0.0) * w[:, h:h + 1]
    t = qt * TQ + lax.broadcasted_iota(jnp.int32, (TQ, T), 0)
    s = lax.broadcasted_iota(jnp.int32, (TQ, T), 1)
    causal = s <= t
    score = jnp.where(causal, score, NEG)
    bits = pltpu.bitcast(score, jnp.int32)
    key = jnp.where(bits < 0, bits ^ jnp.int32(0x7FFFFFFF), bits)
    int_min = jnp.int32(-2 ** 31)

    def value_step(i, lo):
        cand = lo + jnp.left_shift(jnp.int32(1), 31 - i)
        return jnp.where(_count(key >= cand) >= float(topk), cand, lo)
    thr = lax.fori_loop(0, 32, value_step, jnp.full((TQ, 1), int_min, jnp.int32))
    above = key > thr
    tied = key == thr
    need = float(topk) - _count(above)
    n_bits = max(1, (T - 1).bit_length())

    def index_step(i, m):
        cand = m + jnp.left_shift(jnp.int32(1), n_bits - 1 - i)
        return jnp.where(_count(tied & (s < cand)) < need, cand, m)
    last = lax.fori_loop(0, n_bits, index_step, jnp.zeros((TQ, 1), jnp.int32))
    sel = (above | (tied & (s <= last))) & causal
    mask_ref[0, 0] = jnp.where(sel, 1.0, 0.0).astype(BF16)


def dsa_select(qi, ki, wi, topk):
    B, H, T, d = qi.shape
    return pl.pallas_call(
        functools.partial(_dsa_select_body, topk),
        grid=(B, T // TQ),
        in_specs=[pl.BlockSpec((1, H, TQ, d), lambda b, qt: (b, 0, qt, 0)),
                  pl.BlockSpec((1, T, d), lambda b, qt: (b, 0, 0)),
                  pl.BlockSpec((1, TQ, H), lambda b, qt: (b, qt, 0))],
        out_specs=pl.BlockSpec((1, 1, TQ, T), lambda b, qt: (b, 0, qt, 0)),
        out_shape=jax.ShapeDtypeStruct((B, 1, T, T), BF16),
        compiler_params=pltpu.CompilerParams(
            dimension_semantics=("parallel", "parallel"), vmem_limit_bytes=VMEM_LIMIT),
        name="dsa_select",
    )(qi, ki, wi)


def _moba_select_body(q_ref, k_ref, f_ref):
    T = k_ref.shape[2]
    nb = T // MOBA_BLK
    row = lax.broadcasted_iota(jnp.int32, (nb, k_ref.shape[3]), 0)
    kmean = jnp.zeros((nb, k_ref.shape[3]), F32)
    for b in range(nb):
        blk_sum = jnp.sum(k_ref[0, 0, b * MOBA_BLK:(b + 1) * MOBA_BLK, :], axis=0, keepdims=True)
        kmean = jnp.where(row == b, blk_sum * (1.0 / MOBA_BLK), kmean)
    kmean = kmean.astype(BF16)
    j = lax.broadcasted_iota(jnp.int32, (nb, T), 0)
    past = j < lax.broadcasted_iota(jnp.int32, (nb, T), 1) // MOBA_BLK
    for h in range(C_HPG):
        s = lax.dot_general(kmean, q_ref[0, 0, h].astype(BF16), _NT, preferred_element_type=F32)
        s = jnp.where(past, s, NEG)
        rank = jnp.zeros((nb, T), F32)
        for i in range(nb):
            si = s[i:i + 1, :]
            rank = rank + jnp.where((si > s) | ((si == s) & (i < j)), 1.0, 0.0)
        f_ref[0, h] = jnp.where((rank < float(MOBA_TOP)) & past, 1.0, 0.0)


def moba_select(q, k):
    B, G, HPG, T, dh = q.shape
    nb = T // MOBA_BLK
    return pl.pallas_call(
        _moba_select_body,
        grid=(B, G),
        in_specs=[pl.BlockSpec((1, 1, HPG, T, dh), lambda b, g: (b, g, 0, 0, 0)),
                  pl.BlockSpec((1, 1, T, dh), lambda b, g: (b, g, 0, 0))],
        out_specs=pl.BlockSpec((1, HPG, nb, T), lambda b, g: (b, g, 0, 0)),
        out_shape=jax.ShapeDtypeStruct((B, G * HPG, nb, T), F32),
        compiler_params=pltpu.CompilerParams(
            dimension_semantics=("parallel", "parallel"), vmem_limit_bytes=VMEM_LIMIT),
        name="moba_select",
    )(q, k)


def split_cols(z, sizes):
    cuts = [int(c) for c in np.cumsum(sizes)[:-1]]
    return jnp.split(z, cuts, axis=-1)


def masked_softmax(logits, mask):
    p = jax.nn.softmax(jnp.where(mask, logits, NEG), axis=-1)
    return jnp.where(mask, p, 0.0)


def t5_bucket(dist):
    n = jnp.maximum(dist, 0)
    nf = jnp.maximum(n, 1).astype(jnp.float32)
    large = REL_EXACT + (jnp.log(nf / REL_EXACT) / math.log(REL_MAX_DIST / REL_EXACT)
                         * (N_BUCKETS - REL_EXACT)).astype(jnp.int32)
    return jnp.where(n < REL_EXACT, n, jnp.minimum(large, N_BUCKETS - 1))


def gather_paged(pool, page_table, new_rows, pos, *extra):
    past_len = page_table.shape[1] * PAGE_SIZE
    b = jnp.arange(pos.shape[0]).reshape((-1,) + (1,) * (pos.ndim - 1))
    pc = jnp.clip(pos, 0, past_len - 1)
    phys = page_table[b, pc // PAGE_SIZE]
    old = pool[(phys, pc % PAGE_SIZE) + extra]
    new = new_rows[(b, jnp.clip(pos - past_len, 0, new_rows.shape[1] - 1)) + extra]
    is_new = (pos >= past_len).reshape(pos.shape + (1,) * (old.ndim - pos.ndim))
    return jnp.where(is_new, new, old)


def nsa_compress(rows, pe, w1, w2):
    B, L, G, d = rows.shape
    nc = L // CMP_BLK
    blk = rows[:, : nc * CMP_BLK].reshape(B, nc, CMP_BLK, G, d) + pe[None, None, :, None, :]
    flat = blk.transpose(0, 1, 3, 2, 4).reshape(B, nc, G, CMP_BLK * d)
    return jax.nn.silu(flat @ w1) @ w2


def nsa_cmp_branch(q, q_pos, kc, vc):
    nc = kc.shape[1]
    logits = jnp.einsum('bqghd,bjgd->bqghj', q, kc, preferred_element_type=jnp.float32) * A_DK ** -0.5
    vis = (jnp.arange(nc) + 1) * CMP_BLK <= (q_pos + 1)[:, None]
    p = masked_softmax(logits, vis[None, :, None, None, :])
    o = jnp.einsum('bqghj,bjgd->bqghd', p.astype(vc.dtype), vc)
    return o, p.sum(axis=3)


def nsa_pick_blocks(imp, q_pos, n_blocks):
    score = jnp.pad(imp, ((0, 0), (0, 0), (0, 0), (0, n_blocks - imp.shape[-1])))
    j = jnp.arange(n_blocks)[None, :]
    own = (q_pos // CMP_BLK)[:, None]
    forced = (j == 0) | (j == own) | (j == own - 1)
    score = jnp.where(forced[None, :, None, :], FORCE, score)
    score = jnp.where((j <= own)[None, :, None, :], score, NEG)
    top_s, idx = lax.top_k(score, min(N_SEL, n_blocks))
    return idx, top_s > 0.5 * NEG


def nsa_sel_attend(q, q_pos, idx, valid, ksel, vsel, tab_a):
    B, Q, G, HPG, _ = q.shape
    kpos = idx[..., None] * CMP_BLK + jnp.arange(CMP_BLK)
    dist = q_pos[None, :, None, None, None] - kpos
    mask = valid[..., None] & (dist >= 0)
    tab2 = tab_a.reshape(N_BUCKETS, G, HPG).transpose(1, 0, 2)
    bias = tab2[jnp.arange(G).reshape(1, 1, G, 1, 1), t5_bucket(dist)]
    logits = (jnp.einsum('bqghd,bqgnsd->bqghns', q, ksel, preferred_element_type=jnp.float32) * A_DK ** -0.5
              + jnp.moveaxis(bias, -1, 3).astype(jnp.float32))
    shp = logits.shape
    p = masked_softmax(logits.reshape(B, Q, G, HPG, -1), mask.reshape(B, Q, G, 1, -1)).reshape(shp)
    return jnp.einsum('bqghns,bqgnsd->bqghd', p.astype(vsel.dtype), vsel)


def window_attend(q, q_pos, k, v, k_pos, tab_a):
    B, Q, G, HPG, _ = q.shape
    dist = q_pos[:, None] - k_pos[None, :]
    mask = (dist >= 0) & (dist < WINDOW) & (k_pos >= 0)[None, :]
    bias = tab_a[t5_bucket(dist)].reshape(Q, -1, G, HPG).transpose(0, 2, 3, 1).astype(jnp.float32)
    logits = jnp.einsum('bqghd,bsgd->bqghs', q, k, preferred_element_type=jnp.float32) * A_DK ** -0.5 + bias[None]
    p = masked_softmax(logits, mask[None, :, None, None, :])
    return jnp.einsum('bqghs,bsgd->bqghd', p.astype(v.dtype), v)


def nsa_combine(gates, o_c, o_s, o_w):
    g = gates[..., None].astype(o_c.dtype)
    o = g[:, :, 0] * o_c + g[:, :, 1] * o_s + g[:, :, 2] * o_w
    return o.reshape(o.shape[0], o.shape[1], -1)


def _heads_first(a):
    return jnp.moveaxis(a, 1, -2)


def nsa_prompt(q, k3, v3, gates, pe_k, pe_v, wk1, wk2, wv1, wv2, tab_a):
    B, T, G, HPG, DK = q.shape
    assert T % TQ == 0 and TQ == TK and (T // CMP_BLK) * CMP_BLK == T
    kc = nsa_compress(k3[:, :, 0], pe_k, wk1, wk2)
    vc = nsa_compress(v3[:, :, 0], pe_v, wv1, wv2)
    qh = _heads_first(q)
    o_c, sel_mask = nsa_cmp_select(qh, _heads_first(kc), _heads_first(vc))
    tiles = toeplitz_bias_tiles(tab_a)
    scale = A_DK ** -0.5
    o_s = flash_attention('key', qh, _heads_first(k3[:, :, 1]), _heads_first(v3[:, :, 1]), tiles, sel_mask, scale)
    o_w = flash_attention('window', qh, _heads_first(k3[:, :, 2]), _heads_first(v3[:, :, 2]), tiles, None, scale)
    shp = (B, T, G, HPG, A_DV)
    return nsa_combine(gates, o_c.reshape(shp), o_s.reshape(shp), o_w.reshape(shp))


def nsa_sample(q, k3, v3, gates, cache_k, cache_v, win_k, win_v, page_table,
               pe_k, pe_v, wk1, wk2, wv1, wv2, tab_a):
    B, S, G, HPG, DK = q.shape
    P = page_table.shape[1] * PAGE_SIZE
    L = P + S
    pos = P + jnp.arange(S)
    rows_k = jnp.concatenate([cache_k[page_table, :, 0].reshape(B, P, G, DK), k3[:, :, 0]], axis=1)
    rows_v = jnp.concatenate([cache_v[page_table, :, 0].reshape(B, P, G, A_DV), v3[:, :, 0]], axis=1)
    kc = nsa_compress(rows_k, pe_k, wk1, wk2)
    vc = nsa_compress(rows_v, pe_v, wv1, wv2)
    o_c, imp = nsa_cmp_branch(q, pos, kc, vc)
    idx, valid = nsa_pick_blocks(imp, pos, -(-L // CMP_BLK))
    kpos = idx[..., None] * CMP_BLK + jnp.arange(CMP_BLK)
    gi = jnp.arange(G).reshape(1, 1, G, 1, 1)
    ksel = gather_paged(cache_k, page_table, k3, kpos, 1, gi)
    vsel = gather_paged(cache_v, page_table, v3, kpos, 1, gi)
    o_s = nsa_sel_attend(q, pos, idx, valid, ksel, vsel, tab_a)
    wb = win_k.shape[1]
    kw = jnp.concatenate([win_k, k3[:, :, 2]], axis=1)
    vw = jnp.concatenate([win_v, v3[:, :, 2]], axis=1)
    o_w = window_attend(q, pos, kw, vw, P - wb + jnp.arange(wb + S), tab_a)
    return nsa_combine(gates, o_c, o_s, o_w)


def dsa_pick(qi, wi, ki, q_pos, topk):
    L = ki.shape[1]
    dots = jnp.einsum('bqhd,bld->bqhl', qi, ki, preferred_element_type=jnp.float32) * IDX_DIM ** -0.5
    score = jnp.einsum('bqhl,bqh->bql', jax.nn.relu(dots), wi.astype(jnp.float32) * IDX_HEADS ** -0.5)
    score = jnp.where(jnp.arange(L)[None, None, :] <= q_pos[None, :, None], score, NEG)
    _, idx = lax.top_k(score, topk)
    return idx, idx <= q_pos[None, :, None]


def dsa_attend(q, q_pos, idx, valid, ksel, vsel, tab_b):
    B, Q, G, HPG, DH = q.shape
    dist = q_pos[None, :, None] - idx
    bias = tab_b[t5_bucket(dist)].reshape(B, Q, -1, G, HPG).transpose(0, 1, 3, 4, 2).astype(jnp.float32)
    logits = jnp.einsum('bqghd,bqkgd->bqghk', q, ksel, preferred_element_type=jnp.float32) * DH ** -0.5 + bias
    p = masked_softmax(logits, valid[:, :, None, None, :])
    o = jnp.einsum('bqghk,bqkgd->bqghd', p.astype(vsel.dtype), vsel)
    return o.reshape(B, Q, -1)


def dsa_prompt(q, k, v, qi, ki, wi, tab_b):
    B, T = q.shape[:2]
    assert T % TQ == 0 and TQ == TK
    sel_mask = dsa_select(_heads_first(qi), ki, wi, min(IDX_TOPK, T // 4))
    return flash_attention('key', _heads_first(q), _heads_first(k), _heads_first(v),
                           toeplitz_bias_tiles(tab_b), sel_mask, B_DH ** -0.5)


def dsa_sample(q, k, v, qi, ki, wi, cache_k, cache_v, cache_kidx, page_table, tab_b):
    B, S = q.shape[:2]
    P = page_table.shape[1] * PAGE_SIZE
    L = P + S
    q_pos = P + jnp.arange(S)
    ki_all = jnp.concatenate([cache_kidx[page_table].reshape(B, P, IDX_DIM), ki], axis=1)
    idx, valid = dsa_pick(qi, wi, ki_all, q_pos, min(IDX_TOPK, L // 4))
    return dsa_attend(q, q_pos, idx, valid, gather_paged(cache_k, page_table, k, idx),
                      gather_paged(cache_v, page_table, v, idx), tab_b)


def moba_pick(q, q_pos, kmean):
    B, Q, H, DH = q.shape
    nf = kmean.shape[1]
    s = jnp.einsum('bqghd,bjgd->bqghj', q.reshape(B, Q, C_GROUPS, C_HPG, DH).astype(jnp.float32),
                   kmean).reshape(B, Q, H, nf)
    ncand = max(nf, MOBA_TOP)
    s = jnp.pad(s, ((0, 0), (0, 0), (0, 0), (0, ncand - nf)), constant_values=NEG)
    past = jnp.arange(ncand)[None, :] < (q_pos // MOBA_BLK)[:, None]
    s = jnp.where(past[None, :, None, :], s, NEG)
    top_s, idx = lax.top_k(s, MOBA_TOP)
    return idx, top_s > 0.5 * NEG


def moba_attend(q, q_pos, idx, valid, ksel, vsel, own_pos, kown, vown, tab_c):
    B, Q, H, DH = q.shape
    scale = DH ** -0.5
    sel_dist = q_pos[None, :, None, None, None] - (idx[..., None] * MOBA_BLK + jnp.arange(MOBA_BLK))
    hidx = jnp.arange(H).reshape(1, 1, H, 1, 1)
    ls = (jnp.einsum('bqhd,bqhrsd->bqhrs', q, ksel, preferred_element_type=jnp.float32) * scale
          + tab_c.T[hidx, t5_bucket(sel_dist)].astype(jnp.float32))
    ls = jnp.where(valid[..., None], ls, NEG).reshape(B, Q, H, MOBA_TOP * MOBA_BLK)
    own_dist = q_pos[:, None] - own_pos
    lo = jnp.einsum('bqghd,bqgsd->bqghs', q.reshape(B, Q, C_GROUPS, C_HPG, DH), kown,
                    preferred_element_type=jnp.float32).reshape(B, Q, H, MOBA_BLK) * scale
    lo = lo + tab_c[t5_bucket(own_dist)].transpose(0, 2, 1)[None].astype(jnp.float32)
    lo = jnp.where((own_dist >= 0)[None, :, None, :], lo, NEG)
    p = jax.nn.softmax(jnp.concatenate([ls, lo], axis=-1), axis=-1)
    ps = p[..., : MOBA_TOP * MOBA_BLK].reshape(B, Q, H, MOBA_TOP, MOBA_BLK).astype(vsel.dtype)
    po = p[..., MOBA_TOP * MOBA_BLK:].reshape(B, Q, C_GROUPS, C_HPG, MOBA_BLK).astype(vown.dtype)
    o = (jnp.einsum('bqhrs,bqhrsd->bqhd', ps, vsel)
         + jnp.einsum('bqghs,bqgsd->bqghd', po, vown).reshape(B, Q, H, DH))
    return o


def moba_prompt(q, k, v, tab_c):
    B, T, H, DH = q.shape
    assert T % MOBA_BLK == 0 and TQ == MOBA_BLK and TK == MOBA_BLK
    qh = _heads_first(q.reshape(B, T, C_GROUPS, C_HPG, DH))
    kh, vh = _heads_first(k), _heads_first(v)
    flags = moba_select(qh, kh).transpose(0, 1, 3, 2)
    return flash_attention('moba', qh, kh, vh, toeplitz_bias_tiles(tab_c), flags, DH ** -0.5)


def moba_sample(q, k, v, cache_k, cache_v, page_table, tab_c):
    B, S, H, DH = q.shape
    P = page_table.shape[1] * PAGE_SIZE
    L = P + S
    q_pos = P + jnp.arange(S)
    k_all = jnp.concatenate([cache_k[page_table].reshape(B, P, C_GROUPS, DH), k], axis=1)
    nf = L // MOBA_BLK
    kmean = k_all[:, : nf * MOBA_BLK].reshape(B, nf, MOBA_BLK, C_GROUPS, DH).astype(jnp.float32).mean(axis=2)
    idx, valid = moba_pick(q, q_pos, kmean)
    gh = (jnp.arange(H) // C_HPG).reshape(1, 1, H, 1, 1)
    sel_pos = idx[..., None] * MOBA_BLK + jnp.arange(MOBA_BLK)
    own_pos = (q_pos // MOBA_BLK)[:, None] * MOBA_BLK + jnp.arange(MOBA_BLK)
    opos = jnp.broadcast_to(own_pos[None, :, None, :], (B, S, C_GROUPS, MOBA_BLK))
    gi = jnp.arange(C_GROUPS).reshape(1, 1, C_GROUPS, 1)
    o = moba_attend(q, q_pos, idx, valid,
                    gather_paged(cache_k, page_table, k, sel_pos, gh), gather_paged(cache_v, page_table, v, sel_pos, gh),
                    own_pos, gather_paged(cache_k, page_table, k, opos, gi), gather_paged(cache_v, page_table, v, opos, gi),
                    tab_c)
    return o.reshape(B, S, H * DH)


def mem_kv(mem, g, wk, wv):
    m = rms_norm(mem, g)
    B = mem.shape[0]
    return (m @ wk).reshape(B, N_MEM, X_HEADS, X_DH), (m @ wv).reshape(B, N_MEM, X_HEADS, X_DH)


def cross_attend(h, mk, mv, wq, wo):
    B, T = h.shape[:2]
    q = (h @ wq).reshape(B, T, X_HEADS, X_DH)
    logits = jnp.einsum('bthd,bmhd->bthm', q, mk, preferred_element_type=jnp.float32) * X_DH ** -0.5
    p = jax.nn.softmax(logits, axis=-1)
    o = jnp.einsum('bthm,bmhd->bthd', p.astype(mv.dtype), mv).reshape(B, T, X_HEADS * X_DH)
    return o @ wo


def swiglu(h, w1, w3, w2):
    return (jax.nn.silu(h @ w1) * (h @ w3)) @ w2


def moe_swiglu(x2, w_router, b_router, w1, w3, w2):
    N, D = x2.shape
    logits = jnp.dot(x2, w_router, preferred_element_type=jnp.float32) + b_router.astype(jnp.float32)
    top_l, top_e = lax.top_k(logits, TOP_K)
    gate = jax.nn.softmax(top_l, axis=-1)
    A = N * TOP_K
    e_flat = top_e.reshape(A)
    order = jnp.argsort(e_flat)
    tok_sorted = order // TOP_K
    gate_sorted = gate.reshape(A)[order]
    counts = jnp.bincount(e_flat, length=N_EXPERTS)
    starts = jnp.cumsum(counts) - counts
    blk = max(8, min(MOE_MAX_ROWS, A // N_EXPERTS))
    rows = jnp.pad(x2[tok_sorted], ((0, blk), (0, 0)))
    ys = jnp.zeros((A + blk, D), x2.dtype)
    for e in range(N_EXPERTS):
        def body(j, acc, e=e):
            r0 = starts[e] + j * blk
            yb = swiglu(lax.dynamic_slice_in_dim(rows, r0, blk, 0), w1[e], w3[e], w2[e])
            keep = (j * blk + jnp.arange(blk) < counts[e])[:, None]
            cur = lax.dynamic_slice_in_dim(acc, r0, blk, 0)
            return lax.dynamic_update_slice_in_dim(acc, jnp.where(keep, yb, cur), r0, 0)
        ys = lax.fori_loop(0, (counts[e] + blk - 1) // blk, body, ys)
    contrib = ys[:A] * gate_sorted[:, None].astype(x2.dtype)
    return jnp.zeros_like(x2).at[tok_sorted].add(contrib)


def even_split(z):
    B, T = z.shape[:2]
    qa, ka, va, ga, qb, kb, vb, qi, ki, wi = split_cols(z, EVEN_SPLITS)
    return (qa.reshape(B, T, A_GROUPS, A_HPG, A_DK),
            ka.reshape(B, T, 3, A_GROUPS, A_DK),
            va.reshape(B, T, 3, A_GROUPS, A_DV),
            jax.nn.sigmoid(ga.astype(jnp.float32)).reshape(B, T, 3, A_GROUPS, A_HPG),
            qb.reshape(B, T, B_GROUPS, B_HPG, B_DH),
            kb.reshape(B, T, B_GROUPS, B_DH),
            vb.reshape(B, T, B_GROUPS, B_DH),
            qi.reshape(B, T, IDX_HEADS, IDX_DIM), ki, wi)


def odd_split(z):
    B, T = z.shape[:2]
    q, k, v = split_cols(z, ODD_SPLITS)
    return (q.reshape(B, T, C_HEADS, C_DH), k.reshape(B, T, C_GROUPS, C_DH), v.reshape(B, T, C_GROUPS, C_DH))


def kernel(x_prompt, x_sample, mem_prompt, cache_a_k, cache_a_v, state_a_win_k, state_a_win_v,
           cache_b_k, cache_b_v, cache_b_kidx, cache_c_k, cache_c_v, cache_mem_k, cache_mem_v, page_table,
           rel_bias, norm_mix, norm_mem, norm_cross, norm_ffn, norm_final,
           w_cross_q, w_cross_k, w_cross_v, w_cross_o, w_in_even, w_out_even,
           nsa_pe_k, nsa_pe_v, nsa_phi_k1, nsa_phi_k2, nsa_phi_v1, nsa_phi_v2,
           w_ffn1, w_ffn3, w_ffn2, w_in_odd, w_out_odd, w_router, b_router, w_exp1, w_exp3, w_exp2):
    xp, xs = x_prompt, x_sample
    B, T, D = xp.shape
    Bs, S = xs.shape[:2]
    tab_a = rel_bias[:, :A_HEADS]
    tab_b = rel_bias[:, A_HEADS:A_HEADS + B_HEADS]
    tab_c = rel_bias[:, :C_HEADS]
    names = ('a_k_p', 'a_v_p', 'aw_k_p', 'aw_v_p', 'b_k_p', 'b_v_p', 'b_i_p', 'c_k_p', 'c_v_p', 'm_k_p', 'm_v_p',
             'a_k_s', 'a_v_s', 'aw_k_s', 'aw_v_s', 'b_k_s', 'b_v_s', 'b_i_s', 'c_k_s', 'c_v_s')
    new = {n: [] for n in names}
    for layer in range(DEPTH):
        li = layer // 2
        hp = rms_norm(xp, norm_mix[layer])
        hs = rms_norm(xs, norm_mix[layer])
        if layer % 2 == 0:
            phi = (nsa_pe_k[li], nsa_pe_v[li], nsa_phi_k1[li], nsa_phi_k2[li], nsa_phi_v1[li], nsa_phi_v2[li])
            qa, ka, va, ga, qb, kb, vb, qi, ki, wi = even_split(hp @ w_in_even[li])
            o_a = nsa_prompt(qa, ka, va, ga, *phi, tab_a)
            o_b = dsa_prompt(qb, kb, vb, qi, ki, wi, tab_b)
            xp = xp + jnp.concatenate([o_a, o_b], axis=-1) @ w_out_even[li]
            wk = min(WINDOW, T)
            new['a_k_p'].append(ka[:, :, :2]); new['a_v_p'].append(va[:, :, :2])
            new['aw_k_p'].append(ka[:, T - wk:, 2]); new['aw_v_p'].append(va[:, T - wk:, 2])
            new['b_k_p'].append(kb); new['b_v_p'].append(vb); new['b_i_p'].append(ki)
            qa, ka, va, ga, qb, kb, vb, qi, ki, wi = even_split(hs @ w_in_even[li])
            o_a = nsa_sample(qa, ka, va, ga, cache_a_k[li], cache_a_v[li], state_a_win_k[li], state_a_win_v[li],
                             page_table, *phi, tab_a)
            o_b = dsa_sample(qb, kb, vb, qi, ki, wi, cache_b_k[li], cache_b_v[li], cache_b_kidx[li], page_table, tab_b)
            xs = xs + jnp.concatenate([o_a, o_b], axis=-1) @ w_out_even[li]
            new['a_k_s'].append(ka[:, :, :2]); new['a_v_s'].append(va[:, :, :2])
            new['aw_k_s'].append(ka[:, :, 2]); new['aw_v_s'].append(va[:, :, 2])
            new['b_k_s'].append(kb); new['b_v_s'].append(vb); new['b_i_s'].append(ki)
        else:
            q, k, v = odd_split(hp @ w_in_odd[li])
            xp = xp + moba_prompt(q, k, v, tab_c) @ w_out_odd[li]
            new['c_k_p'].append(k); new['c_v_p'].append(v)
            q, k, v = odd_split(hs @ w_in_odd[li])
            xs = xs + moba_sample(q, k, v, cache_c_k[li], cache_c_v[li], page_table, tab_c) @ w_out_odd[li]
            new['c_k_s'].append(k); new['c_v_s'].append(v)
        mk, mv = mem_kv(mem_prompt, norm_mem[layer], w_cross_k[layer], w_cross_v[layer])
        new['m_k_p'].append(mk); new['m_v_p'].append(mv)
        xp = xp + cross_attend(rms_norm(xp, norm_cross[layer]), mk, mv, w_cross_q[layer], w_cross_o[layer])
        xs = xs + cross_attend(rms_norm(xs, norm_cross[layer]), cache_mem_k[layer], cache_mem_v[layer],
                               w_cross_q[layer], w_cross_o[layer])
        hp = rms_norm(xp, norm_ffn[layer]).reshape(B * T, D)
        hs = rms_norm(xs, norm_ffn[layer]).reshape(Bs * S, D)
        if layer % 2 == 0:
            fp = swiglu(hp, w_ffn1[li], w_ffn3[li], w_ffn2[li])
            fs = swiglu(hs, w_ffn1[li], w_ffn3[li], w_ffn2[li])
        else:
            fp = moe_swiglu(hp, w_router[li], b_router[li], w_exp1[li], w_exp3[li], w_exp2[li])
            fs = moe_swiglu(hs, w_router[li], b_router[li], w_exp1[li], w_exp3[li], w_exp2[li])
        xp = xp + fp.reshape(B, T, D)
        xs = xs + fs.reshape(Bs, S, D)
    y_prompt = rms_norm(xp, norm_final)
    y_sample = rms_norm(xs, norm_final)
    return (y_prompt, y_sample,
            jnp.stack(new['a_k_p']), jnp.stack(new['a_v_p']), jnp.stack(new['aw_k_p']), jnp.stack(new['aw_v_p']),
            jnp.stack(new['b_k_p']), jnp.stack(new['b_v_p']), jnp.stack(new['b_i_p']),
            jnp.stack(new['c_k_p']), jnp.stack(new['c_v_p']), jnp.stack(new['m_k_p']), jnp.stack(new['m_v_p']),
            jnp.stack(new['a_k_s']), jnp.stack(new['a_v_s']), jnp.stack(new['aw_k_s']), jnp.stack(new['aw_v_s']),
            jnp.stack(new['b_k_s']), jnp.stack(new['b_v_s']), jnp.stack(new['b_i_s']),
            jnp.stack(new['c_k_s']), jnp.stack(new['c_v_s']))
```

```python
import functools
import math
import jax, jax.numpy as jnp
from jax import lax
import numpy as np
from jax.experimental import pallas as pl
from jax.experimental.pallas import tpu as pltpu

D_MODEL = 4096
BATCH = 4
SEQ = 2048
DEPTH = 2
DEC_BATCH = 8
DEC_SEQ = 1
PAST_LEN = 16384
PAGE_SIZE = 128

N_EVEN = (DEPTH + 1) // 2
N_ODD = DEPTH // 2
HEAD_SLOTS = 32
A_HEADS = 16
A_GROUPS = 2
A_HPG = A_HEADS // A_GROUPS
A_DK = 192
A_DV = 128
CMP_BLK = 64
N_SEL = 16
WINDOW = 512
B_HEADS = 16
B_GROUPS = 2
B_HPG = B_HEADS // B_GROUPS
B_DH = 128
IDX_HEADS = 8
IDX_DIM = 64
IDX_TOPK = 256
C_HEADS = 32
C_GROUPS = 8
C_HPG = C_HEADS // C_GROUPS
C_DH = 128
MOBA_BLK = 256
MOBA_TOP = 3
MOBA_QCHUNK = 32
N_MEM = 256
X_HEADS = 4
X_DH = 128
D_FF = 14336
N_EXPERTS = 8
TOP_K = 2
MOE_MAX_ROWS = 512
N_BUCKETS = 32
REL_EXACT = 16
REL_MAX_DIST = 1024
Q_BLOCK = 128
EPS = 1e-6
NEG = -1e30
FORCE = 1e9
EVEN_SPLITS = (A_HEADS * A_DK, 3 * A_GROUPS * A_DK, 3 * A_GROUPS * A_DV, 3 * A_HEADS,
               B_HEADS * B_DH, B_GROUPS * B_DH, B_GROUPS * B_DH, IDX_HEADS * IDX_DIM, IDX_DIM, IDX_HEADS)
ODD_SPLITS = (C_HEADS * C_DH, C_GROUPS * C_DH, C_GROUPS * C_DH)


def _rmsnorm_body(x_ref, g_ref, o_ref):
    x = x_ref[...]
    y = x * lax.rsqrt(jnp.mean(x * x, axis=-1, keepdims=True) + EPS)
    o_ref[...] = (y * g_ref[...]).astype(o_ref.dtype)


def rms_norm(x, g, out_dtype=None):
    out_dtype = out_dtype or x.dtype
    shape = x.shape
    d = shape[-1]
    x2 = x.reshape(-1, d)
    rows = x2.shape[0]
    tr = min(rows, 256)
    out = pl.pallas_call(
        _rmsnorm_body,
        grid=(rows // tr,),
        in_specs=[pl.BlockSpec((tr, d), lambda i: (i, 0)),
                  pl.BlockSpec((1, d), lambda i: (0, 0))],
        out_specs=pl.BlockSpec((tr, d), lambda i: (i, 0)),
        out_shape=jax.ShapeDtypeStruct((rows, d), out_dtype),
    )(x2, g.reshape(1, d).astype(jnp.float32))
    return out.reshape(shape)


TQ = 256
TK = 256
N_OFF = -(-(REL_MAX_DIST + TK - 1) // TK) + 1
LANES = 128
VMEM_LIMIT = 48 * 1024 * 1024
_NT = (((1,), (1,)), ((), ()))
BF16 = jnp.bfloat16
F32 = jnp.float32


def bias_by_distance(tab, n_dist):
    return tab[t5_bucket(jnp.arange(n_dist))].T.astype(F32)


def toeplitz_bias_tiles(tab):
    H = tab.shape[1]
    bd = bias_by_distance(tab, N_OFF * TK + TQ)
    epad = jnp.concatenate([jnp.broadcast_to(bd[:, :1], (H, TK - 1)), bd], axis=1)
    w = TQ + TK - 1
    rows = []
    for o in range(N_OFF):
        erev = epad[:, o * TK: o * TK + w][:, ::-1]
        z = jnp.concatenate([erev, erev[:, :1]], axis=1)
        rows.append(jnp.roll(z, -(TQ - 1), axis=1))
    x = jnp.stack(rows, axis=1)
    y = jnp.tile(x, (1, 1, TQ))[:, :, : TQ * w].reshape(H, N_OFF, TQ, w)
    return y[..., :TK]


def _flash_body(mode, hpg, dv, scale, *refs):
    if mode == 'window':
        q_ref, k_ref, v_ref, b_ref, o_ref, m_ref, l_ref, acc_ref = refs
        x_ref = None
    else:
        q_ref, k_ref, v_ref, b_ref, x_ref, o_ref, m_ref, l_ref, acc_ref = refs
    qi = pl.program_id(2)
    ki = pl.program_id(3)

    @pl.when(ki == 0)
    def _():
        m_ref[...] = jnp.full(m_ref.shape, NEG, F32)
        l_ref[...] = jnp.zeros(l_ref.shape, F32)
        acc_ref[...] = jnp.zeros(acc_ref.shape, F32)

    active = ki <= qi
    if mode == 'window':
        active = active & (qi - ki <= WINDOW // TK)

    @pl.when(active)
    def _():
        k = k_ref[0, 0].astype(BF16)
        v = v_ref[0, 0].astype(BF16)
        dist = (qi - ki) * TK + (lax.broadcasted_iota(jnp.int32, (TQ, TK), 0)
                                 - lax.broadcasted_iota(jnp.int32, (TQ, TK), 1))
        if mode == 'key':
            shared_mask = x_ref[0, 0] > 0
        elif mode == 'window':
            shared_mask = (dist >= 0) & (dist < WINDOW)
        else:
            own_mask = jnp.where(dist >= 0, 1.0, 0.0)
            blk = lax.broadcasted_iota(jnp.int32, (TQ, x_ref.shape[-1]), 1)
        for h in range(hpg):
            if mode == 'moba':
                flag = jnp.sum(jnp.where(blk == ki, x_ref[0, h], 0.0), axis=1, keepdims=True)
                mask = jnp.where(ki == qi, own_mask, jnp.broadcast_to(flag, (TQ, TK))) > 0.5
            else:
                mask = shared_mask
            q = q_ref[0, 0, h].astype(BF16)
            s = lax.dot_general(q, k, _NT, preferred_element_type=F32) * scale + b_ref[h, 0]
            s = jnp.where(mask, s, NEG)
            m_prev = m_ref[h][:, :1]
            l_prev = l_ref[h][:, :1]
            m_new = jnp.maximum(m_prev, jnp.max(s, axis=1, keepdims=True))
            p = jnp.where(mask, jnp.exp(s - m_new), 0.0)
            alpha = jnp.exp(m_prev - m_new)
            l_new = alpha * l_prev + jnp.sum(p, axis=1, keepdims=True)
            acc_ref[h] = alpha * acc_ref[h] + jnp.dot(p.astype(BF16), v, preferred_element_type=F32)
            m_ref[h] = jnp.broadcast_to(m_new, (TQ, LANES))
            l_ref[h] = jnp.broadcast_to(l_new, (TQ, LANES))

    @pl.when(ki == qi)
    def _():
        for h in range(hpg):
            l = l_ref[h][:, :1]
            o_ref[0, :, h * dv:(h + 1) * dv] = jnp.where(l > 0.0, acc_ref[h] / jnp.where(l > 0.0, l, 1.0), 0.0)


def flash_attention(mode, q, k, v, bias_tiles, extra, scale):
    B, G, HPG, T, dk = q.shape
    dv = v.shape[-1]
    nq, nk = T // TQ, T // TK

    def kv_idx(b, g, qi, ki):
        lo = jnp.maximum(qi - WINDOW // TK, 0) if mode == 'window' else 0
        return (b, g, jnp.clip(ki, lo, qi), 0)

    in_specs = [
        pl.BlockSpec((1, 1, HPG, TQ, dk), lambda b, g, qi, ki: (b, g, 0, qi, 0)),
        pl.BlockSpec((1, 1, TK, dk), kv_idx),
        pl.BlockSpec((1, 1, TK, dv), kv_idx),
        pl.BlockSpec((HPG, 1, TQ, TK), lambda b, g, qi, ki: (g, jnp.clip(qi - ki, 0, N_OFF - 1), 0, 0)),
    ]
    args = [q, k, v, bias_tiles]
    if mode == 'key':
        gm = extra.shape[1]
        in_specs.append(pl.BlockSpec((1, 1, TQ, TK),
                                     lambda b, g, qi, ki: (b, g if gm > 1 else 0, qi, jnp.minimum(ki, qi))))
        args.append(extra)
    elif mode == 'moba':
        in_specs.append(pl.BlockSpec((1, HPG, TQ, extra.shape[-1]), lambda b, g, qi, ki: (b, g, qi, 0)))
        args.append(extra)
    return pl.pallas_call(
        functools.partial(_flash_body, mode, HPG, dv, scale),
        grid=(B, G, nq, nk),
        in_specs=in_specs,
        out_specs=pl.BlockSpec((1, TQ, HPG * dv), lambda b, g, qi, ki: (b, qi, g)),
        out_shape=jax.ShapeDtypeStruct((B, T, G * HPG * dv), F32),
        scratch_shapes=[pltpu.VMEM((HPG, TQ, LANES), F32), pltpu.VMEM((HPG, TQ, LANES), F32),
                        pltpu.VMEM((HPG, TQ, dv), F32)],
        compiler_params=pltpu.CompilerParams(
            dimension_semantics=("parallel", "parallel", "parallel", "arbitrary"),
            vmem_limit_bytes=VMEM_LIMIT),
        name=f"flash_{mode}",
    )(*args)


def _nsa_cmp_body(q_ref, kc_ref, vc_ref, o_ref, mask_ref):
    qi = pl.program_id(2)
    nc = kc_ref.shape[2]
    T = mask_ref.shape[-1]
    kc = kc_ref[0, 0].astype(BF16)
    vc = vc_ref[0, 0].astype(BF16)
    t = qi * TQ + lax.broadcasted_iota(jnp.int32, (TQ, nc), 0)
    j = lax.broadcasted_iota(jnp.int32, (TQ, nc), 1)
    vis = (j + 1) * CMP_BLK <= t + 1
    imp = jnp.zeros((TQ, nc), F32)
    for h in range(A_HPG):
        q = q_ref[0, 0, h].astype(BF16)
        s = lax.dot_general(q, kc, _NT, preferred_element_type=F32) * (A_DK ** -0.5)
        s = jnp.where(vis, s, NEG)
        e = jnp.where(vis, jnp.exp(s - jnp.max(s, axis=1, keepdims=True)), 0.0)
        l = jnp.sum(e, axis=1, keepdims=True)
        p = jnp.where(l > 0.0, e / jnp.where(l > 0.0, l, 1.0), 0.0)
        o_ref[0, :, h * A_DV:(h + 1) * A_DV] = jnp.dot(p.astype(BF16), vc, preferred_element_type=F32)
        imp = imp + p
    own = t // CMP_BLK
    forced = (j == 0) | (j == own) | (j == own - 1)
    score = jnp.where(forced, FORCE, imp)
    score = jnp.where(j <= own, score, NEG)
    rank = jnp.zeros((TQ, nc), F32)
    for i in range(nc):
        si = score[:, i:i + 1]
        rank = rank + jnp.where((si > score) | ((si == score) & (i < j)), 1.0, 0.0)
    sel = jnp.where((rank < float(N_SEL)) & (j <= own), 1.0, 0.0).astype(BF16)
    expand = jnp.where(lax.broadcasted_iota(jnp.int32, (nc, T), 1) // CMP_BLK
                       == lax.broadcasted_iota(jnp.int32, (nc, T), 0), 1.0, 0.0).astype(BF16)
    keys = jnp.dot(sel, expand, preferred_element_type=F32)
    causal = (qi * TQ + lax.broadcasted_iota(jnp.int32, (TQ, T), 0)) >= lax.broadcasted_iota(jnp.int32, (TQ, T), 1)
    mask_ref[0, 0] = jnp.where((keys > 0.5) & causal, 1.0, 0.0).astype(BF16)


def nsa_cmp_select(q, kc, vc):
    B, G, HPG, T, dk = q.shape
    nc = kc.shape[2]
    return pl.pallas_call(
        _nsa_cmp_body,
        grid=(B, G, T // TQ),
        in_specs=[pl.BlockSpec((1, 1, HPG, TQ, dk), lambda b, g, qi: (b, g, 0, qi, 0)),
                  pl.BlockSpec((1, 1, nc, dk), lambda b, g, qi: (b, g, 0, 0)),
                  pl.BlockSpec((1, 1, nc, A_DV), lambda b, g, qi: (b, g, 0, 0))],
        out_specs=[pl.BlockSpec((1, TQ, HPG * A_DV), lambda b, g, qi: (b, qi, g)),
                   pl.BlockSpec((1, 1, TQ, T), lambda b, g, qi: (b, g, qi, 0))],
        out_shape=[jax.ShapeDtypeStruct((B, T, G * HPG * A_DV), F32),
                   jax.ShapeDtypeStruct((B, G, T, T), BF16)],
        compiler_params=pltpu.CompilerParams(
            dimension_semantics=("parallel", "parallel", "parallel"), vmem_limit_bytes=VMEM_LIMIT),
        name="nsa_cmp_select",
    )(q, kc, vc)


def _count(cond):
    return jnp.sum(jnp.where(cond, 1.0, 0.0), axis=1, keepdims=True)


def _dsa_select_body(topk, qi_ref, ki_ref, w_ref, mask_ref):
    qt = pl.program_id(1)
    T = ki_ref.shape[1]
    kidx = ki_ref[0].astype(BF16)
    w = w_ref[0] * (IDX_HEADS ** -0.5)
    score = jnp.zeros((TQ, T), F32)
    for h in range(IDX_HEADS):
        d = lax.dot_general(qi_ref[0, h].astype(BF16), kidx, _NT, preferred_element_type=F32) * (IDX_DIM ** -0.5)
        score = score + jnp.maximum(d, 0.0) * w[:, h:h + 1]
    t = qt * TQ + lax.broadcasted_iota(jnp.int32, (TQ, T), 0)
    s = lax.broadcasted_iota(jnp.int32, (TQ, T), 1)
    causal = s <= t
    score = jnp.where(causal, score, NEG)
    bits = pltpu.bitcast(score, jnp.int32)
    key = jnp.where(bits < 0, bits ^ jnp.int32(0x7FFFFFFF), bits)
    int_min = jnp.int32(-2 ** 31)

    def value_step(i, lo):
        cand = lo + jnp.left_shift(jnp.int32(1), 31 - i)
        return jnp.where(_count(key >= cand) >= float(topk), cand, lo)
    thr = lax.fori_loop(0, 32, value_step, jnp.full((TQ, 1), int_min, jnp.int32))
    above = key > thr
    tied = key == thr
    need = float(topk) - _count(above)
    n_bits = max(1, (T - 1).bit_length())

    def index_step(i, m):
        cand = m + jnp.left_shift(jnp.int32(1), n_bits - 1 - i)
        return jnp.where(_count(tied & (s < cand)) < need, cand, m)
    last = lax.fori_loop(0, n_bits, index_step, jnp.zeros((TQ, 1), jnp.int32))
    sel = (above | (tied & (s <= last))) & causal
    mask_ref[0, 0] = jnp.where(sel, 1.0, 0.0).astype(BF16)


def dsa_select(qi, ki, wi, topk):
    B, H, T, d = qi.shape
    return pl.pallas_call(
        functools.partial(_dsa_select_body, topk),
        grid=(B, T // TQ),
        in_specs=[pl.BlockSpec((1, H, TQ, d), lambda b, qt: (b, 0, qt, 0)),
                  pl.BlockSpec((1, T, d), lambda b, qt: (b, 0, 0)),
                  pl.BlockSpec((1, TQ, H), lambda b, qt: (b, qt, 0))],
        out_specs=pl.BlockSpec((1, 1, TQ, T), lambda b, qt: (b, 0, qt, 0)),
        out_shape=jax.ShapeDtypeStruct((B, 1, T, T), BF16),
        compiler_params=pltpu.CompilerParams(
            dimension_semantics=("parallel", "parallel"), vmem_limit_bytes=VMEM_LIMIT),
        name="dsa_select",
    )(qi, ki, wi)


def _moba_select_body(q_ref, k_ref, f_ref):
    T = k_ref.shape[2]
    nb = T // MOBA_BLK
    row = lax.broadcasted_iota(jnp.int32, (nb, k_ref.shape[3]), 0)
    kmean = jnp.zeros((nb, k_ref.shape[3]), F32)
    for b in range(nb):
        blk_sum = jnp.sum(k_ref[0, 0, b * MOBA_BLK:(b + 1) * MOBA_BLK, :], axis=0, keepdims=True)
        kmean = jnp.where(row == b, blk_sum * (1.0 / MOBA_BLK), kmean)
    kmean = kmean.astype(BF16)
    j = lax.broadcasted_iota(jnp.int32, (nb, T), 0)
    past = j < lax.broadcasted_iota(jnp.int32, (nb, T), 1) // MOBA_BLK
    for h in range(C_HPG):
        s = lax.dot_general(kmean, q_ref[0, 0, h].astype(BF16), _NT, preferred_element_type=F32)
        s = jnp.where(past, s, NEG)
        rank = jnp.zeros((nb, T), F32)
        for i in range(nb):
            si = s[i:i + 1, :]
            rank = rank + jnp.where((si > s) | ((si == s) & (i < j)), 1.0, 0.0)
        f_ref[0, h] = jnp.where((rank < float(MOBA_TOP)) & past, 1.0, 0.0)


def moba_select(q, k):
    B, G, HPG, T, dh = q.shape
    nb = T // MOBA_BLK
    return pl.pallas_call(
        _moba_select_body,
        grid=(B, G),
        in_specs=[pl.BlockSpec((1, 1, HPG, T, dh), lambda b, g: (b, g, 0, 0, 0)),
                  pl.BlockSpec((1, 1, T, dh), lambda b, g: (b, g, 0, 0))],
        out_specs=pl.BlockSpec((1, HPG, nb, T), lambda b, g: (b, g, 0, 0)),
        out_shape=jax.ShapeDtypeStruct((B, G * HPG, nb, T), F32),
        compiler_params=pltpu.CompilerParams(
            dimension_semantics=("parallel", "parallel"), vmem_limit_bytes=VMEM_LIMIT),
        name="moba_select",
    )(q, k)


MM_VMEM_LIMIT = 56 * 1024 * 1024
SUB_ROWS = 256


def _gmm_body(n_w, nk, nsb_max, sb, has_resid, te_ref, ns_ref, src_ref, *refs):
    x_ref = refs[0]
    w_refs = refs[1:1 + n_w]
    pos = 1 + n_w
    r_ref = refs[pos] if has_resid else None
    pos += int(has_resid)
    o_ref = refs[pos]
    acc_refs = refs[pos + 1:pos + 1 + n_w]
    i = pl.program_id(0)
    k = pl.program_id(2)
    n_sb = ns_ref[i]

    @pl.when(k == 0)
    def _():
        for a_ref in acc_refs:
            a_ref[...] = jnp.zeros(a_ref.shape, F32)

    for c in range(1, nsb_max + 1):
        @pl.when(n_sb == c)
        def _(c=c):
            xs = x_ref[:c * sb, :].astype(BF16)
            for w_ref, a_ref in zip(w_refs, acc_refs):
                a_ref[:c * sb, :] += jnp.dot(xs, w_ref[0].astype(BF16), preferred_element_type=F32)

    @pl.when(k == nk - 1)
    def _():
        if n_w == 2:
            val = jax.nn.silu(acc_refs[0][...]) * acc_refs[1][...]
        else:
            val = acc_refs[0][...]
        if has_resid:
            val = val + r_ref[...]
        o_ref[...] = val.astype(o_ref.dtype)


def grouped_matmul(x, ws, tables, tm, tn, tk, out_dtype, resid=None):
    P, K = x.shape
    E, _, N = ws[0].shape
    tk = min(tk, K)
    tn = min(tn, N)
    assert P % tm == 0 and K % tk == 0
    sb = min(SUB_ROWS, tm)
    assert tm % sb == 0
    n_w = len(ws)
    ni, nj, nk = P // tm, pl.cdiv(N, tn), K // tk

    def x_idx(i, j, k, te, ns, src):
        return (src[i], jnp.where(ns[i] > 0, k, nk - 1))

    def w_idx(i, j, k, te, ns, src):
        act = ns[i] > 0
        return (te[i], jnp.where(act, k, nk - 1), jnp.where(act, j, nj - 1))

    def o_idx(i, j, k, te, ns, src):
        return (i, j)

    in_specs = [pl.BlockSpec((tm, tk), x_idx)] + [pl.BlockSpec((1, tk, tn), w_idx)] * n_w
    args = [x] + list(ws)
    if resid is not None:
        in_specs.append(pl.BlockSpec((tm, tn), o_idx))
        args.append(resid)
    return pl.pallas_call(
        functools.partial(_gmm_body, n_w, nk, tm // sb, sb, resid is not None),
        grid_spec=pltpu.PrefetchScalarGridSpec(
            num_scalar_prefetch=3, grid=(ni, nj, nk), in_specs=in_specs,
            out_specs=pl.BlockSpec((tm, tn), o_idx),
            scratch_shapes=[pltpu.VMEM((tm, tn), F32)] * n_w),
        out_shape=jax.ShapeDtypeStruct((P, N), out_dtype),
        compiler_params=pltpu.CompilerParams(
            dimension_semantics=("parallel", "parallel", "arbitrary"), vmem_limit_bytes=MM_VMEM_LIMIT),
        name=f"gmm{n_w}_{tm}x{tn}x{tk}",
    )(*tables, *args)


def _dense_tables(rows, tm):
    n = rows // tm
    return (jnp.zeros((n,), jnp.int32), jnp.full((n,), tm // min(SUB_ROWS, tm), jnp.int32),
            jnp.arange(n, dtype=jnp.int32))


def _row_tile(rows):
    for tm in (2048, 1024, 512, 256):
        if rows % tm == 0:
            return tm
    return rows


def matmul(x, w, out_dtype=F32, resid=None, tn=1024, tk=512):
    lead = x.shape[:-1]
    x2 = x.reshape(-1, x.shape[-1])
    rows = x2.shape[0]
    tm = _row_tile(rows)
    r2 = None if resid is None else resid.reshape(rows, -1)
    out = grouped_matmul(x2, [w[None]], _dense_tables(rows, tm), tm, tn, tk, out_dtype, r2)
    return out.reshape(*lead, w.shape[-1])


def swiglu_up(x, w1, w3, tables, tm, tf=512, tk=1024):
    return grouped_matmul(x, [w1, w3], tables, tm, tf, tk, BF16)


def split_cols(z, sizes):
    cuts = [int(c) for c in np.cumsum(sizes)[:-1]]
    return jnp.split(z, cuts, axis=-1)


def masked_softmax(logits, mask):
    p = jax.nn.softmax(jnp.where(mask, logits, NEG), axis=-1)
    return jnp.where(mask, p, 0.0)


def t5_bucket(dist):
    n = jnp.maximum(dist, 0)
    nf = jnp.maximum(n, 1).astype(jnp.float32)
    large = REL_EXACT + (jnp.log(nf / REL_EXACT) / math.log(REL_MAX_DIST / REL_EXACT)
                         * (N_BUCKETS - REL_EXACT)).astype(jnp.int32)
    return jnp.where(n < REL_EXACT, n, jnp.minimum(large, N_BUCKETS - 1))


def gather_paged(pool, page_table, new_rows, pos, *extra):
    past_len = page_table.shape[1] * PAGE_SIZE
    b = jnp.arange(pos.shape[0]).reshape((-1,) + (1,) * (pos.ndim - 1))
    pc = jnp.clip(pos, 0, past_len - 1)
    phys = page_table[b, pc // PAGE_SIZE]
    old = pool[(phys, pc % PAGE_SIZE) + extra]
    new = new_rows[(b, jnp.clip(pos - past_len, 0, new_rows.shape[1] - 1)) + extra]
    is_new = (pos >= past_len).reshape(pos.shape + (1,) * (old.ndim - pos.ndim))
    return jnp.where(is_new, new, old)


def nsa_compress(rows, pe, w1, w2):
    B, L, G, d = rows.shape
    nc = L // CMP_BLK
    blk = rows[:, : nc * CMP_BLK].reshape(B, nc, CMP_BLK, G, d) + pe[None, None, :, None, :]
    flat = blk.transpose(0, 1, 3, 2, 4).reshape(B, nc, G, CMP_BLK * d)
    return matmul(jax.nn.silu(matmul(flat, w1)), w2)


def nsa_cmp_branch(q, q_pos, kc, vc):
    nc = kc.shape[1]
    logits = jnp.einsum('bqghd,bjgd->bqghj', q, kc, preferred_element_type=jnp.float32) * A_DK ** -0.5
    vis = (jnp.arange(nc) + 1) * CMP_BLK <= (q_pos + 1)[:, None]
    p = masked_softmax(logits, vis[None, :, None, None, :])
    o = jnp.einsum('bqghj,bjgd->bqghd', p.astype(vc.dtype), vc)
    return o, p.sum(axis=3)


def nsa_pick_blocks(imp, q_pos, n_blocks):
    score = jnp.pad(imp, ((0, 0), (0, 0), (0, 0), (0, n_blocks - imp.shape[-1])))
    j = jnp.arange(n_blocks)[None, :]
    own = (q_pos // CMP_BLK)[:, None]
    forced = (j == 0) | (j == own) | (j == own - 1)
    score = jnp.where(forced[None, :, None, :], FORCE, score)
    score = jnp.where((j <= own)[None, :, None, :], score, NEG)
    top_s, idx = lax.top_k(score, min(N_SEL, n_blocks))
    return idx, top_s > 0.5 * NEG


def nsa_sel_attend(q, q_pos, idx, valid, ksel, vsel, tab_a):
    B, Q, G, HPG, _ = q.shape
    kpos = idx[..., None] * CMP_BLK + jnp.arange(CMP_BLK)
    dist = q_pos[None, :, None, None, None] - kpos
    mask = valid[..., None] & (dist >= 0)
    tab2 = tab_a.reshape(N_BUCKETS, G, HPG).transpose(1, 0, 2)
    bias = tab2[jnp.arange(G).reshape(1, 1, G, 1, 1), t5_bucket(dist)]
    logits = (jnp.einsum('bqghd,bqgnsd->bqghns', q, ksel, preferred_element_type=jnp.float32) * A_DK ** -0.5
              + jnp.moveaxis(bias, -1, 3).astype(jnp.float32))
    shp = logits.shape
    p = masked_softmax(logits.reshape(B, Q, G, HPG, -1), mask.reshape(B, Q, G, 1, -1)).reshape(shp)
    return jnp.einsum('bqghns,bqgnsd->bqghd', p.astype(vsel.dtype), vsel)


def window_attend(q, q_pos, k, v, k_pos, tab_a):
    B, Q, G, HPG, _ = q.shape
    dist = q_pos[:, None] - k_pos[None, :]
    mask = (dist >= 0) & (dist < WINDOW) & (k_pos >= 0)[None, :]
    bias = tab_a[t5_bucket(dist)].reshape(Q, -1, G, HPG).transpose(0, 2, 3, 1).astype(jnp.float32)
    logits = jnp.einsum('bqghd,bsgd->bqghs', q, k, preferred_element_type=jnp.float32) * A_DK ** -0.5 + bias[None]
    p = masked_softmax(logits, mask[None, :, None, None, :])
    return jnp.einsum('bqghs,bsgd->bqghd', p.astype(v.dtype), v)


def nsa_combine(gates, o_c, o_s, o_w):
    g = gates[..., None].astype(o_c.dtype)
    o = g[:, :, 0] * o_c + g[:, :, 1] * o_s + g[:, :, 2] * o_w
    return o.reshape(o.shape[0], o.shape[1], -1)


def _heads_first(a):
    return jnp.moveaxis(a, 1, -2)


def nsa_prompt(q, k3, v3, gates, pe_k, pe_v, wk1, wk2, wv1, wv2, tab_a):
    B, T, G, HPG, DK = q.shape
    assert T % TQ == 0 and TQ == TK and (T // CMP_BLK) * CMP_BLK == T
    kc = nsa_compress(k3[:, :, 0], pe_k, wk1, wk2)
    vc = nsa_compress(v3[:, :, 0], pe_v, wv1, wv2)
    qh = _heads_first(q)
    o_c, sel_mask = nsa_cmp_select(qh, _heads_first(kc), _heads_first(vc))
    tiles = toeplitz_bias_tiles(tab_a)
    scale = A_DK ** -0.5
    o_s = flash_attention('key', qh, _heads_first(k3[:, :, 1]), _heads_first(v3[:, :, 1]), tiles, sel_mask, scale)
    o_w = flash_attention('window', qh, _heads_first(k3[:, :, 2]), _heads_first(v3[:, :, 2]), tiles, None, scale)
    shp = (B, T, G, HPG, A_DV)
    return nsa_combine(gates, o_c.reshape(shp), o_s.reshape(shp), o_w.reshape(shp))


def nsa_sample(q, k3, v3, gates, cache_k, cache_v, win_k, win_v, page_table,
               pe_k, pe_v, wk1, wk2, wv1, wv2, tab_a):
    B, S, G, HPG, DK = q.shape
    P = page_table.shape[1] * PAGE_SIZE
    L = P + S
    pos = P + jnp.arange(S)
    rows_k = jnp.concatenate([cache_k[page_table, :, 0].reshape(B, P, G, DK), k3[:, :, 0]], axis=1)
    rows_v = jnp.concatenate([cache_v[page_table, :, 0].reshape(B, P, G, A_DV), v3[:, :, 0]], axis=1)
    kc = nsa_compress(rows_k, pe_k, wk1, wk2)
    vc = nsa_compress(rows_v, pe_v, wv1, wv2)
    o_c, imp = nsa_cmp_branch(q, pos, kc, vc)
    idx, valid = nsa_pick_blocks(imp, pos, -(-L // CMP_BLK))
    kpos = idx[..., None] * CMP_BLK + jnp.arange(CMP_BLK)
    gi = jnp.arange(G).reshape(1, 1, G, 1, 1)
    ksel = gather_paged(cache_k, page_table, k3, kpos, 1, gi)
    vsel = gather_paged(cache_v, page_table, v3, kpos, 1, gi)
    o_s = nsa_sel_attend(q, pos, idx, valid, ksel, vsel, tab_a)
    wb = win_k.shape[1]
    kw = jnp.concatenate([win_k, k3[:, :, 2]], axis=1)
    vw = jnp.concatenate([win_v, v3[:, :, 2]], axis=1)
    o_w = window_attend(q, pos, kw, vw, P - wb + jnp.arange(wb + S), tab_a)
    return nsa_combine(gates, o_c, o_s, o_w)


def dsa_pick(qi, wi, ki, q_pos, topk):
    L = ki.shape[1]
    dots = jnp.einsum('bqhd,bld->bqhl', qi, ki, preferred_element_type=jnp.float32) * IDX_DIM ** -0.5
    score = jnp.einsum('bqhl,bqh->bql', jax.nn.relu(dots), wi.astype(jnp.float32) * IDX_HEADS ** -0.5)
    score = jnp.where(jnp.arange(L)[None, None, :] <= q_pos[None, :, None], score, NEG)
    _, idx = lax.top_k(score, topk)
    return idx, idx <= q_pos[None, :, None]


def dsa_attend(q, q_pos, idx, valid, ksel, vsel, tab_b):
    B, Q, G, HPG, DH = q.shape
    dist = q_pos[None, :, None] - idx
    bias = tab_b[t5_bucket(dist)].reshape(B, Q, -1, G, HPG).transpose(0, 1, 3, 4, 2).astype(jnp.float32)
    logits = jnp.einsum('bqghd,bqkgd->bqghk', q, ksel, preferred_element_type=jnp.float32) * DH ** -0.5 + bias
    p = masked_softmax(logits, valid[:, :, None, None, :])
    o = jnp.einsum('bqghk,bqkgd->bqghd', p.astype(vsel.dtype), vsel)
    return o.reshape(B, Q, -1)


def dsa_prompt(q, k, v, qi, ki, wi, tab_b):
    B, T = q.shape[:2]
    assert T % TQ == 0 and TQ == TK
    sel_mask = dsa_select(_heads_first(qi), ki, wi, min(IDX_TOPK, T // 4))
    return flash_attention('key', _heads_first(q), _heads_first(k), _heads_first(v),
                           toeplitz_bias_tiles(tab_b), sel_mask, B_DH ** -0.5)


def dsa_sample(q, k, v, qi, ki, wi, cache_k, cache_v, cache_kidx, page_table, tab_b):
    B, S = q.shape[:2]
    P = page_table.shape[1] * PAGE_SIZE
    L = P + S
    q_pos = P + jnp.arange(S)
    ki_all = jnp.concatenate([cache_kidx[page_table].reshape(B, P, IDX_DIM), ki], axis=1)
    idx, valid = dsa_pick(qi, wi, ki_all, q_pos, min(IDX_TOPK, L // 4))
    return dsa_attend(q, q_pos, idx, valid, gather_paged(cache_k, page_table, k, idx),
                      gather_paged(cache_v, page_table, v, idx), tab_b)


def moba_pick(q, q_pos, kmean):
    B, Q, H, DH = q.shape
    nf = kmean.shape[1]
    s = jnp.einsum('bqghd,bjgd->bqghj', q.reshape(B, Q, C_GROUPS, C_HPG, DH).astype(jnp.float32),
                   kmean).reshape(B, Q, H, nf)
    ncand = max(nf, MOBA_TOP)
    s = jnp.pad(s, ((0, 0), (0, 0), (0, 0), (0, ncand - nf)), constant_values=NEG)
    past = jnp.arange(ncand)[None, :] < (q_pos // MOBA_BLK)[:, None]
    s = jnp.where(past[None, :, None, :], s, NEG)
    top_s, idx = lax.top_k(s, MOBA_TOP)
    return idx, top_s > 0.5 * NEG


def moba_attend(q, q_pos, idx, valid, ksel, vsel, own_pos, kown, vown, tab_c):
    B, Q, H, DH = q.shape
    scale = DH ** -0.5
    sel_dist = q_pos[None, :, None, None, None] - (idx[..., None] * MOBA_BLK + jnp.arange(MOBA_BLK))
    hidx = jnp.arange(H).reshape(1, 1, H, 1, 1)
    ls = (jnp.einsum('bqhd,bqhrsd->bqhrs', q, ksel, preferred_element_type=jnp.float32) * scale
          + tab_c.T[hidx, t5_bucket(sel_dist)].astype(jnp.float32))
    ls = jnp.where(valid[..., None], ls, NEG).reshape(B, Q, H, MOBA_TOP * MOBA_BLK)
    own_dist = q_pos[:, None] - own_pos
    lo = jnp.einsum('bqghd,bqgsd->bqghs', q.reshape(B, Q, C_GROUPS, C_HPG, DH), kown,
                    preferred_element_type=jnp.float32).reshape(B, Q, H, MOBA_BLK) * scale
    lo = lo + tab_c[t5_bucket(own_dist)].transpose(0, 2, 1)[None].astype(jnp.float32)
    lo = jnp.where((own_dist >= 0)[None, :, None, :], lo, NEG)
    p = jax.nn.softmax(jnp.concatenate([ls, lo], axis=-1), axis=-1)
    ps = p[..., : MOBA_TOP * MOBA_BLK].reshape(B, Q, H, MOBA_TOP, MOBA_BLK).astype(vsel.dtype)
    po = p[..., MOBA_TOP * MOBA_BLK:].reshape(B, Q, C_GROUPS, C_HPG, MOBA_BLK).astype(vown.dtype)
    o = (jnp.einsum('bqhrs,bqhrsd->bqhd', ps, vsel)
         + jnp.einsum('bqghs,bqgsd->bqghd', po, vown).reshape(B, Q, H, DH))
    return o


def moba_prompt(q, k, v, tab_c):
    B, T, H, DH = q.shape
    assert T % MOBA_BLK == 0 and TQ == MOBA_BLK and TK == MOBA_BLK
    qh = _heads_first(q.reshape(B, T, C_GROUPS, C_HPG, DH))
    kh, vh = _heads_first(k), _heads_first(v)
    flags = moba_select(qh, kh).transpose(0, 1, 3, 2)
    return flash_attention('moba', qh, kh, vh, toeplitz_bias_tiles(tab_c), flags, DH ** -0.5)


def moba_sample(q, k, v, cache_k, cache_v, page_table, tab_c):
    B, S, H, DH = q.shape
    P = page_table.shape[1] * PAGE_SIZE
    L = P + S
    q_pos = P + jnp.arange(S)
    k_all = jnp.concatenate([cache_k[page_table].reshape(B, P, C_GROUPS, DH), k], axis=1)
    nf = L // MOBA_BLK
    kmean = k_all[:, : nf * MOBA_BLK].reshape(B, nf, MOBA_BLK, C_GROUPS, DH).astype(jnp.float32).mean(axis=2)
    idx, valid = moba_pick(q, q_pos, kmean)
    gh = (jnp.arange(H) // C_HPG).reshape(1, 1, H, 1, 1)
    sel_pos = idx[..., None] * MOBA_BLK + jnp.arange(MOBA_BLK)
    own_pos = (q_pos // MOBA_BLK)[:, None] * MOBA_BLK + jnp.arange(MOBA_BLK)
    opos = jnp.broadcast_to(own_pos[None, :, None, :], (B, S, C_GROUPS, MOBA_BLK))
    gi = jnp.arange(C_GROUPS).reshape(1, 1, C_GROUPS, 1)
    o = moba_attend(q, q_pos, idx, valid,
                    gather_paged(cache_k, page_table, k, sel_pos, gh), gather_paged(cache_v, page_table, v, sel_pos, gh),
                    own_pos, gather_paged(cache_k, page_table, k, opos, gi), gather_paged(cache_v, page_table, v, opos, gi),
                    tab_c)
    return o.reshape(B, S, H * DH)


def mem_kv(mem, g, wk, wv):
    m = rms_norm(mem, g, BF16)
    B = mem.shape[0]
    return matmul(m, wk).reshape(B, N_MEM, X_HEADS, X_DH), matmul(m, wv).reshape(B, N_MEM, X_HEADS, X_DH)


def cross_attend(x, h, mk, mv, wq, wo):
    B, T = h.shape[:2]
    q = matmul(h, wq).reshape(B, T, X_HEADS, X_DH)
    logits = jnp.einsum('bthd,bmhd->bthm', q, mk, preferred_element_type=jnp.float32) * X_DH ** -0.5
    p = jax.nn.softmax(logits, axis=-1)
    o = jnp.einsum('bthm,bmhd->bthd', p.astype(mv.dtype), mv).reshape(B, T, X_HEADS * X_DH)
    return matmul(o, wo, resid=x)


def dense_swiglu(x, h, w1, w3, w2):
    rows = h.shape[0]
    tm = _row_tile(rows)
    tables = _dense_tables(rows, tm)
    g = swiglu_up(h, w1, w3, tables, tm)
    return grouped_matmul(g, [w2], tables, tm, 1024, 512, F32, resid=x)


MOE_TM = 2304


def moe_tables(top_e):
    A = top_e.size
    n_tiles = -(-(A + N_EXPERTS * (MOE_TM - 1)) // MOE_TM)
    e_flat = top_e.reshape(A).astype(jnp.int32)
    order = jnp.argsort(e_flat).astype(jnp.int32)
    counts = jnp.bincount(e_flat, length=N_EXPERTS).astype(jnp.int32)
    starts = jnp.cumsum(counts) - counts
    tiles_per = (counts + MOE_TM - 1) // MOE_TM
    tile_end = jnp.cumsum(tiles_per)
    tile_start = tile_end - tiles_per
    e_sorted = e_flat[order]
    prow_sorted = tile_start[e_sorted] * MOE_TM + (jnp.arange(A, dtype=jnp.int32) - starts[e_sorted])
    row_token = jnp.zeros((n_tiles * MOE_TM,), jnp.int32).at[prow_sorted].set(order // TOP_K)
    prow_of_assign = jnp.zeros((A,), jnp.int32).at[order].set(prow_sorted)
    ti = jnp.arange(n_tiles, dtype=jnp.int32)
    n_active = tile_end[-1]
    last = n_active - 1
    src = jnp.minimum(ti, last)
    te = jnp.minimum(jnp.searchsorted(tile_end, src, side='right').astype(jnp.int32), N_EXPERTS - 1)
    rows_in = jnp.clip(counts[te] - (src - tile_start[te]) * MOE_TM, 0, MOE_TM)
    nsb = jnp.where(ti < n_active, (rows_in + SUB_ROWS - 1) // SUB_ROWS, 0).astype(jnp.int32)
    return (te, nsb, src), row_token, prow_of_assign


def moe_swiglu(xs, hs, w_router, b_router, w1, w3, w2):
    wr = jnp.pad(w_router, ((0, 0), (0, LANES - N_EXPERTS)))
    logits = jnp.concatenate([matmul(h, wr)[:, :N_EXPERTS] for h in hs], axis=0) + b_router.astype(jnp.float32)
    x = jnp.concatenate(xs, axis=0)
    h = jnp.concatenate(hs, axis=0)
    N = h.shape[0]
    top_l, top_e = lax.top_k(logits, TOP_K)
    gate = jax.nn.softmax(top_l, axis=-1)
    tables, row_token, prow_of_assign = moe_tables(top_e)
    g = swiglu_up(h[row_token], w1, w3, tables, MOE_TM)
    y = grouped_matmul(g, [w2], tables, MOE_TM, 1024, 512, F32)
    pair = y[prow_of_assign.reshape(N, TOP_K)] * gate[:, :, None]
    out = x + jnp.sum(pair, axis=1)
    cuts = [int(c) for c in np.cumsum([a.shape[0] for a in xs])[:-1]]
    return jnp.split(out, cuts, axis=0)


def even_split(z):
    B, T = z.shape[:2]
    qa, ka, va, ga, qb, kb, vb, qi, ki, wi = split_cols(z, EVEN_SPLITS)
    return (qa.reshape(B, T, A_GROUPS, A_HPG, A_DK),
            ka.reshape(B, T, 3, A_GROUPS, A_DK),
            va.reshape(B, T, 3, A_GROUPS, A_DV),
            jax.nn.sigmoid(ga.astype(jnp.float32)).reshape(B, T, 3, A_GROUPS, A_HPG),
            qb.reshape(B, T, B_GROUPS, B_HPG, B_DH),
            kb.reshape(B, T, B_GROUPS, B_DH),
            vb.reshape(B, T, B_GROUPS, B_DH),
            qi.reshape(B, T, IDX_HEADS, IDX_DIM), ki, wi)


def odd_split(z):
    B, T = z.shape[:2]
    q, k, v = split_cols(z, ODD_SPLITS)
    return (q.reshape(B, T, C_HEADS, C_DH), k.reshape(B, T, C_GROUPS, C_DH), v.reshape(B, T, C_GROUPS, C_DH))


def kernel(x_prompt, x_sample, mem_prompt, cache_a_k, cache_a_v, state_a_win_k, state_a_win_v,
           cache_b_k, cache_b_v, cache_b_kidx, cache_c_k, cache_c_v, cache_mem_k, cache_mem_v, page_table,
           rel_bias, norm_mix, norm_mem, norm_cross, norm_ffn, norm_final,
           w_cross_q, w_cross_k, w_cross_v, w_cross_o, w_in_even, w_out_even,
           nsa_pe_k, nsa_pe_v, nsa_phi_k1, nsa_phi_k2, nsa_phi_v1, nsa_phi_v2,
           w_ffn1, w_ffn3, w_ffn2, w_in_odd, w_out_odd, w_router, b_router, w_exp1, w_exp3, w_exp2):
    xp, xs = x_prompt, x_sample
    B, T, D = xp.shape
    Bs, S = xs.shape[:2]
    tab_a = rel_bias[:, :A_HEADS]
    tab_b = rel_bias[:, A_HEADS:A_HEADS + B_HEADS]
    tab_c = rel_bias[:, :C_HEADS]
    names = ('a_k_p', 'a_v_p', 'aw_k_p', 'aw_v_p', 'b_k_p', 'b_v_p', 'b_i_p', 'c_k_p', 'c_v_p', 'm_k_p', 'm_v_p',
             'a_k_s', 'a_v_s', 'aw_k_s', 'aw_v_s', 'b_k_s', 'b_v_s', 'b_i_s', 'c_k_s', 'c_v_s')
    new = {n: [] for n in names}
    for layer in range(DEPTH):
        li = layer // 2
        hp = rms_norm(xp, norm_mix[layer], BF16)
        hs = rms_norm(xs, norm_mix[layer], BF16)
        if layer % 2 == 0:
            phi = (nsa_pe_k[li], nsa_pe_v[li], nsa_phi_k1[li], nsa_phi_k2[li], nsa_phi_v1[li], nsa_phi_v2[li])
            qa, ka, va, ga, qb, kb, vb, qi, ki, wi = even_split(matmul(hp, w_in_even[li]))
            o_a = nsa_prompt(qa, ka, va, ga, *phi, tab_a)
            o_b = dsa_prompt(qb, kb, vb, qi, ki, wi, tab_b)
            xp = matmul(jnp.concatenate([o_a, o_b], axis=-1), w_out_even[li], resid=xp)
            wk = min(WINDOW, T)
            new['a_k_p'].append(ka[:, :, :2]); new['a_v_p'].append(va[:, :, :2])
            new['aw_k_p'].append(ka[:, T - wk:, 2]); new['aw_v_p'].append(va[:, T - wk:, 2])
            new['b_k_p'].append(kb); new['b_v_p'].append(vb); new['b_i_p'].append(ki)
            qa, ka, va, ga, qb, kb, vb, qi, ki, wi = even_split(matmul(hs, w_in_even[li]))
            o_a = nsa_sample(qa, ka, va, ga, cache_a_k[li], cache_a_v[li], state_a_win_k[li], state_a_win_v[li],
                             page_table, *phi, tab_a)
            o_b = dsa_sample(qb, kb, vb, qi, ki, wi, cache_b_k[li], cache_b_v[li], cache_b_kidx[li], page_table, tab_b)
            xs = matmul(jnp.concatenate([o_a, o_b], axis=-1), w_out_even[li], resid=xs)
            new['a_k_s'].append(ka[:, :, :2]); new['a_v_s'].append(va[:, :, :2])
            new['aw_k_s'].append(ka[:, :, 2]); new['aw_v_s'].append(va[:, :, 2])
            new['b_k_s'].append(kb); new['b_v_s'].append(vb); new['b_i_s'].append(ki)
        else:
            q, k, v = odd_split(matmul(hp, w_in_odd[li]))
            xp = matmul(moba_prompt(q, k, v, tab_c), w_out_odd[li], resid=xp)
            new['c_k_p'].append(k); new['c_v_p'].append(v)
            q, k, v = odd_split(matmul(hs, w_in_odd[li]))
            xs = matmul(moba_sample(q, k, v, cache_c_k[li], cache_c_v[li], page_table, tab_c), w_out_odd[li], resid=xs)
            new['c_k_s'].append(k); new['c_v_s'].append(v)
        mk, mv = mem_kv(mem_prompt, norm_mem[layer], w_cross_k[layer], w_cross_v[layer])
        new['m_k_p'].append(mk); new['m_v_p'].append(mv)
        xp = cross_attend(xp, rms_norm(xp, norm_cross[layer], BF16), mk, mv, w_cross_q[layer], w_cross_o[layer])
        xs = cross_attend(xs, rms_norm(xs, norm_cross[layer], BF16), cache_mem_k[layer], cache_mem_v[layer],
                          w_cross_q[layer], w_cross_o[layer])
        hp = rms_norm(xp, norm_ffn[layer], BF16).reshape(B * T, D)
        hs = rms_norm(xs, norm_ffn[layer], BF16).reshape(Bs * S, D)
        xp2, xs2 = xp.reshape(B * T, D), xs.reshape(Bs * S, D)
        if layer % 2 == 0:
            xp2 = dense_swiglu(xp2, hp, w_ffn1[li:li + 1], w_ffn3[li:li + 1], w_ffn2[li:li + 1])
            xs2 = dense_swiglu(xs2, hs, w_ffn1[li:li + 1], w_ffn3[li:li + 1], w_ffn2[li:li + 1])
        else:
            xp2, xs2 = moe_swiglu([xp2, xs2], [hp, hs], w_router[li], b_router[li], w_exp1[li], w_exp3[li], w_exp2[li])
        xp = xp2.reshape(B, T, D)
        xs = xs2.reshape(Bs, S, D)
    y_prompt = rms_norm(xp, norm_final)
    y_sample = rms_norm(xs, norm_final)
    return (y_prompt, y_sample,
            jnp.stack(new['a_k_p']), jnp.stack(new['a_v_p']), jnp.stack(new['aw_k_p']), jnp.stack(new['aw_v_p']),
            jnp.stack(new['b_k_p']), jnp.stack(new['b_v_p']), jnp.stack(new['b_i_p']),
            jnp.stack(new['c_k_p']), jnp.stack(new['c_v_p']), jnp.stack(new['m_k_p']), jnp.stack(new['m_v_p']),
            jnp.stack(new['a_k_s']), jnp.stack(new['a_v_s']), jnp.stack(new['aw_k_s']), jnp.stack(new['aw_v_s']),
            jnp.stack(new['b_k_s']), jnp.stack(new['b_v_s']), jnp.stack(new['b_i_s']),
            jnp.stack(new['c_k_s']), jnp.stack(new['c_v_s']))
```

```python
import functools
import math
import jax, jax.numpy as jnp
from jax import lax
import numpy as np
from jax.experimental import pallas as pl
from jax.experimental.pallas import tpu as pltpu

D_MODEL = 4096
BATCH = 4
SEQ = 2048
DEPTH = 2
DEC_BATCH = 8
DEC_SEQ = 1
PAST_LEN = 16384
PAGE_SIZE = 128

N_EVEN = (DEPTH + 1) // 2
N_ODD = DEPTH // 2
HEAD_SLOTS = 32
A_HEADS = 16
A_GROUPS = 2
A_HPG = A_HEADS // A_GROUPS
A_DK = 192
A_DV = 128
CMP_BLK = 64
N_SEL = 16
WINDOW = 512
B_HEADS = 16
B_GROUPS = 2
B_HPG = B_HEADS // B_GROUPS
B_DH = 128
IDX_HEADS = 8
IDX_DIM = 64
IDX_TOPK = 256
C_HEADS = 32
C_GROUPS = 8
C_HPG = C_HEADS // C_GROUPS
C_DH = 128
MOBA_BLK = 256
MOBA_TOP = 3
MOBA_QCHUNK = 32
N_MEM = 256
X_HEADS = 4
X_DH = 128
D_FF = 14336
N_EXPERTS = 8
TOP_K = 2
MOE_MAX_ROWS = 512
N_BUCKETS = 32
REL_EXACT = 16
REL_MAX_DIST = 1024
Q_BLOCK = 128
EPS = 1e-6
NEG = -1e30
FORCE = 1e9
EVEN_SPLITS = (A_HEADS * A_DK, 3 * A_GROUPS * A_DK, 3 * A_GROUPS * A_DV, 3 * A_HEADS,
               B_HEADS * B_DH, B_GROUPS * B_DH, B_GROUPS * B_DH, IDX_HEADS * IDX_DIM, IDX_DIM, IDX_HEADS)
ODD_SPLITS = (C_HEADS * C_DH, C_GROUPS * C_DH, C_GROUPS * C_DH)


def _rmsnorm_body(x_ref, g_ref, o_ref):
    x = x_ref[...]
    y = x * lax.rsqrt(jnp.mean(x * x, axis=-1, keepdims=True) + EPS)
    o_ref[...] = (y * g_ref[...]).astype(o_ref.dtype)


def rms_norm(x, g, out_dtype=None):
    out_dtype = out_dtype or x.dtype
    shape = x.shape
    d = shape[-1]
    x2 = x.reshape(-1, d)
    rows = x2.shape[0]
    tr = min(rows, 256)
    out = pl.pallas_call(
        _rmsnorm_body,
        grid=(rows // tr,),
        in_specs=[pl.BlockSpec((tr, d), lambda i: (i, 0)),
                  pl.BlockSpec((1, d), lambda i: (0, 0))],
        out_specs=pl.BlockSpec((tr, d), lambda i: (i, 0)),
        out_shape=jax.ShapeDtypeStruct((rows, d), out_dtype),
    )(x2, g.reshape(1, d).astype(jnp.float32))
    return out.reshape(shape)


TQ = 256
TK = 256
N_OFF = -(-(REL_MAX_DIST + TK - 1) // TK) + 1
LANES = 128
VMEM_LIMIT = 48 * 1024 * 1024
_NT = (((1,), (1,)), ((), ()))
BF16 = jnp.bfloat16
F32 = jnp.float32


def bias_by_distance(tab, n_dist):
    return tab[t5_bucket(jnp.arange(n_dist))].T.astype(F32)


def toeplitz_bias_tiles(tab):
    H = tab.shape[1]
    bd = bias_by_distance(tab, N_OFF * TK + TQ)
    epad = jnp.concatenate([jnp.broadcast_to(bd[:, :1], (H, TK - 1)), bd], axis=1)
    w = TQ + TK - 1
    rows = []
    for o in range(N_OFF):
        erev = epad[:, o * TK: o * TK + w][:, ::-1]
        z = jnp.concatenate([erev, erev[:, :1]], axis=1)
        rows.append(jnp.roll(z, -(TQ - 1), axis=1))
    x = jnp.stack(rows, axis=1)
    y = jnp.tile(x, (1, 1, TQ))[:, :, : TQ * w].reshape(H, N_OFF, TQ, w)
    return y[..., :TK]


def _flash_body(mode, hpg, dv, scale, *refs):
    if mode == 'window':
        q_ref, k_ref, v_ref, b_ref, o_ref, m_ref, l_ref, acc_ref = refs
        x_ref = None
    else:
        q_ref, k_ref, v_ref, b_ref, x_ref, o_ref, m_ref, l_ref, acc_ref = refs
    qi = pl.program_id(2)
    ki = pl.program_id(3)

    @pl.when(ki == 0)
    def _():
        m_ref[...] = jnp.full(m_ref.shape, NEG, F32)
        l_ref[...] = jnp.zeros(l_ref.shape, F32)
        acc_ref[...] = jnp.zeros(acc_ref.shape, F32)

    active = ki <= qi
    if mode == 'window':
        active = active & (qi - ki <= WINDOW // TK)

    @pl.when(active)
    def _():
        k = k_ref[0, 0].astype(BF16)
        v = v_ref[0, 0].astype(BF16)
        dist = (qi - ki) * TK + (lax.broadcasted_iota(jnp.int32, (TQ, TK), 0)
                                 - lax.broadcasted_iota(jnp.int32, (TQ, TK), 1))
        if mode == 'key':
            shared_mask = x_ref[0, 0] > 0
        elif mode == 'window':
            shared_mask = (dist >= 0) & (dist < WINDOW)
        else:
            own_mask = jnp.where(dist >= 0, 1.0, 0.0)
            blk = lax.broadcasted_iota(jnp.int32, (TQ, x_ref.shape[-1]), 1)
        for h in range(hpg):
            if mode == 'moba':
                flag = jnp.sum(jnp.where(blk == ki, x_ref[0, h], 0.0), axis=1, keepdims=True)
                mask = jnp.where(ki == qi, own_mask, jnp.broadcast_to(flag, (TQ, TK))) > 0.5
            else:
                mask = shared_mask
            q = q_ref[0, 0, h].astype(BF16)
            s = lax.dot_general(q, k, _NT, preferred_element_type=F32) * scale + b_ref[h, 0]
            s = jnp.where(mask, s, NEG)
            m_prev = m_ref[h][:, :1]
            l_prev = l_ref[h][:, :1]
            m_new = jnp.maximum(m_prev, jnp.max(s, axis=1, keepdims=True))
            p = jnp.where(mask, jnp.exp(s - m_new), 0.0)
            alpha = jnp.exp(m_prev - m_new)
            l_new = alpha * l_prev + jnp.sum(p, axis=1, keepdims=True)
            acc_ref[h] = alpha * acc_ref[h] + jnp.dot(p.astype(BF16), v, preferred_element_type=F32)
            m_ref[h] = jnp.broadcast_to(m_new, (TQ, LANES))
            l_ref[h] = jnp.broadcast_to(l_new, (TQ, LANES))

    @pl.when(ki == qi)
    def _():
        for h in range(hpg):
            l = l_ref[h][:, :1]
            o_ref[0, :, h * dv:(h + 1) * dv] = jnp.where(l > 0.0, acc_ref[h] / jnp.where(l > 0.0, l, 1.0), 0.0)


def flash_attention(mode, q, k, v, bias_tiles, extra, scale):
    B, G, HPG, T, dk = q.shape
    dv = v.shape[-1]
    nq, nk = T // TQ, T // TK

    def kv_idx(b, g, qi, ki):
        lo = jnp.maximum(qi - WINDOW // TK, 0) if mode == 'window' else 0
        return (b, g, jnp.clip(ki, lo, qi), 0)

    in_specs = [
        pl.BlockSpec((1, 1, HPG, TQ, dk), lambda b, g, qi, ki: (b, g, 0, qi, 0)),
        pl.BlockSpec((1, 1, TK, dk), kv_idx),
        pl.BlockSpec((1, 1, TK, dv), kv_idx),
        pl.BlockSpec((HPG, 1, TQ, TK), lambda b, g, qi, ki: (g, jnp.clip(qi - ki, 0, N_OFF - 1), 0, 0)),
    ]
    args = [q, k, v, bias_tiles]
    if mode == 'key':
        gm = extra.shape[1]
        in_specs.append(pl.BlockSpec((1, 1, TQ, TK),
                                     lambda b, g, qi, ki: (b, g if gm > 1 else 0, qi, jnp.minimum(ki, qi))))
        args.append(extra)
    elif mode == 'moba':
        in_specs.append(pl.BlockSpec((1, HPG, TQ, extra.shape[-1]), lambda b, g, qi, ki: (b, g, qi, 0)))
        args.append(extra)
    return pl.pallas_call(
        functools.partial(_flash_body, mode, HPG, dv, scale),
        grid=(B, G, nq, nk),
        in_specs=in_specs,
        out_specs=pl.BlockSpec((1, TQ, HPG * dv), lambda b, g, qi, ki: (b, qi, g)),
        out_shape=jax.ShapeDtypeStruct((B, T, G * HPG * dv), F32),
        scratch_shapes=[pltpu.VMEM((HPG, TQ, LANES), F32), pltpu.VMEM((HPG, TQ, LANES), F32),
                        pltpu.VMEM((HPG, TQ, dv), F32)],
        compiler_params=pltpu.CompilerParams(
            dimension_semantics=("parallel", "parallel", "parallel", "arbitrary"),
            vmem_limit_bytes=VMEM_LIMIT),
        name=f"flash_{mode}",
    )(*args)


def _nsa_cmp_body(q_ref, kc_ref, vc_ref, o_ref, mask_ref):
    qi = pl.program_id(2)
    nc = kc_ref.shape[2]
    T = mask_ref.shape[-1]
    kc = kc_ref[0, 0].astype(BF16)
    vc = vc_ref[0, 0].astype(BF16)
    t = qi * TQ + lax.broadcasted_iota(jnp.int32, (TQ, nc), 0)
    j = lax.broadcasted_iota(jnp.int32, (TQ, nc), 1)
    vis = (j + 1) * CMP_BLK <= t + 1
    imp = jnp.zeros((TQ, nc), F32)
    for h in range(A_HPG):
        q = q_ref[0, 0, h].astype(BF16)
        s = lax.dot_general(q, kc, _NT, preferred_element_type=F32) * (A_DK ** -0.5)
        s = jnp.where(vis, s, NEG)
        e = jnp.where(vis, jnp.exp(s - jnp.max(s, axis=1, keepdims=True)), 0.0)
        l = jnp.sum(e, axis=1, keepdims=True)
        p = jnp.where(l > 0.0, e / jnp.where(l > 0.0, l, 1.0), 0.0)
        o_ref[0, :, h * A_DV:(h + 1) * A_DV] = jnp.dot(p.astype(BF16), vc, preferred_element_type=F32)
        imp = imp + p
    own = t // CMP_BLK
    forced = (j == 0) | (j == own) | (j == own - 1)
    score = jnp.where(forced, FORCE, imp)
    score = jnp.where(j <= own, score, NEG)
    rank = jnp.zeros((TQ, nc), F32)
    for i in range(nc):
        si = score[:, i:i + 1]
        rank = rank + jnp.where((si > score) | ((si == score) & (i < j)), 1.0, 0.0)
    sel = jnp.where((rank < float(N_SEL)) & (j <= own), 1.0, 0.0).astype(BF16)
    expand = jnp.where(lax.broadcasted_iota(jnp.int32, (nc, T), 1) // CMP_BLK
                       == lax.broadcasted_iota(jnp.int32, (nc, T), 0), 1.0, 0.0).astype(BF16)
    keys = jnp.dot(sel, expand, preferred_element_type=F32)
    causal = (qi * TQ + lax.broadcasted_iota(jnp.int32, (TQ, T), 0)) >= lax.broadcasted_iota(jnp.int32, (TQ, T), 1)
    mask_ref[0, 0] = jnp.where((keys > 0.5) & causal, 1.0, 0.0).astype(BF16)


def nsa_cmp_select(q, kc, vc):
    B, G, HPG, T, dk = q.shape
    nc = kc.shape[2]
    return pl.pallas_call(
        _nsa_cmp_body,
        grid=(B, G, T // TQ),
        in_specs=[pl.BlockSpec((1, 1, HPG, TQ, dk), lambda b, g, qi: (b, g, 0, qi, 0)),
                  pl.BlockSpec((1, 1, nc, dk), lambda b, g, qi: (b, g, 0, 0)),
                  pl.BlockSpec((1, 1, nc, A_DV), lambda b, g, qi: (b, g, 0, 0))],
        out_specs=[pl.BlockSpec((1, TQ, HPG * A_DV), lambda b, g, qi: (b, qi, g)),
                   pl.BlockSpec((1, 1, TQ, T), lambda b, g, qi: (b, g, qi, 0))],
        out_shape=[jax.ShapeDtypeStruct((B, T, G * HPG * A_DV), F32),
                   jax.ShapeDtypeStruct((B, G, T, T), BF16)],
        compiler_params=pltpu.CompilerParams(
            dimension_semantics=("parallel", "parallel", "parallel"), vmem_limit_bytes=VMEM_LIMIT),
        name="nsa_cmp_select",
    )(q, kc, vc)


def _count(cond):
    return jnp.sum(jnp.where(cond, 1.0, 0.0), axis=1, keepdims=True)


def _dsa_select_body(topk, qi_ref, ki_ref, w_ref, mask_ref):
    qt = pl.program_id(1)
    T = ki_ref.shape[1]
    kidx = ki_ref[0].astype(BF16)
    w = w_ref[0] * (IDX_HEADS ** -0.5)
    score = jnp.zeros((TQ, T), F32)
    for h in range(IDX_HEADS):
        d = lax.dot_general(qi_ref[0, h].astype(BF16), kidx, _NT, preferred_element_type=F32) * (IDX_DIM ** -0.5)
        score = score + jnp.maximum(d, 0.0) * w[:, h:h + 1]
    t = qt * TQ + lax.broadcasted_iota(jnp.int32, (TQ, T), 0)
    s = lax.broadcasted_iota(jnp.int32, (TQ, T), 1)
    causal = s <= t
    score = jnp.where(causal, score, NEG)
    bits = pltpu.bitcast(score, jnp.int32)
    key = jnp.where(bits < 0, bits ^ jnp.int32(0x7FFFFFFF), bits)
    int_min = jnp.int32(-2 ** 31)

    def value_step(i, lo):
        cand = lo + jnp.left_shift(jnp.int32(1), 31 - i)
        return jnp.where(_count(key >= cand) >= float(topk), cand, lo)
    thr = lax.fori_loop(0, 32, value_step, jnp.full((TQ, 1), int_min, jnp.int32))
    above = key > thr
    tied = key == thr
    need = float(topk) - _count(above)
    n_bits = max(1, (T - 1).bit_length())

    def index_step(i, m):
        cand = m + jnp.left_shift(jnp.int32(1), n_bits - 1 - i)
        return jnp.where(_count(tied & (s < cand)) < need, cand, m)
    last = lax.fori_loop(0, n_bits, index_step, jnp.zeros((TQ, 1), jnp.int32))
    sel = (above | (tied & (s <= last))) & causal
    mask_ref[0, 0] = jnp.where(sel, 1.0, 0.0).astype(BF16)


def dsa_select(qi, ki, wi, topk):
    B, H, T, d = qi.shape
    return pl.pallas_call(
        functools.partial(_dsa_select_body, topk),
        grid=(B, T // TQ),
        in_specs=[pl.BlockSpec((1, H, TQ, d), lambda b, qt: (b, 0, qt, 0)),
                  pl.BlockSpec((1, T, d), lambda b, qt: (b, 0, 0)),
                  pl.BlockSpec((1, TQ, H), lambda b, qt: (b, qt, 0))],
        out_specs=pl.BlockSpec((1, 1, TQ, T), lambda b, qt: (b, 0, qt, 0)),
        out_shape=jax.ShapeDtypeStruct((B, 1, T, T), BF16),
        compiler_params=pltpu.CompilerParams(
            dimension_semantics=("parallel", "parallel"), vmem_limit_bytes=VMEM_LIMIT),
        name="dsa_select",
    )(qi, ki, wi)


def _moba_select_body(q_ref, k_ref, f_ref):
    T = k_ref.shape[2]
    nb = T // MOBA_BLK
    row = lax.broadcasted_iota(jnp.int32, (nb, k_ref.shape[3]), 0)
    kmean = jnp.zeros((nb, k_ref.shape[3]), F32)
    for b in range(nb):
        blk_sum = jnp.sum(k_ref[0, 0, b * MOBA_BLK:(b + 1) * MOBA_BLK, :], axis=0, keepdims=True)
        kmean = jnp.where(row == b, blk_sum * (1.0 / MOBA_BLK), kmean)
    kmean = kmean.astype(BF16)
    j = lax.broadcasted_iota(jnp.int32, (nb, T), 0)
    past = j < lax.broadcasted_iota(jnp.int32, (nb, T), 1) // MOBA_BLK
    for h in range(C_HPG):
        s = lax.dot_general(kmean, q_ref[0, 0, h].astype(BF16), _NT, preferred_element_type=F32)
        s = jnp.where(past, s, NEG)
        rank = jnp.zeros((nb, T), F32)
        for i in range(nb):
            si = s[i:i + 1, :]
            rank = rank + jnp.where((si > s) | ((si == s) & (i < j)), 1.0, 0.0)
        f_ref[0, h] = jnp.where((rank < float(MOBA_TOP)) & past, 1.0, 0.0)


def moba_select(q, k):
    B, G, HPG, T, dh = q.shape
    nb = T // MOBA_BLK
    return pl.pallas_call(
        _moba_select_body,
        grid=(B, G),
        in_specs=[pl.BlockSpec((1, 1, HPG, T, dh), lambda b, g: (b, g, 0, 0, 0)),
                  pl.BlockSpec((1, 1, T, dh), lambda b, g: (b, g, 0, 0))],
        out_specs=pl.BlockSpec((1, HPG, nb, T), lambda b, g: (b, g, 0, 0)),
        out_shape=jax.ShapeDtypeStruct((B, G * HPG, nb, T), F32),
        compiler_params=pltpu.CompilerParams(
            dimension_semantics=("parallel", "parallel"), vmem_limit_bytes=VMEM_LIMIT),
        name="moba_select",
    )(q, k)


MM_VMEM_LIMIT = 56 * 1024 * 1024
SUB_ROWS = 256


def _gmm_body(n_w, nk, nsb_max, sb, has_resid, te_ref, ns_ref, src_ref, *refs):
    x_ref = refs[0]
    w_refs = refs[1:1 + n_w]
    pos = 1 + n_w
    r_ref = refs[pos] if has_resid else None
    pos += int(has_resid)
    o_ref = refs[pos]
    acc_refs = refs[pos + 1:pos + 1 + n_w]
    i = pl.program_id(0)
    k = pl.program_id(2)
    n_sb = ns_ref[i]

    @pl.when(k == 0)
    def _():
        for a_ref in acc_refs:
            a_ref[...] = jnp.zeros(a_ref.shape, F32)

    for c in range(1, nsb_max + 1):
        @pl.when(n_sb == c)
        def _(c=c):
            xs = x_ref[:c * sb, :].astype(BF16)
            for w_ref, a_ref in zip(w_refs, acc_refs):
                a_ref[:c * sb, :] += jnp.dot(xs, w_ref[0].astype(BF16), preferred_element_type=F32)

    @pl.when(k == nk - 1)
    def _():
        if n_w == 2:
            val = jax.nn.silu(acc_refs[0][...]) * acc_refs[1][...]
        else:
            val = acc_refs[0][...]
        if has_resid:
            val = val + r_ref[...]
        o_ref[...] = val.astype(o_ref.dtype)


def grouped_matmul(x, ws, tables, tm, tn, tk, out_dtype, resid=None):
    P, K = x.shape
    E, _, N = ws[0].shape
    tk = min(tk, K)
    tn = min(tn, N)
    assert P % tm == 0 and K % tk == 0
    sb = min(SUB_ROWS, tm)
    assert tm % sb == 0
    n_w = len(ws)
    ni, nj, nk = P // tm, pl.cdiv(N, tn), K // tk

    def x_idx(i, j, k, te, ns, src):
        return (src[i], jnp.where(ns[i] > 0, k, nk - 1))

    def w_idx(i, j, k, te, ns, src):
        act = ns[i] > 0
        return (te[i], jnp.where(act, k, nk - 1), jnp.where(act, j, nj - 1))

    def o_idx(i, j, k, te, ns, src):
        return (i, j)

    in_specs = [pl.BlockSpec((tm, tk), x_idx)] + [pl.BlockSpec((1, tk, tn), w_idx)] * n_w
    args = [x] + list(ws)
    if resid is not None:
        in_specs.append(pl.BlockSpec((tm, tn), o_idx))
        args.append(resid)
    return pl.pallas_call(
        functools.partial(_gmm_body, n_w, nk, tm // sb, sb, resid is not None),
        grid_spec=pltpu.PrefetchScalarGridSpec(
            num_scalar_prefetch=3, grid=(ni, nj, nk), in_specs=in_specs,
            out_specs=pl.BlockSpec((tm, tn), o_idx),
            scratch_shapes=[pltpu.VMEM((tm, tn), F32)] * n_w),
        out_shape=jax.ShapeDtypeStruct((P, N), out_dtype),
        compiler_params=pltpu.CompilerParams(
            dimension_semantics=("parallel", "parallel", "arbitrary"), vmem_limit_bytes=MM_VMEM_LIMIT),
        name=f"gmm{n_w}_{tm}x{tn}x{tk}",
    )(*tables, *args)


def _dense_tables(rows, tm):
    n = rows // tm
    return (jnp.zeros((n,), jnp.int32), jnp.full((n,), tm // min(SUB_ROWS, tm), jnp.int32),
            jnp.arange(n, dtype=jnp.int32))


def _row_tile(rows):
    for tm in (2048, 1024, 512, 256):
        if rows % tm == 0:
            return tm
    return rows


def matmul(x, w, out_dtype=F32, resid=None, tn=1024, tk=512):
    lead = x.shape[:-1]
    x2 = x.reshape(-1, x.shape[-1])
    rows = x2.shape[0]
    tm = _row_tile(rows)
    r2 = None if resid is None else resid.reshape(rows, -1)
    out = grouped_matmul(x2, [w[None]], _dense_tables(rows, tm), tm, tn, tk, out_dtype, r2)
    return out.reshape(*lead, w.shape[-1])


def swiglu_up(x, w1, w3, tables, tm, tf=512, tk=1024):
    return grouped_matmul(x, [w1, w3], tables, tm, tf, tk, BF16)


PAGES_PER_STEP = 8
REMOVED = -3e38


def bias_by_position(tab, past_len):
    bd = bias_by_distance(tab, REL_MAX_DIST + 1)
    H = bd.shape[0]
    near = bd[:, 1:REL_MAX_DIST + 1][:, ::-1]
    far = jnp.broadcast_to(bd[:, REL_MAX_DIST:], (H, past_len - REL_MAX_DIST))
    return jnp.concatenate([far, near], axis=1), bd[:, :1]


def _page_specs(lanes, lane_block, pages_of):
    def spec(r):
        return pl.BlockSpec((None, PAGE_SIZE, lanes), lambda *a: (pages_of(r)(*a), 0, lane_block))
    return [spec(r) for r in range(PAGES_PER_STEP)]


def _chunk_page(r):
    return lambda b, c, pt, *_: pt[b, c * PAGES_PER_STEP + r]


def _topk_rows(score, k):
    R, L = score.shape
    jf = lax.broadcasted_iota(jnp.int32, (R, L), 1).astype(F32)
    slot = lax.broadcasted_iota(jnp.int32, (R, LANES), 1)
    idx = jnp.zeros((R, LANES), F32)
    val = jnp.zeros((R, LANES), F32)
    for n in range(k):
        m = jnp.max(score, axis=1, keepdims=True)
        i = jnp.min(jnp.where(score == m, jf, 1e9), axis=1, keepdims=True)
        idx = jnp.where(slot == n, i, idx)
        val = jnp.where(slot == n, jnp.where(m > 0.5 * NEG, 1.0, 0.0), val)
        score = jnp.where(jf == i, REMOVED, score)
    return idx, val


def _gather_cmp_body(width, pt_ref, *refs):
    pages = refs[:PAGES_PER_STEP]
    pe_ref, o_ref = refs[PAGES_PER_STEP:]
    for r in range(PAGES_PER_STEP):
        for g in range(A_GROUPS):
            o_ref[0, g, r * PAGE_SIZE:(r + 1) * PAGE_SIZE, :] = pages[r][:, g * width:(g + 1) * width] + pe_ref[...]


def gather_compress_rows(cache, page_table, pe, width):
    B, n_pages = page_table.shape
    pe2 = jnp.tile(pe, (PAGE_SIZE // CMP_BLK, 1))
    return pl.pallas_call(
        functools.partial(_gather_cmp_body, width),
        grid_spec=pltpu.PrefetchScalarGridSpec(
            num_scalar_prefetch=1, grid=(B, n_pages // PAGES_PER_STEP),
            in_specs=_page_specs(A_GROUPS * width, 0, _chunk_page)
            + [pl.BlockSpec((PAGE_SIZE, width), lambda b, c, pt: (0, 0))],
            out_specs=pl.BlockSpec((1, A_GROUPS, PAGES_PER_STEP * PAGE_SIZE, width), lambda b, c, pt: (b, 0, c, 0))),
        out_shape=jax.ShapeDtypeStruct((B, A_GROUPS, n_pages * PAGE_SIZE, width), F32),
        compiler_params=pltpu.CompilerParams(dimension_semantics=("parallel", "parallel"),
                                             vmem_limit_bytes=VMEM_LIMIT),
        name="gather_compress_rows",
    )(page_table, *([cache] * PAGES_PER_STEP), pe2)


def _nsa_sample_a_body(q_ref, kc_ref, vc_ref, wk_ref, wv_ref, kn_ref, vn_ref, bw_ref, oc_ref, ow_ref, idx_ref, val_ref):
    scale = A_DK ** -0.5
    q = q_ref[0, 0]
    qb = q.astype(BF16)
    nc = kc_ref.shape[2]
    s = lax.dot_general(qb, kc_ref[0, 0].astype(BF16), _NT, preferred_element_type=F32) * scale
    e = jnp.exp(s - jnp.max(s, axis=1, keepdims=True))
    p = e / jnp.sum(e, axis=1, keepdims=True)
    oc_ref[0, 0] = jnp.dot(p.astype(BF16), vc_ref[0, 0].astype(BF16), preferred_element_type=F32)
    imp = jnp.concatenate([jnp.sum(p, axis=0, keepdims=True), jnp.zeros((1, LANES), F32)], axis=1)
    j = lax.broadcasted_iota(jnp.int32, imp.shape, 1)
    own = nc
    forced = (j == 0) | (j == own) | (j == own - 1)
    score = jnp.where(forced, FORCE, imp)
    score = jnp.where(j <= own, score, REMOVED)
    idx, val = _topk_rows(score, N_SEL)
    idx_ref[0, 0] = idx.astype(jnp.int32)
    val_ref[0, 0] = val.astype(jnp.int32)
    wb = wk_ref.shape[2]
    sw = lax.dot_general(qb, wk_ref[0, 0].astype(BF16), _NT, preferred_element_type=F32) * scale + bw_ref[0][:, :wb]
    dist = wb - lax.broadcasted_iota(jnp.int32, sw.shape, 1)
    in_win = dist < WINDOW
    sn = jnp.sum(q * kn_ref[0, 0], axis=1, keepdims=True) * scale + bw_ref[0][:, wb:wb + 1]
    m = jnp.maximum(jnp.max(jnp.where(in_win, sw, NEG), axis=1, keepdims=True), sn)
    ew = jnp.where(in_win, jnp.exp(sw - m), 0.0)
    en = jnp.exp(sn - m)
    l = jnp.sum(ew, axis=1, keepdims=True) + en
    ow_ref[0, 0] = (jnp.dot(ew.astype(BF16), wv_ref[0, 0].astype(BF16), preferred_element_type=F32)
                    + en * vn_ref[0, 0]) / l


def _nsa_sample_b_body(nc, idx_ref, val_ref, pt_ref, q_ref, ka_ref, kb_ref, va_ref, vb_ref, ba_ref, bb_ref,
                       kn_ref, vn_ref, b0_ref, o_ref, m_ref, l_ref, acc_ref):
    b = pl.program_id(0)
    n = pl.program_id(1)
    scale = A_DK ** -0.5

    @pl.when(n == 0)
    def _():
        m_ref[...] = jnp.full(m_ref.shape, NEG, F32)
        l_ref[...] = jnp.zeros(l_ref.shape, F32)
        acc_ref[...] = jnp.zeros(acc_ref.shape, F32)

    for g, (k_ref, v_ref, bias_ref) in enumerate(((ka_ref, va_ref, ba_ref), (kb_ref, vb_ref, bb_ref))):
        slot = (b * A_GROUPS + g) * N_SEL + n
        cached = (val_ref[slot] > 0) & (idx_ref[slot] < nc)

        @pl.when(cached)
        def _(g=g, k_ref=k_ref, v_ref=v_ref, bias_ref=bias_ref):
            lo_k = (A_GROUPS + g) * A_DK
            lo_v = (A_GROUPS + g) * A_DV
            k = k_ref[:, lo_k:lo_k + A_DK].astype(BF16)
            v = v_ref[:, lo_v:lo_v + A_DV].astype(BF16)
            s = lax.dot_general(q_ref[0, g].astype(BF16), k, _NT, preferred_element_type=F32) * scale + bias_ref[0]
            m_prev = m_ref[g][:, :1]
            m_new = jnp.maximum(m_prev, jnp.max(s, axis=1, keepdims=True))
            p = jnp.exp(s - m_new)
            alpha = jnp.exp(m_prev - m_new)
            l_ref[g] = jnp.broadcast_to(alpha * l_ref[g][:, :1] + jnp.sum(p, axis=1, keepdims=True), l_ref.shape[1:])
            acc_ref[g] = alpha * acc_ref[g] + jnp.dot(p.astype(BF16), v, preferred_element_type=F32)
            m_ref[g] = jnp.broadcast_to(m_new, m_ref.shape[1:])

    @pl.when(n == N_SEL - 1)
    def _():
        for g in range(A_GROUPS):
            sn = jnp.sum(q_ref[0, g] * kn_ref[0, g], axis=1, keepdims=True) * scale + b0_ref[g]
            m_prev = m_ref[g][:, :1]
            m_new = jnp.maximum(m_prev, sn)
            alpha = jnp.exp(m_prev - m_new)
            en = jnp.exp(sn - m_new)
            l = alpha * l_ref[g][:, :1] + en
            o_ref[0, g] = (alpha * acc_ref[g] + en * vn_ref[0, g]) / l


def nsa_decode(q, k3, v3, gates, cache_k, cache_v, win_k, win_v, page_table,
               pe_k, pe_v, wk1, wk2, wv1, wv2, tab_a):
    B, S, G, HPG, DK = q.shape
    n_pool = cache_k.shape[0]
    n_pages = page_table.shape[1]
    P = n_pages * PAGE_SIZE
    nc = P // CMP_BLK
    assert S == 1 and G == A_GROUPS and P % CMP_BLK == 0 and (P + S) // CMP_BLK == nc and win_k.shape[1] == WINDOW
    ck = cache_k.reshape(n_pool, PAGE_SIZE, 2 * G * DK)
    cv = cache_v.reshape(n_pool, PAGE_SIZE, 2 * G * A_DV)
    rows_k = gather_compress_rows(ck, page_table, pe_k, DK).reshape(B, G, nc, CMP_BLK * DK)
    rows_v = gather_compress_rows(cv, page_table, pe_v, A_DV).reshape(B, G, nc, CMP_BLK * A_DV)
    kc = matmul(jax.nn.silu(matmul(rows_k, wk1)), wk2)
    vc = matmul(jax.nn.silu(matmul(rows_v, wv1)), wv2)
    qh = q.reshape(B, G, HPG, DK)
    brev, b0 = bias_by_position(tab_a, P)
    bw = jnp.concatenate([brev[:, P - WINDOW:], b0, jnp.zeros((A_HEADS, LANES - 1), F32)], axis=1)
    bw = bw.reshape(G, HPG, WINDOW + LANES)
    new_k = jnp.moveaxis(k3[:, 0], 1, 2)
    new_v = jnp.moveaxis(v3[:, 0], 1, 2)
    per_bg = lambda *shape: pl.BlockSpec((1, 1) + shape, lambda b, g: (b, g) + (0,) * len(shape))
    o_c, o_w, idx, val = pl.pallas_call(
        _nsa_sample_a_body,
        grid=(B, G),
        in_specs=[per_bg(HPG, DK), per_bg(nc, DK), per_bg(nc, A_DV), per_bg(WINDOW, DK), per_bg(WINDOW, A_DV),
                  per_bg(1, DK), per_bg(1, A_DV),
                  pl.BlockSpec((1, HPG, WINDOW + LANES), lambda b, g: (g, 0, 0))],
        out_specs=[per_bg(HPG, A_DV), per_bg(HPG, A_DV), per_bg(1, LANES), per_bg(1, LANES)],
        out_shape=[jax.ShapeDtypeStruct((B, G, HPG, A_DV), F32)] * 2
        + [jax.ShapeDtypeStruct((B, G, 1, LANES), jnp.int32)] * 2,
        compiler_params=pltpu.CompilerParams(dimension_semantics=("parallel", "parallel"),
                                             vmem_limit_bytes=VMEM_LIMIT),
        name="nsa_sample_cmp_win",
    )(qh, kc, vc, jnp.moveaxis(win_k, 1, 2), jnp.moveaxis(win_v, 1, 2), new_k[:, :, 2:3], new_v[:, :, 2:3], bw)
    idx_flat = idx[:, :, 0, :N_SEL].reshape(-1)
    val_flat = val[:, :, 0, :N_SEL].reshape(-1)
    bsel = brev.reshape(G, HPG, nc, CMP_BLK).transpose(0, 2, 1, 3)

    def blk(g):
        return lambda b, n, ix, vl, pt: jnp.minimum(ix[(b * G + g) * N_SEL + n], nc - 1)

    def kv_spec(g, lanes):
        half = PAGE_SIZE // CMP_BLK
        return pl.BlockSpec((None, CMP_BLK, lanes),
                            lambda b, n, ix, vl, pt: (pt[b, blk(g)(b, n, ix, vl, pt) // half],
                                                       blk(g)(b, n, ix, vl, pt) % half, 0))

    def bias_spec(g):
        return pl.BlockSpec((None, 1, HPG, CMP_BLK), lambda b, n, ix, vl, pt: (g, blk(g)(b, n, ix, vl, pt), 0, 0))
    whole = lambda *shape: pl.BlockSpec((1,) + shape, lambda b, n, ix, vl, pt: (b,) + (0,) * len(shape))
    o_s = pl.pallas_call(
        functools.partial(_nsa_sample_b_body, nc),
        grid_spec=pltpu.PrefetchScalarGridSpec(
            num_scalar_prefetch=3, grid=(B, N_SEL),
            in_specs=[whole(G, HPG, DK), kv_spec(0, 2 * G * DK), kv_spec(1, 2 * G * DK),
                      kv_spec(0, 2 * G * A_DV), kv_spec(1, 2 * G * A_DV), bias_spec(0), bias_spec(1),
                      whole(G, 1, DK), whole(G, 1, A_DV),
                      pl.BlockSpec((G, HPG, 1), lambda b, n, ix, vl, pt: (0, 0, 0))],
            out_specs=whole(G, HPG, A_DV),
            scratch_shapes=[pltpu.VMEM((G, HPG, LANES), F32), pltpu.VMEM((G, HPG, LANES), F32),
                            pltpu.VMEM((G, HPG, A_DV), F32)]),
        out_shape=jax.ShapeDtypeStruct((B, G, HPG, A_DV), F32),
        compiler_params=pltpu.CompilerParams(dimension_semantics=("parallel", "arbitrary"),
                                             vmem_limit_bytes=VMEM_LIMIT),
        name="nsa_sample_selected",
    )(idx_flat, val_flat, page_table, qh, ck, ck, cv, cv, bsel, bsel, new_k[:, :, 1:2], new_v[:, :, 1:2],
      b0.reshape(G, HPG, 1))
    shp = (B, S, G, HPG, A_DV)
    return nsa_combine(gates, o_c.reshape(shp), o_s.reshape(shp), o_w.reshape(shp))


def _dsa_scores_body(pt_ref, *refs):
    pages = refs[:PAGES_PER_STEP]
    qi_ref, w_ref, o_ref = refs[PAGES_PER_STEP:]
    qi = qi_ref[0].astype(BF16)
    w = w_ref[0] * (IDX_HEADS ** -0.5)
    for r in range(PAGES_PER_STEP):
        d = lax.dot_general(qi, pages[r][...].astype(BF16), _NT, preferred_element_type=F32) * (IDX_DIM ** -0.5)
        o_ref[0, :, r * PAGE_SIZE:(r + 1) * PAGE_SIZE] = jnp.sum(jnp.maximum(d, 0.0) * w, axis=0, keepdims=True)


def _dsa_sample_select_body(topk, s_ref, qi_ref, w_ref, kn_ref, m_ref):
    B, P = s_ref.shape
    w = w_ref[...] * (IDX_HEADS ** -0.5)
    dn = jnp.sum(qi_ref[...] * kn_ref[...], axis=2) * (IDX_DIM ** -0.5)
    s_new = jnp.sum(jnp.maximum(dn, 0.0) * w, axis=1, keepdims=True)

    def order_key(x):
        bits = pltpu.bitcast(x, jnp.int32)
        return jnp.where(bits < 0, bits ^ jnp.int32(0x7FFFFFFF), bits)
    key = order_key(s_ref[...])
    key_new = order_key(jnp.broadcast_to(s_new, (B, LANES)))[:, :1]
    pos = lax.broadcasted_iota(jnp.int32, (B, P), 1)

    def count(cond, cond_new):
        return _count(cond) + jnp.where(cond_new, 1.0, 0.0)

    def value_step(i, lo):
        cand = lo + jnp.left_shift(jnp.int32(1), 31 - i)
        return jnp.where(count(key >= cand, key_new >= cand) >= float(topk), cand, lo)
    thr = lax.fori_loop(0, 32, value_step, jnp.full((B, 1), jnp.int32(-2 ** 31), jnp.int32))
    need = float(topk) - count(key > thr, key_new > thr)
    tied = key == thr
    n_bits = max(1, (P - 1).bit_length())

    def index_step(i, m):
        cand = m + jnp.left_shift(jnp.int32(1), n_bits - 1 - i)
        return jnp.where(_count(tied & (pos < cand)) < need, cand, m)
    last = lax.fori_loop(0, n_bits, index_step, jnp.zeros((B, 1), jnp.int32))
    sel = (key > thr) | (tied & (pos <= last))
    taken = _count(sel)
    sel_new = (key_new > thr) | ((key_new == thr) & (taken < float(topk)))
    m_ref[:, :P] = jnp.where(sel, 1.0, 0.0)
    lane = lax.broadcasted_iota(jnp.int32, (B, LANES), 1)
    m_ref[:, P:] = jnp.where((lane == 0) & sel_new, 1.0, 0.0)


def _dsa_sample_attend_body(pt_ref, *refs):
    n = PAGES_PER_STEP
    kp, vp = refs[:n], refs[n:2 * n]
    q_ref, mask_ref, bias_ref, mnew_ref, kn_ref, vn_ref, b0_ref, o_ref, m_ref, l_ref, acc_ref = refs[2 * n:]
    c = pl.program_id(1)
    scale = B_DH ** -0.5

    @pl.when(c == 0)
    def _():
        m_ref[...] = jnp.full(m_ref.shape, NEG, F32)
        l_ref[...] = jnp.zeros(l_ref.shape, F32)
        acc_ref[...] = jnp.zeros(acc_ref.shape, F32)

    mask = mask_ref[0] > 0.5
    for g in range(B_GROUPS):
        sl = slice(g * B_DH, (g + 1) * B_DH)
        k = jnp.concatenate([kp[r][:, sl] for r in range(n)], axis=0).astype(BF16)
        v = jnp.concatenate([vp[r][:, sl] for r in range(n)], axis=0).astype(BF16)
        s = lax.dot_general(q_ref[0, g].astype(BF16), k, _NT, preferred_element_type=F32) * scale + bias_ref[g]
        s = jnp.where(mask, s, NEG)
        m_prev = m_ref[g][:, :1]
        m_new = jnp.maximum(m_prev, jnp.max(s, axis=1, keepdims=True))
        p = jnp.where(mask, jnp.exp(s - m_new), 0.0)
        alpha = jnp.exp(m_prev - m_new)
        l_ref[g] = jnp.broadcast_to(alpha * l_ref[g][:, :1] + jnp.sum(p, axis=1, keepdims=True), l_ref.shape[1:])
        acc_ref[g] = alpha * acc_ref[g] + jnp.dot(p.astype(BF16), v, preferred_element_type=F32)
        m_ref[g] = jnp.broadcast_to(m_new, m_ref.shape[1:])

    @pl.when(c == pl.num_programs(1) - 1)
    def _():
        new_on = mnew_ref[0][:, :1] > 0.5
        for g in range(B_GROUPS):
            sn = jnp.sum(q_ref[0, g] * kn_ref[0, g], axis=1, keepdims=True) * scale + b0_ref[g]
            sn = jnp.where(new_on, sn, NEG)
            m_prev = m_ref[g][:, :1]
            m_new = jnp.maximum(m_prev, sn)
            alpha = jnp.exp(m_prev - m_new)
            en = jnp.where(new_on, jnp.exp(sn - m_new), 0.0)
            l = alpha * l_ref[g][:, :1] + en
            o_ref[0, g] = (alpha * acc_ref[g] + en * vn_ref[0, g]) / l


def dsa_decode(q, k, v, qi, ki, wi, cache_k, cache_v, cache_kidx, page_table, tab_b):
    B, S, G, HPG, DH = q.shape
    n_pool = cache_k.shape[0]
    n_pages = page_table.shape[1]
    P = n_pages * PAGE_SIZE
    n_chunks = n_pages // PAGES_PER_STEP
    chunk = PAGES_PER_STEP * PAGE_SIZE
    assert S == 1 and n_pages % PAGES_PER_STEP == 0
    topk = min(IDX_TOPK, (P + S) // 4)
    params = pltpu.CompilerParams(dimension_semantics=("parallel", "arbitrary"), vmem_limit_bytes=VMEM_LIMIT)
    scores = pl.pallas_call(
        _dsa_scores_body,
        grid_spec=pltpu.PrefetchScalarGridSpec(
            num_scalar_prefetch=1, grid=(B, n_chunks),
            in_specs=_page_specs(IDX_DIM, 0, _chunk_page)
            + [pl.BlockSpec((1, IDX_HEADS, IDX_DIM), lambda b, c, pt: (b, 0, 0)),
               pl.BlockSpec((1, IDX_HEADS, 1), lambda b, c, pt: (b, 0, 0))],
            out_specs=pl.BlockSpec((1, 1, chunk), lambda b, c, pt: (b, 0, c))),
        out_shape=jax.ShapeDtypeStruct((B, 1, P), F32),
        compiler_params=params, name="dsa_sample_scores",
    )(page_table, *([cache_kidx] * PAGES_PER_STEP), qi[:, 0], wi[:, 0, :, None])
    sel = pl.pallas_call(
        functools.partial(_dsa_sample_select_body, topk),
        out_shape=jax.ShapeDtypeStruct((B, P + LANES), F32),
        compiler_params=pltpu.CompilerParams(vmem_limit_bytes=VMEM_LIMIT), name="dsa_sample_select",
    )(scores.reshape(B, P), qi[:, 0], wi[:, 0], ki)
    sel = sel.reshape(B, 1, P + LANES)
    brev, b0 = bias_by_position(tab_b, P)
    ck = cache_k.reshape(n_pool, PAGE_SIZE, G * DH)
    cv = cache_v.reshape(n_pool, PAGE_SIZE, G * DH)
    whole = lambda *shape: pl.BlockSpec((1,) + shape, lambda b, c, pt: (b,) + (0,) * len(shape))
    o = pl.pallas_call(
        _dsa_sample_attend_body,
        grid_spec=pltpu.PrefetchScalarGridSpec(
            num_scalar_prefetch=1, grid=(B, n_chunks),
            in_specs=_page_specs(G * DH, 0, _chunk_page) + _page_specs(G * DH, 0, _chunk_page)
            + [whole(G, HPG, DH),
               pl.BlockSpec((1, 1, chunk), lambda b, c, pt: (b, 0, c)),
               pl.BlockSpec((G, HPG, chunk), lambda b, c, pt: (0, 0, c)),
               pl.BlockSpec((1, 1, LANES), lambda b, c, pt: (b, 0, P // LANES)),
               whole(G, 1, DH), whole(G, 1, DH),
               pl.BlockSpec((G, HPG, 1), lambda b, c, pt: (0, 0, 0))],
            out_specs=whole(G, HPG, DH),
            scratch_shapes=[pltpu.VMEM((G, HPG, LANES), F32), pltpu.VMEM((G, HPG, LANES), F32),
                            pltpu.VMEM((G, HPG, DH), F32)]),
        out_shape=jax.ShapeDtypeStruct((B, G, HPG, DH), F32),
        compiler_params=params, name="dsa_sample_attend",
    )(page_table, *([ck] * PAGES_PER_STEP), *([cv] * PAGES_PER_STEP), q[:, 0], sel, brev.reshape(G, HPG, P), sel,
      jnp.moveaxis(k, 1, 2), jnp.moveaxis(v, 1, 2), b0.reshape(G, HPG, 1))
    return o.reshape(B, S, G * HPG * DH)


def _moba_kmean_body(pt_ref, p0_ref, p1_ref, o_ref):
    o_ref[0] = (jnp.sum(p0_ref[...], axis=0, keepdims=True)
                + jnp.sum(p1_ref[...], axis=0, keepdims=True)) * (1.0 / MOBA_BLK)


def _moba_sample_pick_body(nf, q_ref, km_ref, idx_ref, val_ref):
    B, H, DH = q_ref.shape
    nfp = km_ref.shape[1]
    group = lax.broadcasted_iota(jnp.int32, (H, nfp), 0) // C_HPG
    lane = lax.broadcasted_iota(jnp.int32, (H, nfp), 1)
    for b in range(B):
        qb = q_ref[b].astype(BF16)
        s = jnp.full((H, nfp), REMOVED, F32)
        for g in range(C_GROUPS):
            kg = km_ref[b, :, g * DH:(g + 1) * DH].astype(BF16)
            sg = lax.dot_general(qb, kg, _NT, preferred_element_type=F32)
            s = jnp.where((group == g) & (lane < nf), sg, s)
        idx, val = _topk_rows(s, MOBA_TOP)
        idx_ref[b] = idx.astype(jnp.int32)
        val_ref[b] = val.astype(jnp.int32)


def _moba_sample_attend_body(idx_ref, val_ref, pt_ref, q_ref, k0_ref, k1_ref, v0_ref, v1_ref, bias_ref,
                             kn_ref, vn_ref, b0_ref, o_ref, m_ref, l_ref, acc_ref):
    b, h, r = pl.program_id(0), pl.program_id(1), pl.program_id(2)
    scale = C_DH ** -0.5
    q = q_ref[0, 0]

    @pl.when(r == 0)
    def _():
        m_ref[...] = jnp.full(m_ref.shape, NEG, F32)
        l_ref[...] = jnp.zeros(l_ref.shape, F32)
        acc_ref[...] = jnp.zeros(acc_ref.shape, F32)

    @pl.when(val_ref[(b * C_HEADS + h) * MOBA_TOP + r] > 0)
    def _():
        k = jnp.concatenate([k0_ref[...], k1_ref[...]], axis=0).astype(BF16)
        v = jnp.concatenate([v0_ref[...], v1_ref[...]], axis=0).astype(BF16)
        s = lax.dot_general(q.astype(BF16), k, _NT, preferred_element_type=F32) * scale + bias_ref[0]
        m_prev = m_ref[:, :1]
        m_new = jnp.maximum(m_prev, jnp.max(s, axis=1, keepdims=True))
        p = jnp.exp(s - m_new)
        alpha = jnp.exp(m_prev - m_new)
        l_ref[...] = jnp.broadcast_to(alpha * l_ref[:, :1] + jnp.sum(p, axis=1, keepdims=True), l_ref.shape)
        acc_ref[...] = alpha * acc_ref[...] + jnp.dot(p.astype(BF16), v, preferred_element_type=F32)
        m_ref[...] = jnp.broadcast_to(m_new, m_ref.shape)

    @pl.when(r == MOBA_TOP - 1)
    def _():
        sn = jnp.sum(q * kn_ref[0, 0], axis=1, keepdims=True) * scale + b0_ref[0]
        m_prev = m_ref[:, :1]
        m_new = jnp.maximum(m_prev, sn)
        alpha = jnp.exp(m_prev - m_new)
        en = jnp.exp(sn - m_new)
        o_ref[0, 0] = (alpha * acc_ref[...] + en * vn_ref[0, 0]) / (alpha * l_ref[:, :1] + en)


def moba_decode(q, k, v, cache_k, cache_v, page_table, tab_c):
    B, S, H, DH = q.shape
    n_pool = cache_k.shape[0]
    n_pages = page_table.shape[1]
    P = n_pages * PAGE_SIZE
    ppb = MOBA_BLK // PAGE_SIZE
    nf = (P + S) // MOBA_BLK
    assert S == 1 and ppb == 2 and P % MOBA_BLK == 0 and nf == P // MOBA_BLK and nf >= MOBA_TOP
    ck = cache_k.reshape(n_pool, PAGE_SIZE, C_GROUPS * DH)
    cv = cache_v.reshape(n_pool, PAGE_SIZE, C_GROUPS * DH)
    kmean = pl.pallas_call(
        _moba_kmean_body,
        grid_spec=pltpu.PrefetchScalarGridSpec(
            num_scalar_prefetch=1, grid=(B, nf),
            in_specs=[pl.BlockSpec((None, PAGE_SIZE, C_GROUPS * DH), lambda b, j, pt: (pt[b, ppb * j], 0, 0)),
                      pl.BlockSpec((None, PAGE_SIZE, C_GROUPS * DH), lambda b, j, pt: (pt[b, ppb * j + 1], 0, 0))],
            out_specs=pl.BlockSpec((1, 1, C_GROUPS * DH), lambda b, j, pt: (b * nf + j, 0, 0))),
        out_shape=jax.ShapeDtypeStruct((B * nf, 1, C_GROUPS * DH), F32),
        compiler_params=pltpu.CompilerParams(dimension_semantics=("parallel", "parallel"),
                                             vmem_limit_bytes=VMEM_LIMIT),
        name="moba_sample_kmean",
    )(page_table, ck, ck).reshape(B, nf, C_GROUPS * DH)
    idx, val = pl.pallas_call(
        functools.partial(_moba_sample_pick_body, nf),
        out_shape=[jax.ShapeDtypeStruct((B, H, LANES), jnp.int32)] * 2,
        compiler_params=pltpu.CompilerParams(vmem_limit_bytes=VMEM_LIMIT), name="moba_sample_pick",
    )(q[:, 0], jnp.pad(kmean, ((0, 0), (0, -nf % LANES), (0, 0))))
    idx_flat = idx[:, :, :MOBA_TOP].reshape(-1)
    val_flat = val[:, :, :MOBA_TOP].reshape(-1)
    brev, b0 = bias_by_position(tab_c, P)

    def blk(b, h, r, ix):
        return jnp.minimum(ix[(b * H + h) * MOBA_TOP + r], nf - 1)

    def page_spec(which):
        return pl.BlockSpec((None, PAGE_SIZE, DH),
                            lambda b, h, r, ix, vl, pt: (pt[b, ppb * blk(b, h, r, ix) + which], 0, h // C_HPG))
    per_head = lambda rows, d: pl.BlockSpec((1, 1, rows, d), lambda b, h, r, ix, vl, pt: (b, h, 0, 0))
    per_group = lambda d: pl.BlockSpec((1, 1, 1, d), lambda b, h, r, ix, vl, pt: (b, h // C_HPG, 0, 0))
    o = pl.pallas_call(
        _moba_sample_attend_body,
        grid_spec=pltpu.PrefetchScalarGridSpec(
            num_scalar_prefetch=3, grid=(B, H, MOBA_TOP),
            in_specs=[per_head(1, DH), page_spec(0), page_spec(1), page_spec(0), page_spec(1),
                      pl.BlockSpec((1, 1, MOBA_BLK), lambda b, h, r, ix, vl, pt: (h * nf + blk(b, h, r, ix), 0, 0)),
                      per_group(DH), per_group(DH),
                      pl.BlockSpec((1, 1, 1), lambda b, h, r, ix, vl, pt: (h, 0, 0))],
            out_specs=per_head(1, DH),
            scratch_shapes=[pltpu.VMEM((1, LANES), F32), pltpu.VMEM((1, LANES), F32), pltpu.VMEM((1, DH), F32)]),
        out_shape=jax.ShapeDtypeStruct((B, H, 1, DH), F32),
        compiler_params=pltpu.CompilerParams(dimension_semantics=("parallel", "parallel", "arbitrary"),
                                             vmem_limit_bytes=VMEM_LIMIT),
        name="moba_sample_attend",
    )(idx_flat, val_flat, page_table, q[:, 0, :, None, :], ck, ck, cv, cv, brev.reshape(H * nf, 1, MOBA_BLK),
      jnp.moveaxis(k, 1, 2), jnp.moveaxis(v, 1, 2), b0.reshape(H, 1, 1))
    return o.reshape(B, S, H * DH)


def split_cols(z, sizes):
    cuts = [int(c) for c in np.cumsum(sizes)[:-1]]
    return jnp.split(z, cuts, axis=-1)


def masked_softmax(logits, mask):
    p = jax.nn.softmax(jnp.where(mask, logits, NEG), axis=-1)
    return jnp.where(mask, p, 0.0)


def t5_bucket(dist):
    n = jnp.maximum(dist, 0)
    nf = jnp.maximum(n, 1).astype(jnp.float32)
    large = REL_EXACT + (jnp.log(nf / REL_EXACT) / math.log(REL_MAX_DIST / REL_EXACT)
                         * (N_BUCKETS - REL_EXACT)).astype(jnp.int32)
    return jnp.where(n < REL_EXACT, n, jnp.minimum(large, N_BUCKETS - 1))


def gather_paged(pool, page_table, new_rows, pos, *extra):
    past_len = page_table.shape[1] * PAGE_SIZE
    b = jnp.arange(pos.shape[0]).reshape((-1,) + (1,) * (pos.ndim - 1))
    pc = jnp.clip(pos, 0, past_len - 1)
    phys = page_table[b, pc // PAGE_SIZE]
    old = pool[(phys, pc % PAGE_SIZE) + extra]
    new = new_rows[(b, jnp.clip(pos - past_len, 0, new_rows.shape[1] - 1)) + extra]
    is_new = (pos >= past_len).reshape(pos.shape + (1,) * (old.ndim - pos.ndim))
    return jnp.where(is_new, new, old)


def nsa_compress(rows, pe, w1, w2):
    B, L, G, d = rows.shape
    nc = L // CMP_BLK
    blk = rows[:, : nc * CMP_BLK].reshape(B, nc, CMP_BLK, G, d) + pe[None, None, :, None, :]
    flat = blk.transpose(0, 1, 3, 2, 4).reshape(B, nc, G, CMP_BLK * d)
    return matmul(jax.nn.silu(matmul(flat, w1)), w2)


def nsa_cmp_branch(q, q_pos, kc, vc):
    nc = kc.shape[1]
    logits = jnp.einsum('bqghd,bjgd->bqghj', q, kc, preferred_element_type=jnp.float32) * A_DK ** -0.5
    vis = (jnp.arange(nc) + 1) * CMP_BLK <= (q_pos + 1)[:, None]
    p = masked_softmax(logits, vis[None, :, None, None, :])
    o = jnp.einsum('bqghj,bjgd->bqghd', p.astype(vc.dtype), vc)
    return o, p.sum(axis=3)


def nsa_pick_blocks(imp, q_pos, n_blocks):
    score = jnp.pad(imp, ((0, 0), (0, 0), (0, 0), (0, n_blocks - imp.shape[-1])))
    j = jnp.arange(n_blocks)[None, :]
    own = (q_pos // CMP_BLK)[:, None]
    forced = (j == 0) | (j == own) | (j == own - 1)
    score = jnp.where(forced[None, :, None, :], FORCE, score)
    score = jnp.where((j <= own)[None, :, None, :], score, NEG)
    top_s, idx = lax.top_k(score, min(N_SEL, n_blocks))
    return idx, top_s > 0.5 * NEG


def nsa_sel_attend(q, q_pos, idx, valid, ksel, vsel, tab_a):
    B, Q, G, HPG, _ = q.shape
    kpos = idx[..., None] * CMP_BLK + jnp.arange(CMP_BLK)
    dist = q_pos[None, :, None, None, None] - kpos
    mask = valid[..., None] & (dist >= 0)
    tab2 = tab_a.reshape(N_BUCKETS, G, HPG).transpose(1, 0, 2)
    bias = tab2[jnp.arange(G).reshape(1, 1, G, 1, 1), t5_bucket(dist)]
    logits = (jnp.einsum('bqghd,bqgnsd->bqghns', q, ksel, preferred_element_type=jnp.float32) * A_DK ** -0.5
              + jnp.moveaxis(bias, -1, 3).astype(jnp.float32))
    shp = logits.shape
    p = masked_softmax(logits.reshape(B, Q, G, HPG, -1), mask.reshape(B, Q, G, 1, -1)).reshape(shp)
    return jnp.einsum('bqghns,bqgnsd->bqghd', p.astype(vsel.dtype), vsel)


def window_attend(q, q_pos, k, v, k_pos, tab_a):
    B, Q, G, HPG, _ = q.shape
    dist = q_pos[:, None] - k_pos[None, :]
    mask = (dist >= 0) & (dist < WINDOW) & (k_pos >= 0)[None, :]
    bias = tab_a[t5_bucket(dist)].reshape(Q, -1, G, HPG).transpose(0, 2, 3, 1).astype(jnp.float32)
    logits = jnp.einsum('bqghd,bsgd->bqghs', q, k, preferred_element_type=jnp.float32) * A_DK ** -0.5 + bias[None]
    p = masked_softmax(logits, mask[None, :, None, None, :])
    return jnp.einsum('bqghs,bsgd->bqghd', p.astype(v.dtype), v)


def nsa_combine(gates, o_c, o_s, o_w):
    g = gates[..., None].astype(o_c.dtype)
    o = g[:, :, 0] * o_c + g[:, :, 1] * o_s + g[:, :, 2] * o_w
    return o.reshape(o.shape[0], o.shape[1], -1)


def _heads_first(a):
    return jnp.moveaxis(a, 1, -2)


def nsa_prompt(q, k3, v3, gates, pe_k, pe_v, wk1, wk2, wv1, wv2, tab_a):
    B, T, G, HPG, DK = q.shape
    assert T % TQ == 0 and TQ == TK and (T // CMP_BLK) * CMP_BLK == T
    kc = nsa_compress(k3[:, :, 0], pe_k, wk1, wk2)
    vc = nsa_compress(v3[:, :, 0], pe_v, wv1, wv2)
    qh = _heads_first(q)
    o_c, sel_mask = nsa_cmp_select(qh, _heads_first(kc), _heads_first(vc))
    tiles = toeplitz_bias_tiles(tab_a)
    scale = A_DK ** -0.5
    o_s = flash_attention('key', qh, _heads_first(k3[:, :, 1]), _heads_first(v3[:, :, 1]), tiles, sel_mask, scale)
    o_w = flash_attention('window', qh, _heads_first(k3[:, :, 2]), _heads_first(v3[:, :, 2]), tiles, None, scale)
    shp = (B, T, G, HPG, A_DV)
    return nsa_combine(gates, o_c.reshape(shp), o_s.reshape(shp), o_w.reshape(shp))


def nsa_sample(q, k3, v3, gates, cache_k, cache_v, win_k, win_v, page_table,
               pe_k, pe_v, wk1, wk2, wv1, wv2, tab_a):
    B, S, G, HPG, DK = q.shape
    P = page_table.shape[1] * PAGE_SIZE
    L = P + S
    pos = P + jnp.arange(S)
    rows_k = jnp.concatenate([cache_k[page_table, :, 0].reshape(B, P, G, DK), k3[:, :, 0]], axis=1)
    rows_v = jnp.concatenate([cache_v[page_table, :, 0].reshape(B, P, G, A_DV), v3[:, :, 0]], axis=1)
    kc = nsa_compress(rows_k, pe_k, wk1, wk2)
    vc = nsa_compress(rows_v, pe_v, wv1, wv2)
    o_c, imp = nsa_cmp_branch(q, pos, kc, vc)
    idx, valid = nsa_pick_blocks(imp, pos, -(-L // CMP_BLK))
    kpos = idx[..., None] * CMP_BLK + jnp.arange(CMP_BLK)
    gi = jnp.arange(G).reshape(1, 1, G, 1, 1)
    ksel = gather_paged(cache_k, page_table, k3, kpos, 1, gi)
    vsel = gather_paged(cache_v, page_table, v3, kpos, 1, gi)
    o_s = nsa_sel_attend(q, pos, idx, valid, ksel, vsel, tab_a)
    wb = win_k.shape[1]
    kw = jnp.concatenate([win_k, k3[:, :, 2]], axis=1)
    vw = jnp.concatenate([win_v, v3[:, :, 2]], axis=1)
    o_w = window_attend(q, pos, kw, vw, P - wb + jnp.arange(wb + S), tab_a)
    return nsa_combine(gates, o_c, o_s, o_w)


def dsa_pick(qi, wi, ki, q_pos, topk):
    L = ki.shape[1]
    dots = jnp.einsum('bqhd,bld->bqhl', qi, ki, preferred_element_type=jnp.float32) * IDX_DIM ** -0.5
    score = jnp.einsum('bqhl,bqh->bql', jax.nn.relu(dots), wi.astype(jnp.float32) * IDX_HEADS ** -0.5)
    score = jnp.where(jnp.arange(L)[None, None, :] <= q_pos[None, :, None], score, NEG)
    _, idx = lax.top_k(score, topk)
    return idx, idx <= q_pos[None, :, None]


def dsa_attend(q, q_pos, idx, valid, ksel, vsel, tab_b):
    B, Q, G, HPG, DH = q.shape
    dist = q_pos[None, :, None] - idx
    bias = tab_b[t5_bucket(dist)].reshape(B, Q, -1, G, HPG).transpose(0, 1, 3, 4, 2).astype(jnp.float32)
    logits = jnp.einsum('bqghd,bqkgd->bqghk', q, ksel, preferred_element_type=jnp.float32) * DH ** -0.5 + bias
    p = masked_softmax(logits, valid[:, :, None, None, :])
    o = jnp.einsum('bqghk,bqkgd->bqghd', p.astype(vsel.dtype), vsel)
    return o.reshape(B, Q, -1)


def dsa_prompt(q, k, v, qi, ki, wi, tab_b):
    B, T = q.shape[:2]
    assert T % TQ == 0 and TQ == TK
    sel_mask = dsa_select(_heads_first(qi), ki, wi, min(IDX_TOPK, T // 4))
    return flash_attention('key', _heads_first(q), _heads_first(k), _heads_first(v),
                           toeplitz_bias_tiles(tab_b), sel_mask, B_DH ** -0.5)


def dsa_sample(q, k, v, qi, ki, wi, cache_k, cache_v, cache_kidx, page_table, tab_b):
    B, S = q.shape[:2]
    P = page_table.shape[1] * PAGE_SIZE
    L = P + S
    q_pos = P + jnp.arange(S)
    ki_all = jnp.concatenate([cache_kidx[page_table].reshape(B, P, IDX_DIM), ki], axis=1)
    idx, valid = dsa_pick(qi, wi, ki_all, q_pos, min(IDX_TOPK, L // 4))
    return dsa_attend(q, q_pos, idx, valid, gather_paged(cache_k, page_table, k, idx),
                      gather_paged(cache_v, page_table, v, idx), tab_b)


def moba_pick(q, q_pos, kmean):
    B, Q, H, DH = q.shape
    nf = kmean.shape[1]
    s = jnp.einsum('bqghd,bjgd->bqghj', q.reshape(B, Q, C_GROUPS, C_HPG, DH).astype(jnp.float32),
                   kmean).reshape(B, Q, H, nf)
    ncand = max(nf, MOBA_TOP)
    s = jnp.pad(s, ((0, 0), (0, 0), (0, 0), (0, ncand - nf)), constant_values=NEG)
    past = jnp.arange(ncand)[None, :] < (q_pos // MOBA_BLK)[:, None]
    s = jnp.where(past[None, :, None, :], s, NEG)
    top_s, idx = lax.top_k(s, MOBA_TOP)
    return idx, top_s > 0.5 * NEG


def moba_attend(q, q_pos, idx, valid, ksel, vsel, own_pos, kown, vown, tab_c):
    B, Q, H, DH = q.shape
    scale = DH ** -0.5
    sel_dist = q_pos[None, :, None, None, None] - (idx[..., None] * MOBA_BLK + jnp.arange(MOBA_BLK))
    hidx = jnp.arange(H).reshape(1, 1, H, 1, 1)
    ls = (jnp.einsum('bqhd,bqhrsd->bqhrs', q, ksel, preferred_element_type=jnp.float32) * scale
          + tab_c.T[hidx, t5_bucket(sel_dist)].astype(jnp.float32))
    ls = jnp.where(valid[..., None], ls, NEG).reshape(B, Q, H, MOBA_TOP * MOBA_BLK)
    own_dist = q_pos[:, None] - own_pos
    lo = jnp.einsum('bqghd,bqgsd->bqghs', q.reshape(B, Q, C_GROUPS, C_HPG, DH), kown,
                    preferred_element_type=jnp.float32).reshape(B, Q, H, MOBA_BLK) * scale
    lo = lo + tab_c[t5_bucket(own_dist)].transpose(0, 2, 1)[None].astype(jnp.float32)
    lo = jnp.where((own_dist >= 0)[None, :, None, :], lo, NEG)
    p = jax.nn.softmax(jnp.concatenate([ls, lo], axis=-1), axis=-1)
    ps = p[..., : MOBA_TOP * MOBA_BLK].reshape(B, Q, H, MOBA_TOP, MOBA_BLK).astype(vsel.dtype)
    po = p[..., MOBA_TOP * MOBA_BLK:].reshape(B, Q, C_GROUPS, C_HPG, MOBA_BLK).astype(vown.dtype)
    o = (jnp.einsum('bqhrs,bqhrsd->bqhd', ps, vsel)
         + jnp.einsum('bqghs,bqgsd->bqghd', po, vown).reshape(B, Q, H, DH))
    return o


def moba_prompt(q, k, v, tab_c):
    B, T, H, DH = q.shape
    assert T % MOBA_BLK == 0 and TQ == MOBA_BLK and TK == MOBA_BLK
    qh = _heads_first(q.reshape(B, T, C_GROUPS, C_HPG, DH))
    kh, vh = _heads_first(k), _heads_first(v)
    flags = moba_select(qh, kh).transpose(0, 1, 3, 2)
    return flash_attention('moba', qh, kh, vh, toeplitz_bias_tiles(tab_c), flags, DH ** -0.5)


def moba_sample(q, k, v, cache_k, cache_v, page_table, tab_c):
    B, S, H, DH = q.shape
    P = page_table.shape[1] * PAGE_SIZE
    L = P + S
    q_pos = P + jnp.arange(S)
    k_all = jnp.concatenate([cache_k[page_table].reshape(B, P, C_GROUPS, DH), k], axis=1)
    nf = L // MOBA_BLK
    kmean = k_all[:, : nf * MOBA_BLK].reshape(B, nf, MOBA_BLK, C_GROUPS, DH).astype(jnp.float32).mean(axis=2)
    idx, valid = moba_pick(q, q_pos, kmean)
    gh = (jnp.arange(H) // C_HPG).reshape(1, 1, H, 1, 1)
    sel_pos = idx[..., None] * MOBA_BLK + jnp.arange(MOBA_BLK)
    own_pos = (q_pos // MOBA_BLK)[:, None] * MOBA_BLK + jnp.arange(MOBA_BLK)
    opos = jnp.broadcast_to(own_pos[None, :, None, :], (B, S, C_GROUPS, MOBA_BLK))
    gi = jnp.arange(C_GROUPS).reshape(1, 1, C_GROUPS, 1)
    o = moba_attend(q, q_pos, idx, valid,
                    gather_paged(cache_k, page_table, k, sel_pos, gh), gather_paged(cache_v, page_table, v, sel_pos, gh),
                    own_pos, gather_paged(cache_k, page_table, k, opos, gi), gather_paged(cache_v, page_table, v, opos, gi),
                    tab_c)
    return o.reshape(B, S, H * DH)


def mem_kv(mem, g, wk, wv):
    m = rms_norm(mem, g, BF16)
    B = mem.shape[0]
    return matmul(m, wk).reshape(B, N_MEM, X_HEADS, X_DH), matmul(m, wv).reshape(B, N_MEM, X_HEADS, X_DH)


def cross_attend(x, h, mk, mv, wq, wo):
    B, T = h.shape[:2]
    q = matmul(h, wq).reshape(B, T, X_HEADS, X_DH)
    logits = jnp.einsum('bthd,bmhd->bthm', q, mk, preferred_element_type=jnp.float32) * X_DH ** -0.5
    p = jax.nn.softmax(logits, axis=-1)
    o = jnp.einsum('bthm,bmhd->bthd', p.astype(mv.dtype), mv).reshape(B, T, X_HEADS * X_DH)
    return matmul(o, wo, resid=x)


def dense_swiglu(x, h, w1, w3, w2):
    rows = h.shape[0]
    tm = _row_tile(rows)
    tables = _dense_tables(rows, tm)
    g = swiglu_up(h, w1, w3, tables, tm)
    return grouped_matmul(g, [w2], tables, tm, 1024, 512, F32, resid=x)


MOE_TM = 2304


def moe_tables(top_e):
    A = top_e.size
    n_tiles = -(-(A + N_EXPERTS * (MOE_TM - 1)) // MOE_TM)
    e_flat = top_e.reshape(A).astype(jnp.int32)
    order = jnp.argsort(e_flat).astype(jnp.int32)
    counts = jnp.bincount(e_flat, length=N_EXPERTS).astype(jnp.int32)
    starts = jnp.cumsum(counts) - counts
    tiles_per = (counts + MOE_TM - 1) // MOE_TM
    tile_end = jnp.cumsum(tiles_per)
    tile_start = tile_end - tiles_per
    e_sorted = e_flat[order]
    prow_sorted = tile_start[e_sorted] * MOE_TM + (jnp.arange(A, dtype=jnp.int32) - starts[e_sorted])
    row_token = jnp.zeros((n_tiles * MOE_TM,), jnp.int32).at[prow_sorted].set(order // TOP_K)
    prow_of_assign = jnp.zeros((A,), jnp.int32).at[order].set(prow_sorted)
    ti = jnp.arange(n_tiles, dtype=jnp.int32)
    n_active = tile_end[-1]
    last = n_active - 1
    src = jnp.minimum(ti, last)
    te = jnp.minimum(jnp.searchsorted(tile_end, src, side='right').astype(jnp.int32), N_EXPERTS - 1)
    rows_in = jnp.clip(counts[te] - (src - tile_start[te]) * MOE_TM, 0, MOE_TM)
    nsb = jnp.where(ti < n_active, (rows_in + SUB_ROWS - 1) // SUB_ROWS, 0).astype(jnp.int32)
    return (te, nsb, src), row_token, prow_of_assign


def moe_swiglu(xs, hs, w_router, b_router, w1, w3, w2):
    wr = jnp.pad(w_router, ((0, 0), (0, LANES - N_EXPERTS)))
    logits = jnp.concatenate([matmul(h, wr)[:, :N_EXPERTS] for h in hs], axis=0) + b_router.astype(jnp.float32)
    x = jnp.concatenate(xs, axis=0)
    h = jnp.concatenate(hs, axis=0)
    N = h.shape[0]
    top_l, top_e = lax.top_k(logits, TOP_K)
    gate = jax.nn.softmax(top_l, axis=-1)
    tables, row_token, prow_of_assign = moe_tables(top_e)
    g = swiglu_up(h[row_token], w1, w3, tables, MOE_TM)
    y = grouped_matmul(g, [w2], tables, MOE_TM, 1024, 512, F32)
    pair = y[prow_of_assign.reshape(N, TOP_K)] * gate[:, :, None]
    out = x + jnp.sum(pair, axis=1)
    cuts = [int(c) for c in np.cumsum([a.shape[0] for a in xs])[:-1]]
    return jnp.split(out, cuts, axis=0)


def even_split(z):
    B, T = z.shape[:2]
    qa, ka, va, ga, qb, kb, vb, qi, ki, wi = split_cols(z, EVEN_SPLITS)
    return (qa.reshape(B, T, A_GROUPS, A_HPG, A_DK),
            ka.reshape(B, T, 3, A_GROUPS, A_DK),
            va.reshape(B, T, 3, A_GROUPS, A_DV),
            jax.nn.sigmoid(ga.astype(jnp.float32)).reshape(B, T, 3, A_GROUPS, A_HPG),
            qb.reshape(B, T, B_GROUPS, B_HPG, B_DH),
            kb.reshape(B, T, B_GROUPS, B_DH),
            vb.reshape(B, T, B_GROUPS, B_DH),
            qi.reshape(B, T, IDX_HEADS, IDX_DIM), ki, wi)


def odd_split(z):
    B, T = z.shape[:2]
    q, k, v = split_cols(z, ODD_SPLITS)
    return (q.reshape(B, T, C_HEADS, C_DH), k.reshape(B, T, C_GROUPS, C_DH), v.reshape(B, T, C_GROUPS, C_DH))


def kernel(x_prompt, x_sample, mem_prompt, cache_a_k, cache_a_v, state_a_win_k, state_a_win_v,
           cache_b_k, cache_b_v, cache_b_kidx, cache_c_k, cache_c_v, cache_mem_k, cache_mem_v, page_table,
           rel_bias, norm_mix, norm_mem, norm_cross, norm_ffn, norm_final,
           w_cross_q, w_cross_k, w_cross_v, w_cross_o, w_in_even, w_out_even,
           nsa_pe_k, nsa_pe_v, nsa_phi_k1, nsa_phi_k2, nsa_phi_v1, nsa_phi_v2,
           w_ffn1, w_ffn3, w_ffn2, w_in_odd, w_out_odd, w_router, b_router, w_exp1, w_exp3, w_exp2):
    xp, xs = x_prompt, x_sample
    B, T, D = xp.shape
    Bs, S = xs.shape[:2]
    tab_a = rel_bias[:, :A_HEADS]
    tab_b = rel_bias[:, A_HEADS:A_HEADS + B_HEADS]
    tab_c = rel_bias[:, :C_HEADS]
    names = ('a_k_p', 'a_v_p', 'aw_k_p', 'aw_v_p', 'b_k_p', 'b_v_p', 'b_i_p', 'c_k_p', 'c_v_p', 'm_k_p', 'm_v_p',
             'a_k_s', 'a_v_s', 'aw_k_s', 'aw_v_s', 'b_k_s', 'b_v_s', 'b_i_s', 'c_k_s', 'c_v_s')
    new = {n: [] for n in names}
    for layer in range(DEPTH):
        li = layer // 2
        hp = rms_norm(xp, norm_mix[layer], BF16)
        hs = rms_norm(xs, norm_mix[layer], BF16)
        if layer % 2 == 0:
            phi = (nsa_pe_k[li], nsa_pe_v[li], nsa_phi_k1[li], nsa_phi_k2[li], nsa_phi_v1[li], nsa_phi_v2[li])
            qa, ka, va, ga, qb, kb, vb, qi, ki, wi = even_split(matmul(hp, w_in_even[li]))
            o_a = nsa_prompt(qa, ka, va, ga, *phi, tab_a)
            o_b = dsa_prompt(qb, kb, vb, qi, ki, wi, tab_b)
            xp = matmul(jnp.concatenate([o_a, o_b], axis=-1), w_out_even[li], resid=xp)
            wk = min(WINDOW, T)
            new['a_k_p'].append(ka[:, :, :2]); new['a_v_p'].append(va[:, :, :2])
            new['aw_k_p'].append(ka[:, T - wk:, 2]); new['aw_v_p'].append(va[:, T - wk:, 2])
            new['b_k_p'].append(kb); new['b_v_p'].append(vb); new['b_i_p'].append(ki)
            qa, ka, va, ga, qb, kb, vb, qi, ki, wi = even_split(matmul(hs, w_in_even[li]))
            o_a = nsa_decode(qa, ka, va, ga, cache_a_k[li], cache_a_v[li], state_a_win_k[li], state_a_win_v[li],
                             page_table, *phi, tab_a)
            o_b = dsa_decode(qb, kb, vb, qi, ki, wi, cache_b_k[li], cache_b_v[li], cache_b_kidx[li], page_table, tab_b)
            xs = matmul(jnp.concatenate([o_a, o_b], axis=-1), w_out_even[li], resid=xs)
            new['a_k_s'].append(ka[:, :, :2]); new['a_v_s'].append(va[:, :, :2])
            new['aw_k_s'].append(ka[:, :, 2]); new['aw_v_s'].append(va[:, :, 2])
            new['b_k_s'].append(kb); new['b_v_s'].append(vb); new['b_i_s'].append(ki)
        else:
            q, k, v = odd_split(matmul(hp, w_in_odd[li]))
            xp = matmul(moba_prompt(q, k, v, tab_c), w_out_odd[li], resid=xp)
            new['c_k_p'].append(k); new['c_v_p'].append(v)
            q, k, v = odd_split(matmul(hs, w_in_odd[li]))
            xs = matmul(moba_decode(q, k, v, cache_c_k[li], cache_c_v[li], page_table, tab_c), w_out_odd[li], resid=xs)
            new['c_k_s'].append(k); new['c_v_s'].append(v)
        mk, mv = mem_kv(mem_prompt, norm_mem[layer], w_cross_k[layer], w_cross_v[layer])
        new['m_k_p'].append(mk); new['m_v_p'].append(mv)
        xp = cross_attend(xp, rms_norm(xp, norm_cross[layer], BF16), mk, mv, w_cross_q[layer], w_cross_o[layer])
        xs = cross_attend(xs, rms_norm(xs, norm_cross[layer], BF16), cache_mem_k[layer], cache_mem_v[layer],
                          w_cross_q[layer], w_cross_o[layer])
        hp = rms_norm(xp, norm_ffn[layer], BF16).reshape(B * T, D)
        hs = rms_norm(xs, norm_ffn[layer], BF16).reshape(Bs * S, D)
        xp2, xs2 = xp.reshape(B * T, D), xs.reshape(Bs * S, D)
        if layer % 2 == 0:
            xp2 = dense_swiglu(xp2, hp, w_ffn1[li:li + 1], w_ffn3[li:li + 1], w_ffn2[li:li + 1])
            xs2 = dense_swiglu(xs2, hs, w_ffn1[li:li + 1], w_ffn3[li:li + 1], w_ffn2[li:li + 1])
        else:
            xp2, xs2 = moe_swiglu([xp2, xs2], [hp, hs], w_router[li], b_router[li], w_exp1[li], w_exp3[li], w_exp2[li])
        xp = xp2.reshape(B, T, D)
        xs = xs2.reshape(Bs, S, D)
    y_prompt = rms_norm(xp, norm_final)
    y_sample = rms_norm(xs, norm_final)
    return (y_prompt, y_sample,
            jnp.stack(new['a_k_p']), jnp.stack(new['a_v_p']), jnp.stack(new['aw_k_p']), jnp.stack(new['aw_v_p']),
            jnp.stack(new['b_k_p']), jnp.stack(new['b_v_p']), jnp.stack(new['b_i_p']),
            jnp.stack(new['c_k_p']), jnp.stack(new['c_v_p']), jnp.stack(new['m_k_p']), jnp.stack(new['m_v_p']),
            jnp.stack(new['a_k_s']), jnp.stack(new['a_v_s']), jnp.stack(new['aw_k_s']), jnp.stack(new['aw_v_s']),
            jnp.stack(new['b_k_s']), jnp.stack(new['b_v_s']), jnp.stack(new['b_i_s']),
            jnp.stack(new['c_k_s']), jnp.stack(new['c_v_s']))
```

```python
import functools
import math
import jax, jax.numpy as jnp
from jax import lax
import numpy as np
from jax.experimental import pallas as pl
from jax.experimental.pallas import tpu as pltpu

D_MODEL = 4096
BATCH = 4
SEQ = 2048
DEPTH = 2
DEC_BATCH = 8
DEC_SEQ = 1
PAST_LEN = 16384
PAGE_SIZE = 128

N_EVEN = (DEPTH + 1) // 2
N_ODD = DEPTH // 2
HEAD_SLOTS = 32
A_HEADS = 16
A_GROUPS = 2
A_HPG = A_HEADS // A_GROUPS
A_DK = 192
A_DV = 128
CMP_BLK = 64
N_SEL = 16
WINDOW = 512
B_HEADS = 16
B_GROUPS = 2
B_HPG = B_HEADS // B_GROUPS
B_DH = 128
IDX_HEADS = 8
IDX_DIM = 64
IDX_TOPK = 256
C_HEADS = 32
C_GROUPS = 8
C_HPG = C_HEADS // C_GROUPS
C_DH = 128
MOBA_BLK = 256
MOBA_TOP = 3
MOBA_QCHUNK = 32
N_MEM = 256
X_HEADS = 4
X_DH = 128
D_FF = 14336
N_EXPERTS = 8
TOP_K = 2
MOE_MAX_ROWS = 512
N_BUCKETS = 32
REL_EXACT = 16
REL_MAX_DIST = 1024
Q_BLOCK = 128
EPS = 1e-6
NEG = -1e30
FORCE = 1e9
EVEN_SPLITS = (A_HEADS * A_DK, 3 * A_GROUPS * A_DK, 3 * A_GROUPS * A_DV, 3 * A_HEADS,
               B_HEADS * B_DH, B_GROUPS * B_DH, B_GROUPS * B_DH, IDX_HEADS * IDX_DIM, IDX_DIM, IDX_HEADS)
ODD_SPLITS = (C_HEADS * C_DH, C_GROUPS * C_DH, C_GROUPS * C_DH)


def _rmsnorm_body(x_ref, g_ref, o_ref):
    x = x_ref[...]
    y = x * lax.rsqrt(jnp.mean(x * x, axis=-1, keepdims=True) + EPS)
    o_ref[...] = (y * g_ref[...]).astype(o_ref.dtype)


def rms_norm(x, g, out_dtype=None):
    out_dtype = out_dtype or x.dtype
    shape = x.shape
    d = shape[-1]
    x2 = x.reshape(-1, d)
    rows = x2.shape[0]
    tr = min(rows, 256)
    out = pl.pallas_call(
        _rmsnorm_body,
        grid=(rows // tr,),
        in_specs=[pl.BlockSpec((tr, d), lambda i: (i, 0)),
                  pl.BlockSpec((1, d), lambda i: (0, 0))],
        out_specs=pl.BlockSpec((tr, d), lambda i: (i, 0)),
        out_shape=jax.ShapeDtypeStruct((rows, d), out_dtype),
    )(x2, g.reshape(1, d).astype(jnp.float32))
    return out.reshape(shape)


TQ = 256
TK = 256
N_OFF = -(-(REL_MAX_DIST + TK - 1) // TK) + 1
LANES = 128
VMEM_LIMIT = 48 * 1024 * 1024
_NT = (((1,), (1,)), ((), ()))
BF16 = jnp.bfloat16
F32 = jnp.float32


def bias_by_distance(tab, n_dist):
    return tab[t5_bucket(jnp.arange(n_dist))].T.astype(F32)


def toeplitz_bias_tiles(tab):
    H = tab.shape[1]
    bd = bias_by_distance(tab, N_OFF * TK + TQ)
    epad = jnp.concatenate([jnp.broadcast_to(bd[:, :1], (H, TK - 1)), bd], axis=1)
    w = TQ + TK - 1
    rows = []
    for o in range(N_OFF):
        erev = epad[:, o * TK: o * TK + w][:, ::-1]
        z = jnp.concatenate([erev, erev[:, :1]], axis=1)
        rows.append(jnp.roll(z, -(TQ - 1), axis=1))
    x = jnp.stack(rows, axis=1)
    y = jnp.tile(x, (1, 1, TQ))[:, :, : TQ * w].reshape(H, N_OFF, TQ, w)
    return y[..., :TK]


def _flash_body(mode, hpg, dv, scale, *refs):
    if mode == 'window':
        q_ref, k_ref, v_ref, b_ref, o_ref, m_ref, l_ref, acc_ref = refs
        x_ref = None
    else:
        q_ref, k_ref, v_ref, b_ref, x_ref, o_ref, m_ref, l_ref, acc_ref = refs
    qi = pl.program_id(2)
    ki = pl.program_id(3)

    @pl.when(ki == 0)
    def _():
        m_ref[...] = jnp.full(m_ref.shape, NEG, F32)
        l_ref[...] = jnp.zeros(l_ref.shape, F32)
        acc_ref[...] = jnp.zeros(acc_ref.shape, F32)

    active = ki <= qi
    if mode == 'window':
        active = active & (qi - ki <= WINDOW // TK)

    @pl.when(active)
    def _():
        k = k_ref[0, 0].astype(BF16)
        v = v_ref[0, 0].astype(BF16)
        dist = (qi - ki) * TK + (lax.broadcasted_iota(jnp.int32, (TQ, TK), 0)
                                 - lax.broadcasted_iota(jnp.int32, (TQ, TK), 1))
        if mode == 'key':
            shared_mask = x_ref[0, 0] > 0
        elif mode == 'window':
            shared_mask = (dist >= 0) & (dist < WINDOW)
        else:
            own_mask = jnp.where(dist >= 0, 1.0, 0.0)
            blk = lax.broadcasted_iota(jnp.int32, (TQ, x_ref.shape[-1]), 1)
        for h in range(hpg):
            if mode == 'moba':
                flag = jnp.sum(jnp.where(blk == ki, x_ref[0, h], 0.0), axis=1, keepdims=True)
                mask = jnp.where(ki == qi, own_mask, jnp.broadcast_to(flag, (TQ, TK))) > 0.5
            else:
                mask = shared_mask
            q = q_ref[0, 0, h].astype(BF16)
            s = lax.dot_general(q, k, _NT, preferred_element_type=F32) * scale + b_ref[h, 0]
            s = jnp.where(mask, s, NEG)
            m_prev = m_ref[h][:, :1]
            l_prev = l_ref[h][:, :1]
            m_new = jnp.maximum(m_prev, jnp.max(s, axis=1, keepdims=True))
            p = jnp.where(mask, jnp.exp(s - m_new), 0.0)
            alpha = jnp.exp(m_prev - m_new)
            l_new = alpha * l_prev + jnp.sum(p, axis=1, keepdims=True)
            acc_ref[h] = alpha * acc_ref[h] + jnp.dot(p.astype(BF16), v, preferred_element_type=F32)
            m_ref[h] = jnp.broadcast_to(m_new, (TQ, LANES))
            l_ref[h] = jnp.broadcast_to(l_new, (TQ, LANES))

    @pl.when(ki == qi)
    def _():
        for h in range(hpg):
            l = l_ref[h][:, :1]
            o_ref[0, :, h * dv:(h + 1) * dv] = jnp.where(l > 0.0, acc_ref[h] / jnp.where(l > 0.0, l, 1.0), 0.0)


def flash_attention(mode, q, k, v, bias_tiles, extra, scale):
    B, G, HPG, T, dk = q.shape
    dv = v.shape[-1]
    nq, nk = T // TQ, T // TK

    def kv_idx(b, g, qi, ki):
        lo = jnp.maximum(qi - WINDOW // TK, 0) if mode == 'window' else 0
        return (b, g, jnp.clip(ki, lo, qi), 0)

    in_specs = [
        pl.BlockSpec((1, 1, HPG, TQ, dk), lambda b, g, qi, ki: (b, g, 0, qi, 0)),
        pl.BlockSpec((1, 1, TK, dk), kv_idx),
        pl.BlockSpec((1, 1, TK, dv), kv_idx),
        pl.BlockSpec((HPG, 1, TQ, TK), lambda b, g, qi, ki: (g, jnp.clip(qi - ki, 0, N_OFF - 1), 0, 0)),
    ]
    args = [q, k, v, bias_tiles]
    if mode == 'key':
        gm = extra.shape[1]
        in_specs.append(pl.BlockSpec((1, 1, TQ, TK),
                                     lambda b, g, qi, ki: (b, g if gm > 1 else 0, qi, jnp.minimum(ki, qi))))
        args.append(extra)
    elif mode == 'moba':
        in_specs.append(pl.BlockSpec((1, HPG, TQ, extra.shape[-1]), lambda b, g, qi, ki: (b, g, qi, 0)))
        args.append(extra)
    return pl.pallas_call(
        functools.partial(_flash_body, mode, HPG, dv, scale),
        grid=(B, G, nq, nk),
        in_specs=in_specs,
        out_specs=pl.BlockSpec((1, TQ, HPG * dv), lambda b, g, qi, ki: (b, qi, g)),
        out_shape=jax.ShapeDtypeStruct((B, T, G * HPG * dv), F32),
        scratch_shapes=[pltpu.VMEM((HPG, TQ, LANES), F32), pltpu.VMEM((HPG, TQ, LANES), F32),
                        pltpu.VMEM((HPG, TQ, dv), F32)],
        compiler_params=pltpu.CompilerParams(
            dimension_semantics=("parallel", "parallel", "parallel", "arbitrary"),
            vmem_limit_bytes=VMEM_LIMIT),
        name=f"flash_{mode}",
    )(*args)


HEADS_PER_STEP = 4


def _row_attn_body(mode, hs, dv, scale, n_tiles, *refs):
    if mode == 'window':
        q_ref, k_ref, v_ref, b_ref, o_ref = refs
        x_ref = None
    else:
        q_ref, k_ref, v_ref, b_ref, x_ref, o_ref = refs
    qi = pl.program_id(3)
    w = n_tiles * TK
    if mode == 'window':
        t0 = jnp.maximum(qi - (n_tiles - 1), 0)
        c0 = pl.multiple_of(t0 * TK, TK)
        k = k_ref[0, 0, pl.ds(c0, w), :].astype(BF16)
        v = v_ref[0, 0, pl.ds(c0, w), :].astype(BF16)
        dist = (qi * TQ + lax.broadcasted_iota(jnp.int32, (TQ, w), 0)) - (c0 + lax.broadcasted_iota(jnp.int32, (TQ, w), 1))
        shared_mask = (dist >= 0) & (dist < WINDOW)
    else:
        t0 = 0
        k = k_ref[0, 0].astype(BF16)
        v = v_ref[0, 0].astype(BF16)
        if mode == 'key':
            shared_mask = x_ref[0, 0] > 0
        else:
            causal = jnp.where(lax.broadcasted_iota(jnp.int32, (TQ, TK), 0)
                               >= lax.broadcasted_iota(jnp.int32, (TQ, TK), 1), 1.0, 0.0)
    for h in range(hs):
        if mode == 'moba':
            flags = x_ref[0, h]
            mask = jnp.concatenate(
                [jnp.where(qi == j, causal, jnp.broadcast_to(flags[:, j:j + 1], (TQ, TK))) for j in range(n_tiles)],
                axis=1) > 0.5
        else:
            mask = shared_mask
        bias = jnp.concatenate([b_ref[h, jnp.clip(qi - (t0 + j), 0, N_OFF - 1)] for j in range(n_tiles)], axis=1)
        s = lax.dot_general(q_ref[0, 0, h].astype(BF16), k, _NT, preferred_element_type=F32) * scale + bias
        s = jnp.where(mask, s, NEG)
        p = jnp.where(mask, jnp.exp(s - jnp.max(s, axis=1, keepdims=True)), 0.0)
        l = jnp.sum(p, axis=1, keepdims=True)
        o = jnp.dot(p.astype(BF16), v, preferred_element_type=F32)
        o_ref[0, :, h * dv:(h + 1) * dv] = jnp.where(l > 0.0, o / jnp.where(l > 0.0, l, 1.0), 0.0)


def block_attention(mode, q, k, v, bias_tiles, extra, scale):
    B, G, HPG, T, dk = q.shape
    dv = v.shape[-1]
    hs = min(HEADS_PER_STEP, HPG)
    nh = HPG // hs
    n_tiles = min(WINDOW // TK + 1, T // TK) if mode == 'window' else T // TK
    assert HPG % hs == 0 and T % TK == 0
    in_specs = [
        pl.BlockSpec((1, 1, hs, TQ, dk), lambda b, g, hh, qi: (b, g, hh, qi, 0)),
        pl.BlockSpec((1, 1, T, dk), lambda b, g, hh, qi: (b, g, 0, 0)),
        pl.BlockSpec((1, 1, T, dv), lambda b, g, hh, qi: (b, g, 0, 0)),
        pl.BlockSpec((hs, N_OFF, TQ, TK), lambda b, g, hh, qi: (g * nh + hh, 0, 0, 0)),
    ]
    args = [q, k, v, bias_tiles]
    if mode == 'key':
        gm = extra.shape[1]
        in_specs.append(pl.BlockSpec((1, 1, TQ, T), lambda b, g, hh, qi: (b, g if gm > 1 else 0, qi, 0)))
        args.append(extra)
    elif mode == 'moba':
        in_specs.append(pl.BlockSpec((1, hs, TQ, extra.shape[-1]), lambda b, g, hh, qi: (b, g * nh + hh, qi, 0)))
        args.append(extra)
    return pl.pallas_call(
        functools.partial(_row_attn_body, mode, hs, dv, scale, n_tiles),
        grid=(B, G, nh, T // TQ),
        in_specs=in_specs,
        out_specs=pl.BlockSpec((1, TQ, hs * dv), lambda b, g, hh, qi: (b, qi, g * nh + hh)),
        out_shape=jax.ShapeDtypeStruct((B, T, G * HPG * dv), F32),
        compiler_params=pltpu.CompilerParams(
            dimension_semantics=("parallel", "parallel", "parallel", "parallel"), vmem_limit_bytes=VMEM_LIMIT),
        name=f"attn_{mode}",
    )(*args)


def _nsa_cmp_body(q_ref, kc_ref, vc_ref, o_ref, mask_ref):
    qi = pl.program_id(2)
    nc = kc_ref.shape[2]
    T = mask_ref.shape[-1]
    kc = kc_ref[0, 0].astype(BF16)
    vc = vc_ref[0, 0].astype(BF16)
    t = qi * TQ + lax.broadcasted_iota(jnp.int32, (TQ, nc), 0)
    j = lax.broadcasted_iota(jnp.int32, (TQ, nc), 1)
    vis = (j + 1) * CMP_BLK <= t + 1
    imp = jnp.zeros((TQ, nc), F32)
    for h in range(A_HPG):
        q = q_ref[0, 0, h].astype(BF16)
        s = lax.dot_general(q, kc, _NT, preferred_element_type=F32) * (A_DK ** -0.5)
        s = jnp.where(vis, s, NEG)
        e = jnp.where(vis, jnp.exp(s - jnp.max(s, axis=1, keepdims=True)), 0.0)
        l = jnp.sum(e, axis=1, keepdims=True)
        p = jnp.where(l > 0.0, e / jnp.where(l > 0.0, l, 1.0), 0.0)
        o_ref[0, :, h * A_DV:(h + 1) * A_DV] = jnp.dot(p.astype(BF16), vc, preferred_element_type=F32)
        imp = imp + p
    own = t // CMP_BLK
    forced = (j == 0) | (j == own) | (j == own - 1)
    score = jnp.where(forced, FORCE, imp)
    score = jnp.where(j <= own, score, NEG)
    rank = jnp.zeros((TQ, nc), F32)
    for i in range(nc):
        si = score[:, i:i + 1]
        rank = rank + jnp.where((si > score) | ((si == score) & (i < j)), 1.0, 0.0)
    sel = jnp.where((rank < float(N_SEL)) & (j <= own), 1.0, 0.0).astype(BF16)
    expand = jnp.where(lax.broadcasted_iota(jnp.int32, (nc, T), 1) // CMP_BLK
                       == lax.broadcasted_iota(jnp.int32, (nc, T), 0), 1.0, 0.0).astype(BF16)
    keys = jnp.dot(sel, expand, preferred_element_type=F32)
    causal = (qi * TQ + lax.broadcasted_iota(jnp.int32, (TQ, T), 0)) >= lax.broadcasted_iota(jnp.int32, (TQ, T), 1)
    mask_ref[0, 0] = jnp.where((keys > 0.5) & causal, 1.0, 0.0).astype(BF16)


def nsa_cmp_select(q, kc, vc):
    B, G, HPG, T, dk = q.shape
    nc = kc.shape[2]
    return pl.pallas_call(
        _nsa_cmp_body,
        grid=(B, G, T // TQ),
        in_specs=[pl.BlockSpec((1, 1, HPG, TQ, dk), lambda b, g, qi: (b, g, 0, qi, 0)),
                  pl.BlockSpec((1, 1, nc, dk), lambda b, g, qi: (b, g, 0, 0)),
                  pl.BlockSpec((1, 1, nc, A_DV), lambda b, g, qi: (b, g, 0, 0))],
        out_specs=[pl.BlockSpec((1, TQ, HPG * A_DV), lambda b, g, qi: (b, qi, g)),
                   pl.BlockSpec((1, 1, TQ, T), lambda b, g, qi: (b, g, qi, 0))],
        out_shape=[jax.ShapeDtypeStruct((B, T, G * HPG * A_DV), F32),
                   jax.ShapeDtypeStruct((B, G, T, T), BF16)],
        compiler_params=pltpu.CompilerParams(
            dimension_semantics=("parallel", "parallel", "parallel"), vmem_limit_bytes=VMEM_LIMIT),
        name="nsa_cmp_select",
    )(q, kc, vc)


def _count(cond):
    return jnp.sum(jnp.where(cond, 1.0, 0.0), axis=1, keepdims=True)


def _dsa_select_body(topk, qi_ref, ki_ref, w_ref, mask_ref):
    qt = pl.program_id(1)
    T = ki_ref.shape[1]
    kidx = ki_ref[0].astype(BF16)
    w = w_ref[0] * (IDX_HEADS ** -0.5)
    score = jnp.zeros((TQ, T), F32)
    for h in range(IDX_HEADS):
        d = lax.dot_general(qi_ref[0, h].astype(BF16), kidx, _NT, preferred_element_type=F32) * (IDX_DIM ** -0.5)
        score = score + jnp.maximum(d, 0.0) * w[:, h:h + 1]
    t = qt * TQ + lax.broadcasted_iota(jnp.int32, (TQ, T), 0)
    s = lax.broadcasted_iota(jnp.int32, (TQ, T), 1)
    causal = s <= t
    score = jnp.where(causal, score, NEG)
    bits = pltpu.bitcast(score, jnp.int32)
    key = jnp.where(bits < 0, bits ^ jnp.int32(0x7FFFFFFF), bits)
    int_min = jnp.int32(-2 ** 31)

    def value_step(i, lo):
        cand = lo + jnp.left_shift(jnp.int32(1), 31 - i)
        return jnp.where(_count(key >= cand) >= float(topk), cand, lo)
    thr = lax.fori_loop(0, 32, value_step, jnp.full((TQ, 1), int_min, jnp.int32))
    above = key > thr
    tied = key == thr
    need = float(topk) - _count(above)
    n_bits = max(1, (T - 1).bit_length())

    def index_step(i, m):
        cand = m + jnp.left_shift(jnp.int32(1), n_bits - 1 - i)
        return jnp.where(_count(tied & (s < cand)) < need, cand, m)
    last = lax.fori_loop(0, n_bits, index_step, jnp.zeros((TQ, 1), jnp.int32))
    sel = (above | (tied & (s <= last))) & causal
    mask_ref[0, 0] = jnp.where(sel, 1.0, 0.0).astype(BF16)


def dsa_select(qi, ki, wi, topk):
    B, H, T, d = qi.shape
    return pl.pallas_call(
        functools.partial(_dsa_select_body, topk),
        grid=(B, T // TQ),
        in_specs=[pl.BlockSpec((1, H, TQ, d), lambda b, qt: (b, 0, qt, 0)),
                  pl.BlockSpec((1, T, d), lambda b, qt: (b, 0, 0)),
                  pl.BlockSpec((1, TQ, H), lambda b, qt: (b, qt, 0))],
        out_specs=pl.BlockSpec((1, 1, TQ, T), lambda b, qt: (b, 0, qt, 0)),
        out_shape=jax.ShapeDtypeStruct((B, 1, T, T), BF16),
        compiler_params=pltpu.CompilerParams(
            dimension_semantics=("parallel", "parallel"), vmem_limit_bytes=VMEM_LIMIT),
        name="dsa_select",
    )(qi, ki, wi)


def _moba_select_body(q_ref, k_ref, f_ref):
    T = k_ref.shape[2]
    nb = T // MOBA_BLK
    row = lax.broadcasted_iota(jnp.int32, (nb, k_ref.shape[3]), 0)
    kmean = jnp.zeros((nb, k_ref.shape[3]), F32)
    for b in range(nb):
        blk_sum = jnp.sum(k_ref[0, 0, b * MOBA_BLK:(b + 1) * MOBA_BLK, :], axis=0, keepdims=True)
        kmean = jnp.where(row == b, blk_sum * (1.0 / MOBA_BLK), kmean)
    kmean = kmean.astype(BF16)
    j = lax.broadcasted_iota(jnp.int32, (nb, T), 0)
    past = j < lax.broadcasted_iota(jnp.int32, (nb, T), 1) // MOBA_BLK
    for h in range(C_HPG):
        s = lax.dot_general(kmean, q_ref[0, 0, h].astype(BF16), _NT, preferred_element_type=F32)
        s = jnp.where(past, s, NEG)
        rank = jnp.zeros((nb, T), F32)
        for i in range(nb):
            si = s[i:i + 1, :]
            rank = rank + jnp.where((si > s) | ((si == s) & (i < j)), 1.0, 0.0)
        f_ref[0, h] = jnp.where((rank < float(MOBA_TOP)) & past, 1.0, 0.0)


def moba_select(q, k):
    B, G, HPG, T, dh = q.shape
    nb = T // MOBA_BLK
    return pl.pallas_call(
        _moba_select_body,
        grid=(B, G),
        in_specs=[pl.BlockSpec((1, 1, HPG, T, dh), lambda b, g: (b, g, 0, 0, 0)),
                  pl.BlockSpec((1, 1, T, dh), lambda b, g: (b, g, 0, 0))],
        out_specs=pl.BlockSpec((1, HPG, nb, T), lambda b, g: (b, g, 0, 0)),
        out_shape=jax.ShapeDtypeStruct((B, G * HPG, nb, T), F32),
        compiler_params=pltpu.CompilerParams(
            dimension_semantics=("parallel", "parallel"), vmem_limit_bytes=VMEM_LIMIT),
        name="moba_select",
    )(q, k)


MM_VMEM_LIMIT = 56 * 1024 * 1024
SUB_ROWS = 256


def _gmm_body(n_w, nk, nsb_max, sb, has_resid, te_ref, ns_ref, src_ref, *refs):
    x_ref = refs[0]
    w_refs = refs[1:1 + n_w]
    pos = 1 + n_w
    r_ref = refs[pos] if has_resid else None
    pos += int(has_resid)
    o_ref = refs[pos]
    acc_refs = refs[pos + 1:pos + 1 + n_w]
    i = pl.program_id(0)
    k = pl.program_id(2)
    n_sb = ns_ref[i]

    @pl.when(k == 0)
    def _():
        for a_ref in acc_refs:
            a_ref[...] = jnp.zeros(a_ref.shape, F32)

    for c in range(1, nsb_max + 1):
        @pl.when(n_sb == c)
        def _(c=c):
            xs = x_ref[:c * sb, :].astype(BF16)
            for w_ref, a_ref in zip(w_refs, acc_refs):
                a_ref[:c * sb, :] += jnp.dot(xs, w_ref[0].astype(BF16), preferred_element_type=F32)

    @pl.when(k == nk - 1)
    def _():
        if n_w == 2:
            val = jax.nn.silu(acc_refs[0][...]) * acc_refs[1][...]
        else:
            val = acc_refs[0][...]
        if has_resid:
            val = val + r_ref[...]
        o_ref[...] = val.astype(o_ref.dtype)


def grouped_matmul(x, ws, tables, tm, tn, tk, out_dtype, resid=None, layer=0):
    P, K = x.shape
    N = ws[0].shape[-1]
    tk = min(tk, K)
    tn = min(tn, N)
    assert P % tm == 0 and K % tk == 0
    sb = min(SUB_ROWS, tm)
    assert tm % sb == 0
    n_w = len(ws)
    ni, nj, nk = P // tm, pl.cdiv(N, tn), K // tk

    def x_idx(i, j, k, te, ns, src):
        return (src[i], jnp.where(ns[i] > 0, k, nk - 1))

    def w_idx(i, j, k, te, ns, src):
        act = ns[i] > 0
        return (layer, te[i], jnp.where(act, k, nk - 1), jnp.where(act, j, nj - 1))

    def o_idx(i, j, k, te, ns, src):
        return (i, j)

    in_specs = [pl.BlockSpec((tm, tk), x_idx)] + [pl.BlockSpec((None, 1, tk, tn), w_idx)] * n_w
    args = [x] + list(ws)
    if resid is not None:
        in_specs.append(pl.BlockSpec((tm, tn), o_idx))
        args.append(resid)
    return pl.pallas_call(
        functools.partial(_gmm_body, n_w, nk, tm // sb, sb, resid is not None),
        grid_spec=pltpu.PrefetchScalarGridSpec(
            num_scalar_prefetch=3, grid=(ni, nj, nk), in_specs=in_specs,
            out_specs=pl.BlockSpec((tm, tn), o_idx),
            scratch_shapes=[pltpu.VMEM((tm, tn), F32)] * n_w),
        out_shape=jax.ShapeDtypeStruct((P, N), out_dtype),
        compiler_params=pltpu.CompilerParams(
            dimension_semantics=("parallel", "parallel", "arbitrary"), vmem_limit_bytes=MM_VMEM_LIMIT),
        name=f"gmm{n_w}_{tm}x{tn}x{tk}",
    )(*tables, *args)


def _dense_tables(rows, tm):
    n = rows // tm
    return (jnp.zeros((n,), jnp.int32), jnp.full((n,), tm // min(SUB_ROWS, tm), jnp.int32),
            jnp.arange(n, dtype=jnp.int32))


def _row_tile(rows):
    for tm in (2048, 1024, 512, 256):
        if rows % tm == 0:
            return tm
    return rows


def matmul(x, w, out_dtype=F32, resid=None, tn=1024, tk=512, layer=None):
    lead = x.shape[:-1]
    x2 = x.reshape(-1, x.shape[-1])
    rows = x2.shape[0]
    tm = _row_tile(rows)
    r2 = None if resid is None else resid.reshape(rows, -1)
    w4 = w[None, None] if layer is None else w[:, None]
    out = grouped_matmul(x2, [w4], _dense_tables(rows, tm), tm, tn, tk, out_dtype, r2, layer or 0)
    return out.reshape(*lead, w.shape[-1])


def swiglu_up(x, w1, w3, tables, tm, layer, tf=512, tk=1024):
    return grouped_matmul(x, [w1, w3], tables, tm, tf, tk, BF16, layer=layer)


PAGES_PER_STEP = 8
REMOVED = -3e38


def bias_by_position(tab, past_len):
    bd = bias_by_distance(tab, REL_MAX_DIST + 1)
    H = bd.shape[0]
    near = bd[:, 1:REL_MAX_DIST + 1][:, ::-1]
    far = jnp.broadcast_to(bd[:, REL_MAX_DIST:], (H, past_len - REL_MAX_DIST))
    return jnp.concatenate([far, near], axis=1), bd[:, :1]


def _page_specs(lanes, lane_block, pages_of):
    def spec(r):
        return pl.BlockSpec((None, PAGE_SIZE, lanes), lambda *a: (pages_of(r)(*a), 0, lane_block))
    return [spec(r) for r in range(PAGES_PER_STEP)]


def _chunk_page(r):
    return lambda b, c, pt, *_: pt[b, c * PAGES_PER_STEP + r]


def _topk_rows(score, k):
    R, L = score.shape
    jf = lax.broadcasted_iota(jnp.int32, (R, L), 1).astype(F32)
    slot = lax.broadcasted_iota(jnp.int32, (R, LANES), 1)
    idx = jnp.zeros((R, LANES), F32)
    val = jnp.zeros((R, LANES), F32)
    for n in range(k):
        m = jnp.max(score, axis=1, keepdims=True)
        i = jnp.min(jnp.where(score == m, jf, 1e9), axis=1, keepdims=True)
        idx = jnp.where(slot == n, i, idx)
        val = jnp.where(slot == n, jnp.where(m > 0.5 * NEG, 1.0, 0.0), val)
        score = jnp.where(jf == i, REMOVED, score)
    return idx, val


def _gather_cmp_body(width, pt_ref, *refs):
    pages = refs[:PAGES_PER_STEP]
    pe_ref, o_ref = refs[PAGES_PER_STEP:]
    for r in range(PAGES_PER_STEP):
        for g in range(A_GROUPS):
            o_ref[0, g, r * PAGE_SIZE:(r + 1) * PAGE_SIZE, :] = pages[r][:, g * width:(g + 1) * width] + pe_ref[...]


def gather_compress_rows(cache, page_table, pe, width):
    B, n_pages = page_table.shape
    pe2 = jnp.tile(pe, (PAGE_SIZE // CMP_BLK, 1))
    return pl.pallas_call(
        functools.partial(_gather_cmp_body, width),
        grid_spec=pltpu.PrefetchScalarGridSpec(
            num_scalar_prefetch=1, grid=(B, n_pages // PAGES_PER_STEP),
            in_specs=_page_specs(A_GROUPS * width, 0, _chunk_page)
            + [pl.BlockSpec((PAGE_SIZE, width), lambda b, c, pt: (0, 0))],
            out_specs=pl.BlockSpec((1, A_GROUPS, PAGES_PER_STEP * PAGE_SIZE, width), lambda b, c, pt: (b, 0, c, 0))),
        out_shape=jax.ShapeDtypeStruct((B, A_GROUPS, n_pages * PAGE_SIZE, width), F32),
        compiler_params=pltpu.CompilerParams(dimension_semantics=("parallel", "parallel"),
                                             vmem_limit_bytes=VMEM_LIMIT),
        name="gather_compress_rows",
    )(page_table, *([cache] * PAGES_PER_STEP), pe2)


def _nsa_sample_a_body(q_ref, kc_ref, vc_ref, wk_ref, wv_ref, kn_ref, vn_ref, bw_ref, oc_ref, ow_ref, idx_ref, val_ref):
    scale = A_DK ** -0.5
    q = q_ref[0, 0]
    qb = q.astype(BF16)
    nc = kc_ref.shape[2]
    s = lax.dot_general(qb, kc_ref[0, 0].astype(BF16), _NT, preferred_element_type=F32) * scale
    e = jnp.exp(s - jnp.max(s, axis=1, keepdims=True))
    p = e / jnp.sum(e, axis=1, keepdims=True)
    oc_ref[0, 0] = jnp.dot(p.astype(BF16), vc_ref[0, 0].astype(BF16), preferred_element_type=F32)
    imp = jnp.concatenate([jnp.sum(p, axis=0, keepdims=True), jnp.zeros((1, LANES), F32)], axis=1)
    j = lax.broadcasted_iota(jnp.int32, imp.shape, 1)
    own = nc
    forced = (j == 0) | (j == own) | (j == own - 1)
    score = jnp.where(forced, FORCE, imp)
    score = jnp.where(j <= own, score, REMOVED)
    idx, val = _topk_rows(score, N_SEL)
    idx_ref[0, 0] = idx.astype(jnp.int32)
    val_ref[0, 0] = val.astype(jnp.int32)
    wb = wk_ref.shape[2]
    sw = lax.dot_general(qb, wk_ref[0, 0].astype(BF16), _NT, preferred_element_type=F32) * scale + bw_ref[0][:, :wb]
    dist = wb - lax.broadcasted_iota(jnp.int32, sw.shape, 1)
    in_win = dist < WINDOW
    sn = jnp.sum(q * kn_ref[0, 0], axis=1, keepdims=True) * scale + bw_ref[0][:, wb:wb + 1]
    m = jnp.maximum(jnp.max(jnp.where(in_win, sw, NEG), axis=1, keepdims=True), sn)
    ew = jnp.where(in_win, jnp.exp(sw - m), 0.0)
    en = jnp.exp(sn - m)
    l = jnp.sum(ew, axis=1, keepdims=True) + en
    ow_ref[0, 0] = (jnp.dot(ew.astype(BF16), wv_ref[0, 0].astype(BF16), preferred_element_type=F32)
                    + en * vn_ref[0, 0]) / l


def _nsa_sample_b_body(nc, idx_ref, val_ref, pt_ref, q_ref, ka_ref, kb_ref, va_ref, vb_ref, ba_ref, bb_ref,
                       kn_ref, vn_ref, b0_ref, o_ref, m_ref, l_ref, acc_ref):
    b = pl.program_id(0)
    n = pl.program_id(1)
    scale = A_DK ** -0.5

    @pl.when(n == 0)
    def _():
        m_ref[...] = jnp.full(m_ref.shape, NEG, F32)
        l_ref[...] = jnp.zeros(l_ref.shape, F32)
        acc_ref[...] = jnp.zeros(acc_ref.shape, F32)

    for g, (k_ref, v_ref, bias_ref) in enumerate(((ka_ref, va_ref, ba_ref), (kb_ref, vb_ref, bb_ref))):
        slot = (b * A_GROUPS + g) * N_SEL + n
        cached = (val_ref[slot] > 0) & (idx_ref[slot] < nc)

        @pl.when(cached)
        def _(g=g, k_ref=k_ref, v_ref=v_ref, bias_ref=bias_ref):
            lo_k = (A_GROUPS + g) * A_DK
            lo_v = (A_GROUPS + g) * A_DV
            k = k_ref[:, lo_k:lo_k + A_DK].astype(BF16)
            v = v_ref[:, lo_v:lo_v + A_DV].astype(BF16)
            s = lax.dot_general(q_ref[0, g].astype(BF16), k, _NT, preferred_element_type=F32) * scale + bias_ref[0]
            m_prev = m_ref[g][:, :1]
            m_new = jnp.maximum(m_prev, jnp.max(s, axis=1, keepdims=True))
            p = jnp.exp(s - m_new)
            alpha = jnp.exp(m_prev - m_new)
            l_ref[g] = jnp.broadcast_to(alpha * l_ref[g][:, :1] + jnp.sum(p, axis=1, keepdims=True), l_ref.shape[1:])
            acc_ref[g] = alpha * acc_ref[g] + jnp.dot(p.astype(BF16), v, preferred_element_type=F32)
            m_ref[g] = jnp.broadcast_to(m_new, m_ref.shape[1:])

    @pl.when(n == N_SEL - 1)
    def _():
        for g in range(A_GROUPS):
            sn = jnp.sum(q_ref[0, g] * kn_ref[0, g], axis=1, keepdims=True) * scale + b0_ref[g]
            m_prev = m_ref[g][:, :1]
            m_new = jnp.maximum(m_prev, sn)
            alpha = jnp.exp(m_prev - m_new)
            en = jnp.exp(sn - m_new)
            l = alpha * l_ref[g][:, :1] + en
            o_ref[0, g] = (alpha * acc_ref[g] + en * vn_ref[0, g]) / l


def nsa_decode(q, k3, v3, gates, cache_k, cache_v, win_k, win_v, page_table,
               pe_k, pe_v, wk1, wk2, wv1, wv2, tab_a):
    B, S, G, HPG, DK = q.shape
    n_pool = cache_k.shape[0]
    n_pages = page_table.shape[1]
    P = n_pages * PAGE_SIZE
    nc = P // CMP_BLK
    assert S == 1 and G == A_GROUPS and P % CMP_BLK == 0 and (P + S) // CMP_BLK == nc and win_k.shape[1] == WINDOW
    ck = cache_k.reshape(n_pool, PAGE_SIZE, 2 * G * DK)
    cv = cache_v.reshape(n_pool, PAGE_SIZE, 2 * G * A_DV)
    rows_k = gather_compress_rows(ck, page_table, pe_k, DK).reshape(B, G, nc, CMP_BLK * DK)
    rows_v = gather_compress_rows(cv, page_table, pe_v, A_DV).reshape(B, G, nc, CMP_BLK * A_DV)
    kc = matmul(jax.nn.silu(matmul(rows_k, wk1)), wk2)
    vc = matmul(jax.nn.silu(matmul(rows_v, wv1)), wv2)
    qh = q.reshape(B, G, HPG, DK)
    brev, b0 = bias_by_position(tab_a, P)
    bw = jnp.concatenate([brev[:, P - WINDOW:], b0, jnp.zeros((A_HEADS, LANES - 1), F32)], axis=1)
    bw = bw.reshape(G, HPG, WINDOW + LANES)
    new_k = jnp.moveaxis(k3[:, 0], 1, 2)
    new_v = jnp.moveaxis(v3[:, 0], 1, 2)
    per_bg = lambda *shape: pl.BlockSpec((1, 1) + shape, lambda b, g: (b, g) + (0,) * len(shape))
    o_c, o_w, idx, val = pl.pallas_call(
        _nsa_sample_a_body,
        grid=(B, G),
        in_specs=[per_bg(HPG, DK), per_bg(nc, DK), per_bg(nc, A_DV), per_bg(WINDOW, DK), per_bg(WINDOW, A_DV),
                  per_bg(1, DK), per_bg(1, A_DV),
                  pl.BlockSpec((1, HPG, WINDOW + LANES), lambda b, g: (g, 0, 0))],
        out_specs=[per_bg(HPG, A_DV), per_bg(HPG, A_DV), per_bg(1, LANES), per_bg(1, LANES)],
        out_shape=[jax.ShapeDtypeStruct((B, G, HPG, A_DV), F32)] * 2
        + [jax.ShapeDtypeStruct((B, G, 1, LANES), jnp.int32)] * 2,
        compiler_params=pltpu.CompilerParams(dimension_semantics=("parallel", "parallel"),
                                             vmem_limit_bytes=VMEM_LIMIT),
        name="nsa_sample_cmp_win",
    )(qh, kc, vc, jnp.moveaxis(win_k, 1, 2), jnp.moveaxis(win_v, 1, 2), new_k[:, :, 2:3], new_v[:, :, 2:3], bw)
    idx_flat = idx[:, :, 0, :N_SEL].reshape(-1)
    val_flat = val[:, :, 0, :N_SEL].reshape(-1)
    bsel = brev.reshape(G, HPG, nc, CMP_BLK).transpose(0, 2, 1, 3)

    def blk(g):
        return lambda b, n, ix, vl, pt: jnp.minimum(ix[(b * G + g) * N_SEL + n], nc - 1)

    def kv_spec(g, lanes):
        half = PAGE_SIZE // CMP_BLK
        return pl.BlockSpec((None, CMP_BLK, lanes),
                            lambda b, n, ix, vl, pt: (pt[b, blk(g)(b, n, ix, vl, pt) // half],
                                                       blk(g)(b, n, ix, vl, pt) % half, 0))

    def bias_spec(g):
        return pl.BlockSpec((None, 1, HPG, CMP_BLK), lambda b, n, ix, vl, pt: (g, blk(g)(b, n, ix, vl, pt), 0, 0))
    whole = lambda *shape: pl.BlockSpec((1,) + shape, lambda b, n, ix, vl, pt: (b,) + (0,) * len(shape))
    o_s = pl.pallas_call(
        functools.partial(_nsa_sample_b_body, nc),
        grid_spec=pltpu.PrefetchScalarGridSpec(
            num_scalar_prefetch=3, grid=(B, N_SEL),
            in_specs=[whole(G, HPG, DK), kv_spec(0, 2 * G * DK), kv_spec(1, 2 * G * DK),
                      kv_spec(0, 2 * G * A_DV), kv_spec(1, 2 * G * A_DV), bias_spec(0), bias_spec(1),
                      whole(G, 1, DK), whole(G, 1, A_DV),
                      pl.BlockSpec((G, HPG, 1), lambda b, n, ix, vl, pt: (0, 0, 0))],
            out_specs=whole(G, HPG, A_DV),
            scratch_shapes=[pltpu.VMEM((G, HPG, LANES), F32), pltpu.VMEM((G, HPG, LANES), F32),
                            pltpu.VMEM((G, HPG, A_DV), F32)]),
        out_shape=jax.ShapeDtypeStruct((B, G, HPG, A_DV), F32),
        compiler_params=pltpu.CompilerParams(dimension_semantics=("parallel", "arbitrary"),
                                             vmem_limit_bytes=VMEM_LIMIT),
        name="nsa_sample_selected",
    )(idx_flat, val_flat, page_table, qh, ck, ck, cv, cv, bsel, bsel, new_k[:, :, 1:2], new_v[:, :, 1:2],
      b0.reshape(G, HPG, 1))
    shp = (B, S, G, HPG, A_DV)
    return nsa_combine(gates, o_c.reshape(shp), o_s.reshape(shp), o_w.reshape(shp))


def _dsa_scores_body(pt_ref, *refs):
    pages = refs[:PAGES_PER_STEP]
    qi_ref, w_ref, o_ref = refs[PAGES_PER_STEP:]
    qi = qi_ref[0].astype(BF16)
    w = w_ref[0] * (IDX_HEADS ** -0.5)
    for r in range(PAGES_PER_STEP):
        d = lax.dot_general(qi, pages[r][...].astype(BF16), _NT, preferred_element_type=F32) * (IDX_DIM ** -0.5)
        o_ref[0, :, r * PAGE_SIZE:(r + 1) * PAGE_SIZE] = jnp.sum(jnp.maximum(d, 0.0) * w, axis=0, keepdims=True)


def _dsa_sample_select_body(topk, s_ref, qi_ref, w_ref, kn_ref, m_ref):
    B, P = s_ref.shape
    w = w_ref[...] * (IDX_HEADS ** -0.5)
    dn = jnp.sum(qi_ref[...] * kn_ref[...], axis=2) * (IDX_DIM ** -0.5)
    s_new = jnp.sum(jnp.maximum(dn, 0.0) * w, axis=1, keepdims=True)

    def order_key(x):
        bits = pltpu.bitcast(x, jnp.int32)
        return jnp.where(bits < 0, bits ^ jnp.int32(0x7FFFFFFF), bits)
    key = order_key(s_ref[...])
    key_new = order_key(jnp.broadcast_to(s_new, (B, LANES)))[:, :1]
    pos = lax.broadcasted_iota(jnp.int32, (B, P), 1)

    def count(cond, cond_new):
        return _count(cond) + jnp.where(cond_new, 1.0, 0.0)

    def value_step(i, lo):
        cand = lo + jnp.left_shift(jnp.int32(1), 31 - i)
        return jnp.where(count(key >= cand, key_new >= cand) >= float(topk), cand, lo)
    thr = lax.fori_loop(0, 32, value_step, jnp.full((B, 1), jnp.int32(-2 ** 31), jnp.int32))
    need = float(topk) - count(key > thr, key_new > thr)
    tied = key == thr
    n_bits = max(1, (P - 1).bit_length())

    def index_step(i, m):
        cand = m + jnp.left_shift(jnp.int32(1), n_bits - 1 - i)
        return jnp.where(_count(tied & (pos < cand)) < need, cand, m)
    last = lax.fori_loop(0, n_bits, index_step, jnp.zeros((B, 1), jnp.int32))
    sel = (key > thr) | (tied & (pos <= last))
    taken = _count(sel)
    sel_new = (key_new > thr) | ((key_new == thr) & (taken < float(topk)))
    m_ref[:, :P] = jnp.where(sel, 1.0, 0.0)
    lane = lax.broadcasted_iota(jnp.int32, (B, LANES), 1)
    m_ref[:, P:] = jnp.where((lane == 0) & sel_new, 1.0, 0.0)


def _dsa_sample_attend_body(pt_ref, *refs):
    n = PAGES_PER_STEP
    kp, vp = refs[:n], refs[n:2 * n]
    q_ref, mask_ref, bias_ref, mnew_ref, kn_ref, vn_ref, b0_ref, o_ref, m_ref, l_ref, acc_ref = refs[2 * n:]
    c = pl.program_id(1)
    scale = B_DH ** -0.5

    @pl.when(c == 0)
    def _():
        m_ref[...] = jnp.full(m_ref.shape, NEG, F32)
        l_ref[...] = jnp.zeros(l_ref.shape, F32)
        acc_ref[...] = jnp.zeros(acc_ref.shape, F32)

    mask = mask_ref[0] > 0.5
    for g in range(B_GROUPS):
        sl = slice(g * B_DH, (g + 1) * B_DH)
        k = jnp.concatenate([kp[r][:, sl] for r in range(n)], axis=0).astype(BF16)
        v = jnp.concatenate([vp[r][:, sl] for r in range(n)], axis=0).astype(BF16)
        s = lax.dot_general(q_ref[0, g].astype(BF16), k, _NT, preferred_element_type=F32) * scale + bias_ref[g]
        s = jnp.where(mask, s, NEG)
        m_prev = m_ref[g][:, :1]
        m_new = jnp.maximum(m_prev, jnp.max(s, axis=1, keepdims=True))
        p = jnp.where(mask, jnp.exp(s - m_new), 0.0)
        alpha = jnp.exp(m_prev - m_new)
        l_ref[g] = jnp.broadcast_to(alpha * l_ref[g][:, :1] + jnp.sum(p, axis=1, keepdims=True), l_ref.shape[1:])
        acc_ref[g] = alpha * acc_ref[g] + jnp.dot(p.astype(BF16), v, preferred_element_type=F32)
        m_ref[g] = jnp.broadcast_to(m_new, m_ref.shape[1:])

    @pl.when(c == pl.num_programs(1) - 1)
    def _():
        new_on = mnew_ref[0][:, :1] > 0.5
        for g in range(B_GROUPS):
            sn = jnp.sum(q_ref[0, g] * kn_ref[0, g], axis=1, keepdims=True) * scale + b0_ref[g]
            sn = jnp.where(new_on, sn, NEG)
            m_prev = m_ref[g][:, :1]
            m_new = jnp.maximum(m_prev, sn)
            alpha = jnp.exp(m_prev - m_new)
            en = jnp.where(new_on, jnp.exp(sn - m_new), 0.0)
            l = alpha * l_ref[g][:, :1] + en
            o_ref[0, g] = (alpha * acc_ref[g] + en * vn_ref[0, g]) / l


def dsa_decode(q, k, v, qi, ki, wi, cache_k, cache_v, cache_kidx, page_table, tab_b):
    B, S, G, HPG, DH = q.shape
    n_pool = cache_k.shape[0]
    n_pages = page_table.shape[1]
    P = n_pages * PAGE_SIZE
    n_chunks = n_pages // PAGES_PER_STEP
    chunk = PAGES_PER_STEP * PAGE_SIZE
    assert S == 1 and n_pages % PAGES_PER_STEP == 0
    topk = min(IDX_TOPK, (P + S) // 4)
    params = pltpu.CompilerParams(dimension_semantics=("parallel", "arbitrary"), vmem_limit_bytes=VMEM_LIMIT)
    scores = pl.pallas_call(
        _dsa_scores_body,
        grid_spec=pltpu.PrefetchScalarGridSpec(
            num_scalar_prefetch=1, grid=(B, n_chunks),
            in_specs=_page_specs(IDX_DIM, 0, _chunk_page)
            + [pl.BlockSpec((1, IDX_HEADS, IDX_DIM), lambda b, c, pt: (b, 0, 0)),
               pl.BlockSpec((1, IDX_HEADS, 1), lambda b, c, pt: (b, 0, 0))],
            out_specs=pl.BlockSpec((1, 1, chunk), lambda b, c, pt: (b, 0, c))),
        out_shape=jax.ShapeDtypeStruct((B, 1, P), F32),
        compiler_params=params, name="dsa_sample_scores",
    )(page_table, *([cache_kidx] * PAGES_PER_STEP), qi[:, 0], wi[:, 0, :, None])
    sel = pl.pallas_call(
        functools.partial(_dsa_sample_select_body, topk),
        out_shape=jax.ShapeDtypeStruct((B, P + LANES), F32),
        compiler_params=pltpu.CompilerParams(vmem_limit_bytes=VMEM_LIMIT), name="dsa_sample_select",
    )(scores.reshape(B, P), qi[:, 0], wi[:, 0], ki)
    sel = sel.reshape(B, 1, P + LANES)
    brev, b0 = bias_by_position(tab_b, P)
    ck = cache_k.reshape(n_pool, PAGE_SIZE, G * DH)
    cv = cache_v.reshape(n_pool, PAGE_SIZE, G * DH)
    whole = lambda *shape: pl.BlockSpec((1,) + shape, lambda b, c, pt: (b,) + (0,) * len(shape))
    o = pl.pallas_call(
        _dsa_sample_attend_body,
        grid_spec=pltpu.PrefetchScalarGridSpec(
            num_scalar_prefetch=1, grid=(B, n_chunks),
            in_specs=_page_specs(G * DH, 0, _chunk_page) + _page_specs(G * DH, 0, _chunk_page)
            + [whole(G, HPG, DH),
               pl.BlockSpec((1, 1, chunk), lambda b, c, pt: (b, 0, c)),
               pl.BlockSpec((G, HPG, chunk), lambda b, c, pt: (0, 0, c)),
               pl.BlockSpec((1, 1, LANES), lambda b, c, pt: (b, 0, P // LANES)),
               whole(G, 1, DH), whole(G, 1, DH),
               pl.BlockSpec((G, HPG, 1), lambda b, c, pt: (0, 0, 0))],
            out_specs=whole(G, HPG, DH),
            scratch_shapes=[pltpu.VMEM((G, HPG, LANES), F32), pltpu.VMEM((G, HPG, LANES), F32),
                            pltpu.VMEM((G, HPG, DH), F32)]),
        out_shape=jax.ShapeDtypeStruct((B, G, HPG, DH), F32),
        compiler_params=params, name="dsa_sample_attend",
    )(page_table, *([ck] * PAGES_PER_STEP), *([cv] * PAGES_PER_STEP), q[:, 0], sel, brev.reshape(G, HPG, P), sel,
      jnp.moveaxis(k, 1, 2), jnp.moveaxis(v, 1, 2), b0.reshape(G, HPG, 1))
    return o.reshape(B, S, G * HPG * DH)


def _moba_kmean_body(pt_ref, p0_ref, p1_ref, o_ref):
    o_ref[0] = (jnp.sum(p0_ref[...], axis=0, keepdims=True)
                + jnp.sum(p1_ref[...], axis=0, keepdims=True)) * (1.0 / MOBA_BLK)


def _moba_sample_pick_body(nf, q_ref, km_ref, idx_ref, val_ref):
    B, H, DH = q_ref.shape
    nfp = km_ref.shape[1]
    group = lax.broadcasted_iota(jnp.int32, (H, nfp), 0) // C_HPG
    lane = lax.broadcasted_iota(jnp.int32, (H, nfp), 1)
    for b in range(B):
        qb = q_ref[b].astype(BF16)
        s = jnp.full((H, nfp), REMOVED, F32)
        for g in range(C_GROUPS):
            kg = km_ref[b, :, g * DH:(g + 1) * DH].astype(BF16)
            sg = lax.dot_general(qb, kg, _NT, preferred_element_type=F32)
            s = jnp.where((group == g) & (lane < nf), sg, s)
        idx, val = _topk_rows(s, MOBA_TOP)
        idx_ref[b] = idx.astype(jnp.int32)
        val_ref[b] = val.astype(jnp.int32)


def _moba_sample_attend_body(idx_ref, val_ref, pt_ref, q_ref, k0_ref, k1_ref, v0_ref, v1_ref, bias_ref,
                             kn_ref, vn_ref, b0_ref, o_ref, m_ref, l_ref, acc_ref):
    b, h, r = pl.program_id(0), pl.program_id(1), pl.program_id(2)
    scale = C_DH ** -0.5
    q = q_ref[0, 0]

    @pl.when(r == 0)
    def _():
        m_ref[...] = jnp.full(m_ref.shape, NEG, F32)
        l_ref[...] = jnp.zeros(l_ref.shape, F32)
        acc_ref[...] = jnp.zeros(acc_ref.shape, F32)

    @pl.when(val_ref[(b * C_HEADS + h) * MOBA_TOP + r] > 0)
    def _():
        k = jnp.concatenate([k0_ref[...], k1_ref[...]], axis=0).astype(BF16)
        v = jnp.concatenate([v0_ref[...], v1_ref[...]], axis=0).astype(BF16)
        s = lax.dot_general(q.astype(BF16), k, _NT, preferred_element_type=F32) * scale + bias_ref[0]
        m_prev = m_ref[:, :1]
        m_new = jnp.maximum(m_prev, jnp.max(s, axis=1, keepdims=True))
        p = jnp.exp(s - m_new)
        alpha = jnp.exp(m_prev - m_new)
        l_ref[...] = jnp.broadcast_to(alpha * l_ref[:, :1] + jnp.sum(p, axis=1, keepdims=True), l_ref.shape)
        acc_ref[...] = alpha * acc_ref[...] + jnp.dot(p.astype(BF16), v, preferred_element_type=F32)
        m_ref[...] = jnp.broadcast_to(m_new, m_ref.shape)

    @pl.when(r == MOBA_TOP - 1)
    def _():
        sn = jnp.sum(q * kn_ref[0, 0], axis=1, keepdims=True) * scale + b0_ref[0]
        m_prev = m_ref[:, :1]
        m_new = jnp.maximum(m_prev, sn)
        alpha = jnp.exp(m_prev - m_new)
        en = jnp.exp(sn - m_new)
        o_ref[0, 0] = (alpha * acc_ref[...] + en * vn_ref[0, 0]) / (alpha * l_ref[:, :1] + en)


def moba_decode(q, k, v, cache_k, cache_v, page_table, tab_c):
    B, S, H, DH = q.shape
    n_pool = cache_k.shape[0]
    n_pages = page_table.shape[1]
    P = n_pages * PAGE_SIZE
    ppb = MOBA_BLK // PAGE_SIZE
    nf = (P + S) // MOBA_BLK
    assert S == 1 and ppb == 2 and P % MOBA_BLK == 0 and nf == P // MOBA_BLK and nf >= MOBA_TOP
    ck = cache_k.reshape(n_pool, PAGE_SIZE, C_GROUPS * DH)
    cv = cache_v.reshape(n_pool, PAGE_SIZE, C_GROUPS * DH)
    kmean = pl.pallas_call(
        _moba_kmean_body,
        grid_spec=pltpu.PrefetchScalarGridSpec(
            num_scalar_prefetch=1, grid=(B, nf),
            in_specs=[pl.BlockSpec((None, PAGE_SIZE, C_GROUPS * DH), lambda b, j, pt: (pt[b, ppb * j], 0, 0)),
                      pl.BlockSpec((None, PAGE_SIZE, C_GROUPS * DH), lambda b, j, pt: (pt[b, ppb * j + 1], 0, 0))],
            out_specs=pl.BlockSpec((1, 1, C_GROUPS * DH), lambda b, j, pt: (b * nf + j, 0, 0))),
        out_shape=jax.ShapeDtypeStruct((B * nf, 1, C_GROUPS * DH), F32),
        compiler_params=pltpu.CompilerParams(dimension_semantics=("parallel", "parallel"),
                                             vmem_limit_bytes=VMEM_LIMIT),
        name="moba_sample_kmean",
    )(page_table, ck, ck).reshape(B, nf, C_GROUPS * DH)
    idx, val = pl.pallas_call(
        functools.partial(_moba_sample_pick_body, nf),
        out_shape=[jax.ShapeDtypeStruct((B, H, LANES), jnp.int32)] * 2,
        compiler_params=pltpu.CompilerParams(vmem_limit_bytes=VMEM_LIMIT), name="moba_sample_pick",
    )(q[:, 0], jnp.pad(kmean, ((0, 0), (0, -nf % LANES), (0, 0))))
    idx_flat = idx[:, :, :MOBA_TOP].reshape(-1)
    val_flat = val[:, :, :MOBA_TOP].reshape(-1)
    brev, b0 = bias_by_position(tab_c, P)

    def blk(b, h, r, ix):
        return jnp.minimum(ix[(b * H + h) * MOBA_TOP + r], nf - 1)

    def page_spec(which):
        return pl.BlockSpec((None, PAGE_SIZE, DH),
                            lambda b, h, r, ix, vl, pt: (pt[b, ppb * blk(b, h, r, ix) + which], 0, h // C_HPG))
    per_head = lambda rows, d: pl.BlockSpec((1, 1, rows, d), lambda b, h, r, ix, vl, pt: (b, h, 0, 0))
    per_group = lambda d: pl.BlockSpec((1, 1, 1, d), lambda b, h, r, ix, vl, pt: (b, h // C_HPG, 0, 0))
    o = pl.pallas_call(
        _moba_sample_attend_body,
        grid_spec=pltpu.PrefetchScalarGridSpec(
            num_scalar_prefetch=3, grid=(B, H, MOBA_TOP),
            in_specs=[per_head(1, DH), page_spec(0), page_spec(1), page_spec(0), page_spec(1),
                      pl.BlockSpec((1, 1, MOBA_BLK), lambda b, h, r, ix, vl, pt: (h * nf + blk(b, h, r, ix), 0, 0)),
                      per_group(DH), per_group(DH),
                      pl.BlockSpec((1, 1, 1), lambda b, h, r, ix, vl, pt: (h, 0, 0))],
            out_specs=per_head(1, DH),
            scratch_shapes=[pltpu.VMEM((1, LANES), F32), pltpu.VMEM((1, LANES), F32), pltpu.VMEM((1, DH), F32)]),
        out_shape=jax.ShapeDtypeStruct((B, H, 1, DH), F32),
        compiler_params=pltpu.CompilerParams(dimension_semantics=("parallel", "parallel", "arbitrary"),
                                             vmem_limit_bytes=VMEM_LIMIT),
        name="moba_sample_attend",
    )(idx_flat, val_flat, page_table, q[:, 0, :, None, :], ck, ck, cv, cv, brev.reshape(H * nf, 1, MOBA_BLK),
      jnp.moveaxis(k, 1, 2), jnp.moveaxis(v, 1, 2), b0.reshape(H, 1, 1))
    return o.reshape(B, S, H * DH)


def split_cols(z, sizes):
    cuts = [int(c) for c in np.cumsum(sizes)[:-1]]
    return jnp.split(z, cuts, axis=-1)


def masked_softmax(logits, mask):
    p = jax.nn.softmax(jnp.where(mask, logits, NEG), axis=-1)
    return jnp.where(mask, p, 0.0)


def t5_bucket(dist):
    n = jnp.maximum(dist, 0)
    nf = jnp.maximum(n, 1).astype(jnp.float32)
    large = REL_EXACT + (jnp.log(nf / REL_EXACT) / math.log(REL_MAX_DIST / REL_EXACT)
                         * (N_BUCKETS - REL_EXACT)).astype(jnp.int32)
    return jnp.where(n < REL_EXACT, n, jnp.minimum(large, N_BUCKETS - 1))


def gather_paged(pool, page_table, new_rows, pos, *extra):
    past_len = page_table.shape[1] * PAGE_SIZE
    b = jnp.arange(pos.shape[0]).reshape((-1,) + (1,) * (pos.ndim - 1))
    pc = jnp.clip(pos, 0, past_len - 1)
    phys = page_table[b, pc // PAGE_SIZE]
    old = pool[(phys, pc % PAGE_SIZE) + extra]
    new = new_rows[(b, jnp.clip(pos - past_len, 0, new_rows.shape[1] - 1)) + extra]
    is_new = (pos >= past_len).reshape(pos.shape + (1,) * (old.ndim - pos.ndim))
    return jnp.where(is_new, new, old)


def nsa_compress(rows, pe, w1, w2):
    B, L, G, d = rows.shape
    nc = L // CMP_BLK
    blk = rows[:, : nc * CMP_BLK].reshape(B, nc, CMP_BLK, G, d) + pe[None, None, :, None, :]
    flat = blk.transpose(0, 1, 3, 2, 4).reshape(B, nc, G, CMP_BLK * d)
    return matmul(jax.nn.silu(matmul(flat, w1)), w2)


def nsa_cmp_branch(q, q_pos, kc, vc):
    nc = kc.shape[1]
    logits = jnp.einsum('bqghd,bjgd->bqghj', q, kc, preferred_element_type=jnp.float32) * A_DK ** -0.5
    vis = (jnp.arange(nc) + 1) * CMP_BLK <= (q_pos + 1)[:, None]
    p = masked_softmax(logits, vis[None, :, None, None, :])
    o = jnp.einsum('bqghj,bjgd->bqghd', p.astype(vc.dtype), vc)
    return o, p.sum(axis=3)


def nsa_pick_blocks(imp, q_pos, n_blocks):
    score = jnp.pad(imp, ((0, 0), (0, 0), (0, 0), (0, n_blocks - imp.shape[-1])))
    j = jnp.arange(n_blocks)[None, :]
    own = (q_pos // CMP_BLK)[:, None]
    forced = (j == 0) | (j == own) | (j == own - 1)
    score = jnp.where(forced[None, :, None, :], FORCE, score)
    score = jnp.where((j <= own)[None, :, None, :], score, NEG)
    top_s, idx = lax.top_k(score, min(N_SEL, n_blocks))
    return idx, top_s > 0.5 * NEG


def nsa_sel_attend(q, q_pos, idx, valid, ksel, vsel, tab_a):
    B, Q, G, HPG, _ = q.shape
    kpos = idx[..., None] * CMP_BLK + jnp.arange(CMP_BLK)
    dist = q_pos[None, :, None, None, None] - kpos
    mask = valid[..., None] & (dist >= 0)
    tab2 = tab_a.reshape(N_BUCKETS, G, HPG).transpose(1, 0, 2)
    bias = tab2[jnp.arange(G).reshape(1, 1, G, 1, 1), t5_bucket(dist)]
    logits = (jnp.einsum('bqghd,bqgnsd->bqghns', q, ksel, preferred_element_type=jnp.float32) * A_DK ** -0.5
              + jnp.moveaxis(bias, -1, 3).astype(jnp.float32))
    shp = logits.shape
    p = masked_softmax(logits.reshape(B, Q, G, HPG, -1), mask.reshape(B, Q, G, 1, -1)).reshape(shp)
    return jnp.einsum('bqghns,bqgnsd->bqghd', p.astype(vsel.dtype), vsel)


def window_attend(q, q_pos, k, v, k_pos, tab_a):
    B, Q, G, HPG, _ = q.shape
    dist = q_pos[:, None] - k_pos[None, :]
    mask = (dist >= 0) & (dist < WINDOW) & (k_pos >= 0)[None, :]
    bias = tab_a[t5_bucket(dist)].reshape(Q, -1, G, HPG).transpose(0, 2, 3, 1).astype(jnp.float32)
    logits = jnp.einsum('bqghd,bsgd->bqghs', q, k, preferred_element_type=jnp.float32) * A_DK ** -0.5 + bias[None]
    p = masked_softmax(logits, mask[None, :, None, None, :])
    return jnp.einsum('bqghs,bsgd->bqghd', p.astype(v.dtype), v)


def nsa_combine(gates, o_c, o_s, o_w):
    g = gates[..., None].astype(o_c.dtype)
    o = g[:, :, 0] * o_c + g[:, :, 1] * o_s + g[:, :, 2] * o_w
    return o.reshape(o.shape[0], o.shape[1], -1)


def _heads_first(a):
    return jnp.moveaxis(a, 1, -2)


def nsa_prompt(q, k3, v3, gates, pe_k, pe_v, wk1, wk2, wv1, wv2, tab_a):
    B, T, G, HPG, DK = q.shape
    assert T % TQ == 0 and TQ == TK and (T // CMP_BLK) * CMP_BLK == T
    kc = nsa_compress(k3[:, :, 0], pe_k, wk1, wk2)
    vc = nsa_compress(v3[:, :, 0], pe_v, wv1, wv2)
    qh = _heads_first(q)
    o_c, sel_mask = nsa_cmp_select(qh, _heads_first(kc), _heads_first(vc))
    tiles = toeplitz_bias_tiles(tab_a)
    scale = A_DK ** -0.5
    o_s = block_attention('key', qh, _heads_first(k3[:, :, 1]), _heads_first(v3[:, :, 1]), tiles, sel_mask, scale)
    o_w = block_attention('window', qh, _heads_first(k3[:, :, 2]), _heads_first(v3[:, :, 2]), tiles, None, scale)
    shp = (B, T, G, HPG, A_DV)
    return nsa_combine(gates, o_c.reshape(shp), o_s.reshape(shp), o_w.reshape(shp))


def nsa_sample(q, k3, v3, gates, cache_k, cache_v, win_k, win_v, page_table,
               pe_k, pe_v, wk1, wk2, wv1, wv2, tab_a):
    B, S, G, HPG, DK = q.shape
    P = page_table.shape[1] * PAGE_SIZE
    L = P + S
    pos = P + jnp.arange(S)
    rows_k = jnp.concatenate([cache_k[page_table, :, 0].reshape(B, P, G, DK), k3[:, :, 0]], axis=1)
    rows_v = jnp.concatenate([cache_v[page_table, :, 0].reshape(B, P, G, A_DV), v3[:, :, 0]], axis=1)
    kc = nsa_compress(rows_k, pe_k, wk1, wk2)
    vc = nsa_compress(rows_v, pe_v, wv1, wv2)
    o_c, imp = nsa_cmp_branch(q, pos, kc, vc)
    idx, valid = nsa_pick_blocks(imp, pos, -(-L // CMP_BLK))
    kpos = idx[..., None] * CMP_BLK + jnp.arange(CMP_BLK)
    gi = jnp.arange(G).reshape(1, 1, G, 1, 1)
    ksel = gather_paged(cache_k, page_table, k3, kpos, 1, gi)
    vsel = gather_paged(cache_v, page_table, v3, kpos, 1, gi)
    o_s = nsa_sel_attend(q, pos, idx, valid, ksel, vsel, tab_a)
    wb = win_k.shape[1]
    kw = jnp.concatenate([win_k, k3[:, :, 2]], axis=1)
    vw = jnp.concatenate([win_v, v3[:, :, 2]], axis=1)
    o_w = window_attend(q, pos, kw, vw, P - wb + jnp.arange(wb + S), tab_a)
    return nsa_combine(gates, o_c, o_s, o_w)


def dsa_pick(qi, wi, ki, q_pos, topk):
    L = ki.shape[1]
    dots = jnp.einsum('bqhd,bld->bqhl', qi, ki, preferred_element_type=jnp.float32) * IDX_DIM ** -0.5
    score = jnp.einsum('bqhl,bqh->bql', jax.nn.relu(dots), wi.astype(jnp.float32) * IDX_HEADS ** -0.5)
    score = jnp.where(jnp.arange(L)[None, None, :] <= q_pos[None, :, None], score, NEG)
    _, idx = lax.top_k(score, topk)
    return idx, idx <= q_pos[None, :, None]


def dsa_attend(q, q_pos, idx, valid, ksel, vsel, tab_b):
    B, Q, G, HPG, DH = q.shape
    dist = q_pos[None, :, None] - idx
    bias = tab_b[t5_bucket(dist)].reshape(B, Q, -1, G, HPG).transpose(0, 1, 3, 4, 2).astype(jnp.float32)
    logits = jnp.einsum('bqghd,bqkgd->bqghk', q, ksel, preferred_element_type=jnp.float32) * DH ** -0.5 + bias
    p = masked_softmax(logits, valid[:, :, None, None, :])
    o = jnp.einsum('bqghk,bqkgd->bqghd', p.astype(vsel.dtype), vsel)
    return o.reshape(B, Q, -1)


def dsa_prompt(q, k, v, qi, ki, wi, tab_b):
    B, T = q.shape[:2]
    assert T % TQ == 0 and TQ == TK
    sel_mask = dsa_select(_heads_first(qi), ki, wi, min(IDX_TOPK, T // 4))
    return block_attention('key', _heads_first(q), _heads_first(k), _heads_first(v),
                           toeplitz_bias_tiles(tab_b), sel_mask, B_DH ** -0.5)


def dsa_sample(q, k, v, qi, ki, wi, cache_k, cache_v, cache_kidx, page_table, tab_b):
    B, S = q.shape[:2]
    P = page_table.shape[1] * PAGE_SIZE
    L = P + S
    q_pos = P + jnp.arange(S)
    ki_all = jnp.concatenate([cache_kidx[page_table].reshape(B, P, IDX_DIM), ki], axis=1)
    idx, valid = dsa_pick(qi, wi, ki_all, q_pos, min(IDX_TOPK, L // 4))
    return dsa_attend(q, q_pos, idx, valid, gather_paged(cache_k, page_table, k, idx),
                      gather_paged(cache_v, page_table, v, idx), tab_b)


def moba_pick(q, q_pos, kmean):
    B, Q, H, DH = q.shape
    nf = kmean.shape[1]
    s = jnp.einsum('bqghd,bjgd->bqghj', q.reshape(B, Q, C_GROUPS, C_HPG, DH).astype(jnp.float32),
                   kmean).reshape(B, Q, H, nf)
    ncand = max(nf, MOBA_TOP)
    s = jnp.pad(s, ((0, 0), (0, 0), (0, 0), (0, ncand - nf)), constant_values=NEG)
    past = jnp.arange(ncand)[None, :] < (q_pos // MOBA_BLK)[:, None]
    s = jnp.where(past[None, :, None, :], s, NEG)
    top_s, idx = lax.top_k(s, MOBA_TOP)
    return idx, top_s > 0.5 * NEG


def moba_attend(q, q_pos, idx, valid, ksel, vsel, own_pos, kown, vown, tab_c):
    B, Q, H, DH = q.shape
    scale = DH ** -0.5
    sel_dist = q_pos[None, :, None, None, None] - (idx[..., None] * MOBA_BLK + jnp.arange(MOBA_BLK))
    hidx = jnp.arange(H).reshape(1, 1, H, 1, 1)
    ls = (jnp.einsum('bqhd,bqhrsd->bqhrs', q, ksel, preferred_element_type=jnp.float32) * scale
          + tab_c.T[hidx, t5_bucket(sel_dist)].astype(jnp.float32))
    ls = jnp.where(valid[..., None], ls, NEG).reshape(B, Q, H, MOBA_TOP * MOBA_BLK)
    own_dist = q_pos[:, None] - own_pos
    lo = jnp.einsum('bqghd,bqgsd->bqghs', q.reshape(B, Q, C_GROUPS, C_HPG, DH), kown,
                    preferred_element_type=jnp.float32).reshape(B, Q, H, MOBA_BLK) * scale
    lo = lo + tab_c[t5_bucket(own_dist)].transpose(0, 2, 1)[None].astype(jnp.float32)
    lo = jnp.where((own_dist >= 0)[None, :, None, :], lo, NEG)
    p = jax.nn.softmax(jnp.concatenate([ls, lo], axis=-1), axis=-1)
    ps = p[..., : MOBA_TOP * MOBA_BLK].reshape(B, Q, H, MOBA_TOP, MOBA_BLK).astype(vsel.dtype)
    po = p[..., MOBA_TOP * MOBA_BLK:].reshape(B, Q, C_GROUPS, C_HPG, MOBA_BLK).astype(vown.dtype)
    o = (jnp.einsum('bqhrs,bqhrsd->bqhd', ps, vsel)
         + jnp.einsum('bqghs,bqgsd->bqghd', po, vown).reshape(B, Q, H, DH))
    return o


def moba_prompt(q, k, v, tab_c):
    B, T, H, DH = q.shape
    assert T % MOBA_BLK == 0 and TQ == MOBA_BLK and TK == MOBA_BLK
    qh = _heads_first(q.reshape(B, T, C_GROUPS, C_HPG, DH))
    kh, vh = _heads_first(k), _heads_first(v)
    flags = moba_select(qh, kh).transpose(0, 1, 3, 2)
    return block_attention('moba', qh, kh, vh, toeplitz_bias_tiles(tab_c), flags, DH ** -0.5)


def moba_sample(q, k, v, cache_k, cache_v, page_table, tab_c):
    B, S, H, DH = q.shape
    P = page_table.shape[1] * PAGE_SIZE
    L = P + S
    q_pos = P + jnp.arange(S)
    k_all = jnp.concatenate([cache_k[page_table].reshape(B, P, C_GROUPS, DH), k], axis=1)
    nf = L // MOBA_BLK
    kmean = k_all[:, : nf * MOBA_BLK].reshape(B, nf, MOBA_BLK, C_GROUPS, DH).astype(jnp.float32).mean(axis=2)
    idx, valid = moba_pick(q, q_pos, kmean)
    gh = (jnp.arange(H) // C_HPG).reshape(1, 1, H, 1, 1)
    sel_pos = idx[..., None] * MOBA_BLK + jnp.arange(MOBA_BLK)
    own_pos = (q_pos // MOBA_BLK)[:, None] * MOBA_BLK + jnp.arange(MOBA_BLK)
    opos = jnp.broadcast_to(own_pos[None, :, None, :], (B, S, C_GROUPS, MOBA_BLK))
    gi = jnp.arange(C_GROUPS).reshape(1, 1, C_GROUPS, 1)
    o = moba_attend(q, q_pos, idx, valid,
                    gather_paged(cache_k, page_table, k, sel_pos, gh), gather_paged(cache_v, page_table, v, sel_pos, gh),
                    own_pos, gather_paged(cache_k, page_table, k, opos, gi), gather_paged(cache_v, page_table, v, opos, gi),
                    tab_c)
    return o.reshape(B, S, H * DH)


def mem_kv(mem, g, wk, wv, layer):
    m = rms_norm(mem, g, BF16)
    B = mem.shape[0]
    return (matmul(m, wk, layer=layer).reshape(B, N_MEM, X_HEADS, X_DH),
            matmul(m, wv, layer=layer).reshape(B, N_MEM, X_HEADS, X_DH))


def cross_attend(x, h, mk, mv, wq, wo, layer):
    B, T = h.shape[:2]
    q = matmul(h, wq, layer=layer).reshape(B, T, X_HEADS, X_DH)
    logits = jnp.einsum('bthd,bmhd->bthm', q, mk, preferred_element_type=jnp.float32) * X_DH ** -0.5
    p = jax.nn.softmax(logits, axis=-1)
    o = jnp.einsum('bthm,bmhd->bthd', p.astype(mv.dtype), mv).reshape(B, T, X_HEADS * X_DH)
    return matmul(o, wo, resid=x, layer=layer)


def dense_swiglu(x, h, w1, w3, w2, layer):
    rows = h.shape[0]
    tm = _row_tile(rows)
    tables = _dense_tables(rows, tm)
    g = swiglu_up(h, w1[:, None], w3[:, None], tables, tm, layer)
    return grouped_matmul(g, [w2[:, None]], tables, tm, 1024, 512, F32, resid=x, layer=layer)


MOE_TM = 2304


def moe_tables(top_e):
    A = top_e.size
    n_tiles = -(-(A + N_EXPERTS * (MOE_TM - 1)) // MOE_TM)
    e_flat = top_e.reshape(A).astype(jnp.int32)
    order = jnp.argsort(e_flat).astype(jnp.int32)
    counts = jnp.bincount(e_flat, length=N_EXPERTS).astype(jnp.int32)
    starts = jnp.cumsum(counts) - counts
    tiles_per = (counts + MOE_TM - 1) // MOE_TM
    tile_end = jnp.cumsum(tiles_per)
    tile_start = tile_end - tiles_per
    e_sorted = e_flat[order]
    prow_sorted = tile_start[e_sorted] * MOE_TM + (jnp.arange(A, dtype=jnp.int32) - starts[e_sorted])
    row_token = (jnp.arange(n_tiles * MOE_TM, dtype=jnp.int32) % (A // TOP_K)).at[prow_sorted].set(order // TOP_K)
    prow_of_assign = jnp.zeros((A,), jnp.int32).at[order].set(prow_sorted)
    ti = jnp.arange(n_tiles, dtype=jnp.int32)
    n_active = tile_end[-1]
    last = n_active - 1
    src = jnp.minimum(ti, last)
    te = jnp.minimum(jnp.searchsorted(tile_end, src, side='right').astype(jnp.int32), N_EXPERTS - 1)
    rows_in = jnp.clip(counts[te] - (src - tile_start[te]) * MOE_TM, 0, MOE_TM)
    nsb = jnp.where(ti < n_active, (rows_in + SUB_ROWS - 1) // SUB_ROWS, 0).astype(jnp.int32)
    return (te, nsb, src), row_token, prow_of_assign


def moe_swiglu(xs, hs, w_router, b_router, w1, w3, w2, layer):
    wr = jnp.pad(w_router, ((0, 0), (0, LANES - N_EXPERTS)))
    logits = jnp.concatenate([matmul(h, wr)[:, :N_EXPERTS] for h in hs], axis=0) + b_router.astype(jnp.float32)
    x = jnp.concatenate(xs, axis=0)
    h = jnp.concatenate(hs, axis=0)
    N = h.shape[0]
    top_l, top_e = lax.top_k(logits, TOP_K)
    gate = jax.nn.softmax(top_l, axis=-1)
    tables, row_token, prow_of_assign = moe_tables(top_e)
    g = swiglu_up(h[row_token], w1, w3, tables, MOE_TM, layer)
    y = grouped_matmul(g, [w2], tables, MOE_TM, 1024, 512, F32, layer=layer)
    pair = y[prow_of_assign.reshape(N, TOP_K)] * gate[:, :, None]
    out = x + jnp.sum(pair, axis=1)
    cuts = [int(c) for c in np.cumsum([a.shape[0] for a in xs])[:-1]]
    return jnp.split(out, cuts, axis=0)


def even_split(z):
    B, T = z.shape[:2]
    qa, ka, va, ga, qb, kb, vb, qi, ki, wi = split_cols(z, EVEN_SPLITS)
    return (qa.reshape(B, T, A_GROUPS, A_HPG, A_DK),
            ka.reshape(B, T, 3, A_GROUPS, A_DK),
            va.reshape(B, T, 3, A_GROUPS, A_DV),
            jax.nn.sigmoid(ga.astype(jnp.float32)).reshape(B, T, 3, A_GROUPS, A_HPG),
            qb.reshape(B, T, B_GROUPS, B_HPG, B_DH),
            kb.reshape(B, T, B_GROUPS, B_DH),
            vb.reshape(B, T, B_GROUPS, B_DH),
            qi.reshape(B, T, IDX_HEADS, IDX_DIM), ki, wi)


def odd_split(z):
    B, T = z.shape[:2]
    q, k, v = split_cols(z, ODD_SPLITS)
    return (q.reshape(B, T, C_HEADS, C_DH), k.reshape(B, T, C_GROUPS, C_DH), v.reshape(B, T, C_GROUPS, C_DH))


def kernel(x_prompt, x_sample, mem_prompt, cache_a_k, cache_a_v, state_a_win_k, state_a_win_v,
           cache_b_k, cache_b_v, cache_b_kidx, cache_c_k, cache_c_v, cache_mem_k, cache_mem_v, page_table,
           rel_bias, norm_mix, norm_mem, norm_cross, norm_ffn, norm_final,
           w_cross_q, w_cross_k, w_cross_v, w_cross_o, w_in_even, w_out_even,
           nsa_pe_k, nsa_pe_v, nsa_phi_k1, nsa_phi_k2, nsa_phi_v1, nsa_phi_v2,
           w_ffn1, w_ffn3, w_ffn2, w_in_odd, w_out_odd, w_router, b_router, w_exp1, w_exp3, w_exp2):
    xp, xs = x_prompt, x_sample
    B, T, D = xp.shape
    Bs, S = xs.shape[:2]
    tab_a = rel_bias[:, :A_HEADS]
    tab_b = rel_bias[:, A_HEADS:A_HEADS + B_HEADS]
    tab_c = rel_bias[:, :C_HEADS]
    names = ('a_k_p', 'a_v_p', 'aw_k_p', 'aw_v_p', 'b_k_p', 'b_v_p', 'b_i_p', 'c_k_p', 'c_v_p', 'm_k_p', 'm_v_p',
             'a_k_s', 'a_v_s', 'aw_k_s', 'aw_v_s', 'b_k_s', 'b_v_s', 'b_i_s', 'c_k_s', 'c_v_s')
    new = {n: [] for n in names}
    for layer in range(DEPTH):
        li = layer // 2
        hp = rms_norm(xp, norm_mix[layer], BF16)
        hs = rms_norm(xs, norm_mix[layer], BF16)
        if layer % 2 == 0:
            phi = (nsa_pe_k[li], nsa_pe_v[li], nsa_phi_k1[li], nsa_phi_k2[li], nsa_phi_v1[li], nsa_phi_v2[li])
            qa, ka, va, ga, qb, kb, vb, qi, ki, wi = even_split(matmul(hp, w_in_even, layer=li))
            o_a = nsa_prompt(qa, ka, va, ga, *phi, tab_a)
            o_b = dsa_prompt(qb, kb, vb, qi, ki, wi, tab_b)
            xp = matmul(jnp.concatenate([o_a, o_b], axis=-1), w_out_even, resid=xp, layer=li)
            wk = min(WINDOW, T)
            new['a_k_p'].append(ka[:, :, :2]); new['a_v_p'].append(va[:, :, :2])
            new['aw_k_p'].append(ka[:, T - wk:, 2]); new['aw_v_p'].append(va[:, T - wk:, 2])
            new['b_k_p'].append(kb); new['b_v_p'].append(vb); new['b_i_p'].append(ki)
            qa, ka, va, ga, qb, kb, vb, qi, ki, wi = even_split(matmul(hs, w_in_even, layer=li))
            o_a = nsa_decode(qa, ka, va, ga, cache_a_k[li], cache_a_v[li], state_a_win_k[li], state_a_win_v[li],
                             page_table, *phi, tab_a)
            o_b = dsa_decode(qb, kb, vb, qi, ki, wi, cache_b_k[li], cache_b_v[li], cache_b_kidx[li], page_table, tab_b)
            xs = matmul(jnp.concatenate([o_a, o_b], axis=-1), w_out_even, resid=xs, layer=li)
            new['a_k_s'].append(ka[:, :, :2]); new['a_v_s'].append(va[:, :, :2])
            new['aw_k_s'].append(ka[:, :, 2]); new['aw_v_s'].append(va[:, :, 2])
            new['b_k_s'].append(kb); new['b_v_s'].append(vb); new['b_i_s'].append(ki)
        else:
            q, k, v = odd_split(matmul(hp, w_in_odd, layer=li))
            xp = matmul(moba_prompt(q, k, v, tab_c), w_out_odd, resid=xp, layer=li)
            new['c_k_p'].append(k); new['c_v_p'].append(v)
            q, k, v = odd_split(matmul(hs, w_in_odd, layer=li))
            xs = matmul(moba_decode(q, k, v, cache_c_k[li], cache_c_v[li], page_table, tab_c), w_out_odd,
                        resid=xs, layer=li)
            new['c_k_s'].append(k); new['c_v_s'].append(v)
        mk, mv = mem_kv(mem_prompt, norm_mem[layer], w_cross_k, w_cross_v, layer)
        new['m_k_p'].append(mk); new['m_v_p'].append(mv)
        xp = cross_attend(xp, rms_norm(xp, norm_cross[layer], BF16), mk, mv, w_cross_q, w_cross_o, layer)
        xs = cross_attend(xs, rms_norm(xs, norm_cross[layer], BF16), cache_mem_k[layer], cache_mem_v[layer],
                          w_cross_q, w_cross_o, layer)
        hp = rms_norm(xp, norm_ffn[layer], BF16).reshape(B * T, D)
        hs = rms_norm(xs, norm_ffn[layer], BF16).reshape(Bs * S, D)
        xp2, xs2 = xp.reshape(B * T, D), xs.reshape(Bs * S, D)
        if layer % 2 == 0:
            xp2 = dense_swiglu(xp2, hp, w_ffn1, w_ffn3, w_ffn2, li)
            xs2 = dense_swiglu(xs2, hs, w_ffn1, w_ffn3, w_ffn2, li)
        else:
            xp2, xs2 = moe_swiglu([xp2, xs2], [hp, hs], w_router[li], b_router[li], w_exp1, w_exp3, w_exp2, li)
        xp = xp2.reshape(B, T, D)
        xs = xs2.reshape(Bs, S, D)
    y_prompt = rms_norm(xp, norm_final)
    y_sample = rms_norm(xs, norm_final)
    return (y_prompt, y_sample,
            jnp.stack(new['a_k_p']), jnp.stack(new['a_v_p']), jnp.stack(new['aw_k_p']), jnp.stack(new['aw_v_p']),
            jnp.stack(new['b_k_p']), jnp.stack(new['b_v_p']), jnp.stack(new['b_i_p']),
            jnp.stack(new['c_k_p']), jnp.stack(new['c_v_p']), jnp.stack(new['m_k_p']), jnp.stack(new['m_v_p']),
            jnp.stack(new['a_k_s']), jnp.stack(new['a_v_s']), jnp.stack(new['aw_k_s']), jnp.stack(new['aw_v_s']),
            jnp.stack(new['b_k_s']), jnp.stack(new['b_v_s']), jnp.stack(new['b_i_s']),
            jnp.stack(new['c_k_s']), jnp.stack(new['c_v_s']))
```

```python
import functools
import math
import jax, jax.numpy as jnp
from jax import lax
import numpy as np
from jax.experimental import pallas as pl
from jax.experimental.pallas import tpu as pltpu

D_MODEL = 4096
BATCH = 4
SEQ = 2048
DEPTH = 2
DEC_BATCH = 8
DEC_SEQ = 1
PAST_LEN = 16384
PAGE_SIZE = 128

N_EVEN = (DEPTH + 1) // 2
N_ODD = DEPTH // 2
HEAD_SLOTS = 32
A_HEADS = 16
A_GROUPS = 2
A_HPG = A_HEADS // A_GROUPS
A_DK = 192
A_DV = 128
CMP_BLK = 64
N_SEL = 16
WINDOW = 512
B_HEADS = 16
B_GROUPS = 2
B_HPG = B_HEADS // B_GROUPS
B_DH = 128
IDX_HEADS = 8
IDX_DIM = 64
IDX_TOPK = 256
C_HEADS = 32
C_GROUPS = 8
C_HPG = C_HEADS // C_GROUPS
C_DH = 128
MOBA_BLK = 256
MOBA_TOP = 3
MOBA_QCHUNK = 32
N_MEM = 256
X_HEADS = 4
X_DH = 128
D_FF = 14336
N_EXPERTS = 8
TOP_K = 2
MOE_MAX_ROWS = 512
N_BUCKETS = 32
REL_EXACT = 16
REL_MAX_DIST = 1024
Q_BLOCK = 128
EPS = 1e-6
NEG = -1e30
FORCE = 1e9
EVEN_SPLITS = (A_HEADS * A_DK, 3 * A_GROUPS * A_DK, 3 * A_GROUPS * A_DV, 3 * A_HEADS,
               B_HEADS * B_DH, B_GROUPS * B_DH, B_GROUPS * B_DH, IDX_HEADS * IDX_DIM, IDX_DIM, IDX_HEADS)
ODD_SPLITS = (C_HEADS * C_DH, C_GROUPS * C_DH, C_GROUPS * C_DH)


def _rmsnorm_body(x_ref, g_ref, o_ref):
    x = x_ref[...]
    y = x * lax.rsqrt(jnp.mean(x * x, axis=-1, keepdims=True) + EPS)
    o_ref[...] = (y * g_ref[...]).astype(o_ref.dtype)


def rms_norm(x, g, out_dtype=None):
    out_dtype = out_dtype or x.dtype
    shape = x.shape
    d = shape[-1]
    x2 = x.reshape(-1, d)
    rows = x2.shape[0]
    tr = min(rows, 256)
    out = pl.pallas_call(
        _rmsnorm_body,
        grid=(rows // tr,),
        in_specs=[pl.BlockSpec((tr, d), lambda i: (i, 0)),
                  pl.BlockSpec((1, d), lambda i: (0, 0))],
        out_specs=pl.BlockSpec((tr, d), lambda i: (i, 0)),
        out_shape=jax.ShapeDtypeStruct((rows, d), out_dtype),
    )(x2, g.reshape(1, d).astype(jnp.float32))
    return out.reshape(shape)


TQ = 256
TK = 256
N_OFF = -(-(REL_MAX_DIST + TK - 1) // TK) + 1
LANES = 128
VMEM_LIMIT = 48 * 1024 * 1024
_NT = (((1,), (1,)), ((), ()))
BF16 = jnp.bfloat16
F32 = jnp.float32


def bias_by_distance(tab, n_dist):
    return tab[t5_bucket(jnp.arange(n_dist))].T.astype(F32)


def toeplitz_bias_tiles(tab):
    H = tab.shape[1]
    bd = bias_by_distance(tab, N_OFF * TK + TQ)
    epad = jnp.concatenate([jnp.broadcast_to(bd[:, :1], (H, TK - 1)), bd], axis=1)
    w = TQ + TK - 1
    rows = []
    for o in range(N_OFF):
        erev = epad[:, o * TK: o * TK + w][:, ::-1]
        z = jnp.concatenate([erev, erev[:, :1]], axis=1)
        rows.append(jnp.roll(z, -(TQ - 1), axis=1))
    x = jnp.stack(rows, axis=1)
    y = jnp.tile(x, (1, 1, TQ))[:, :, : TQ * w].reshape(H, N_OFF, TQ, w)
    return y[..., :TK]


def _flash_body(mode, hpg, dv, scale, *refs):
    if mode == 'window':
        q_ref, k_ref, v_ref, b_ref, o_ref, m_ref, l_ref, acc_ref = refs
        x_ref = None
    else:
        q_ref, k_ref, v_ref, b_ref, x_ref, o_ref, m_ref, l_ref, acc_ref = refs
    qi = pl.program_id(2)
    ki = pl.program_id(3)

    @pl.when(ki == 0)
    def _():
        m_ref[...] = jnp.full(m_ref.shape, NEG, F32)
        l_ref[...] = jnp.zeros(l_ref.shape, F32)
        acc_ref[...] = jnp.zeros(acc_ref.shape, F32)

    active = ki <= qi
    if mode == 'window':
        active = active & (qi - ki <= WINDOW // TK)

    @pl.when(active)
    def _():
        k = k_ref[0, 0].astype(BF16)
        v = v_ref[0, 0].astype(BF16)
        dist = (qi - ki) * TK + (lax.broadcasted_iota(jnp.int32, (TQ, TK), 0)
                                 - lax.broadcasted_iota(jnp.int32, (TQ, TK), 1))
        if mode == 'key':
            shared_mask = x_ref[0, 0] > 0
        elif mode == 'window':
            shared_mask = (dist >= 0) & (dist < WINDOW)
        else:
            own_mask = jnp.where(dist >= 0, 1.0, 0.0)
            blk = lax.broadcasted_iota(jnp.int32, (TQ, x_ref.shape[-1]), 1)
        for h in range(hpg):
            if mode == 'moba':
                flag = jnp.sum(jnp.where(blk == ki, x_ref[0, h], 0.0), axis=1, keepdims=True)
                mask = jnp.where(ki == qi, own_mask, jnp.broadcast_to(flag, (TQ, TK))) > 0.5
            else:
                mask = shared_mask
            q = q_ref[0, 0, h].astype(BF16)
            s = lax.dot_general(q, k, _NT, preferred_element_type=F32) * scale + b_ref[h, 0]
            s = jnp.where(mask, s, NEG)
            m_prev = m_ref[h][:, :1]
            l_prev = l_ref[h][:, :1]
            m_new = jnp.maximum(m_prev, jnp.max(s, axis=1, keepdims=True))
            p = jnp.where(mask, jnp.exp(s - m_new), 0.0)
            alpha = jnp.exp(m_prev - m_new)
            l_new = alpha * l_prev + jnp.sum(p, axis=1, keepdims=True)
            acc_ref[h] = alpha * acc_ref[h] + jnp.dot(p.astype(BF16), v, preferred_element_type=F32)
            m_ref[h] = jnp.broadcast_to(m_new, (TQ, LANES))
            l_ref[h] = jnp.broadcast_to(l_new, (TQ, LANES))

    @pl.when(ki == qi)
    def _():
        for h in range(hpg):
            l = l_ref[h][:, :1]
            o_ref[0, :, h * dv:(h + 1) * dv] = jnp.where(l > 0.0, acc_ref[h] / jnp.where(l > 0.0, l, 1.0), 0.0)


def flash_attention(mode, q, k, v, bias_tiles, extra, scale):
    B, G, HPG, T, dk = q.shape
    dv = v.shape[-1]
    nq, nk = T // TQ, T // TK

    def kv_idx(b, g, qi, ki):
        lo = jnp.maximum(qi - WINDOW // TK, 0) if mode == 'window' else 0
        return (b, g, jnp.clip(ki, lo, qi), 0)

    in_specs = [
        pl.BlockSpec((1, 1, HPG, TQ, dk), lambda b, g, qi, ki: (b, g, 0, qi, 0)),
        pl.BlockSpec((1, 1, TK, dk), kv_idx),
        pl.BlockSpec((1, 1, TK, dv), kv_idx),
        pl.BlockSpec((HPG, 1, TQ, TK), lambda b, g, qi, ki: (g, jnp.clip(qi - ki, 0, N_OFF - 1), 0, 0)),
    ]
    args = [q, k, v, bias_tiles]
    if mode == 'key':
        gm = extra.shape[1]
        in_specs.append(pl.BlockSpec((1, 1, TQ, TK),
                                     lambda b, g, qi, ki: (b, g if gm > 1 else 0, qi, jnp.minimum(ki, qi))))
        args.append(extra)
    elif mode == 'moba':
        in_specs.append(pl.BlockSpec((1, HPG, TQ, extra.shape[-1]), lambda b, g, qi, ki: (b, g, qi, 0)))
        args.append(extra)
    return pl.pallas_call(
        functools.partial(_flash_body, mode, HPG, dv, scale),
        grid=(B, G, nq, nk),
        in_specs=in_specs,
        out_specs=pl.BlockSpec((1, TQ, HPG * dv), lambda b, g, qi, ki: (b, qi, g)),
        out_shape=jax.ShapeDtypeStruct((B, T, G * HPG * dv), F32),
        scratch_shapes=[pltpu.VMEM((HPG, TQ, LANES), F32), pltpu.VMEM((HPG, TQ, LANES), F32),
                        pltpu.VMEM((HPG, TQ, dv), F32)],
        compiler_params=pltpu.CompilerParams(
            dimension_semantics=("parallel", "parallel", "parallel", "arbitrary"),
            vmem_limit_bytes=VMEM_LIMIT),
        name=f"flash_{mode}",
    )(*args)


HEADS_PER_STEP = 4
PREFIX_STEP = 2


def _row_attn_body(mode, hs, dv, scale, n_tiles, *refs):
    if mode == 'window':
        q_ref, k_ref, v_ref, b_ref, o_ref = refs
        x_ref = None
    else:
        q_ref, k_ref, v_ref, b_ref, x_ref, o_ref = refs
    qi = pl.program_id(3)
    w = n_tiles * TK

    def heads(n, t0, k, v, mask_of):
        for h in range(hs):
            mask = mask_of(h, n)
            bias = jnp.concatenate([b_ref[h, jnp.clip(qi - (t0 + j), 0, N_OFF - 1)] for j in range(n)], axis=1)
            s = lax.dot_general(q_ref[0, 0, h].astype(BF16), k, _NT, preferred_element_type=F32) * scale + bias
            s = jnp.where(mask, s, NEG)
            p = jnp.where(mask, jnp.exp(s - jnp.max(s, axis=1, keepdims=True)), 0.0)
            l = jnp.sum(p, axis=1, keepdims=True)
            o = jnp.dot(p.astype(BF16), v, preferred_element_type=F32)
            o_ref[0, :, h * dv:(h + 1) * dv] = jnp.where(l > 0.0, o / jnp.where(l > 0.0, l, 1.0), 0.0)

    if mode == 'window':
        t0 = jnp.maximum(qi - (n_tiles - 1), 0)
        c0 = pl.multiple_of(t0 * TK, TK)
        k = k_ref[0, 0, pl.ds(c0, w), :].astype(BF16)
        v = v_ref[0, 0, pl.ds(c0, w), :].astype(BF16)
        dist = (qi * TQ + lax.broadcasted_iota(jnp.int32, (TQ, w), 0)) - (c0 + lax.broadcasted_iota(jnp.int32, (TQ, w), 1))
        shared_mask = (dist >= 0) & (dist < WINDOW)
        heads(n_tiles, t0, k, v, lambda h, n: shared_mask)
        return

    def causal_prefix(n):
        k = k_ref[0, 0, :n * TK, :].astype(BF16)
        v = v_ref[0, 0, :n * TK, :].astype(BF16)
        if mode == 'key':
            shared_mask = x_ref[0, 0, :, :n * TK] > 0
            heads(n, 0, k, v, lambda h, n: shared_mask)
        else:
            causal = jnp.where(lax.broadcasted_iota(jnp.int32, (TQ, TK), 0)
                               >= lax.broadcasted_iota(jnp.int32, (TQ, TK), 1), 1.0, 0.0)

            def moba_mask(h, n):
                flags = x_ref[0, h]
                return jnp.concatenate(
                    [jnp.where(qi == j, causal, jnp.broadcast_to(flags[:, j:j + 1], (TQ, TK))) for j in range(n)],
                    axis=1) > 0.5
            heads(n, 0, k, v, moba_mask)

    prefixes = sorted({min(n_tiles, p) for p in range(PREFIX_STEP, n_tiles + PREFIX_STEP, PREFIX_STEP)})
    lo = 0
    for n in prefixes:
        pl.when((qi + 1 > lo) & (qi + 1 <= n))(functools.partial(causal_prefix, n))
        lo = n


def block_attention(mode, q, k, v, bias_tiles, extra, scale):
    B, G, HPG, T, dk = q.shape
    dv = v.shape[-1]
    hs = min(HEADS_PER_STEP, HPG)
    nh = HPG // hs
    n_tiles = min(WINDOW // TK + 1, T // TK) if mode == 'window' else T // TK
    assert HPG % hs == 0 and T % TK == 0
    in_specs = [
        pl.BlockSpec((1, 1, hs, TQ, dk), lambda b, g, hh, qi: (b, g, hh, qi, 0)),
        pl.BlockSpec((1, 1, T, dk), lambda b, g, hh, qi: (b, g, 0, 0)),
        pl.BlockSpec((1, 1, T, dv), lambda b, g, hh, qi: (b, g, 0, 0)),
        pl.BlockSpec((hs, N_OFF, TQ, TK), lambda b, g, hh, qi: (g * nh + hh, 0, 0, 0)),
    ]
    args = [q, k, v, bias_tiles]
    if mode == 'key':
        gm = extra.shape[1]
        in_specs.append(pl.BlockSpec((1, 1, TQ, T), lambda b, g, hh, qi: (b, g if gm > 1 else 0, qi, 0)))
        args.append(extra)
    elif mode == 'moba':
        in_specs.append(pl.BlockSpec((1, hs, TQ, extra.shape[-1]), lambda b, g, hh, qi: (b, g * nh + hh, qi, 0)))
        args.append(extra)
    return pl.pallas_call(
        functools.partial(_row_attn_body, mode, hs, dv, scale, n_tiles),
        grid=(B, G, nh, T // TQ),
        in_specs=in_specs,
        out_specs=pl.BlockSpec((1, TQ, hs * dv), lambda b, g, hh, qi: (b, qi, g * nh + hh)),
        out_shape=jax.ShapeDtypeStruct((B, T, G * HPG * dv), F32),
        compiler_params=pltpu.CompilerParams(
            dimension_semantics=("parallel", "parallel", "parallel", "parallel"), vmem_limit_bytes=VMEM_LIMIT),
        name=f"attn_{mode}",
    )(*args)


def _nsa_cmp_body(q_ref, kc_ref, vc_ref, o_ref, mask_ref):
    qi = pl.program_id(2)
    nc = kc_ref.shape[2]
    T = mask_ref.shape[-1]
    kc = kc_ref[0, 0].astype(BF16)
    vc = vc_ref[0, 0].astype(BF16)
    t = qi * TQ + lax.broadcasted_iota(jnp.int32, (TQ, nc), 0)
    j = lax.broadcasted_iota(jnp.int32, (TQ, nc), 1)
    vis = (j + 1) * CMP_BLK <= t + 1
    imp = jnp.zeros((TQ, nc), F32)
    for h in range(A_HPG):
        q = q_ref[0, 0, h].astype(BF16)
        s = lax.dot_general(q, kc, _NT, preferred_element_type=F32) * (A_DK ** -0.5)
        s = jnp.where(vis, s, NEG)
        e = jnp.where(vis, jnp.exp(s - jnp.max(s, axis=1, keepdims=True)), 0.0)
        l = jnp.sum(e, axis=1, keepdims=True)
        p = jnp.where(l > 0.0, e / jnp.where(l > 0.0, l, 1.0), 0.0)
        o_ref[0, :, h * A_DV:(h + 1) * A_DV] = jnp.dot(p.astype(BF16), vc, preferred_element_type=F32)
        imp = imp + p
    own = t // CMP_BLK
    forced = (j == 0) | (j == own) | (j == own - 1)
    score = jnp.where(forced, FORCE, imp)
    score = jnp.where(j <= own, score, NEG)
    rank = jnp.zeros((TQ, nc), F32)
    for i in range(nc):
        si = score[:, i:i + 1]
        rank = rank + jnp.where((si > score) | ((si == score) & (i < j)), 1.0, 0.0)
    sel = jnp.where((rank < float(N_SEL)) & (j <= own), 1.0, 0.0).astype(BF16)
    expand = jnp.where(lax.broadcasted_iota(jnp.int32, (nc, T), 1) // CMP_BLK
                       == lax.broadcasted_iota(jnp.int32, (nc, T), 0), 1.0, 0.0).astype(BF16)
    keys = jnp.dot(sel, expand, preferred_element_type=F32)
    causal = (qi * TQ + lax.broadcasted_iota(jnp.int32, (TQ, T), 0)) >= lax.broadcasted_iota(jnp.int32, (TQ, T), 1)
    mask_ref[0, 0] = jnp.where((keys > 0.5) & causal, 1.0, 0.0).astype(BF16)


def nsa_cmp_select(q, kc, vc):
    B, G, HPG, T, dk = q.shape
    nc = kc.shape[2]
    return pl.pallas_call(
        _nsa_cmp_body,
        grid=(B, G, T // TQ),
        in_specs=[pl.BlockSpec((1, 1, HPG, TQ, dk), lambda b, g, qi: (b, g, 0, qi, 0)),
                  pl.BlockSpec((1, 1, nc, dk), lambda b, g, qi: (b, g, 0, 0)),
                  pl.BlockSpec((1, 1, nc, A_DV), lambda b, g, qi: (b, g, 0, 0))],
        out_specs=[pl.BlockSpec((1, TQ, HPG * A_DV), lambda b, g, qi: (b, qi, g)),
                   pl.BlockSpec((1, 1, TQ, T), lambda b, g, qi: (b, g, qi, 0))],
        out_shape=[jax.ShapeDtypeStruct((B, T, G * HPG * A_DV), F32),
                   jax.ShapeDtypeStruct((B, G, T, T), BF16)],
        compiler_params=pltpu.CompilerParams(
            dimension_semantics=("parallel", "parallel", "parallel"), vmem_limit_bytes=VMEM_LIMIT),
        name="nsa_cmp_select",
    )(q, kc, vc)


def _count(cond):
    return jnp.sum(jnp.where(cond, 1.0, 0.0), axis=1, keepdims=True)


def _dsa_select_body(topk, qi_ref, ki_ref, w_ref, mask_ref):
    qt = pl.program_id(1)
    T = ki_ref.shape[1]
    kidx = ki_ref[0].astype(BF16)
    w = w_ref[0] * (IDX_HEADS ** -0.5)
    score = jnp.zeros((TQ, T), F32)
    for h in range(IDX_HEADS):
        d = lax.dot_general(qi_ref[0, h].astype(BF16), kidx, _NT, preferred_element_type=F32) * (IDX_DIM ** -0.5)
        score = score + jnp.maximum(d, 0.0) * w[:, h:h + 1]
    t = qt * TQ + lax.broadcasted_iota(jnp.int32, (TQ, T), 0)
    s = lax.broadcasted_iota(jnp.int32, (TQ, T), 1)
    causal = s <= t
    score = jnp.where(causal, score, NEG)
    bits = pltpu.bitcast(score, jnp.int32)
    key = jnp.where(bits < 0, bits ^ jnp.int32(0x7FFFFFFF), bits)
    int_min = jnp.int32(-2 ** 31)

    def value_step(i, lo):
        cand = lo + jnp.left_shift(jnp.int32(1), 31 - i)
        return jnp.where(_count(key >= cand) >= float(topk), cand, lo)
    thr = lax.fori_loop(0, 32, value_step, jnp.full((TQ, 1), int_min, jnp.int32))
    above = key > thr
    tied = key == thr
    need = float(topk) - _count(above)
    n_bits = max(1, (T - 1).bit_length())

    def index_step(i, m):
        cand = m + jnp.left_shift(jnp.int32(1), n_bits - 1 - i)
        return jnp.where(_count(tied & (s < cand)) < need, cand, m)
    last = lax.fori_loop(0, n_bits, index_step, jnp.zeros((TQ, 1), jnp.int32))
    sel = (above | (tied & (s <= last))) & causal
    mask_ref[0, 0] = jnp.where(sel, 1.0, 0.0).astype(BF16)


def dsa_select(qi, ki, wi, topk):
    B, H, T, d = qi.shape
    return pl.pallas_call(
        functools.partial(_dsa_select_body, topk),
        grid=(B, T // TQ),
        in_specs=[pl.BlockSpec((1, H, TQ, d), lambda b, qt: (b, 0, qt, 0)),
                  pl.BlockSpec((1, T, d), lambda b, qt: (b, 0, 0)),
                  pl.BlockSpec((1, TQ, H), lambda b, qt: (b, qt, 0))],
        out_specs=pl.BlockSpec((1, 1, TQ, T), lambda b, qt: (b, 0, qt, 0)),
        out_shape=jax.ShapeDtypeStruct((B, 1, T, T), BF16),
        compiler_params=pltpu.CompilerParams(
            dimension_semantics=("parallel", "parallel"), vmem_limit_bytes=VMEM_LIMIT),
        name="dsa_select",
    )(qi, ki, wi)


def _moba_select_body(q_ref, k_ref, f_ref):
    T = k_ref.shape[2]
    nb = T // MOBA_BLK
    row = lax.broadcasted_iota(jnp.int32, (nb, k_ref.shape[3]), 0)
    kmean = jnp.zeros((nb, k_ref.shape[3]), F32)
    for b in range(nb):
        blk_sum = jnp.sum(k_ref[0, 0, b * MOBA_BLK:(b + 1) * MOBA_BLK, :], axis=0, keepdims=True)
        kmean = jnp.where(row == b, blk_sum * (1.0 / MOBA_BLK), kmean)
    kmean = kmean.astype(BF16)
    j = lax.broadcasted_iota(jnp.int32, (nb, T), 0)
    past = j < lax.broadcasted_iota(jnp.int32, (nb, T), 1) // MOBA_BLK
    for h in range(C_HPG):
        s = lax.dot_general(kmean, q_ref[0, 0, h].astype(BF16), _NT, preferred_element_type=F32)
        s = jnp.where(past, s, NEG)
        rank = jnp.zeros((nb, T), F32)
        for i in range(nb):
            si = s[i:i + 1, :]
            rank = rank + jnp.where((si > s) | ((si == s) & (i < j)), 1.0, 0.0)
        f_ref[0, h] = jnp.where((rank < float(MOBA_TOP)) & past, 1.0, 0.0)


def moba_select(q, k):
    B, G, HPG, T, dh = q.shape
    nb = T // MOBA_BLK
    return pl.pallas_call(
        _moba_select_body,
        grid=(B, G),
        in_specs=[pl.BlockSpec((1, 1, HPG, T, dh), lambda b, g: (b, g, 0, 0, 0)),
                  pl.BlockSpec((1, 1, T, dh), lambda b, g: (b, g, 0, 0))],
        out_specs=pl.BlockSpec((1, HPG, nb, T), lambda b, g: (b, g, 0, 0)),
        out_shape=jax.ShapeDtypeStruct((B, G * HPG, nb, T), F32),
        compiler_params=pltpu.CompilerParams(
            dimension_semantics=("parallel", "parallel"), vmem_limit_bytes=VMEM_LIMIT),
        name="moba_select",
    )(q, k)


MM_VMEM_LIMIT = 56 * 1024 * 1024
SUB_ROWS = 256


def _gmm_body(n_w, nk, nsb_max, sb, has_resid, te_ref, ns_ref, src_ref, *refs):
    x_ref = refs[0]
    w_refs = refs[1:1 + n_w]
    pos = 1 + n_w
    r_ref = refs[pos] if has_resid else None
    pos += int(has_resid)
    o_ref = refs[pos]
    acc_refs = refs[pos + 1:pos + 1 + n_w]
    i = pl.program_id(0)
    k = pl.program_id(2)
    n_sb = ns_ref[i]

    @pl.when(k == 0)
    def _():
        for a_ref in acc_refs:
            a_ref[...] = jnp.zeros(a_ref.shape, F32)

    for c in range(1, nsb_max + 1):
        @pl.when(n_sb == c)
        def _(c=c):
            xs = x_ref[:c * sb, :].astype(BF16)
            for w_ref, a_ref in zip(w_refs, acc_refs):
                a_ref[:c * sb, :] += jnp.dot(xs, w_ref[0].astype(BF16), preferred_element_type=F32)

    @pl.when(k == nk - 1)
    def _():
        if n_w == 2:
            val = jax.nn.silu(acc_refs[0][...]) * acc_refs[1][...]
        else:
            val = acc_refs[0][...]
        if has_resid:
            val = val + r_ref[...]
        o_ref[...] = val.astype(o_ref.dtype)


def grouped_matmul(x, ws, tables, tm, tn, tk, out_dtype, resid=None, layer=0):
    P, K = x.shape
    N = ws[0].shape[-1]
    tk = min(tk, K)
    tn = min(tn, N)
    assert P % tm == 0 and K % tk == 0
    sb = min(SUB_ROWS, tm)
    assert tm % sb == 0
    n_w = len(ws)
    ni, nj, nk = P // tm, pl.cdiv(N, tn), K // tk

    def x_idx(i, j, k, te, ns, src):
        return (src[i], jnp.where(ns[i] > 0, k, nk - 1))

    def w_idx(i, j, k, te, ns, src):
        act = ns[i] > 0
        return (layer, te[i], jnp.where(act, k, nk - 1), jnp.where(act, j, nj - 1))

    def o_idx(i, j, k, te, ns, src):
        return (i, j)

    in_specs = [pl.BlockSpec((tm, tk), x_idx)] + [pl.BlockSpec((None, 1, tk, tn), w_idx)] * n_w
    args = [x] + list(ws)
    if resid is not None:
        in_specs.append(pl.BlockSpec((tm, tn), o_idx))
        args.append(resid)
    return pl.pallas_call(
        functools.partial(_gmm_body, n_w, nk, tm // sb, sb, resid is not None),
        grid_spec=pltpu.PrefetchScalarGridSpec(
            num_scalar_prefetch=3, grid=(ni, nj, nk), in_specs=in_specs,
            out_specs=pl.BlockSpec((tm, tn), o_idx),
            scratch_shapes=[pltpu.VMEM((tm, tn), F32)] * n_w),
        out_shape=jax.ShapeDtypeStruct((P, N), out_dtype),
        compiler_params=pltpu.CompilerParams(
            dimension_semantics=("parallel", "parallel", "arbitrary"), vmem_limit_bytes=MM_VMEM_LIMIT),
        name=f"gmm{n_w}_{tm}x{tn}x{tk}",
    )(*tables, *args)


def _dense_tables(rows, tm):
    n = rows // tm
    return (jnp.zeros((n,), jnp.int32), jnp.full((n,), tm // min(SUB_ROWS, tm), jnp.int32),
            jnp.arange(n, dtype=jnp.int32))


def _row_tile(rows):
    for tm in (2048, 1024, 512, 256):
        if rows % tm == 0:
            return tm
    return rows


def matmul(x, w, out_dtype=F32, resid=None, tn=1024, tk=512, layer=None):
    lead = x.shape[:-1]
    x2 = x.reshape(-1, x.shape[-1])
    rows = x2.shape[0]
    tm = _row_tile(rows)
    r2 = None if resid is None else resid.reshape(rows, -1)
    w4 = w[None, None] if layer is None else w[:, None]
    out = grouped_matmul(x2, [w4], _dense_tables(rows, tm), tm, tn, tk, out_dtype, r2, layer or 0)
    return out.reshape(*lead, w.shape[-1])


def swiglu_up(x, w1, w3, tables, tm, layer, tf=512, tk=1024):
    return grouped_matmul(x, [w1, w3], tables, tm, tf, tk, BF16, layer=layer)


PAGES_PER_STEP = 8
REMOVED = -3e38


def bias_by_position(tab, past_len):
    bd = bias_by_distance(tab, REL_MAX_DIST + 1)
    H = bd.shape[0]
    near = bd[:, 1:REL_MAX_DIST + 1][:, ::-1]
    far = jnp.broadcast_to(bd[:, REL_MAX_DIST:], (H, past_len - REL_MAX_DIST))
    return jnp.concatenate([far, near], axis=1), bd[:, :1]


def _page_specs(lanes, lane_block, pages_of):
    def spec(r):
        return pl.BlockSpec((None, PAGE_SIZE, lanes), lambda *a: (pages_of(r)(*a), 0, lane_block))
    return [spec(r) for r in range(PAGES_PER_STEP)]


def _chunk_page(r):
    return lambda b, c, pt, *_: pt[b, c * PAGES_PER_STEP + r]


def _topk_rows(score, k):
    R, L = score.shape
    jf = lax.broadcasted_iota(jnp.int32, (R, L), 1).astype(F32)
    slot = lax.broadcasted_iota(jnp.int32, (R, LANES), 1)
    idx = jnp.zeros((R, LANES), F32)
    val = jnp.zeros((R, LANES), F32)
    for n in range(k):
        m = jnp.max(score, axis=1, keepdims=True)
        i = jnp.min(jnp.where(score == m, jf, 1e9), axis=1, keepdims=True)
        idx = jnp.where(slot == n, i, idx)
        val = jnp.where(slot == n, jnp.where(m > 0.5 * NEG, 1.0, 0.0), val)
        score = jnp.where(jf == i, REMOVED, score)
    return idx, val


def _gather_cmp_body(width, pt_ref, *refs):
    pages = refs[:PAGES_PER_STEP]
    pe_ref, o_ref = refs[PAGES_PER_STEP:]
    for r in range(PAGES_PER_STEP):
        for g in range(A_GROUPS):
            o_ref[0, g, r * PAGE_SIZE:(r + 1) * PAGE_SIZE, :] = pages[r][:, g * width:(g + 1) * width] + pe_ref[...]


def gather_compress_rows(cache, page_table, pe, width):
    B, n_pages = page_table.shape
    pe2 = jnp.tile(pe, (PAGE_SIZE // CMP_BLK, 1))
    return pl.pallas_call(
        functools.partial(_gather_cmp_body, width),
        grid_spec=pltpu.PrefetchScalarGridSpec(
            num_scalar_prefetch=1, grid=(B, n_pages // PAGES_PER_STEP),
            in_specs=_page_specs(A_GROUPS * width, 0, _chunk_page)
            + [pl.BlockSpec((PAGE_SIZE, width), lambda b, c, pt: (0, 0))],
            out_specs=pl.BlockSpec((1, A_GROUPS, PAGES_PER_STEP * PAGE_SIZE, width), lambda b, c, pt: (b, 0, c, 0))),
        out_shape=jax.ShapeDtypeStruct((B, A_GROUPS, n_pages * PAGE_SIZE, width), F32),
        compiler_params=pltpu.CompilerParams(dimension_semantics=("parallel", "parallel"),
                                             vmem_limit_bytes=VMEM_LIMIT),
        name="gather_compress_rows",
    )(page_table, *([cache] * PAGES_PER_STEP), pe2)


def _nsa_sample_a_body(q_ref, kc_ref, vc_ref, wk_ref, wv_ref, kn_ref, vn_ref, bw_ref, oc_ref, ow_ref, idx_ref, val_ref):
    scale = A_DK ** -0.5
    q = q_ref[0, 0]
    qb = q.astype(BF16)
    nc = kc_ref.shape[2]
    s = lax.dot_general(qb, kc_ref[0, 0].astype(BF16), _NT, preferred_element_type=F32) * scale
    e = jnp.exp(s - jnp.max(s, axis=1, keepdims=True))
    p = e / jnp.sum(e, axis=1, keepdims=True)
    oc_ref[0, 0] = jnp.dot(p.astype(BF16), vc_ref[0, 0].astype(BF16), preferred_element_type=F32)
    imp = jnp.concatenate([jnp.sum(p, axis=0, keepdims=True), jnp.zeros((1, LANES), F32)], axis=1)
    j = lax.broadcasted_iota(jnp.int32, imp.shape, 1)
    own = nc
    forced = (j == 0) | (j == own) | (j == own - 1)
    score = jnp.where(forced, FORCE, imp)
    score = jnp.where(j <= own, score, REMOVED)
    idx, val = _topk_rows(score, N_SEL)
    idx_ref[0, 0] = idx.astype(jnp.int32)
    val_ref[0, 0] = val.astype(jnp.int32)
    wb = wk_ref.shape[2]
    sw = lax.dot_general(qb, wk_ref[0, 0].astype(BF16), _NT, preferred_element_type=F32) * scale + bw_ref[0][:, :wb]
    dist = wb - lax.broadcasted_iota(jnp.int32, sw.shape, 1)
    in_win = dist < WINDOW
    sn = jnp.sum(q * kn_ref[0, 0], axis=1, keepdims=True) * scale + bw_ref[0][:, wb:wb + 1]
    m = jnp.maximum(jnp.max(jnp.where(in_win, sw, NEG), axis=1, keepdims=True), sn)
    ew = jnp.where(in_win, jnp.exp(sw - m), 0.0)
    en = jnp.exp(sn - m)
    l = jnp.sum(ew, axis=1, keepdims=True) + en
    ow_ref[0, 0] = (jnp.dot(ew.astype(BF16), wv_ref[0, 0].astype(BF16), preferred_element_type=F32)
                    + en * vn_ref[0, 0]) / l


def _nsa_sample_b_body(nc, idx_ref, val_ref, pt_ref, q_ref, ka_ref, kb_ref, va_ref, vb_ref, ba_ref, bb_ref,
                       kn_ref, vn_ref, b0_ref, o_ref, m_ref, l_ref, acc_ref):
    b = pl.program_id(0)
    n = pl.program_id(1)
    scale = A_DK ** -0.5

    @pl.when(n == 0)
    def _():
        m_ref[...] = jnp.full(m_ref.shape, NEG, F32)
        l_ref[...] = jnp.zeros(l_ref.shape, F32)
        acc_ref[...] = jnp.zeros(acc_ref.shape, F32)

    for g, (k_ref, v_ref, bias_ref) in enumerate(((ka_ref, va_ref, ba_ref), (kb_ref, vb_ref, bb_ref))):
        slot = (b * A_GROUPS + g) * N_SEL + n
        cached = (val_ref[slot] > 0) & (idx_ref[slot] < nc)

        @pl.when(cached)
        def _(g=g, k_ref=k_ref, v_ref=v_ref, bias_ref=bias_ref):
            lo_k = (A_GROUPS + g) * A_DK
            lo_v = (A_GROUPS + g) * A_DV
            k = k_ref[:, lo_k:lo_k + A_DK].astype(BF16)
            v = v_ref[:, lo_v:lo_v + A_DV].astype(BF16)
            s = lax.dot_general(q_ref[0, g].astype(BF16), k, _NT, preferred_element_type=F32) * scale + bias_ref[0]
            m_prev = m_ref[g][:, :1]
            m_new = jnp.maximum(m_prev, jnp.max(s, axis=1, keepdims=True))
            p = jnp.exp(s - m_new)
            alpha = jnp.exp(m_prev - m_new)
            l_ref[g] = jnp.broadcast_to(alpha * l_ref[g][:, :1] + jnp.sum(p, axis=1, keepdims=True), l_ref.shape[1:])
            acc_ref[g] = alpha * acc_ref[g] + jnp.dot(p.astype(BF16), v, preferred_element_type=F32)
            m_ref[g] = jnp.broadcast_to(m_new, m_ref.shape[1:])

    @pl.when(n == N_SEL - 1)
    def _():
        for g in range(A_GROUPS):
            sn = jnp.sum(q_ref[0, g] * kn_ref[0, g], axis=1, keepdims=True) * scale + b0_ref[g]
            m_prev = m_ref[g][:, :1]
            m_new = jnp.maximum(m_prev, sn)
            alpha = jnp.exp(m_prev - m_new)
            en = jnp.exp(sn - m_new)
            l = alpha * l_ref[g][:, :1] + en
            o_ref[0, g] = (alpha * acc_ref[g] + en * vn_ref[0, g]) / l


def nsa_decode(q, k3, v3, gates, cache_k, cache_v, win_k, win_v, page_table,
               pe_k, pe_v, wk1, wk2, wv1, wv2, tab_a):
    B, S, G, HPG, DK = q.shape
    n_pool = cache_k.shape[0]
    n_pages = page_table.shape[1]
    P = n_pages * PAGE_SIZE
    nc = P // CMP_BLK
    assert S == 1 and G == A_GROUPS and P % CMP_BLK == 0 and (P + S) // CMP_BLK == nc and win_k.shape[1] == WINDOW
    ck = cache_k.reshape(n_pool, PAGE_SIZE, 2 * G * DK)
    cv = cache_v.reshape(n_pool, PAGE_SIZE, 2 * G * A_DV)
    rows_k = gather_compress_rows(ck, page_table, pe_k, DK).reshape(B, G, nc, CMP_BLK * DK)
    rows_v = gather_compress_rows(cv, page_table, pe_v, A_DV).reshape(B, G, nc, CMP_BLK * A_DV)
    kc = matmul(jax.nn.silu(matmul(rows_k, wk1)), wk2)
    vc = matmul(jax.nn.silu(matmul(rows_v, wv1)), wv2)
    qh = q.reshape(B, G, HPG, DK)
    brev, b0 = bias_by_position(tab_a, P)
    bw = jnp.concatenate([brev[:, P - WINDOW:], b0, jnp.zeros((A_HEADS, LANES - 1), F32)], axis=1)
    bw = bw.reshape(G, HPG, WINDOW + LANES)
    new_k = jnp.moveaxis(k3[:, 0], 1, 2)
    new_v = jnp.moveaxis(v3[:, 0], 1, 2)
    per_bg = lambda *shape: pl.BlockSpec((1, 1) + shape, lambda b, g: (b, g) + (0,) * len(shape))
    o_c, o_w, idx, val = pl.pallas_call(
        _nsa_sample_a_body,
        grid=(B, G),
        in_specs=[per_bg(HPG, DK), per_bg(nc, DK), per_bg(nc, A_DV), per_bg(WINDOW, DK), per_bg(WINDOW, A_DV),
                  per_bg(1, DK), per_bg(1, A_DV),
                  pl.BlockSpec((1, HPG, WINDOW + LANES), lambda b, g: (g, 0, 0))],
        out_specs=[per_bg(HPG, A_DV), per_bg(HPG, A_DV), per_bg(1, LANES), per_bg(1, LANES)],
        out_shape=[jax.ShapeDtypeStruct((B, G, HPG, A_DV), F32)] * 2
        + [jax.ShapeDtypeStruct((B, G, 1, LANES), jnp.int32)] * 2,
        compiler_params=pltpu.CompilerParams(dimension_semantics=("parallel", "parallel"),
                                             vmem_limit_bytes=VMEM_LIMIT),
        name="nsa_sample_cmp_win",
    )(qh, kc, vc, jnp.moveaxis(win_k, 1, 2), jnp.moveaxis(win_v, 1, 2), new_k[:, :, 2:3], new_v[:, :, 2:3], bw)
    idx_flat = idx[:, :, 0, :N_SEL].reshape(-1)
    val_flat = val[:, :, 0, :N_SEL].reshape(-1)
    bsel = brev.reshape(G, HPG, nc, CMP_BLK).transpose(0, 2, 1, 3)

    def blk(g):
        return lambda b, n, ix, vl, pt: jnp.minimum(ix[(b * G + g) * N_SEL + n], nc - 1)

    def kv_spec(g, lanes):
        half = PAGE_SIZE // CMP_BLK
        return pl.BlockSpec((None, CMP_BLK, lanes),
                            lambda b, n, ix, vl, pt: (pt[b, blk(g)(b, n, ix, vl, pt) // half],
                                                       blk(g)(b, n, ix, vl, pt) % half, 0))

    def bias_spec(g):
        return pl.BlockSpec((None, 1, HPG, CMP_BLK), lambda b, n, ix, vl, pt: (g, blk(g)(b, n, ix, vl, pt), 0, 0))
    whole = lambda *shape: pl.BlockSpec((1,) + shape, lambda b, n, ix, vl, pt: (b,) + (0,) * len(shape))
    o_s = pl.pallas_call(
        functools.partial(_nsa_sample_b_body, nc),
        grid_spec=pltpu.PrefetchScalarGridSpec(
            num_scalar_prefetch=3, grid=(B, N_SEL),
            in_specs=[whole(G, HPG, DK), kv_spec(0, 2 * G * DK), kv_spec(1, 2 * G * DK),
                      kv_spec(0, 2 * G * A_DV), kv_spec(1, 2 * G * A_DV), bias_spec(0), bias_spec(1),
                      whole(G, 1, DK), whole(G, 1, A_DV),
                      pl.BlockSpec((G, HPG, 1), lambda b, n, ix, vl, pt: (0, 0, 0))],
            out_specs=whole(G, HPG, A_DV),
            scratch_shapes=[pltpu.VMEM((G, HPG, LANES), F32), pltpu.VMEM((G, HPG, LANES), F32),
                            pltpu.VMEM((G, HPG, A_DV), F32)]),
        out_shape=jax.ShapeDtypeStruct((B, G, HPG, A_DV), F32),
        compiler_params=pltpu.CompilerParams(dimension_semantics=("parallel", "arbitrary"),
                                             vmem_limit_bytes=VMEM_LIMIT),
        name="nsa_sample_selected",
    )(idx_flat, val_flat, page_table, qh, ck, ck, cv, cv, bsel, bsel, new_k[:, :, 1:2], new_v[:, :, 1:2],
      b0.reshape(G, HPG, 1))
    shp = (B, S, G, HPG, A_DV)
    return nsa_combine(gates, o_c.reshape(shp), o_s.reshape(shp), o_w.reshape(shp))


def _dsa_scores_body(pt_ref, *refs):
    pages = refs[:PAGES_PER_STEP]
    qi_ref, w_ref, o_ref = refs[PAGES_PER_STEP:]
    qi = qi_ref[0].astype(BF16)
    w = w_ref[0] * (IDX_HEADS ** -0.5)
    for r in range(PAGES_PER_STEP):
        d = lax.dot_general(qi, pages[r][...].astype(BF16), _NT, preferred_element_type=F32) * (IDX_DIM ** -0.5)
        o_ref[0, :, r * PAGE_SIZE:(r + 1) * PAGE_SIZE] = jnp.sum(jnp.maximum(d, 0.0) * w, axis=0, keepdims=True)


def _dsa_sample_select_body(topk, s_ref, qi_ref, w_ref, kn_ref, m_ref):
    B, P = s_ref.shape
    w = w_ref[...] * (IDX_HEADS ** -0.5)
    dn = jnp.sum(qi_ref[...] * kn_ref[...], axis=2) * (IDX_DIM ** -0.5)
    s_new = jnp.sum(jnp.maximum(dn, 0.0) * w, axis=1, keepdims=True)

    def order_key(x):
        bits = pltpu.bitcast(x, jnp.int32)
        return jnp.where(bits < 0, bits ^ jnp.int32(0x7FFFFFFF), bits)
    key = order_key(s_ref[...])
    key_new = order_key(jnp.broadcast_to(s_new, (B, LANES)))[:, :1]
    pos = lax.broadcasted_iota(jnp.int32, (B, P), 1)

    def count(cond, cond_new):
        return _count(cond) + jnp.where(cond_new, 1.0, 0.0)

    def value_step(i, lo):
        cand = lo + jnp.left_shift(jnp.int32(1), 31 - i)
        return jnp.where(count(key >= cand, key_new >= cand) >= float(topk), cand, lo)
    thr = lax.fori_loop(0, 32, value_step, jnp.full((B, 1), jnp.int32(-2 ** 31), jnp.int32))
    need = float(topk) - count(key > thr, key_new > thr)
    tied = key == thr
    n_bits = max(1, (P - 1).bit_length())

    def index_step(i, m):
        cand = m + jnp.left_shift(jnp.int32(1), n_bits - 1 - i)
        return jnp.where(_count(tied & (pos < cand)) < need, cand, m)
    last = lax.fori_loop(0, n_bits, index_step, jnp.zeros((B, 1), jnp.int32))
    sel = (key > thr) | (tied & (pos <= last))
    taken = _count(sel)
    sel_new = (key_new > thr) | ((key_new == thr) & (taken < float(topk)))
    m_ref[:, :P] = jnp.where(sel, 1.0, 0.0)
    lane = lax.broadcasted_iota(jnp.int32, (B, LANES), 1)
    m_ref[:, P:] = jnp.where((lane == 0) & sel_new, 1.0, 0.0)


def _dsa_sample_attend_body(pt_ref, *refs):
    n = PAGES_PER_STEP
    kp, vp = refs[:n], refs[n:2 * n]
    q_ref, mask_ref, bias_ref, mnew_ref, kn_ref, vn_ref, b0_ref, o_ref, m_ref, l_ref, acc_ref = refs[2 * n:]
    c = pl.program_id(1)
    scale = B_DH ** -0.5

    @pl.when(c == 0)
    def _():
        m_ref[...] = jnp.full(m_ref.shape, NEG, F32)
        l_ref[...] = jnp.zeros(l_ref.shape, F32)
        acc_ref[...] = jnp.zeros(acc_ref.shape, F32)

    mask = mask_ref[0] > 0.5
    for g in range(B_GROUPS):
        sl = slice(g * B_DH, (g + 1) * B_DH)
        k = jnp.concatenate([kp[r][:, sl] for r in range(n)], axis=0).astype(BF16)
        v = jnp.concatenate([vp[r][:, sl] for r in range(n)], axis=0).astype(BF16)
        s = lax.dot_general(q_ref[0, g].astype(BF16), k, _NT, preferred_element_type=F32) * scale + bias_ref[g]
        s = jnp.where(mask, s, NEG)
        m_prev = m_ref[g][:, :1]
        m_new = jnp.maximum(m_prev, jnp.max(s, axis=1, keepdims=True))
        p = jnp.where(mask, jnp.exp(s - m_new), 0.0)
        alpha = jnp.exp(m_prev - m_new)
        l_ref[g] = jnp.broadcast_to(alpha * l_ref[g][:, :1] + jnp.sum(p, axis=1, keepdims=True), l_ref.shape[1:])
        acc_ref[g] = alpha * acc_ref[g] + jnp.dot(p.astype(BF16), v, preferred_element_type=F32)
        m_ref[g] = jnp.broadcast_to(m_new, m_ref.shape[1:])

    @pl.when(c == pl.num_programs(1) - 1)
    def _():
        new_on = mnew_ref[0][:, :1] > 0.5
        for g in range(B_GROUPS):
            sn = jnp.sum(q_ref[0, g] * kn_ref[0, g], axis=1, keepdims=True) * scale + b0_ref[g]
            sn = jnp.where(new_on, sn, NEG)
            m_prev = m_ref[g][:, :1]
            m_new = jnp.maximum(m_prev, sn)
            alpha = jnp.exp(m_prev - m_new)
            en = jnp.where(new_on, jnp.exp(sn - m_new), 0.0)
            l = alpha * l_ref[g][:, :1] + en
            o_ref[0, g] = (alpha * acc_ref[g] + en * vn_ref[0, g]) / l


def dsa_decode(q, k, v, qi, ki, wi, cache_k, cache_v, cache_kidx, page_table, tab_b):
    B, S, G, HPG, DH = q.shape
    n_pool = cache_k.shape[0]
    n_pages = page_table.shape[1]
    P = n_pages * PAGE_SIZE
    n_chunks = n_pages // PAGES_PER_STEP
    chunk = PAGES_PER_STEP * PAGE_SIZE
    assert S == 1 and n_pages % PAGES_PER_STEP == 0
    topk = min(IDX_TOPK, (P + S) // 4)
    params = pltpu.CompilerParams(dimension_semantics=("parallel", "arbitrary"), vmem_limit_bytes=VMEM_LIMIT)
    scores = pl.pallas_call(
        _dsa_scores_body,
        grid_spec=pltpu.PrefetchScalarGridSpec(
            num_scalar_prefetch=1, grid=(B, n_chunks),
            in_specs=_page_specs(IDX_DIM, 0, _chunk_page)
            + [pl.BlockSpec((1, IDX_HEADS, IDX_DIM), lambda b, c, pt: (b, 0, 0)),
               pl.BlockSpec((1, IDX_HEADS, 1), lambda b, c, pt: (b, 0, 0))],
            out_specs=pl.BlockSpec((1, 1, chunk), lambda b, c, pt: (b, 0, c))),
        out_shape=jax.ShapeDtypeStruct((B, 1, P), F32),
        compiler_params=params, name="dsa_sample_scores",
    )(page_table, *([cache_kidx] * PAGES_PER_STEP), qi[:, 0], wi[:, 0, :, None])
    sel = pl.pallas_call(
        functools.partial(_dsa_sample_select_body, topk),
        out_shape=jax.ShapeDtypeStruct((B, P + LANES), F32),
        compiler_params=pltpu.CompilerParams(vmem_limit_bytes=VMEM_LIMIT), name="dsa_sample_select",
    )(scores.reshape(B, P), qi[:, 0], wi[:, 0], ki)
    sel = sel.reshape(B, 1, P + LANES)
    brev, b0 = bias_by_position(tab_b, P)
    ck = cache_k.reshape(n_pool, PAGE_SIZE, G * DH)
    cv = cache_v.reshape(n_pool, PAGE_SIZE, G * DH)
    whole = lambda *shape: pl.BlockSpec((1,) + shape, lambda b, c, pt: (b,) + (0,) * len(shape))
    o = pl.pallas_call(
        _dsa_sample_attend_body,
        grid_spec=pltpu.PrefetchScalarGridSpec(
            num_scalar_prefetch=1, grid=(B, n_chunks),
            in_specs=_page_specs(G * DH, 0, _chunk_page) + _page_specs(G * DH, 0, _chunk_page)
            + [whole(G, HPG, DH),
               pl.BlockSpec((1, 1, chunk), lambda b, c, pt: (b, 0, c)),
               pl.BlockSpec((G, HPG, chunk), lambda b, c, pt: (0, 0, c)),
               pl.BlockSpec((1, 1, LANES), lambda b, c, pt: (b, 0, P // LANES)),
               whole(G, 1, DH), whole(G, 1, DH),
               pl.BlockSpec((G, HPG, 1), lambda b, c, pt: (0, 0, 0))],
            out_specs=whole(G, HPG, DH),
            scratch_shapes=[pltpu.VMEM((G, HPG, LANES), F32), pltpu.VMEM((G, HPG, LANES), F32),
                            pltpu.VMEM((G, HPG, DH), F32)]),
        out_shape=jax.ShapeDtypeStruct((B, G, HPG, DH), F32),
        compiler_params=params, name="dsa_sample_attend",
    )(page_table, *([ck] * PAGES_PER_STEP), *([cv] * PAGES_PER_STEP), q[:, 0], sel, brev.reshape(G, HPG, P), sel,
      jnp.moveaxis(k, 1, 2), jnp.moveaxis(v, 1, 2), b0.reshape(G, HPG, 1))
    return o.reshape(B, S, G * HPG * DH)


def _moba_kmean_body(pt_ref, p0_ref, p1_ref, o_ref):
    o_ref[0] = (jnp.sum(p0_ref[...], axis=0) + jnp.sum(p1_ref[...], axis=0)) * (1.0 / MOBA_BLK)


def _moba_sample_pick_body(nf, q_ref, km_ref, idx_ref, val_ref):
    B, H, DH = q_ref.shape
    nfp = km_ref.shape[1]
    group = lax.broadcasted_iota(jnp.int32, (H, nfp), 0) // C_HPG
    lane = lax.broadcasted_iota(jnp.int32, (H, nfp), 1)
    for b in range(B):
        qb = q_ref[b].astype(BF16)
        s = jnp.full((H, nfp), REMOVED, F32)
        for g in range(C_GROUPS):
            kg = km_ref[b, :, g, :].astype(BF16)
            sg = lax.dot_general(qb, kg, _NT, preferred_element_type=F32)
            s = jnp.where((group == g) & (lane < nf), sg, s)
        idx, val = _topk_rows(s, MOBA_TOP)
        idx_ref[b] = idx.astype(jnp.int32)
        val_ref[b] = val.astype(jnp.int32)


def _moba_sample_attend_body(idx_ref, val_ref, pt_ref, q_ref, k0_ref, k1_ref, v0_ref, v1_ref, bias_ref,
                             kn_ref, vn_ref, b0_ref, o_ref, m_ref, l_ref, acc_ref):
    b, h, r = pl.program_id(0), pl.program_id(1), pl.program_id(2)
    scale = C_DH ** -0.5
    q = q_ref[0, 0]
    group = h // C_HPG

    @pl.when(r == 0)
    def _():
        m_ref[...] = jnp.full(m_ref.shape, NEG, F32)
        l_ref[...] = jnp.zeros(l_ref.shape, F32)
        acc_ref[...] = jnp.zeros(acc_ref.shape, F32)

    def attend(g):
        k = jnp.concatenate([k0_ref[:, g, :], k1_ref[:, g, :]], axis=0).astype(BF16)
        v = jnp.concatenate([v0_ref[:, g, :], v1_ref[:, g, :]], axis=0).astype(BF16)
        s = lax.dot_general(q.astype(BF16), k, _NT, preferred_element_type=F32) * scale + bias_ref[0]
        m_prev = m_ref[:, :1]
        m_new = jnp.maximum(m_prev, jnp.max(s, axis=1, keepdims=True))
        p = jnp.exp(s - m_new)
        alpha = jnp.exp(m_prev - m_new)
        l_ref[...] = jnp.broadcast_to(alpha * l_ref[:, :1] + jnp.sum(p, axis=1, keepdims=True), l_ref.shape)
        acc_ref[...] = alpha * acc_ref[...] + jnp.dot(p.astype(BF16), v, preferred_element_type=F32)
        m_ref[...] = jnp.broadcast_to(m_new, m_ref.shape)

    chosen = val_ref[(b * C_HEADS + h) * MOBA_TOP + r] > 0
    for g in range(C_GROUPS):
        pl.when(chosen & (group == g))(functools.partial(attend, g))

    @pl.when(r == MOBA_TOP - 1)
    def _():
        sn = jnp.sum(q * kn_ref[0, 0], axis=1, keepdims=True) * scale + b0_ref[0]
        m_prev = m_ref[:, :1]
        m_new = jnp.maximum(m_prev, sn)
        alpha = jnp.exp(m_prev - m_new)
        en = jnp.exp(sn - m_new)
        o_ref[0, 0] = (alpha * acc_ref[...] + en * vn_ref[0, 0]) / (alpha * l_ref[:, :1] + en)


def moba_decode(q, k, v, cache_k, cache_v, layer, page_table, tab_c):
    B, S, H, DH = q.shape
    n_pages = page_table.shape[1]
    P = n_pages * PAGE_SIZE
    ppb = MOBA_BLK // PAGE_SIZE
    nf = (P + S) // MOBA_BLK
    assert S == 1 and ppb == 2 and P % MOBA_BLK == 0 and nf == P // MOBA_BLK and nf >= MOBA_TOP
    page_block = (None, None, PAGE_SIZE, C_GROUPS, DH)
    kmean = pl.pallas_call(
        _moba_kmean_body,
        grid_spec=pltpu.PrefetchScalarGridSpec(
            num_scalar_prefetch=1, grid=(B, nf),
            in_specs=[pl.BlockSpec(page_block, lambda b, j, pt: (layer, pt[b, ppb * j], 0, 0, 0)),
                      pl.BlockSpec(page_block, lambda b, j, pt: (layer, pt[b, ppb * j + 1], 0, 0, 0))],
            out_specs=pl.BlockSpec((1, C_GROUPS, DH), lambda b, j, pt: (b * nf + j, 0, 0))),
        out_shape=jax.ShapeDtypeStruct((B * nf, C_GROUPS, DH), F32),
        compiler_params=pltpu.CompilerParams(dimension_semantics=("parallel", "parallel"),
                                             vmem_limit_bytes=VMEM_LIMIT),
        name="moba_sample_kmean",
    )(page_table, cache_k, cache_k).reshape(B, nf, C_GROUPS, DH)
    idx, val = pl.pallas_call(
        functools.partial(_moba_sample_pick_body, nf),
        out_shape=[jax.ShapeDtypeStruct((B, H, LANES), jnp.int32)] * 2,
        compiler_params=pltpu.CompilerParams(vmem_limit_bytes=VMEM_LIMIT), name="moba_sample_pick",
    )(q[:, 0], jnp.pad(kmean, ((0, 0), (0, -nf % LANES), (0, 0), (0, 0))))
    idx_flat = idx[:, :, :MOBA_TOP].reshape(-1)
    val_flat = val[:, :, :MOBA_TOP].reshape(-1)
    brev, b0 = bias_by_position(tab_c, P)

    def blk(b, h, r, ix):
        return jnp.minimum(ix[(b * H + h) * MOBA_TOP + r], nf - 1)

    def page_spec(which):
        return pl.BlockSpec(page_block,
                            lambda b, h, r, ix, vl, pt: (layer, pt[b, ppb * blk(b, h, r, ix) + which], 0, 0, 0))
    per_head = lambda rows, d: pl.BlockSpec((1, 1, rows, d), lambda b, h, r, ix, vl, pt: (b, h, 0, 0))
    per_group = lambda d: pl.BlockSpec((1, 1, 1, d), lambda b, h, r, ix, vl, pt: (b, h // C_HPG, 0, 0))
    o = pl.pallas_call(
        _moba_sample_attend_body,
        grid_spec=pltpu.PrefetchScalarGridSpec(
            num_scalar_prefetch=3, grid=(B, H, MOBA_TOP),
            in_specs=[per_head(1, DH), page_spec(0), page_spec(1), page_spec(0), page_spec(1),
                      pl.BlockSpec((1, 1, MOBA_BLK), lambda b, h, r, ix, vl, pt: (h * nf + blk(b, h, r, ix), 0, 0)),
                      per_group(DH), per_group(DH),
                      pl.BlockSpec((1, 1, 1), lambda b, h, r, ix, vl, pt: (h, 0, 0))],
            out_specs=per_head(1, DH),
            scratch_shapes=[pltpu.VMEM((1, LANES), F32), pltpu.VMEM((1, LANES), F32), pltpu.VMEM((1, DH), F32)]),
        out_shape=jax.ShapeDtypeStruct((B, H, 1, DH), F32),
        compiler_params=pltpu.CompilerParams(dimension_semantics=("parallel", "parallel", "arbitrary"),
                                             vmem_limit_bytes=VMEM_LIMIT),
        name="moba_sample_attend",
    )(idx_flat, val_flat, page_table, q[:, 0, :, None, :], cache_k, cache_k, cache_v, cache_v,
      brev.reshape(H * nf, 1, MOBA_BLK),
      jnp.moveaxis(k, 1, 2), jnp.moveaxis(v, 1, 2), b0.reshape(H, 1, 1))
    return o.reshape(B, S, H * DH)


def split_cols(z, sizes):
    cuts = [int(c) for c in np.cumsum(sizes)[:-1]]
    return jnp.split(z, cuts, axis=-1)


def masked_softmax(logits, mask):
    p = jax.nn.softmax(jnp.where(mask, logits, NEG), axis=-1)
    return jnp.where(mask, p, 0.0)


def t5_bucket(dist):
    n = jnp.maximum(dist, 0)
    nf = jnp.maximum(n, 1).astype(jnp.float32)
    large = REL_EXACT + (jnp.log(nf / REL_EXACT) / math.log(REL_MAX_DIST / REL_EXACT)
                         * (N_BUCKETS - REL_EXACT)).astype(jnp.int32)
    return jnp.where(n < REL_EXACT, n, jnp.minimum(large, N_BUCKETS - 1))


def gather_paged(pool, page_table, new_rows, pos, *extra):
    past_len = page_table.shape[1] * PAGE_SIZE
    b = jnp.arange(pos.shape[0]).reshape((-1,) + (1,) * (pos.ndim - 1))
    pc = jnp.clip(pos, 0, past_len - 1)
    phys = page_table[b, pc // PAGE_SIZE]
    old = pool[(phys, pc % PAGE_SIZE) + extra]
    new = new_rows[(b, jnp.clip(pos - past_len, 0, new_rows.shape[1] - 1)) + extra]
    is_new = (pos >= past_len).reshape(pos.shape + (1,) * (old.ndim - pos.ndim))
    return jnp.where(is_new, new, old)


def nsa_compress(rows, pe, w1, w2):
    B, L, G, d = rows.shape
    nc = L // CMP_BLK
    blk = rows[:, : nc * CMP_BLK].reshape(B, nc, CMP_BLK, G, d) + pe[None, None, :, None, :]
    flat = blk.transpose(0, 1, 3, 2, 4).reshape(B, nc, G, CMP_BLK * d)
    return matmul(jax.nn.silu(matmul(flat, w1)), w2)


def nsa_cmp_branch(q, q_pos, kc, vc):
    nc = kc.shape[1]
    logits = jnp.einsum('bqghd,bjgd->bqghj', q, kc, preferred_element_type=jnp.float32) * A_DK ** -0.5
    vis = (jnp.arange(nc) + 1) * CMP_BLK <= (q_pos + 1)[:, None]
    p = masked_softmax(logits, vis[None, :, None, None, :])
    o = jnp.einsum('bqghj,bjgd->bqghd', p.astype(vc.dtype), vc)
    return o, p.sum(axis=3)


def nsa_pick_blocks(imp, q_pos, n_blocks):
    score = jnp.pad(imp, ((0, 0), (0, 0), (0, 0), (0, n_blocks - imp.shape[-1])))
    j = jnp.arange(n_blocks)[None, :]
    own = (q_pos // CMP_BLK)[:, None]
    forced = (j == 0) | (j == own) | (j == own - 1)
    score = jnp.where(forced[None, :, None, :], FORCE, score)
    score = jnp.where((j <= own)[None, :, None, :], score, NEG)
    top_s, idx = lax.top_k(score, min(N_SEL, n_blocks))
    return idx, top_s > 0.5 * NEG


def nsa_sel_attend(q, q_pos, idx, valid, ksel, vsel, tab_a):
    B, Q, G, HPG, _ = q.shape
    kpos = idx[..., None] * CMP_BLK + jnp.arange(CMP_BLK)
    dist = q_pos[None, :, None, None, None] - kpos
    mask = valid[..., None] & (dist >= 0)
    tab2 = tab_a.reshape(N_BUCKETS, G, HPG).transpose(1, 0, 2)
    bias = tab2[jnp.arange(G).reshape(1, 1, G, 1, 1), t5_bucket(dist)]
    logits = (jnp.einsum('bqghd,bqgnsd->bqghns', q, ksel, preferred_element_type=jnp.float32) * A_DK ** -0.5
              + jnp.moveaxis(bias, -1, 3).astype(jnp.float32))
    shp = logits.shape
    p = masked_softmax(logits.reshape(B, Q, G, HPG, -1), mask.reshape(B, Q, G, 1, -1)).reshape(shp)
    return jnp.einsum('bqghns,bqgnsd->bqghd', p.astype(vsel.dtype), vsel)


def window_attend(q, q_pos, k, v, k_pos, tab_a):
    B, Q, G, HPG, _ = q.shape
    dist = q_pos[:, None] - k_pos[None, :]
    mask = (dist >= 0) & (dist < WINDOW) & (k_pos >= 0)[None, :]
    bias = tab_a[t5_bucket(dist)].reshape(Q, -1, G, HPG).transpose(0, 2, 3, 1).astype(jnp.float32)
    logits = jnp.einsum('bqghd,bsgd->bqghs', q, k, preferred_element_type=jnp.float32) * A_DK ** -0.5 + bias[None]
    p = masked_softmax(logits, mask[None, :, None, None, :])
    return jnp.einsum('bqghs,bsgd->bqghd', p.astype(v.dtype), v)


def nsa_combine(gates, o_c, o_s, o_w):
    g = gates[..., None].astype(o_c.dtype)
    o = g[:, :, 0] * o_c + g[:, :, 1] * o_s + g[:, :, 2] * o_w
    return o.reshape(o.shape[0], o.shape[1], -1)


def _heads_first(a):
    return jnp.moveaxis(a, 1, -2)


def nsa_prompt(q, k3, v3, gates, pe_k, pe_v, wk1, wk2, wv1, wv2, tab_a):
    B, T, G, HPG, DK = q.shape
    assert T % TQ == 0 and TQ == TK and (T // CMP_BLK) * CMP_BLK == T
    kc = nsa_compress(k3[:, :, 0], pe_k, wk1, wk2)
    vc = nsa_compress(v3[:, :, 0], pe_v, wv1, wv2)
    qh = _heads_first(q)
    o_c, sel_mask = nsa_cmp_select(qh, _heads_first(kc), _heads_first(vc))
    tiles = toeplitz_bias_tiles(tab_a)
    scale = A_DK ** -0.5
    o_s = block_attention('key', qh, _heads_first(k3[:, :, 1]), _heads_first(v3[:, :, 1]), tiles, sel_mask, scale)
    o_w = block_attention('window', qh, _heads_first(k3[:, :, 2]), _heads_first(v3[:, :, 2]), tiles, None, scale)
    shp = (B, T, G, HPG, A_DV)
    return nsa_combine(gates, o_c.reshape(shp), o_s.reshape(shp), o_w.reshape(shp))


def nsa_sample(q, k3, v3, gates, cache_k, cache_v, win_k, win_v, page_table,
               pe_k, pe_v, wk1, wk2, wv1, wv2, tab_a):
    B, S, G, HPG, DK = q.shape
    P = page_table.shape[1] * PAGE_SIZE
    L = P + S
    pos = P + jnp.arange(S)
    rows_k = jnp.concatenate([cache_k[page_table, :, 0].reshape(B, P, G, DK), k3[:, :, 0]], axis=1)
    rows_v = jnp.concatenate([cache_v[page_table, :, 0].reshape(B, P, G, A_DV), v3[:, :, 0]], axis=1)
    kc = nsa_compress(rows_k, pe_k, wk1, wk2)
    vc = nsa_compress(rows_v, pe_v, wv1, wv2)
    o_c, imp = nsa_cmp_branch(q, pos, kc, vc)
    idx, valid = nsa_pick_blocks(imp, pos, -(-L // CMP_BLK))
    kpos = idx[..., None] * CMP_BLK + jnp.arange(CMP_BLK)
    gi = jnp.arange(G).reshape(1, 1, G, 1, 1)
    ksel = gather_paged(cache_k, page_table, k3, kpos, 1, gi)
    vsel = gather_paged(cache_v, page_table, v3, kpos, 1, gi)
    o_s = nsa_sel_attend(q, pos, idx, valid, ksel, vsel, tab_a)
    wb = win_k.shape[1]
    kw = jnp.concatenate([win_k, k3[:, :, 2]], axis=1)
    vw = jnp.concatenate([win_v, v3[:, :, 2]], axis=1)
    o_w = window_attend(q, pos, kw, vw, P - wb + jnp.arange(wb + S), tab_a)
    return nsa_combine(gates, o_c, o_s, o_w)


def dsa_pick(qi, wi, ki, q_pos, topk):
    L = ki.shape[1]
    dots = jnp.einsum('bqhd,bld->bqhl', qi, ki, preferred_element_type=jnp.float32) * IDX_DIM ** -0.5
    score = jnp.einsum('bqhl,bqh->bql', jax.nn.relu(dots), wi.astype(jnp.float32) * IDX_HEADS ** -0.5)
    score = jnp.where(jnp.arange(L)[None, None, :] <= q_pos[None, :, None], score, NEG)
    _, idx = lax.top_k(score, topk)
    return idx, idx <= q_pos[None, :, None]


def dsa_attend(q, q_pos, idx, valid, ksel, vsel, tab_b):
    B, Q, G, HPG, DH = q.shape
    dist = q_pos[None, :, None] - idx
    bias = tab_b[t5_bucket(dist)].reshape(B, Q, -1, G, HPG).transpose(0, 1, 3, 4, 2).astype(jnp.float32)
    logits = jnp.einsum('bqghd,bqkgd->bqghk', q, ksel, preferred_element_type=jnp.float32) * DH ** -0.5 + bias
    p = masked_softmax(logits, valid[:, :, None, None, :])
    o = jnp.einsum('bqghk,bqkgd->bqghd', p.astype(vsel.dtype), vsel)
    return o.reshape(B, Q, -1)


def dsa_prompt(q, k, v, qi, ki, wi, tab_b):
    B, T = q.shape[:2]
    assert T % TQ == 0 and TQ == TK
    sel_mask = dsa_select(_heads_first(qi), ki, wi, min(IDX_TOPK, T // 4))
    return block_attention('key', _heads_first(q), _heads_first(k), _heads_first(v),
                           toeplitz_bias_tiles(tab_b), sel_mask, B_DH ** -0.5)


def dsa_sample(q, k, v, qi, ki, wi, cache_k, cache_v, cache_kidx, page_table, tab_b):
    B, S = q.shape[:2]
    P = page_table.shape[1] * PAGE_SIZE
    L = P + S
    q_pos = P + jnp.arange(S)
    ki_all = jnp.concatenate([cache_kidx[page_table].reshape(B, P, IDX_DIM), ki], axis=1)
    idx, valid = dsa_pick(qi, wi, ki_all, q_pos, min(IDX_TOPK, L // 4))
    return dsa_attend(q, q_pos, idx, valid, gather_paged(cache_k, page_table, k, idx),
                      gather_paged(cache_v, page_table, v, idx), tab_b)


def moba_pick(q, q_pos, kmean):
    B, Q, H, DH = q.shape
    nf = kmean.shape[1]
    s = jnp.einsum('bqghd,bjgd->bqghj', q.reshape(B, Q, C_GROUPS, C_HPG, DH).astype(jnp.float32),
                   kmean).reshape(B, Q, H, nf)
    ncand = max(nf, MOBA_TOP)
    s = jnp.pad(s, ((0, 0), (0, 0), (0, 0), (0, ncand - nf)), constant_values=NEG)
    past = jnp.arange(ncand)[None, :] < (q_pos // MOBA_BLK)[:, None]
    s = jnp.where(past[None, :, None, :], s, NEG)
    top_s, idx = lax.top_k(s, MOBA_TOP)
    return idx, top_s > 0.5 * NEG


def moba_attend(q, q_pos, idx, valid, ksel, vsel, own_pos, kown, vown, tab_c):
    B, Q, H, DH = q.shape
    scale = DH ** -0.5
    sel_dist = q_pos[None, :, None, None, None] - (idx[..., None] * MOBA_BLK + jnp.arange(MOBA_BLK))
    hidx = jnp.arange(H).reshape(1, 1, H, 1, 1)
    ls = (jnp.einsum('bqhd,bqhrsd->bqhrs', q, ksel, preferred_element_type=jnp.float32) * scale
          + tab_c.T[hidx, t5_bucket(sel_dist)].astype(jnp.float32))
    ls = jnp.where(valid[..., None], ls, NEG).reshape(B, Q, H, MOBA_TOP * MOBA_BLK)
    own_dist = q_pos[:, None] - own_pos
    lo = jnp.einsum('bqghd,bqgsd->bqghs', q.reshape(B, Q, C_GROUPS, C_HPG, DH), kown,
                    preferred_element_type=jnp.float32).reshape(B, Q, H, MOBA_BLK) * scale
    lo = lo + tab_c[t5_bucket(own_dist)].transpose(0, 2, 1)[None].astype(jnp.float32)
    lo = jnp.where((own_dist >= 0)[None, :, None, :], lo, NEG)
    p = jax.nn.softmax(jnp.concatenate([ls, lo], axis=-1), axis=-1)
    ps = p[..., : MOBA_TOP * MOBA_BLK].reshape(B, Q, H, MOBA_TOP, MOBA_BLK).astype(vsel.dtype)
    po = p[..., MOBA_TOP * MOBA_BLK:].reshape(B, Q, C_GROUPS, C_HPG, MOBA_BLK).astype(vown.dtype)
    o = (jnp.einsum('bqhrs,bqhrsd->bqhd', ps, vsel)
         + jnp.einsum('bqghs,bqgsd->bqghd', po, vown).reshape(B, Q, H, DH))
    return o


def moba_prompt(q, k, v, tab_c):
    B, T, H, DH = q.shape
    assert T % MOBA_BLK == 0 and TQ == MOBA_BLK and TK == MOBA_BLK
    qh = _heads_first(q.reshape(B, T, C_GROUPS, C_HPG, DH))
    kh, vh = _heads_first(k), _heads_first(v)
    flags = moba_select(qh, kh).transpose(0, 1, 3, 2)
    return block_attention('moba', qh, kh, vh, toeplitz_bias_tiles(tab_c), flags, DH ** -0.5)


def moba_sample(q, k, v, cache_k, cache_v, page_table, tab_c):
    B, S, H, DH = q.shape
    P = page_table.shape[1] * PAGE_SIZE
    L = P + S
    q_pos = P + jnp.arange(S)
    k_all = jnp.concatenate([cache_k[page_table].reshape(B, P, C_GROUPS, DH), k], axis=1)
    nf = L // MOBA_BLK
    kmean = k_all[:, : nf * MOBA_BLK].reshape(B, nf, MOBA_BLK, C_GROUPS, DH).astype(jnp.float32).mean(axis=2)
    idx, valid = moba_pick(q, q_pos, kmean)
    gh = (jnp.arange(H) // C_HPG).reshape(1, 1, H, 1, 1)
    sel_pos = idx[..., None] * MOBA_BLK + jnp.arange(MOBA_BLK)
    own_pos = (q_pos // MOBA_BLK)[:, None] * MOBA_BLK + jnp.arange(MOBA_BLK)
    opos = jnp.broadcast_to(own_pos[None, :, None, :], (B, S, C_GROUPS, MOBA_BLK))
    gi = jnp.arange(C_GROUPS).reshape(1, 1, C_GROUPS, 1)
    o = moba_attend(q, q_pos, idx, valid,
                    gather_paged(cache_k, page_table, k, sel_pos, gh), gather_paged(cache_v, page_table, v, sel_pos, gh),
                    own_pos, gather_paged(cache_k, page_table, k, opos, gi), gather_paged(cache_v, page_table, v, opos, gi),
                    tab_c)
    return o.reshape(B, S, H * DH)


def mem_kv(mem, g, wk, wv, layer):
    m = rms_norm(mem, g, BF16)
    B = mem.shape[0]
    return (matmul(m, wk, layer=layer).reshape(B, N_MEM, X_HEADS, X_DH),
            matmul(m, wv, layer=layer).reshape(B, N_MEM, X_HEADS, X_DH))


def cross_attend(x, h, mk, mv, wq, wo, layer):
    B, T = h.shape[:2]
    q = matmul(h, wq, layer=layer).reshape(B, T, X_HEADS, X_DH)
    logits = jnp.einsum('bthd,bmhd->bthm', q, mk, preferred_element_type=jnp.float32) * X_DH ** -0.5
    p = jax.nn.softmax(logits, axis=-1)
    o = jnp.einsum('bthm,bmhd->bthd', p.astype(mv.dtype), mv).reshape(B, T, X_HEADS * X_DH)
    return matmul(o, wo, resid=x, layer=layer)


def dense_swiglu(x, h, w1, w3, w2, layer):
    rows = h.shape[0]
    tm = _row_tile(rows)
    tables = _dense_tables(rows, tm)
    g = swiglu_up(h, w1[:, None], w3[:, None], tables, tm, layer)
    return grouped_matmul(g, [w2[:, None]], tables, tm, 1024, 512, F32, resid=x, layer=layer)


MOE_TM = 2304


def moe_tables(top_e):
    A = top_e.size
    n_tiles = -(-(A + N_EXPERTS * (MOE_TM - 1)) // MOE_TM)
    e_flat = top_e.reshape(A).astype(jnp.int32)
    order = jnp.argsort(e_flat).astype(jnp.int32)
    counts = jnp.bincount(e_flat, length=N_EXPERTS).astype(jnp.int32)
    starts = jnp.cumsum(counts) - counts
    tiles_per = (counts + MOE_TM - 1) // MOE_TM
    tile_end = jnp.cumsum(tiles_per)
    tile_start = tile_end - tiles_per
    e_sorted = e_flat[order]
    prow_sorted = tile_start[e_sorted] * MOE_TM + (jnp.arange(A, dtype=jnp.int32) - starts[e_sorted])
    row_token = (jnp.arange(n_tiles * MOE_TM, dtype=jnp.int32) % (A // TOP_K)).at[prow_sorted].set(order // TOP_K)
    prow_of_assign = jnp.zeros((A,), jnp.int32).at[order].set(prow_sorted)
    ti = jnp.arange(n_tiles, dtype=jnp.int32)
    n_active = tile_end[-1]
    last = n_active - 1
    src = jnp.minimum(ti, last)
    te = jnp.minimum(jnp.searchsorted(tile_end, src, side='right').astype(jnp.int32), N_EXPERTS - 1)
    rows_in = jnp.clip(counts[te] - (src - tile_start[te]) * MOE_TM, 0, MOE_TM)
    nsb = jnp.where(ti < n_active, (rows_in + SUB_ROWS - 1) // SUB_ROWS, 0).astype(jnp.int32)
    return (te, nsb, src), row_token, prow_of_assign


def moe_swiglu(xs, hs, w_router, b_router, w1, w3, w2, layer):
    wr = jnp.pad(w_router, ((0, 0), (0, LANES - N_EXPERTS)))
    logits = jnp.concatenate([matmul(h, wr)[:, :N_EXPERTS] for h in hs], axis=0) + b_router.astype(jnp.float32)
    h = jnp.concatenate(hs, axis=0)
    N = h.shape[0]
    top_l, top_e = lax.top_k(logits, TOP_K)
    gate = jax.nn.softmax(top_l, axis=-1)
    tables, row_token, prow_of_assign = moe_tables(top_e)
    g = swiglu_up(h[row_token], w1, w3, tables, MOE_TM, layer)
    y = grouped_matmul(g, [w2], tables, MOE_TM, 1024, 512, F32, layer=layer)
    slots = prow_of_assign.reshape(N, TOP_K)
    mix = sum(y[slots[:, c]] * gate[:, c:c + 1] for c in range(TOP_K))
    outs, lo = [], 0
    for xg in xs:
        outs.append(xg + mix[lo:lo + xg.shape[0]])
        lo += xg.shape[0]
    return outs


def even_split(z):
    B, T = z.shape[:2]
    qa, ka, va, ga, qb, kb, vb, qi, ki, wi = split_cols(z, EVEN_SPLITS)
    return (qa.reshape(B, T, A_GROUPS, A_HPG, A_DK),
            ka.reshape(B, T, 3, A_GROUPS, A_DK),
            va.reshape(B, T, 3, A_GROUPS, A_DV),
            jax.nn.sigmoid(ga.astype(jnp.float32)).reshape(B, T, 3, A_GROUPS, A_HPG),
            qb.reshape(B, T, B_GROUPS, B_HPG, B_DH),
            kb.reshape(B, T, B_GROUPS, B_DH),
            vb.reshape(B, T, B_GROUPS, B_DH),
            qi.reshape(B, T, IDX_HEADS, IDX_DIM), ki, wi)


def odd_split(z):
    B, T = z.shape[:2]
    q, k, v = split_cols(z, ODD_SPLITS)
    return (q.reshape(B, T, C_HEADS, C_DH), k.reshape(B, T, C_GROUPS, C_DH), v.reshape(B, T, C_GROUPS, C_DH))


def kernel(x_prompt, x_sample, mem_prompt, cache_a_k, cache_a_v, state_a_win_k, state_a_win_v,
           cache_b_k, cache_b_v, cache_b_kidx, cache_c_k, cache_c_v, cache_mem_k, cache_mem_v, page_table,
           rel_bias, norm_mix, norm_mem, norm_cross, norm_ffn, norm_final,
           w_cross_q, w_cross_k, w_cross_v, w_cross_o, w_in_even, w_out_even,
           nsa_pe_k, nsa_pe_v, nsa_phi_k1, nsa_phi_k2, nsa_phi_v1, nsa_phi_v2,
           w_ffn1, w_ffn3, w_ffn2, w_in_odd, w_out_odd, w_router, b_router, w_exp1, w_exp3, w_exp2):
    xp, xs = x_prompt, x_sample
    B, T, D = xp.shape
    Bs, S = xs.shape[:2]
    tab_a = rel_bias[:, :A_HEADS]
    tab_b = rel_bias[:, A_HEADS:A_HEADS + B_HEADS]
    tab_c = rel_bias[:, :C_HEADS]
    names = ('a_k_p', 'a_v_p', 'aw_k_p', 'aw_v_p', 'b_k_p', 'b_v_p', 'b_i_p', 'c_k_p', 'c_v_p', 'm_k_p', 'm_v_p',
             'a_k_s', 'a_v_s', 'aw_k_s', 'aw_v_s', 'b_k_s', 'b_v_s', 'b_i_s', 'c_k_s', 'c_v_s')
    new = {n: [] for n in names}
    for layer in range(DEPTH):
        li = layer // 2
        hp = rms_norm(xp, norm_mix[layer], BF16)
        hs = rms_norm(xs, norm_mix[layer], BF16)
        if layer % 2 == 0:
            phi = (nsa_pe_k[li], nsa_pe_v[li], nsa_phi_k1[li], nsa_phi_k2[li], nsa_phi_v1[li], nsa_phi_v2[li])
            qa, ka, va, ga, qb, kb, vb, qi, ki, wi = even_split(matmul(hp, w_in_even, layer=li))
            o_a = nsa_prompt(qa, ka, va, ga, *phi, tab_a)
            o_b = dsa_prompt(qb, kb, vb, qi, ki, wi, tab_b)
            xp = matmul(jnp.concatenate([o_a, o_b], axis=-1), w_out_even, resid=xp, layer=li)
            wk = min(WINDOW, T)
            new['a_k_p'].append(ka[:, :, :2]); new['a_v_p'].append(va[:, :, :2])
            new['aw_k_p'].append(ka[:, T - wk:, 2]); new['aw_v_p'].append(va[:, T - wk:, 2])
            new['b_k_p'].append(kb); new['b_v_p'].append(vb); new['b_i_p'].append(ki)
            qa, ka, va, ga, qb, kb, vb, qi, ki, wi = even_split(matmul(hs, w_in_even, layer=li))
            o_a = nsa_decode(qa, ka, va, ga, cache_a_k[li], cache_a_v[li], state_a_win_k[li], state_a_win_v[li],
                             page_table, *phi, tab_a)
            o_b = dsa_decode(qb, kb, vb, qi, ki, wi, cache_b_k[li], cache_b_v[li], cache_b_kidx[li], page_table, tab_b)
            xs = matmul(jnp.concatenate([o_a, o_b], axis=-1), w_out_even, resid=xs, layer=li)
            new['a_k_s'].append(ka[:, :, :2]); new['a_v_s'].append(va[:, :, :2])
            new['aw_k_s'].append(ka[:, :, 2]); new['aw_v_s'].append(va[:, :, 2])
            new['b_k_s'].append(kb); new['b_v_s'].append(vb); new['b_i_s'].append(ki)
        else:
            q, k, v = odd_split(matmul(hp, w_in_odd, layer=li))
            xp = matmul(moba_prompt(q, k, v, tab_c), w_out_odd, resid=xp, layer=li)
            new['c_k_p'].append(k); new['c_v_p'].append(v)
            q, k, v = odd_split(matmul(hs, w_in_odd, layer=li))
            xs = matmul(moba_decode(q, k, v, cache_c_k, cache_c_v, li, page_table, tab_c), w_out_odd,
                        resid=xs, layer=li)
            new['c_k_s'].append(k); new['c_v_s'].append(v)
        mk, mv = mem_kv(mem_prompt, norm_mem[layer], w_cross_k, w_cross_v, layer)
        new['m_k_p'].append(mk); new['m_v_p'].append(mv)
        xp = cross_attend(xp, rms_norm(xp, norm_cross[layer], BF16), mk, mv, w_cross_q, w_cross_o, layer)
        xs = cross_attend(xs, rms_norm(xs, norm_cross[layer], BF16), cache_mem_k[layer], cache_mem_v[layer],
                          w_cross_q, w_cross_o, layer)
        hp = rms_norm(xp, norm_ffn[layer], BF16).reshape(B * T, D)
        hs = rms_norm(xs, norm_ffn[layer], BF16).reshape(Bs * S, D)
        xp2, xs2 = xp.reshape(B * T, D), xs.reshape(Bs * S, D)
        if layer % 2 == 0:
            xp2 = dense_swiglu(xp2, hp, w_ffn1, w_ffn3, w_ffn2, li)
            xs2 = dense_swiglu(xs2, hs, w_ffn1, w_ffn3, w_ffn2, li)
        else:
            xp2, xs2 = moe_swiglu([xp2, xs2], [hp, hs], w_router[li], b_router[li], w_exp1, w_exp3, w_exp2, li)
        xp = xp2.reshape(B, T, D)
        xs = xs2.reshape(Bs, S, D)
    y_prompt = rms_norm(xp, norm_final)
    y_sample = rms_norm(xs, norm_final)
    return (y_prompt, y_sample,
            jnp.stack(new['a_k_p']), jnp.stack(new['a_v_p']), jnp.stack(new['aw_k_p']), jnp.stack(new['aw_v_p']),
            jnp.stack(new['b_k_p']), jnp.stack(new['b_v_p']), jnp.stack(new['b_i_p']),
            jnp.stack(new['c_k_p']), jnp.stack(new['c_v_p']), jnp.stack(new['m_k_p']), jnp.stack(new['m_v_p']),
            jnp.stack(new['a_k_s']), jnp.stack(new['a_v_s']), jnp.stack(new['aw_k_s']), jnp.stack(new['aw_v_s']),
            jnp.stack(new['b_k_s']), jnp.stack(new['b_v_s']), jnp.stack(new['b_i_s']),
            jnp.stack(new['c_k_s']), jnp.stack(new['c_v_s']))
```

```python
import functools
import math
import jax, jax.numpy as jnp
from jax import lax
import numpy as np
from jax.experimental import pallas as pl
from jax.experimental.pallas import tpu as pltpu

D_MODEL = 4096
BATCH = 4
SEQ = 2048
DEPTH = 2
DEC_BATCH = 8
DEC_SEQ = 1
PAST_LEN = 16384
PAGE_SIZE = 128

N_EVEN = (DEPTH + 1) // 2
N_ODD = DEPTH // 2
HEAD_SLOTS = 32
A_HEADS = 16
A_GROUPS = 2
A_HPG = A_HEADS // A_GROUPS
A_DK = 192
A_DV = 128
CMP_BLK = 64
N_SEL = 16
WINDOW = 512
B_HEADS = 16
B_GROUPS = 2
B_HPG = B_HEADS // B_GROUPS
B_DH = 128
IDX_HEADS = 8
IDX_DIM = 64
IDX_TOPK = 256
C_HEADS = 32
C_GROUPS = 8
C_HPG = C_HEADS // C_GROUPS
C_DH = 128
MOBA_BLK = 256
MOBA_TOP = 3
MOBA_QCHUNK = 32
N_MEM = 256
X_HEADS = 4
X_DH = 128
D_FF = 14336
N_EXPERTS = 8
TOP_K = 2
MOE_MAX_ROWS = 512
N_BUCKETS = 32
REL_EXACT = 16
REL_MAX_DIST = 1024
Q_BLOCK = 128
EPS = 1e-6
NEG = -1e30
FORCE = 1e9
EVEN_SPLITS = (A_HEADS * A_DK, 3 * A_GROUPS * A_DK, 3 * A_GROUPS * A_DV, 3 * A_HEADS,
               B_HEADS * B_DH, B_GROUPS * B_DH, B_GROUPS * B_DH, IDX_HEADS * IDX_DIM, IDX_DIM, IDX_HEADS)
ODD_SPLITS = (C_HEADS * C_DH, C_GROUPS * C_DH, C_GROUPS * C_DH)


def _rmsnorm_body(x_ref, g_ref, o_ref):
    x = x_ref[...]
    y = x * lax.rsqrt(jnp.mean(x * x, axis=-1, keepdims=True) + EPS)
    o_ref[...] = (y * g_ref[...]).astype(o_ref.dtype)


def rms_norm(x, g, out_dtype=None):
    out_dtype = out_dtype or x.dtype
    shape = x.shape
    d = shape[-1]
    x2 = x.reshape(-1, d)
    rows = x2.shape[0]
    tr = min(rows, 256)
    out = pl.pallas_call(
        _rmsnorm_body,
        grid=(rows // tr,),
        in_specs=[pl.BlockSpec((tr, d), lambda i: (i, 0)),
                  pl.BlockSpec((1, d), lambda i: (0, 0))],
        out_specs=pl.BlockSpec((tr, d), lambda i: (i, 0)),
        out_shape=jax.ShapeDtypeStruct((rows, d), out_dtype),
    )(x2, g.reshape(1, d).astype(jnp.float32))
    return out.reshape(shape)


TQ = 256
TK = 256
N_OFF = -(-(REL_MAX_DIST + TK - 1) // TK) + 1
LANES = 128
VMEM_LIMIT = 48 * 1024 * 1024
_NT = (((1,), (1,)), ((), ()))
BF16 = jnp.bfloat16
F32 = jnp.float32


def bias_by_distance(tab, n_dist):
    return tab[t5_bucket(jnp.arange(n_dist))].T.astype(F32)


def toeplitz_bias_tiles(tab):
    H = tab.shape[1]
    bd = bias_by_distance(tab, N_OFF * TK + TQ)
    epad = jnp.concatenate([jnp.broadcast_to(bd[:, :1], (H, TK - 1)), bd], axis=1)
    w = TQ + TK - 1
    rows = []
    for o in range(N_OFF):
        erev = epad[:, o * TK: o * TK + w][:, ::-1]
        z = jnp.concatenate([erev, erev[:, :1]], axis=1)
        rows.append(jnp.roll(z, -(TQ - 1), axis=1))
    x = jnp.stack(rows, axis=1)
    y = jnp.tile(x, (1, 1, TQ))[:, :, : TQ * w].reshape(H, N_OFF, TQ, w)
    return y[..., :TK]


def _flash_body(mode, hpg, dv, scale, *refs):
    if mode == 'window':
        q_ref, k_ref, v_ref, b_ref, o_ref, m_ref, l_ref, acc_ref = refs
        x_ref = None
    else:
        q_ref, k_ref, v_ref, b_ref, x_ref, o_ref, m_ref, l_ref, acc_ref = refs
    qi = pl.program_id(2)
    ki = pl.program_id(3)

    @pl.when(ki == 0)
    def _():
        m_ref[...] = jnp.full(m_ref.shape, NEG, F32)
        l_ref[...] = jnp.zeros(l_ref.shape, F32)
        acc_ref[...] = jnp.zeros(acc_ref.shape, F32)

    active = ki <= qi
    if mode == 'window':
        active = active & (qi - ki <= WINDOW // TK)

    @pl.when(active)
    def _():
        k = k_ref[0, 0].astype(BF16)
        v = v_ref[0, 0].astype(BF16)
        dist = (qi - ki) * TK + (lax.broadcasted_iota(jnp.int32, (TQ, TK), 0)
                                 - lax.broadcasted_iota(jnp.int32, (TQ, TK), 1))
        if mode == 'key':
            shared_mask = x_ref[0, 0] > 0
        elif mode == 'window':
            shared_mask = (dist >= 0) & (dist < WINDOW)
        else:
            own_mask = jnp.where(dist >= 0, 1.0, 0.0)
            blk = lax.broadcasted_iota(jnp.int32, (TQ, x_ref.shape[-1]), 1)
        for h in range(hpg):
            if mode == 'moba':
                flag = jnp.sum(jnp.where(blk == ki, x_ref[0, h], 0.0), axis=1, keepdims=True)
                mask = jnp.where(ki == qi, own_mask, jnp.broadcast_to(flag, (TQ, TK))) > 0.5
            else:
                mask = shared_mask
            q = q_ref[0, 0, h].astype(BF16)
            s = lax.dot_general(q, k, _NT, preferred_element_type=F32) * scale + b_ref[h, 0]
            s = jnp.where(mask, s, NEG)
            m_prev = m_ref[h][:, :1]
            l_prev = l_ref[h][:, :1]
            m_new = jnp.maximum(m_prev, jnp.max(s, axis=1, keepdims=True))
            p = jnp.where(mask, jnp.exp(s - m_new), 0.0)
            alpha = jnp.exp(m_prev - m_new)
            l_new = alpha * l_prev + jnp.sum(p, axis=1, keepdims=True)
            acc_ref[h] = alpha * acc_ref[h] + jnp.dot(p.astype(BF16), v, preferred_element_type=F32)
            m_ref[h] = jnp.broadcast_to(m_new, (TQ, LANES))
            l_ref[h] = jnp.broadcast_to(l_new, (TQ, LANES))

    @pl.when(ki == qi)
    def _():
        for h in range(hpg):
            l = l_ref[h][:, :1]
            o_ref[0, :, h * dv:(h + 1) * dv] = jnp.where(l > 0.0, acc_ref[h] / jnp.where(l > 0.0, l, 1.0), 0.0)


def flash_attention(mode, q, k, v, bias_tiles, extra, scale):
    B, G, HPG, T, dk = q.shape
    dv = v.shape[-1]
    nq, nk = T // TQ, T // TK

    def kv_idx(b, g, qi, ki):
        lo = jnp.maximum(qi - WINDOW // TK, 0) if mode == 'window' else 0
        return (b, g, jnp.clip(ki, lo, qi), 0)

    in_specs = [
        pl.BlockSpec((1, 1, HPG, TQ, dk), lambda b, g, qi, ki: (b, g, 0, qi, 0)),
        pl.BlockSpec((1, 1, TK, dk), kv_idx),
        pl.BlockSpec((1, 1, TK, dv), kv_idx),
        pl.BlockSpec((HPG, 1, TQ, TK), lambda b, g, qi, ki: (g, jnp.clip(qi - ki, 0, N_OFF - 1), 0, 0)),
    ]
    args = [q, k, v, bias_tiles]
    if mode == 'key':
        gm = extra.shape[1]
        in_specs.append(pl.BlockSpec((1, 1, TQ, TK),
                                     lambda b, g, qi, ki: (b, g if gm > 1 else 0, qi, jnp.minimum(ki, qi))))
        args.append(extra)
    elif mode == 'moba':
        in_specs.append(pl.BlockSpec((1, HPG, TQ, extra.shape[-1]), lambda b, g, qi, ki: (b, g, qi, 0)))
        args.append(extra)
    return pl.pallas_call(
        functools.partial(_flash_body, mode, HPG, dv, scale),
        grid=(B, G, nq, nk),
        in_specs=in_specs,
        out_specs=pl.BlockSpec((1, TQ, HPG * dv), lambda b, g, qi, ki: (b, qi, g)),
        out_shape=jax.ShapeDtypeStruct((B, T, G * HPG * dv), F32),
        scratch_shapes=[pltpu.VMEM((HPG, TQ, LANES), F32), pltpu.VMEM((HPG, TQ, LANES), F32),
                        pltpu.VMEM((HPG, TQ, dv), F32)],
        compiler_params=pltpu.CompilerParams(
            dimension_semantics=("parallel", "parallel", "parallel", "arbitrary"),
            vmem_limit_bytes=VMEM_LIMIT),
        name=f"flash_{mode}",
    )(*args)


HEADS_PER_STEP = 4
PREFIX_STEP = 2


def _row_attn_body(mode, hs, dv, scale, n_tiles, *refs):
    if mode == 'window':
        q_ref, k_ref, v_ref, b_ref, o_ref = refs
        x_ref = None
    else:
        q_ref, k_ref, v_ref, b_ref, x_ref, o_ref = refs
    qi = pl.program_id(3)
    w = n_tiles * TK

    def heads(n, t0, k, v, mask_of):
        for h in range(hs):
            mask = mask_of(h, n)
            bias = jnp.concatenate([b_ref[h, jnp.clip(qi - (t0 + j), 0, N_OFF - 1)] for j in range(n)], axis=1)
            s = lax.dot_general(q_ref[0, 0, h].astype(BF16), k, _NT, preferred_element_type=F32) * scale + bias
            s = jnp.where(mask, s, NEG)
            p = jnp.where(mask, jnp.exp(s - jnp.max(s, axis=1, keepdims=True)), 0.0)
            l = jnp.sum(p, axis=1, keepdims=True)
            o = jnp.dot(p.astype(BF16), v, preferred_element_type=F32)
            o_ref[0, :, h * dv:(h + 1) * dv] = jnp.where(l > 0.0, o / jnp.where(l > 0.0, l, 1.0), 0.0)

    if mode == 'window':
        t0 = jnp.maximum(qi - (n_tiles - 1), 0)
        c0 = pl.multiple_of(t0 * TK, TK)
        k = k_ref[0, 0, pl.ds(c0, w), :].astype(BF16)
        v = v_ref[0, 0, pl.ds(c0, w), :].astype(BF16)
        dist = (qi * TQ + lax.broadcasted_iota(jnp.int32, (TQ, w), 0)) - (c0 + lax.broadcasted_iota(jnp.int32, (TQ, w), 1))
        shared_mask = (dist >= 0) & (dist < WINDOW)
        heads(n_tiles, t0, k, v, lambda h, n: shared_mask)
        return

    def causal_prefix(n):
        k = k_ref[0, 0, :n * TK, :].astype(BF16)
        v = v_ref[0, 0, :n * TK, :].astype(BF16)
        if mode == 'key':
            shared_mask = x_ref[0, 0, :, :n * TK] > 0
            heads(n, 0, k, v, lambda h, n: shared_mask)
        else:
            causal = jnp.where(lax.broadcasted_iota(jnp.int32, (TQ, TK), 0)
                               >= lax.broadcasted_iota(jnp.int32, (TQ, TK), 1), 1.0, 0.0)

            def moba_mask(h, n):
                flags = x_ref[0, h]
                return jnp.concatenate(
                    [jnp.where(qi == j, causal, jnp.broadcast_to(flags[:, j:j + 1], (TQ, TK))) for j in range(n)],
                    axis=1) > 0.5
            heads(n, 0, k, v, moba_mask)

    prefixes = sorted({min(n_tiles, p) for p in range(PREFIX_STEP, n_tiles + PREFIX_STEP, PREFIX_STEP)})
    lo = 0
    for n in prefixes:
        pl.when((qi + 1 > lo) & (qi + 1 <= n))(functools.partial(causal_prefix, n))
        lo = n


def block_attention(mode, q, k, v, bias_tiles, extra, scale):
    B, G, HPG, T, dk = q.shape
    dv = v.shape[-1]
    hs = min(HEADS_PER_STEP, HPG)
    nh = HPG // hs
    n_tiles = min(WINDOW // TK + 1, T // TK) if mode == 'window' else T // TK
    assert HPG % hs == 0 and T % TK == 0
    in_specs = [
        pl.BlockSpec((1, 1, hs, TQ, dk), lambda b, g, hh, qi: (b, g, hh, qi, 0)),
        pl.BlockSpec((1, 1, T, dk), lambda b, g, hh, qi: (b, g, 0, 0)),
        pl.BlockSpec((1, 1, T, dv), lambda b, g, hh, qi: (b, g, 0, 0)),
        pl.BlockSpec((hs, N_OFF, TQ, TK), lambda b, g, hh, qi: (g * nh + hh, 0, 0, 0)),
    ]
    args = [q, k, v, bias_tiles]
    if mode == 'key':
        gm = extra.shape[1]
        in_specs.append(pl.BlockSpec((1, 1, TQ, T), lambda b, g, hh, qi: (b, g if gm > 1 else 0, qi, 0)))
        args.append(extra)
    elif mode == 'moba':
        in_specs.append(pl.BlockSpec((1, hs, TQ, extra.shape[-1]), lambda b, g, hh, qi: (b, g * nh + hh, qi, 0)))
        args.append(extra)
    return pl.pallas_call(
        functools.partial(_row_attn_body, mode, hs, dv, scale, n_tiles),
        grid=(B, G, nh, T // TQ),
        in_specs=in_specs,
        out_specs=pl.BlockSpec((1, TQ, hs * dv), lambda b, g, hh, qi: (b, qi, g * nh + hh)),
        out_shape=jax.ShapeDtypeStruct((B, T, G * HPG * dv), F32),
        compiler_params=pltpu.CompilerParams(
            dimension_semantics=("parallel", "parallel", "parallel", "parallel"), vmem_limit_bytes=VMEM_LIMIT),
        name=f"attn_{mode}",
    )(*args)


def _nsa_cmp_body(q_ref, kc_ref, vc_ref, o_ref, mask_ref):
    qi = pl.program_id(2)
    nc = kc_ref.shape[2]
    T = mask_ref.shape[-1]
    kc = kc_ref[0, 0].astype(BF16)
    vc = vc_ref[0, 0].astype(BF16)
    t = qi * TQ + lax.broadcasted_iota(jnp.int32, (TQ, nc), 0)
    j = lax.broadcasted_iota(jnp.int32, (TQ, nc), 1)
    vis = (j + 1) * CMP_BLK <= t + 1
    imp = jnp.zeros((TQ, nc), F32)
    for h in range(A_HPG):
        q = q_ref[0, 0, h].astype(BF16)
        s = lax.dot_general(q, kc, _NT, preferred_element_type=F32) * (A_DK ** -0.5)
        s = jnp.where(vis, s, NEG)
        e = jnp.where(vis, jnp.exp(s - jnp.max(s, axis=1, keepdims=True)), 0.0)
        l = jnp.sum(e, axis=1, keepdims=True)
        p = jnp.where(l > 0.0, e / jnp.where(l > 0.0, l, 1.0), 0.0)
        o_ref[0, :, h * A_DV:(h + 1) * A_DV] = jnp.dot(p.astype(BF16), vc, preferred_element_type=F32)
        imp = imp + p
    own = t // CMP_BLK
    forced = (j == 0) | (j == own) | (j == own - 1)
    score = jnp.where(forced, FORCE, imp)
    score = jnp.where(j <= own, score, NEG)
    rank = jnp.zeros((TQ, nc), F32)
    for i in range(nc):
        si = score[:, i:i + 1]
        rank = rank + jnp.where((si > score) | ((si == score) & (i < j)), 1.0, 0.0)
    sel = jnp.where((rank < float(N_SEL)) & (j <= own), 1.0, 0.0).astype(BF16)
    expand = jnp.where(lax.broadcasted_iota(jnp.int32, (nc, T), 1) // CMP_BLK
                       == lax.broadcasted_iota(jnp.int32, (nc, T), 0), 1.0, 0.0).astype(BF16)
    keys = jnp.dot(sel, expand, preferred_element_type=F32)
    causal = (qi * TQ + lax.broadcasted_iota(jnp.int32, (TQ, T), 0)) >= lax.broadcasted_iota(jnp.int32, (TQ, T), 1)
    mask_ref[0, 0] = jnp.where((keys > 0.5) & causal, 1.0, 0.0).astype(BF16)


def nsa_cmp_select(q, kc, vc):
    B, G, HPG, T, dk = q.shape
    nc = kc.shape[2]
    return pl.pallas_call(
        _nsa_cmp_body,
        grid=(B, G, T // TQ),
        in_specs=[pl.BlockSpec((1, 1, HPG, TQ, dk), lambda b, g, qi: (b, g, 0, qi, 0)),
                  pl.BlockSpec((1, 1, nc, dk), lambda b, g, qi: (b, g, 0, 0)),
                  pl.BlockSpec((1, 1, nc, A_DV), lambda b, g, qi: (b, g, 0, 0))],
        out_specs=[pl.BlockSpec((1, TQ, HPG * A_DV), lambda b, g, qi: (b, qi, g)),
                   pl.BlockSpec((1, 1, TQ, T), lambda b, g, qi: (b, g, qi, 0))],
        out_shape=[jax.ShapeDtypeStruct((B, T, G * HPG * A_DV), F32),
                   jax.ShapeDtypeStruct((B, G, T, T), BF16)],
        compiler_params=pltpu.CompilerParams(
            dimension_semantics=("parallel", "parallel", "parallel"), vmem_limit_bytes=VMEM_LIMIT),
        name="nsa_cmp_select",
    )(q, kc, vc)


def _count(cond):
    return jnp.sum(jnp.where(cond, 1.0, 0.0), axis=1, keepdims=True)


def _dsa_select_body(topk, qi_ref, ki_ref, w_ref, mask_ref):
    qt = pl.program_id(1)
    T = ki_ref.shape[1]
    kidx = ki_ref[0].astype(BF16)
    w = w_ref[0] * (IDX_HEADS ** -0.5)
    score = jnp.zeros((TQ, T), F32)
    for h in range(IDX_HEADS):
        d = lax.dot_general(qi_ref[0, h].astype(BF16), kidx, _NT, preferred_element_type=F32) * (IDX_DIM ** -0.5)
        score = score + jnp.maximum(d, 0.0) * w[:, h:h + 1]
    t = qt * TQ + lax.broadcasted_iota(jnp.int32, (TQ, T), 0)
    s = lax.broadcasted_iota(jnp.int32, (TQ, T), 1)
    causal = s <= t
    score = jnp.where(causal, score, NEG)
    bits = pltpu.bitcast(score, jnp.int32)
    key = jnp.where(bits < 0, bits ^ jnp.int32(0x7FFFFFFF), bits)
    int_min = jnp.int32(-2 ** 31)

    def value_step(i, lo):
        cand = lo + jnp.left_shift(jnp.int32(1), 31 - i)
        return jnp.where(_count(key >= cand) >= float(topk), cand, lo)
    thr = lax.fori_loop(0, 32, value_step, jnp.full((TQ, 1), int_min, jnp.int32))
    above = key > thr
    tied = key == thr
    need = float(topk) - _count(above)
    n_bits = max(1, (T - 1).bit_length())

    def index_step(i, m):
        cand = m + jnp.left_shift(jnp.int32(1), n_bits - 1 - i)
        return jnp.where(_count(tied & (s < cand)) < need, cand, m)
    last = lax.fori_loop(0, n_bits, index_step, jnp.zeros((TQ, 1), jnp.int32))
    sel = (above | (tied & (s <= last))) & causal
    mask_ref[0, 0] = jnp.where(sel, 1.0, 0.0).astype(BF16)


def dsa_select(qi, ki, wi, topk):
    B, H, T, d = qi.shape
    return pl.pallas_call(
        functools.partial(_dsa_select_body, topk),
        grid=(B, T // TQ),
        in_specs=[pl.BlockSpec((1, H, TQ, d), lambda b, qt: (b, 0, qt, 0)),
                  pl.BlockSpec((1, T, d), lambda b, qt: (b, 0, 0)),
                  pl.BlockSpec((1, TQ, H), lambda b, qt: (b, qt, 0))],
        out_specs=pl.BlockSpec((1, 1, TQ, T), lambda b, qt: (b, 0, qt, 0)),
        out_shape=jax.ShapeDtypeStruct((B, 1, T, T), BF16),
        compiler_params=pltpu.CompilerParams(
            dimension_semantics=("parallel", "parallel"), vmem_limit_bytes=VMEM_LIMIT),
        name="dsa_select",
    )(qi, ki, wi)


def _moba_select_body(q_ref, k_ref, f_ref):
    T = k_ref.shape[2]
    nb = T // MOBA_BLK
    row = lax.broadcasted_iota(jnp.int32, (nb, k_ref.shape[3]), 0)
    kmean = jnp.zeros((nb, k_ref.shape[3]), F32)
    for b in range(nb):
        blk_sum = jnp.sum(k_ref[0, 0, b * MOBA_BLK:(b + 1) * MOBA_BLK, :], axis=0, keepdims=True)
        kmean = jnp.where(row == b, blk_sum * (1.0 / MOBA_BLK), kmean)
    kmean = kmean.astype(BF16)
    j = lax.broadcasted_iota(jnp.int32, (nb, T), 0)
    past = j < lax.broadcasted_iota(jnp.int32, (nb, T), 1) // MOBA_BLK
    for h in range(C_HPG):
        s = lax.dot_general(kmean, q_ref[0, 0, h].astype(BF16), _NT, preferred_element_type=F32)
        s = jnp.where(past, s, NEG)
        rank = jnp.zeros((nb, T), F32)
        for i in range(nb):
            si = s[i:i + 1, :]
            rank = rank + jnp.where((si > s) | ((si == s) & (i < j)), 1.0, 0.0)
        f_ref[0, h] = jnp.where((rank < float(MOBA_TOP)) & past, 1.0, 0.0)


def moba_select(q, k):
    B, G, HPG, T, dh = q.shape
    nb = T // MOBA_BLK
    return pl.pallas_call(
        _moba_select_body,
        grid=(B, G),
        in_specs=[pl.BlockSpec((1, 1, HPG, T, dh), lambda b, g: (b, g, 0, 0, 0)),
                  pl.BlockSpec((1, 1, T, dh), lambda b, g: (b, g, 0, 0))],
        out_specs=pl.BlockSpec((1, HPG, nb, T), lambda b, g: (b, g, 0, 0)),
        out_shape=jax.ShapeDtypeStruct((B, G * HPG, nb, T), F32),
        compiler_params=pltpu.CompilerParams(
            dimension_semantics=("parallel", "parallel"), vmem_limit_bytes=VMEM_LIMIT),
        name="moba_select",
    )(q, k)


MM_VMEM_LIMIT = 56 * 1024 * 1024
SUB_ROWS = 256


def _gmm_body(n_w, nk, nsb_max, sb, has_resid, te_ref, ns_ref, src_ref, *refs):
    x_ref = refs[0]
    w_refs = refs[1:1 + n_w]
    pos = 1 + n_w
    r_ref = refs[pos] if has_resid else None
    pos += int(has_resid)
    o_ref = refs[pos]
    acc_refs = refs[pos + 1:pos + 1 + n_w]
    i = pl.program_id(0)
    k = pl.program_id(2)
    n_sb = ns_ref[i]

    @pl.when(k == 0)
    def _():
        for a_ref in acc_refs:
            a_ref[...] = jnp.zeros(a_ref.shape, F32)

    for c in range(1, nsb_max + 1):
        @pl.when(n_sb == c)
        def _(c=c):
            xs = x_ref[:c * sb, :].astype(BF16)
            for w_ref, a_ref in zip(w_refs, acc_refs):
                a_ref[:c * sb, :] += jnp.dot(xs, w_ref[0].astype(BF16), preferred_element_type=F32)

    @pl.when(k == nk - 1)
    def _():
        if n_w == 2:
            val = jax.nn.silu(acc_refs[0][...]) * acc_refs[1][...]
        else:
            val = acc_refs[0][...]
        if has_resid:
            val = val + r_ref[...]
        o_ref[...] = val.astype(o_ref.dtype)


def grouped_matmul(x, ws, tables, tm, tn, tk, out_dtype, resid=None, layer=0):
    P, K = x.shape
    N = ws[0].shape[-1]
    tk = min(tk, K)
    tn = min(tn, N)
    assert P % tm == 0 and K % tk == 0
    sb = min(SUB_ROWS, tm)
    assert tm % sb == 0
    n_w = len(ws)
    ni, nj, nk = P // tm, pl.cdiv(N, tn), K // tk

    def x_idx(i, j, k, te, ns, src):
        return (src[i], jnp.where(ns[i] > 0, k, nk - 1))

    def w_idx(i, j, k, te, ns, src):
        act = ns[i] > 0
        return (layer, te[i], jnp.where(act, k, nk - 1), jnp.where(act, j, nj - 1))

    def o_idx(i, j, k, te, ns, src):
        return (i, j)

    in_specs = [pl.BlockSpec((tm, tk), x_idx)] + [pl.BlockSpec((None, 1, tk, tn), w_idx)] * n_w
    args = [x] + list(ws)
    if resid is not None:
        in_specs.append(pl.BlockSpec((tm, tn), o_idx))
        args.append(resid)
    return pl.pallas_call(
        functools.partial(_gmm_body, n_w, nk, tm // sb, sb, resid is not None),
        grid_spec=pltpu.PrefetchScalarGridSpec(
            num_scalar_prefetch=3, grid=(ni, nj, nk), in_specs=in_specs,
            out_specs=pl.BlockSpec((tm, tn), o_idx),
            scratch_shapes=[pltpu.VMEM((tm, tn), F32)] * n_w),
        out_shape=jax.ShapeDtypeStruct((P, N), out_dtype),
        compiler_params=pltpu.CompilerParams(
            dimension_semantics=("parallel", "parallel", "arbitrary"), vmem_limit_bytes=MM_VMEM_LIMIT),
        name=f"gmm{n_w}_{tm}x{tn}x{tk}",
    )(*tables, *args)


def _dense_tables(rows, tm):
    n = rows // tm
    return (jnp.zeros((n,), jnp.int32), jnp.full((n,), tm // min(SUB_ROWS, tm), jnp.int32),
            jnp.arange(n, dtype=jnp.int32))


def _row_tile(rows):
    for tm in (2048, 1024, 512, 256):
        if rows % tm == 0:
            return tm
    return rows


def matmul(x, w, out_dtype=F32, resid=None, tn=1024, tk=512, layer=None):
    lead = x.shape[:-1]
    x2 = x.reshape(-1, x.shape[-1])
    rows = x2.shape[0]
    tm = _row_tile(rows)
    r2 = None if resid is None else resid.reshape(rows, -1)
    w4 = w[None, None] if layer is None else w[:, None]
    out = grouped_matmul(x2, [w4], _dense_tables(rows, tm), tm, tn, tk, out_dtype, r2, layer or 0)
    return out.reshape(*lead, w.shape[-1])


def swiglu_up(x, w1, w3, tables, tm, layer, tf=512, tk=1024):
    return grouped_matmul(x, [w1, w3], tables, tm, tf, tk, BF16, layer=layer)


PAGES_PER_STEP = 8
REMOVED = -3e38


def bias_by_position(tab, past_len):
    bd = bias_by_distance(tab, REL_MAX_DIST + 1)
    H = bd.shape[0]
    near = bd[:, 1:REL_MAX_DIST + 1][:, ::-1]
    far = jnp.broadcast_to(bd[:, REL_MAX_DIST:], (H, past_len - REL_MAX_DIST))
    return jnp.concatenate([far, near], axis=1), bd[:, :1]


def _page_specs(lanes, lane_block, pages_of):
    def spec(r):
        return pl.BlockSpec((None, PAGE_SIZE, lanes), lambda *a: (pages_of(r)(*a), 0, lane_block))
    return [spec(r) for r in range(PAGES_PER_STEP)]


def _chunk_page(r):
    return lambda b, c, pt, *_: pt[b, c * PAGES_PER_STEP + r]


def _topk_rows(score, k):
    R, L = score.shape
    jf = lax.broadcasted_iota(jnp.int32, (R, L), 1).astype(F32)
    slot = lax.broadcasted_iota(jnp.int32, (R, LANES), 1)
    idx = jnp.zeros((R, LANES), F32)
    val = jnp.zeros((R, LANES), F32)
    for n in range(k):
        m = jnp.max(score, axis=1, keepdims=True)
        i = jnp.min(jnp.where(score == m, jf, 1e9), axis=1, keepdims=True)
        idx = jnp.where(slot == n, i, idx)
        val = jnp.where(slot == n, jnp.where(m > 0.5 * NEG, 1.0, 0.0), val)
        score = jnp.where(jf == i, REMOVED, score)
    return idx, val


def _gather_cmp_body(width, native, pt_ref, *refs):
    pages = refs[:PAGES_PER_STEP]
    pe_ref, o_ref = refs[PAGES_PER_STEP:]
    for r in range(PAGES_PER_STEP):
        for g in range(A_GROUPS):
            rows = pages[r][:, g, :] if native else pages[r][:, g * width:(g + 1) * width]
            o_ref[0, g, r * PAGE_SIZE:(r + 1) * PAGE_SIZE, :] = rows + pe_ref[...]


def gather_compress_rows(cache, page_table, pe, width, layer=None):
    B, n_pages = page_table.shape
    pe2 = jnp.tile(pe, (PAGE_SIZE // CMP_BLK, 1))
    if layer is None:
        page_specs = _page_specs(A_GROUPS * width, 0, _chunk_page)
    else:
        page_specs = [pl.BlockSpec((None, None, PAGE_SIZE, None, A_GROUPS, width),
                                   lambda b, c, pt, r=r: (layer, pt[b, c * PAGES_PER_STEP + r], 0, 0, 0, 0))
                      for r in range(PAGES_PER_STEP)]
    return pl.pallas_call(
        functools.partial(_gather_cmp_body, width, layer is not None),
        grid_spec=pltpu.PrefetchScalarGridSpec(
            num_scalar_prefetch=1, grid=(B, n_pages // PAGES_PER_STEP),
            in_specs=page_specs
            + [pl.BlockSpec((PAGE_SIZE, width), lambda b, c, pt: (0, 0))],
            out_specs=pl.BlockSpec((1, A_GROUPS, PAGES_PER_STEP * PAGE_SIZE, width), lambda b, c, pt: (b, 0, c, 0))),
        out_shape=jax.ShapeDtypeStruct((B, A_GROUPS, n_pages * PAGE_SIZE, width), F32),
        compiler_params=pltpu.CompilerParams(dimension_semantics=("parallel", "parallel"),
                                             vmem_limit_bytes=VMEM_LIMIT),
        name="gather_compress_rows",
    )(page_table, *([cache] * PAGES_PER_STEP), pe2)


def _nsa_sample_a_body(q_ref, kc_ref, vc_ref, wk_ref, wv_ref, kn_ref, vn_ref, bw_ref, oc_ref, ow_ref, idx_ref, val_ref):
    scale = A_DK ** -0.5
    q = q_ref[0, 0]
    qb = q.astype(BF16)
    nc = kc_ref.shape[2]
    s = lax.dot_general(qb, kc_ref[0, 0].astype(BF16), _NT, preferred_element_type=F32) * scale
    e = jnp.exp(s - jnp.max(s, axis=1, keepdims=True))
    p = e / jnp.sum(e, axis=1, keepdims=True)
    oc_ref[0, 0] = jnp.dot(p.astype(BF16), vc_ref[0, 0].astype(BF16), preferred_element_type=F32)
    imp = jnp.concatenate([jnp.sum(p, axis=0, keepdims=True), jnp.zeros((1, LANES), F32)], axis=1)
    j = lax.broadcasted_iota(jnp.int32, imp.shape, 1)
    own = nc
    forced = (j == 0) | (j == own) | (j == own - 1)
    score = jnp.where(forced, FORCE, imp)
    score = jnp.where(j <= own, score, REMOVED)
    idx, val = _topk_rows(score, N_SEL)
    idx_ref[0, 0] = idx.astype(jnp.int32)
    val_ref[0, 0] = val.astype(jnp.int32)
    wb = wk_ref.shape[2]
    sw = lax.dot_general(qb, wk_ref[0, 0].astype(BF16), _NT, preferred_element_type=F32) * scale + bw_ref[0][:, :wb]
    dist = wb - lax.broadcasted_iota(jnp.int32, sw.shape, 1)
    in_win = dist < WINDOW
    sn = jnp.sum(q * kn_ref[0, 0], axis=1, keepdims=True) * scale + bw_ref[0][:, wb:wb + 1]
    m = jnp.maximum(jnp.max(jnp.where(in_win, sw, NEG), axis=1, keepdims=True), sn)
    ew = jnp.where(in_win, jnp.exp(sw - m), 0.0)
    en = jnp.exp(sn - m)
    l = jnp.sum(ew, axis=1, keepdims=True) + en
    ow_ref[0, 0] = (jnp.dot(ew.astype(BF16), wv_ref[0, 0].astype(BF16), preferred_element_type=F32)
                    + en * vn_ref[0, 0]) / l


def _nsa_sample_b_body(nc, idx_ref, val_ref, pt_ref, q_ref, ka_ref, kb_ref, va_ref, vb_ref, ba_ref, bb_ref,
                       kn_ref, vn_ref, b0_ref, o_ref, m_ref, l_ref, acc_ref):
    b = pl.program_id(0)
    n = pl.program_id(1)
    scale = A_DK ** -0.5

    @pl.when(n == 0)
    def _():
        m_ref[...] = jnp.full(m_ref.shape, NEG, F32)
        l_ref[...] = jnp.zeros(l_ref.shape, F32)
        acc_ref[...] = jnp.zeros(acc_ref.shape, F32)

    for g, (k_ref, v_ref, bias_ref) in enumerate(((ka_ref, va_ref, ba_ref), (kb_ref, vb_ref, bb_ref))):
        slot = (b * A_GROUPS + g) * N_SEL + n
        cached = (val_ref[slot] > 0) & (idx_ref[slot] < nc)

        @pl.when(cached)
        def _(g=g, k_ref=k_ref, v_ref=v_ref, bias_ref=bias_ref):
            lo_k = (A_GROUPS + g) * A_DK
            k = k_ref[:, lo_k:lo_k + A_DK].astype(BF16)
            v = v_ref[:, g, :].astype(BF16)
            s = lax.dot_general(q_ref[0, g].astype(BF16), k, _NT, preferred_element_type=F32) * scale + bias_ref[0]
            m_prev = m_ref[g][:, :1]
            m_new = jnp.maximum(m_prev, jnp.max(s, axis=1, keepdims=True))
            p = jnp.exp(s - m_new)
            alpha = jnp.exp(m_prev - m_new)
            l_ref[g] = jnp.broadcast_to(alpha * l_ref[g][:, :1] + jnp.sum(p, axis=1, keepdims=True), l_ref.shape[1:])
            acc_ref[g] = alpha * acc_ref[g] + jnp.dot(p.astype(BF16), v, preferred_element_type=F32)
            m_ref[g] = jnp.broadcast_to(m_new, m_ref.shape[1:])

    @pl.when(n == N_SEL - 1)
    def _():
        for g in range(A_GROUPS):
            sn = jnp.sum(q_ref[0, g] * kn_ref[0, g], axis=1, keepdims=True) * scale + b0_ref[g]
            m_prev = m_ref[g][:, :1]
            m_new = jnp.maximum(m_prev, sn)
            alpha = jnp.exp(m_prev - m_new)
            en = jnp.exp(sn - m_new)
            l = alpha * l_ref[g][:, :1] + en
            o_ref[0, g] = (alpha * acc_ref[g] + en * vn_ref[0, g]) / l


def nsa_decode(q, k3, v3, gates, cache_k, cache_v, layer, win_k, win_v, page_table,
               pe_k, pe_v, wk1, wk2, wv1, wv2, tab_a):
    B, S, G, HPG, DK = q.shape
    n_pool = cache_k.shape[0]
    n_pages = page_table.shape[1]
    P = n_pages * PAGE_SIZE
    nc = P // CMP_BLK
    assert S == 1 and G == A_GROUPS and P % CMP_BLK == 0 and (P + S) // CMP_BLK == nc and win_k.shape[1] == WINDOW
    ck = cache_k.reshape(n_pool, PAGE_SIZE, 2 * G * DK)
    rows_k = gather_compress_rows(ck, page_table, pe_k, DK).reshape(B, G, nc, CMP_BLK * DK)
    rows_v = gather_compress_rows(cache_v, page_table, pe_v, A_DV, layer).reshape(B, G, nc, CMP_BLK * A_DV)
    kc = matmul(jax.nn.silu(matmul(rows_k, wk1)), wk2)
    vc = matmul(jax.nn.silu(matmul(rows_v, wv1)), wv2)
    qh = q.reshape(B, G, HPG, DK)
    brev, b0 = bias_by_position(tab_a, P)
    bw = jnp.concatenate([brev[:, P - WINDOW:], b0, jnp.zeros((A_HEADS, LANES - 1), F32)], axis=1)
    bw = bw.reshape(G, HPG, WINDOW + LANES)
    new_k = jnp.moveaxis(k3[:, 0], 1, 2)
    new_v = jnp.moveaxis(v3[:, 0], 1, 2)
    per_bg = lambda *shape: pl.BlockSpec((1, 1) + shape, lambda b, g: (b, g) + (0,) * len(shape))
    o_c, o_w, idx, val = pl.pallas_call(
        _nsa_sample_a_body,
        grid=(B, G),
        in_specs=[per_bg(HPG, DK), per_bg(nc, DK), per_bg(nc, A_DV), per_bg(WINDOW, DK), per_bg(WINDOW, A_DV),
                  per_bg(1, DK), per_bg(1, A_DV),
                  pl.BlockSpec((1, HPG, WINDOW + LANES), lambda b, g: (g, 0, 0))],
        out_specs=[per_bg(HPG, A_DV), per_bg(HPG, A_DV), per_bg(1, LANES), per_bg(1, LANES)],
        out_shape=[jax.ShapeDtypeStruct((B, G, HPG, A_DV), F32)] * 2
        + [jax.ShapeDtypeStruct((B, G, 1, LANES), jnp.int32)] * 2,
        compiler_params=pltpu.CompilerParams(dimension_semantics=("parallel", "parallel"),
                                             vmem_limit_bytes=VMEM_LIMIT),
        name="nsa_sample_cmp_win",
    )(qh, kc, vc, jnp.moveaxis(win_k, 1, 2), jnp.moveaxis(win_v, 1, 2), new_k[:, :, 2:3], new_v[:, :, 2:3], bw)
    idx_flat = idx[:, :, 0, :N_SEL].reshape(-1)
    val_flat = val[:, :, 0, :N_SEL].reshape(-1)
    bsel = brev.reshape(G, HPG, nc, CMP_BLK).transpose(0, 2, 1, 3)

    def blk(g):
        return lambda b, n, ix, vl, pt: jnp.minimum(ix[(b * G + g) * N_SEL + n], nc - 1)

    def kv_spec(g, lanes):
        half = PAGE_SIZE // CMP_BLK
        return pl.BlockSpec((None, CMP_BLK, lanes),
                            lambda b, n, ix, vl, pt: (pt[b, blk(g)(b, n, ix, vl, pt) // half],
                                                       blk(g)(b, n, ix, vl, pt) % half, 0))

    def v_spec(g):
        half = PAGE_SIZE // CMP_BLK
        return pl.BlockSpec((None, None, CMP_BLK, None, G, A_DV),
                            lambda b, n, ix, vl, pt: (layer, pt[b, blk(g)(b, n, ix, vl, pt) // half],
                                                       blk(g)(b, n, ix, vl, pt) % half, 1, 0, 0))

    def bias_spec(g):
        return pl.BlockSpec((None, 1, HPG, CMP_BLK), lambda b, n, ix, vl, pt: (g, blk(g)(b, n, ix, vl, pt), 0, 0))
    whole = lambda *shape: pl.BlockSpec((1,) + shape, lambda b, n, ix, vl, pt: (b,) + (0,) * len(shape))
    o_s = pl.pallas_call(
        functools.partial(_nsa_sample_b_body, nc),
        grid_spec=pltpu.PrefetchScalarGridSpec(
            num_scalar_prefetch=3, grid=(B, N_SEL),
            in_specs=[whole(G, HPG, DK), kv_spec(0, 2 * G * DK), kv_spec(1, 2 * G * DK),
                      v_spec(0), v_spec(1), bias_spec(0), bias_spec(1),
                      whole(G, 1, DK), whole(G, 1, A_DV),
                      pl.BlockSpec((G, HPG, 1), lambda b, n, ix, vl, pt: (0, 0, 0))],
            out_specs=whole(G, HPG, A_DV),
            scratch_shapes=[pltpu.VMEM((G, HPG, LANES), F32), pltpu.VMEM((G, HPG, LANES), F32),
                            pltpu.VMEM((G, HPG, A_DV), F32)]),
        out_shape=jax.ShapeDtypeStruct((B, G, HPG, A_DV), F32),
        compiler_params=pltpu.CompilerParams(dimension_semantics=("parallel", "arbitrary"),
                                             vmem_limit_bytes=VMEM_LIMIT),
        name="nsa_sample_selected",
    )(idx_flat, val_flat, page_table, qh, ck, ck, cache_v, cache_v, bsel, bsel, new_k[:, :, 1:2], new_v[:, :, 1:2],
      b0.reshape(G, HPG, 1))
    shp = (B, S, G, HPG, A_DV)
    return nsa_combine(gates, o_c.reshape(shp), o_s.reshape(shp), o_w.reshape(shp))


def _dsa_scores_body(pt_ref, *refs):
    pages = refs[:PAGES_PER_STEP]
    qi_ref, w_ref, o_ref = refs[PAGES_PER_STEP:]
    qi = qi_ref[0].astype(BF16)
    w = w_ref[0] * (IDX_HEADS ** -0.5)
    for r in range(PAGES_PER_STEP):
        d = lax.dot_general(qi, pages[r][...].astype(BF16), _NT, preferred_element_type=F32) * (IDX_DIM ** -0.5)
        o_ref[0, :, r * PAGE_SIZE:(r + 1) * PAGE_SIZE] = jnp.sum(jnp.maximum(d, 0.0) * w, axis=0, keepdims=True)


def _dsa_sample_select_body(topk, s_ref, qi_ref, w_ref, kn_ref, m_ref):
    B, P = s_ref.shape
    w = w_ref[...] * (IDX_HEADS ** -0.5)
    dn = jnp.sum(qi_ref[...] * kn_ref[...], axis=2) * (IDX_DIM ** -0.5)
    s_new = jnp.sum(jnp.maximum(dn, 0.0) * w, axis=1, keepdims=True)

    def order_key(x):
        bits = pltpu.bitcast(x, jnp.int32)
        return jnp.where(bits < 0, bits ^ jnp.int32(0x7FFFFFFF), bits)
    key = order_key(s_ref[...])
    key_new = order_key(jnp.broadcast_to(s_new, (B, LANES)))[:, :1]
    pos = lax.broadcasted_iota(jnp.int32, (B, P), 1)

    def count(cond, cond_new):
        return _count(cond) + jnp.where(cond_new, 1.0, 0.0)

    def value_step(i, lo):
        cand = lo + jnp.left_shift(jnp.int32(1), 31 - i)
        return jnp.where(count(key >= cand, key_new >= cand) >= float(topk), cand, lo)
    thr = lax.fori_loop(0, 32, value_step, jnp.full((B, 1), jnp.int32(-2 ** 31), jnp.int32))
    need = float(topk) - count(key > thr, key_new > thr)
    tied = key == thr
    n_bits = max(1, (P - 1).bit_length())

    def index_step(i, m):
        cand = m + jnp.left_shift(jnp.int32(1), n_bits - 1 - i)
        return jnp.where(_count(tied & (pos < cand)) < need, cand, m)
    last = lax.fori_loop(0, n_bits, index_step, jnp.zeros((B, 1), jnp.int32))
    sel = (key > thr) | (tied & (pos <= last))
    taken = _count(sel)
    sel_new = (key_new > thr) | ((key_new == thr) & (taken < float(topk)))
    m_ref[:, :P] = jnp.where(sel, 1.0, 0.0)
    lane = lax.broadcasted_iota(jnp.int32, (B, LANES), 1)
    m_ref[:, P:] = jnp.where((lane == 0) & sel_new, 1.0, 0.0)


def _dsa_sample_attend_body(pt_ref, *refs):
    n = PAGES_PER_STEP
    kp, vp = refs[:n], refs[n:2 * n]
    q_ref, mask_ref, bias_ref, mnew_ref, kn_ref, vn_ref, b0_ref, o_ref, m_ref, l_ref, acc_ref = refs[2 * n:]
    c = pl.program_id(1)
    scale = B_DH ** -0.5

    @pl.when(c == 0)
    def _():
        m_ref[...] = jnp.full(m_ref.shape, NEG, F32)
        l_ref[...] = jnp.zeros(l_ref.shape, F32)
        acc_ref[...] = jnp.zeros(acc_ref.shape, F32)

    mask = mask_ref[0] > 0.5
    for g in range(B_GROUPS):
        k = jnp.concatenate([kp[r][:, g, :] for r in range(n)], axis=0).astype(BF16)
        v = jnp.concatenate([vp[r][:, g, :] for r in range(n)], axis=0).astype(BF16)
        s = lax.dot_general(q_ref[0, g].astype(BF16), k, _NT, preferred_element_type=F32) * scale + bias_ref[g]
        s = jnp.where(mask, s, NEG)
        m_prev = m_ref[g][:, :1]
        m_new = jnp.maximum(m_prev, jnp.max(s, axis=1, keepdims=True))
        p = jnp.where(mask, jnp.exp(s - m_new), 0.0)
        alpha = jnp.exp(m_prev - m_new)
        l_ref[g] = jnp.broadcast_to(alpha * l_ref[g][:, :1] + jnp.sum(p, axis=1, keepdims=True), l_ref.shape[1:])
        acc_ref[g] = alpha * acc_ref[g] + jnp.dot(p.astype(BF16), v, preferred_element_type=F32)
        m_ref[g] = jnp.broadcast_to(m_new, m_ref.shape[1:])

    @pl.when(c == pl.num_programs(1) - 1)
    def _():
        new_on = mnew_ref[0][:, :1] > 0.5
        for g in range(B_GROUPS):
            sn = jnp.sum(q_ref[0, g] * kn_ref[0, g], axis=1, keepdims=True) * scale + b0_ref[g]
            sn = jnp.where(new_on, sn, NEG)
            m_prev = m_ref[g][:, :1]
            m_new = jnp.maximum(m_prev, sn)
            alpha = jnp.exp(m_prev - m_new)
            en = jnp.where(new_on, jnp.exp(sn - m_new), 0.0)
            l = alpha * l_ref[g][:, :1] + en
            o_ref[0, g] = (alpha * acc_ref[g] + en * vn_ref[0, g]) / l


def dsa_decode(q, k, v, qi, ki, wi, cache_k, cache_v, cache_kidx, layer, page_table, tab_b):
    B, S, G, HPG, DH = q.shape
    n_pages = page_table.shape[1]
    P = n_pages * PAGE_SIZE
    n_chunks = n_pages // PAGES_PER_STEP
    chunk = PAGES_PER_STEP * PAGE_SIZE
    assert S == 1 and n_pages % PAGES_PER_STEP == 0
    topk = min(IDX_TOPK, (P + S) // 4)
    params = pltpu.CompilerParams(dimension_semantics=("parallel", "arbitrary"), vmem_limit_bytes=VMEM_LIMIT)

    def pool_specs(*page_shape):
        return [pl.BlockSpec((None, None) + page_shape,
                             lambda b, c, pt, r=r: (layer, pt[b, c * PAGES_PER_STEP + r]) + (0,) * len(page_shape))
                for r in range(PAGES_PER_STEP)]
    scores = pl.pallas_call(
        _dsa_scores_body,
        grid_spec=pltpu.PrefetchScalarGridSpec(
            num_scalar_prefetch=1, grid=(B, n_chunks),
            in_specs=pool_specs(PAGE_SIZE, IDX_DIM)
            + [pl.BlockSpec((1, IDX_HEADS, IDX_DIM), lambda b, c, pt: (b, 0, 0)),
               pl.BlockSpec((1, IDX_HEADS, 1), lambda b, c, pt: (b, 0, 0))],
            out_specs=pl.BlockSpec((1, 1, chunk), lambda b, c, pt: (b, 0, c))),
        out_shape=jax.ShapeDtypeStruct((B, 1, P), F32),
        compiler_params=params, name="dsa_sample_scores",
    )(page_table, *([cache_kidx] * PAGES_PER_STEP), qi[:, 0], wi[:, 0, :, None])
    sel = pl.pallas_call(
        functools.partial(_dsa_sample_select_body, topk),
        out_shape=jax.ShapeDtypeStruct((B, P + LANES), F32),
        compiler_params=pltpu.CompilerParams(vmem_limit_bytes=VMEM_LIMIT), name="dsa_sample_select",
    )(scores.reshape(B, P), qi[:, 0], wi[:, 0], ki)
    sel = sel.reshape(B, 1, P + LANES)
    brev, b0 = bias_by_position(tab_b, P)
    whole = lambda *shape: pl.BlockSpec((1,) + shape, lambda b, c, pt: (b,) + (0,) * len(shape))
    o = pl.pallas_call(
        _dsa_sample_attend_body,
        grid_spec=pltpu.PrefetchScalarGridSpec(
            num_scalar_prefetch=1, grid=(B, n_chunks),
            in_specs=pool_specs(PAGE_SIZE, G, DH) + pool_specs(PAGE_SIZE, G, DH)
            + [whole(G, HPG, DH),
               pl.BlockSpec((1, 1, chunk), lambda b, c, pt: (b, 0, c)),
               pl.BlockSpec((G, HPG, chunk), lambda b, c, pt: (0, 0, c)),
               pl.BlockSpec((1, 1, LANES), lambda b, c, pt: (b, 0, P // LANES)),
               whole(G, 1, DH), whole(G, 1, DH),
               pl.BlockSpec((G, HPG, 1), lambda b, c, pt: (0, 0, 0))],
            out_specs=whole(G, HPG, DH),
            scratch_shapes=[pltpu.VMEM((G, HPG, LANES), F32), pltpu.VMEM((G, HPG, LANES), F32),
                            pltpu.VMEM((G, HPG, DH), F32)]),
        out_shape=jax.ShapeDtypeStruct((B, G, HPG, DH), F32),
        compiler_params=params, name="dsa_sample_attend",
    )(page_table, *([cache_k] * PAGES_PER_STEP), *([cache_v] * PAGES_PER_STEP), q[:, 0], sel,
      brev.reshape(G, HPG, P), sel,
      jnp.moveaxis(k, 1, 2), jnp.moveaxis(v, 1, 2), b0.reshape(G, HPG, 1))
    return o.reshape(B, S, G * HPG * DH)


def _moba_kmean_body(pt_ref, p0_ref, p1_ref, o_ref):
    o_ref[0] = (jnp.sum(p0_ref[...], axis=0) + jnp.sum(p1_ref[...], axis=0)) * (1.0 / MOBA_BLK)


def _moba_sample_pick_body(nf, q_ref, km_ref, idx_ref, val_ref):
    B, H, DH = q_ref.shape
    nfp = km_ref.shape[1]
    group = lax.broadcasted_iota(jnp.int32, (H, nfp), 0) // C_HPG
    lane = lax.broadcasted_iota(jnp.int32, (H, nfp), 1)
    for b in range(B):
        qb = q_ref[b].astype(BF16)
        s = jnp.full((H, nfp), REMOVED, F32)
        for g in range(C_GROUPS):
            kg = km_ref[b, :, g, :].astype(BF16)
            sg = lax.dot_general(qb, kg, _NT, preferred_element_type=F32)
            s = jnp.where((group == g) & (lane < nf), sg, s)
        idx, val = _topk_rows(s, MOBA_TOP)
        idx_ref[b] = idx.astype(jnp.int32)
        val_ref[b] = val.astype(jnp.int32)


def _moba_sample_attend_body(idx_ref, val_ref, pt_ref, q_ref, k0_ref, k1_ref, v0_ref, v1_ref, bias_ref,
                             kn_ref, vn_ref, b0_ref, o_ref, m_ref, l_ref, acc_ref):
    b, h, r = pl.program_id(0), pl.program_id(1), pl.program_id(2)
    scale = C_DH ** -0.5
    q = q_ref[0, 0]
    group = h // C_HPG

    @pl.when(r == 0)
    def _():
        m_ref[...] = jnp.full(m_ref.shape, NEG, F32)
        l_ref[...] = jnp.zeros(l_ref.shape, F32)
        acc_ref[...] = jnp.zeros(acc_ref.shape, F32)

    def attend(g):
        k = jnp.concatenate([k0_ref[:, g, :], k1_ref[:, g, :]], axis=0).astype(BF16)
        v = jnp.concatenate([v0_ref[:, g, :], v1_ref[:, g, :]], axis=0).astype(BF16)
        s = lax.dot_general(q.astype(BF16), k, _NT, preferred_element_type=F32) * scale + bias_ref[0]
        m_prev = m_ref[:, :1]
        m_new = jnp.maximum(m_prev, jnp.max(s, axis=1, keepdims=True))
        p = jnp.exp(s - m_new)
        alpha = jnp.exp(m_prev - m_new)
        l_ref[...] = jnp.broadcast_to(alpha * l_ref[:, :1] + jnp.sum(p, axis=1, keepdims=True), l_ref.shape)
        acc_ref[...] = alpha * acc_ref[...] + jnp.dot(p.astype(BF16), v, preferred_element_type=F32)
        m_ref[...] = jnp.broadcast_to(m_new, m_ref.shape)

    chosen = val_ref[(b * C_HEADS + h) * MOBA_TOP + r] > 0
    for g in range(C_GROUPS):
        pl.when(chosen & (group == g))(functools.partial(attend, g))

    @pl.when(r == MOBA_TOP - 1)
    def _():
        sn = jnp.sum(q * kn_ref[0, 0], axis=1, keepdims=True) * scale + b0_ref[0]
        m_prev = m_ref[:, :1]
        m_new = jnp.maximum(m_prev, sn)
        alpha = jnp.exp(m_prev - m_new)
        en = jnp.exp(sn - m_new)
        o_ref[0, 0] = (alpha * acc_ref[...] + en * vn_ref[0, 0]) / (alpha * l_ref[:, :1] + en)


def moba_decode(q, k, v, cache_k, cache_v, layer, page_table, tab_c):
    B, S, H, DH = q.shape
    n_pages = page_table.shape[1]
    P = n_pages * PAGE_SIZE
    ppb = MOBA_BLK // PAGE_SIZE
    nf = (P + S) // MOBA_BLK
    assert S == 1 and ppb == 2 and P % MOBA_BLK == 0 and nf == P // MOBA_BLK and nf >= MOBA_TOP
    page_block = (None, None, PAGE_SIZE, C_GROUPS, DH)
    kmean = pl.pallas_call(
        _moba_kmean_body,
        grid_spec=pltpu.PrefetchScalarGridSpec(
            num_scalar_prefetch=1, grid=(B, nf),
            in_specs=[pl.BlockSpec(page_block, lambda b, j, pt: (layer, pt[b, ppb * j], 0, 0, 0)),
                      pl.BlockSpec(page_block, lambda b, j, pt: (layer, pt[b, ppb * j + 1], 0, 0, 0))],
            out_specs=pl.BlockSpec((1, C_GROUPS, DH), lambda b, j, pt: (b * nf + j, 0, 0))),
        out_shape=jax.ShapeDtypeStruct((B * nf, C_GROUPS, DH), F32),
        compiler_params=pltpu.CompilerParams(dimension_semantics=("parallel", "parallel"),
                                             vmem_limit_bytes=VMEM_LIMIT),
        name="moba_sample_kmean",
    )(page_table, cache_k, cache_k).reshape(B, nf, C_GROUPS, DH)
    idx, val = pl.pallas_call(
        functools.partial(_moba_sample_pick_body, nf),
        out_shape=[jax.ShapeDtypeStruct((B, H, LANES), jnp.int32)] * 2,
        compiler_params=pltpu.CompilerParams(vmem_limit_bytes=VMEM_LIMIT), name="moba_sample_pick",
    )(q[:, 0], jnp.pad(kmean, ((0, 0), (0, -nf % LANES), (0, 0), (0, 0))))
    idx_flat = idx[:, :, :MOBA_TOP].reshape(-1)
    val_flat = val[:, :, :MOBA_TOP].reshape(-1)
    brev, b0 = bias_by_position(tab_c, P)

    def blk(b, h, r, ix):
        return jnp.minimum(ix[(b * H + h) * MOBA_TOP + r], nf - 1)

    def page_spec(which):
        return pl.BlockSpec(page_block,
                            lambda b, h, r, ix, vl, pt: (layer, pt[b, ppb * blk(b, h, r, ix) + which], 0, 0, 0))
    per_head = lambda rows, d: pl.BlockSpec((1, 1, rows, d), lambda b, h, r, ix, vl, pt: (b, h, 0, 0))
    per_group = lambda d: pl.BlockSpec((1, 1, 1, d), lambda b, h, r, ix, vl, pt: (b, h // C_HPG, 0, 0))
    o = pl.pallas_call(
        _moba_sample_attend_body,
        grid_spec=pltpu.PrefetchScalarGridSpec(
            num_scalar_prefetch=3, grid=(B, H, MOBA_TOP),
            in_specs=[per_head(1, DH), page_spec(0), page_spec(1), page_spec(0), page_spec(1),
                      pl.BlockSpec((1, 1, MOBA_BLK), lambda b, h, r, ix, vl, pt: (h * nf + blk(b, h, r, ix), 0, 0)),
                      per_group(DH), per_group(DH),
                      pl.BlockSpec((1, 1, 1), lambda b, h, r, ix, vl, pt: (h, 0, 0))],
            out_specs=per_head(1, DH),
            scratch_shapes=[pltpu.VMEM((1, LANES), F32), pltpu.VMEM((1, LANES), F32), pltpu.VMEM((1, DH), F32)]),
        out_shape=jax.ShapeDtypeStruct((B, H, 1, DH), F32),
        compiler_params=pltpu.CompilerParams(dimension_semantics=("parallel", "parallel", "arbitrary"),
                                             vmem_limit_bytes=VMEM_LIMIT),
        name="moba_sample_attend",
    )(idx_flat, val_flat, page_table, q[:, 0, :, None, :], cache_k, cache_k, cache_v, cache_v,
      brev.reshape(H * nf, 1, MOBA_BLK),
      jnp.moveaxis(k, 1, 2), jnp.moveaxis(v, 1, 2), b0.reshape(H, 1, 1))
    return o.reshape(B, S, H * DH)


def split_cols(z, sizes):
    cuts = [int(c) for c in np.cumsum(sizes)[:-1]]
    return jnp.split(z, cuts, axis=-1)


def masked_softmax(logits, mask):
    p = jax.nn.softmax(jnp.where(mask, logits, NEG), axis=-1)
    return jnp.where(mask, p, 0.0)


def t5_bucket(dist):
    n = jnp.maximum(dist, 0)
    nf = jnp.maximum(n, 1).astype(jnp.float32)
    large = REL_EXACT + (jnp.log(nf / REL_EXACT) / math.log(REL_MAX_DIST / REL_EXACT)
                         * (N_BUCKETS - REL_EXACT)).astype(jnp.int32)
    return jnp.where(n < REL_EXACT, n, jnp.minimum(large, N_BUCKETS - 1))


def gather_paged(pool, page_table, new_rows, pos, *extra):
    past_len = page_table.shape[1] * PAGE_SIZE
    b = jnp.arange(pos.shape[0]).reshape((-1,) + (1,) * (pos.ndim - 1))
    pc = jnp.clip(pos, 0, past_len - 1)
    phys = page_table[b, pc // PAGE_SIZE]
    old = pool[(phys, pc % PAGE_SIZE) + extra]
    new = new_rows[(b, jnp.clip(pos - past_len, 0, new_rows.shape[1] - 1)) + extra]
    is_new = (pos >= past_len).reshape(pos.shape + (1,) * (old.ndim - pos.ndim))
    return jnp.where(is_new, new, old)


def nsa_compress(rows, pe, w1, w2):
    B, L, G, d = rows.shape
    nc = L // CMP_BLK
    blk = rows[:, : nc * CMP_BLK].reshape(B, nc, CMP_BLK, G, d) + pe[None, None, :, None, :]
    flat = blk.transpose(0, 1, 3, 2, 4).reshape(B, nc, G, CMP_BLK * d)
    return matmul(jax.nn.silu(matmul(flat, w1)), w2)


def nsa_cmp_branch(q, q_pos, kc, vc):
    nc = kc.shape[1]
    logits = jnp.einsum('bqghd,bjgd->bqghj', q, kc, preferred_element_type=jnp.float32) * A_DK ** -0.5
    vis = (jnp.arange(nc) + 1) * CMP_BLK <= (q_pos + 1)[:, None]
    p = masked_softmax(logits, vis[None, :, None, None, :])
    o = jnp.einsum('bqghj,bjgd->bqghd', p.astype(vc.dtype), vc)
    return o, p.sum(axis=3)


def nsa_pick_blocks(imp, q_pos, n_blocks):
    score = jnp.pad(imp, ((0, 0), (0, 0), (0, 0), (0, n_blocks - imp.shape[-1])))
    j = jnp.arange(n_blocks)[None, :]
    own = (q_pos // CMP_BLK)[:, None]
    forced = (j == 0) | (j == own) | (j == own - 1)
    score = jnp.where(forced[None, :, None, :], FORCE, score)
    score = jnp.where((j <= own)[None, :, None, :], score, NEG)
    top_s, idx = lax.top_k(score, min(N_SEL, n_blocks))
    return idx, top_s > 0.5 * NEG


def nsa_sel_attend(q, q_pos, idx, valid, ksel, vsel, tab_a):
    B, Q, G, HPG, _ = q.shape
    kpos = idx[..., None] * CMP_BLK + jnp.arange(CMP_BLK)
    dist = q_pos[None, :, None, None, None] - kpos
    mask = valid[..., None] & (dist >= 0)
    tab2 = tab_a.reshape(N_BUCKETS, G, HPG).transpose(1, 0, 2)
    bias = tab2[jnp.arange(G).reshape(1, 1, G, 1, 1), t5_bucket(dist)]
    logits = (jnp.einsum('bqghd,bqgnsd->bqghns', q, ksel, preferred_element_type=jnp.float32) * A_DK ** -0.5
              + jnp.moveaxis(bias, -1, 3).astype(jnp.float32))
    shp = logits.shape
    p = masked_softmax(logits.reshape(B, Q, G, HPG, -1), mask.reshape(B, Q, G, 1, -1)).reshape(shp)
    return jnp.einsum('bqghns,bqgnsd->bqghd', p.astype(vsel.dtype), vsel)


def window_attend(q, q_pos, k, v, k_pos, tab_a):
    B, Q, G, HPG, _ = q.shape
    dist = q_pos[:, None] - k_pos[None, :]
    mask = (dist >= 0) & (dist < WINDOW) & (k_pos >= 0)[None, :]
    bias = tab_a[t5_bucket(dist)].reshape(Q, -1, G, HPG).transpose(0, 2, 3, 1).astype(jnp.float32)
    logits = jnp.einsum('bqghd,bsgd->bqghs', q, k, preferred_element_type=jnp.float32) * A_DK ** -0.5 + bias[None]
    p = masked_softmax(logits, mask[None, :, None, None, :])
    return jnp.einsum('bqghs,bsgd->bqghd', p.astype(v.dtype), v)


def nsa_combine(gates, o_c, o_s, o_w):
    g = gates[..., None].astype(o_c.dtype)
    o = g[:, :, 0] * o_c + g[:, :, 1] * o_s + g[:, :, 2] * o_w
    return o.reshape(o.shape[0], o.shape[1], -1)


def _heads_first(a):
    return jnp.moveaxis(a, 1, -2)


def nsa_prompt(q, k3, v3, gates, pe_k, pe_v, wk1, wk2, wv1, wv2, tab_a):
    B, T, G, HPG, DK = q.shape
    assert T % TQ == 0 and TQ == TK and (T // CMP_BLK) * CMP_BLK == T
    kc = nsa_compress(k3[:, :, 0], pe_k, wk1, wk2)
    vc = nsa_compress(v3[:, :, 0], pe_v, wv1, wv2)
    qh = _heads_first(q)
    o_c, sel_mask = nsa_cmp_select(qh, _heads_first(kc), _heads_first(vc))
    tiles = toeplitz_bias_tiles(tab_a)
    scale = A_DK ** -0.5
    o_s = block_attention('key', qh, _heads_first(k3[:, :, 1]), _heads_first(v3[:, :, 1]), tiles, sel_mask, scale)
    o_w = block_attention('window', qh, _heads_first(k3[:, :, 2]), _heads_first(v3[:, :, 2]), tiles, None, scale)
    shp = (B, T, G, HPG, A_DV)
    return nsa_combine(gates, o_c.reshape(shp), o_s.reshape(shp), o_w.reshape(shp))


def nsa_sample(q, k3, v3, gates, cache_k, cache_v, win_k, win_v, page_table,
               pe_k, pe_v, wk1, wk2, wv1, wv2, tab_a):
    B, S, G, HPG, DK = q.shape
    P = page_table.shape[1] * PAGE_SIZE
    L = P + S
    pos = P + jnp.arange(S)
    rows_k = jnp.concatenate([cache_k[page_table, :, 0].reshape(B, P, G, DK), k3[:, :, 0]], axis=1)
    rows_v = jnp.concatenate([cache_v[page_table, :, 0].reshape(B, P, G, A_DV), v3[:, :, 0]], axis=1)
    kc = nsa_compress(rows_k, pe_k, wk1, wk2)
    vc = nsa_compress(rows_v, pe_v, wv1, wv2)
    o_c, imp = nsa_cmp_branch(q, pos, kc, vc)
    idx, valid = nsa_pick_blocks(imp, pos, -(-L // CMP_BLK))
    kpos = idx[..., None] * CMP_BLK + jnp.arange(CMP_BLK)
    gi = jnp.arange(G).reshape(1, 1, G, 1, 1)
    ksel = gather_paged(cache_k, page_table, k3, kpos, 1, gi)
    vsel = gather_paged(cache_v, page_table, v3, kpos, 1, gi)
    o_s = nsa_sel_attend(q, pos, idx, valid, ksel, vsel, tab_a)
    wb = win_k.shape[1]
    kw = jnp.concatenate([win_k, k3[:, :, 2]], axis=1)
    vw = jnp.concatenate([win_v, v3[:, :, 2]], axis=1)
    o_w = window_attend(q, pos, kw, vw, P - wb + jnp.arange(wb + S), tab_a)
    return nsa_combine(gates, o_c, o_s, o_w)


def dsa_pick(qi, wi, ki, q_pos, topk):
    L = ki.shape[1]
    dots = jnp.einsum('bqhd,bld->bqhl', qi, ki, preferred_element_type=jnp.float32) * IDX_DIM ** -0.5
    score = jnp.einsum('bqhl,bqh->bql', jax.nn.relu(dots), wi.astype(jnp.float32) * IDX_HEADS ** -0.5)
    score = jnp.where(jnp.arange(L)[None, None, :] <= q_pos[None, :, None], score, NEG)
    _, idx = lax.top_k(score, topk)
    return idx, idx <= q_pos[None, :, None]


def dsa_attend(q, q_pos, idx, valid, ksel, vsel, tab_b):
    B, Q, G, HPG, DH = q.shape
    dist = q_pos[None, :, None] - idx
    bias = tab_b[t5_bucket(dist)].reshape(B, Q, -1, G, HPG).transpose(0, 1, 3, 4, 2).astype(jnp.float32)
    logits = jnp.einsum('bqghd,bqkgd->bqghk', q, ksel, preferred_element_type=jnp.float32) * DH ** -0.5 + bias
    p = masked_softmax(logits, valid[:, :, None, None, :])
    o = jnp.einsum('bqghk,bqkgd->bqghd', p.astype(vsel.dtype), vsel)
    return o.reshape(B, Q, -1)


def dsa_prompt(q, k, v, qi, ki, wi, tab_b):
    B, T = q.shape[:2]
    assert T % TQ == 0 and TQ == TK
    sel_mask = dsa_select(_heads_first(qi), ki, wi, min(IDX_TOPK, T // 4))
    return block_attention('key', _heads_first(q), _heads_first(k), _heads_first(v),
                           toeplitz_bias_tiles(tab_b), sel_mask, B_DH ** -0.5)


def dsa_sample(q, k, v, qi, ki, wi, cache_k, cache_v, cache_kidx, page_table, tab_b):
    B, S = q.shape[:2]
    P = page_table.shape[1] * PAGE_SIZE
    L = P + S
    q_pos = P + jnp.arange(S)
    ki_all = jnp.concatenate([cache_kidx[page_table].reshape(B, P, IDX_DIM), ki], axis=1)
    idx, valid = dsa_pick(qi, wi, ki_all, q_pos, min(IDX_TOPK, L // 4))
    return dsa_attend(q, q_pos, idx, valid, gather_paged(cache_k, page_table, k, idx),
                      gather_paged(cache_v, page_table, v, idx), tab_b)


def moba_pick(q, q_pos, kmean):
    B, Q, H, DH = q.shape
    nf = kmean.shape[1]
    s = jnp.einsum('bqghd,bjgd->bqghj', q.reshape(B, Q, C_GROUPS, C_HPG, DH).astype(jnp.float32),
                   kmean).reshape(B, Q, H, nf)
    ncand = max(nf, MOBA_TOP)
    s = jnp.pad(s, ((0, 0), (0, 0), (0, 0), (0, ncand - nf)), constant_values=NEG)
    past = jnp.arange(ncand)[None, :] < (q_pos // MOBA_BLK)[:, None]
    s = jnp.where(past[None, :, None, :], s, NEG)
    top_s, idx = lax.top_k(s, MOBA_TOP)
    return idx, top_s > 0.5 * NEG


def moba_attend(q, q_pos, idx, valid, ksel, vsel, own_pos, kown, vown, tab_c):
    B, Q, H, DH = q.shape
    scale = DH ** -0.5
    sel_dist = q_pos[None, :, None, None, None] - (idx[..., None] * MOBA_BLK + jnp.arange(MOBA_BLK))
    hidx = jnp.arange(H).reshape(1, 1, H, 1, 1)
    ls = (jnp.einsum('bqhd,bqhrsd->bqhrs', q, ksel, preferred_element_type=jnp.float32) * scale
          + tab_c.T[hidx, t5_bucket(sel_dist)].astype(jnp.float32))
    ls = jnp.where(valid[..., None], ls, NEG).reshape(B, Q, H, MOBA_TOP * MOBA_BLK)
    own_dist = q_pos[:, None] - own_pos
    lo = jnp.einsum('bqghd,bqgsd->bqghs', q.reshape(B, Q, C_GROUPS, C_HPG, DH), kown,
                    preferred_element_type=jnp.float32).reshape(B, Q, H, MOBA_BLK) * scale
    lo = lo + tab_c[t5_bucket(own_dist)].transpose(0, 2, 1)[None].astype(jnp.float32)
    lo = jnp.where((own_dist >= 0)[None, :, None, :], lo, NEG)
    p = jax.nn.softmax(jnp.concatenate([ls, lo], axis=-1), axis=-1)
    ps = p[..., : MOBA_TOP * MOBA_BLK].reshape(B, Q, H, MOBA_TOP, MOBA_BLK).astype(vsel.dtype)
    po = p[..., MOBA_TOP * MOBA_BLK:].reshape(B, Q, C_GROUPS, C_HPG, MOBA_BLK).astype(vown.dtype)
    o = (jnp.einsum('bqhrs,bqhrsd->bqhd', ps, vsel)
         + jnp.einsum('bqghs,bqgsd->bqghd', po, vown).reshape(B, Q, H, DH))
    return o


def moba_prompt(q, k, v, tab_c):
    B, T, H, DH = q.shape
    assert T % MOBA_BLK == 0 and TQ == MOBA_BLK and TK == MOBA_BLK
    qh = _heads_first(q.reshape(B, T, C_GROUPS, C_HPG, DH))
    kh, vh = _heads_first(k), _heads_first(v)
    flags = moba_select(qh, kh).transpose(0, 1, 3, 2)
    return block_attention('moba', qh, kh, vh, toeplitz_bias_tiles(tab_c), flags, DH ** -0.5)


def moba_sample(q, k, v, cache_k, cache_v, page_table, tab_c):
    B, S, H, DH = q.shape
    P = page_table.shape[1] * PAGE_SIZE
    L = P + S
    q_pos = P + jnp.arange(S)
    k_all = jnp.concatenate([cache_k[page_table].reshape(B, P, C_GROUPS, DH), k], axis=1)
    nf = L // MOBA_BLK
    kmean = k_all[:, : nf * MOBA_BLK].reshape(B, nf, MOBA_BLK, C_GROUPS, DH).astype(jnp.float32).mean(axis=2)
    idx, valid = moba_pick(q, q_pos, kmean)
    gh = (jnp.arange(H) // C_HPG).reshape(1, 1, H, 1, 1)
    sel_pos = idx[..., None] * MOBA_BLK + jnp.arange(MOBA_BLK)
    own_pos = (q_pos // MOBA_BLK)[:, None] * MOBA_BLK + jnp.arange(MOBA_BLK)
    opos = jnp.broadcast_to(own_pos[None, :, None, :], (B, S, C_GROUPS, MOBA_BLK))
    gi = jnp.arange(C_GROUPS).reshape(1, 1, C_GROUPS, 1)
    o = moba_attend(q, q_pos, idx, valid,
                    gather_paged(cache_k, page_table, k, sel_pos, gh), gather_paged(cache_v, page_table, v, sel_pos, gh),
                    own_pos, gather_paged(cache_k, page_table, k, opos, gi), gather_paged(cache_v, page_table, v, opos, gi),
                    tab_c)
    return o.reshape(B, S, H * DH)


def mem_kv(mem, g, wk, wv, layer):
    m = rms_norm(mem, g, BF16)
    B = mem.shape[0]
    return (matmul(m, wk, layer=layer).reshape(B, N_MEM, X_HEADS, X_DH),
            matmul(m, wv, layer=layer).reshape(B, N_MEM, X_HEADS, X_DH))


def cross_attend(x, h, mk, mv, wq, wo, layer):
    B, T = h.shape[:2]
    q = matmul(h, wq, layer=layer).reshape(B, T, X_HEADS, X_DH)
    logits = jnp.einsum('bthd,bmhd->bthm', q, mk, preferred_element_type=jnp.float32) * X_DH ** -0.5
    p = jax.nn.softmax(logits, axis=-1)
    o = jnp.einsum('bthm,bmhd->bthd', p.astype(mv.dtype), mv).reshape(B, T, X_HEADS * X_DH)
    return matmul(o, wo, resid=x, layer=layer)


def dense_swiglu(x, h, w1, w3, w2, layer):
    rows = h.shape[0]
    tm = _row_tile(rows)
    tables = _dense_tables(rows, tm)
    g = swiglu_up(h, w1[:, None], w3[:, None], tables, tm, layer)
    return grouped_matmul(g, [w2[:, None]], tables, tm, 1024, 512, F32, resid=x, layer=layer)


MOE_TM = 2304


def moe_tables(top_e):
    A = top_e.size
    n_tiles = -(-(A + N_EXPERTS * (MOE_TM - 1)) // MOE_TM)
    e_flat = top_e.reshape(A).astype(jnp.int32)
    order = jnp.argsort(e_flat).astype(jnp.int32)
    counts = jnp.bincount(e_flat, length=N_EXPERTS).astype(jnp.int32)
    starts = jnp.cumsum(counts) - counts
    tiles_per = (counts + MOE_TM - 1) // MOE_TM
    tile_end = jnp.cumsum(tiles_per)
    tile_start = tile_end - tiles_per
    e_sorted = e_flat[order]
    prow_sorted = tile_start[e_sorted] * MOE_TM + (jnp.arange(A, dtype=jnp.int32) - starts[e_sorted])
    row_token = (jnp.arange(n_tiles * MOE_TM, dtype=jnp.int32) % (A // TOP_K)).at[prow_sorted].set(order // TOP_K)
    prow_of_assign = jnp.zeros((A,), jnp.int32).at[order].set(prow_sorted)
    ti = jnp.arange(n_tiles, dtype=jnp.int32)
    n_active = tile_end[-1]
    last = n_active - 1
    src = jnp.minimum(ti, last)
    te = jnp.minimum(jnp.searchsorted(tile_end, src, side='right').astype(jnp.int32), N_EXPERTS - 1)
    rows_in = jnp.clip(counts[te] - (src - tile_start[te]) * MOE_TM, 0, MOE_TM)
    nsb = jnp.where(ti < n_active, (rows_in + SUB_ROWS - 1) // SUB_ROWS, 0).astype(jnp.int32)
    return (te, nsb, src), row_token, prow_of_assign


def moe_swiglu(xs, hs, w_router, b_router, w1, w3, w2, layer):
    wr = jnp.pad(w_router, ((0, 0), (0, LANES - N_EXPERTS)))
    logits = jnp.concatenate([matmul(h, wr)[:, :N_EXPERTS] for h in hs], axis=0) + b_router.astype(jnp.float32)
    h = jnp.concatenate(hs, axis=0)
    N = h.shape[0]
    top_l, top_e = lax.top_k(logits, TOP_K)
    gate = jax.nn.softmax(top_l, axis=-1)
    tables, row_token, prow_of_assign = moe_tables(top_e)
    g = swiglu_up(h[row_token], w1, w3, tables, MOE_TM, layer)
    y = grouped_matmul(g, [w2], tables, MOE_TM, 1024, 512, F32, layer=layer)
    slots = prow_of_assign.reshape(N, TOP_K)
    mix = sum(y[slots[:, c]] * gate[:, c:c + 1] for c in range(TOP_K))
    outs, lo = [], 0
    for xg in xs:
        outs.append(xg + mix[lo:lo + xg.shape[0]])
        lo += xg.shape[0]
    return outs


def even_split(z):
    B, T = z.shape[:2]
    qa, ka, va, ga, qb, kb, vb, qi, ki, wi = split_cols(z, EVEN_SPLITS)
    return (qa.reshape(B, T, A_GROUPS, A_HPG, A_DK),
            ka.reshape(B, T, 3, A_GROUPS, A_DK),
            va.reshape(B, T, 3, A_GROUPS, A_DV),
            jax.nn.sigmoid(ga.astype(jnp.float32)).reshape(B, T, 3, A_GROUPS, A_HPG),
            qb.reshape(B, T, B_GROUPS, B_HPG, B_DH),
            kb.reshape(B, T, B_GROUPS, B_DH),
            vb.reshape(B, T, B_GROUPS, B_DH),
            qi.reshape(B, T, IDX_HEADS, IDX_DIM), ki, wi)


def odd_split(z):
    B, T = z.shape[:2]
    q, k, v = split_cols(z, ODD_SPLITS)
    return (q.reshape(B, T, C_HEADS, C_DH), k.reshape(B, T, C_GROUPS, C_DH), v.reshape(B, T, C_GROUPS, C_DH))


def kernel(x_prompt, x_sample, mem_prompt, cache_a_k, cache_a_v, state_a_win_k, state_a_win_v,
           cache_b_k, cache_b_v, cache_b_kidx, cache_c_k, cache_c_v, cache_mem_k, cache_mem_v, page_table,
           rel_bias, norm_mix, norm_mem, norm_cross, norm_ffn, norm_final,
           w_cross_q, w_cross_k, w_cross_v, w_cross_o, w_in_even, w_out_even,
           nsa_pe_k, nsa_pe_v, nsa_phi_k1, nsa_phi_k2, nsa_phi_v1, nsa_phi_v2,
           w_ffn1, w_ffn3, w_ffn2, w_in_odd, w_out_odd, w_router, b_router, w_exp1, w_exp3, w_exp2):
    xp, xs = x_prompt, x_sample
    B, T, D = xp.shape
    Bs, S = xs.shape[:2]
    tab_a = rel_bias[:, :A_HEADS]
    tab_b = rel_bias[:, A_HEADS:A_HEADS + B_HEADS]
    tab_c = rel_bias[:, :C_HEADS]
    names = ('a_k_p', 'a_v_p', 'aw_k_p', 'aw_v_p', 'b_k_p', 'b_v_p', 'b_i_p', 'c_k_p', 'c_v_p', 'm_k_p', 'm_v_p',
             'a_k_s', 'a_v_s', 'aw_k_s', 'aw_v_s', 'b_k_s', 'b_v_s', 'b_i_s', 'c_k_s', 'c_v_s')
    new = {n: [] for n in names}
    for layer in range(DEPTH):
        li = layer // 2
        hp = rms_norm(xp, norm_mix[layer], BF16)
        hs = rms_norm(xs, norm_mix[layer], BF16)
        if layer % 2 == 0:
            phi = (nsa_pe_k[li], nsa_pe_v[li], nsa_phi_k1[li], nsa_phi_k2[li], nsa_phi_v1[li], nsa_phi_v2[li])
            qa, ka, va, ga, qb, kb, vb, qi, ki, wi = even_split(matmul(hp, w_in_even, layer=li))
            o_a = nsa_prompt(qa, ka, va, ga, *phi, tab_a)
            o_b = dsa_prompt(qb, kb, vb, qi, ki, wi, tab_b)
            xp = matmul(jnp.concatenate([o_a, o_b], axis=-1), w_out_even, resid=xp, layer=li)
            wk = min(WINDOW, T)
            new['a_k_p'].append(ka[:, :, :2]); new['a_v_p'].append(va[:, :, :2])
            new['aw_k_p'].append(ka[:, T - wk:, 2]); new['aw_v_p'].append(va[:, T - wk:, 2])
            new['b_k_p'].append(kb); new['b_v_p'].append(vb); new['b_i_p'].append(ki)
            qa, ka, va, ga, qb, kb, vb, qi, ki, wi = even_split(matmul(hs, w_in_even, layer=li))
            o_a = nsa_decode(qa, ka, va, ga, cache_a_k[li], cache_a_v, li, state_a_win_k[li], state_a_win_v[li],
                             page_table, *phi, tab_a)
            o_b = dsa_decode(qb, kb, vb, qi, ki, wi, cache_b_k, cache_b_v, cache_b_kidx, li, page_table, tab_b)
            xs = matmul(jnp.concatenate([o_a, o_b], axis=-1), w_out_even, resid=xs, layer=li)
            new['a_k_s'].append(ka[:, :, :2]); new['a_v_s'].append(va[:, :, :2])
            new['aw_k_s'].append(ka[:, :, 2]); new['aw_v_s'].append(va[:, :, 2])
            new['b_k_s'].append(kb); new['b_v_s'].append(vb); new['b_i_s'].append(ki)
        else:
            q, k, v = odd_split(matmul(hp, w_in_odd, layer=li))
            xp = matmul(moba_prompt(q, k, v, tab_c), w_out_odd, resid=xp, layer=li)
            new['c_k_p'].append(k); new['c_v_p'].append(v)
            q, k, v = odd_split(matmul(hs, w_in_odd, layer=li))
            xs = matmul(moba_decode(q, k, v, cache_c_k, cache_c_v, li, page_table, tab_c), w_out_odd,
                        resid=xs, layer=li)
            new['c_k_s'].append(k); new['c_v_s'].append(v)
        mk, mv = mem_kv(mem_prompt, norm_mem[layer], w_cross_k, w_cross_v, layer)
        new['m_k_p'].append(mk); new['m_v_p'].append(mv)
        xp = cross_attend(xp, rms_norm(xp, norm_cross[layer], BF16), mk, mv, w_cross_q, w_cross_o, layer)
        xs = cross_attend(xs, rms_norm(xs, norm_cross[layer], BF16), cache_mem_k[layer], cache_mem_v[layer],
                          w_cross_q, w_cross_o, layer)
        hp = rms_norm(xp, norm_ffn[layer], BF16).reshape(B * T, D)
        hs = rms_norm(xs, norm_ffn[layer], BF16).reshape(Bs * S, D)
        xp2, xs2 = xp.reshape(B * T, D), xs.reshape(Bs * S, D)
        if layer % 2 == 0:
            xp2 = dense_swiglu(xp2, hp, w_ffn1, w_ffn3, w_ffn2, li)
            xs2 = dense_swiglu(xs2, hs, w_ffn1, w_ffn3, w_ffn2, li)
        else:
            xp2, xs2 = moe_swiglu([xp2, xs2], [hp, hs], w_router[li], b_router[li], w_exp1, w_exp3, w_exp2, li)
        xp = xp2.reshape(B, T, D)
        xs = xs2.reshape(Bs, S, D)
    y_prompt = rms_norm(xp, norm_final)
    y_sample = rms_norm(xs, norm_final)
    return (y_prompt, y_sample,
            jnp.stack(new['a_k_p']), jnp.stack(new['a_v_p']), jnp.stack(new['aw_k_p']), jnp.stack(new['aw_v_p']),
            jnp.stack(new['b_k_p']), jnp.stack(new['b_v_p']), jnp.stack(new['b_i_p']),
            jnp.stack(new['c_k_p']), jnp.stack(new['c_v_p']), jnp.stack(new['m_k_p']), jnp.stack(new['m_v_p']),
            jnp.stack(new['a_k_s']), jnp.stack(new['a_v_s']), jnp.stack(new['aw_k_s']), jnp.stack(new['aw_v_s']),
            jnp.stack(new['b_k_s']), jnp.stack(new['b_v_s']), jnp.stack(new['b_i_s']),
            jnp.stack(new['c_k_s']), jnp.stack(new['c_v_s']))
```

```python
import functools
import math
import jax, jax.numpy as jnp
from jax import lax
import numpy as np
from jax.experimental import pallas as pl
from jax.experimental.pallas import tpu as pltpu

D_MODEL = 4096
BATCH = 4
SEQ = 2048
DEPTH = 2
DEC_BATCH = 8
DEC_SEQ = 1
PAST_LEN = 16384
PAGE_SIZE = 128

N_EVEN = (DEPTH + 1) // 2
N_ODD = DEPTH // 2
HEAD_SLOTS = 32
A_HEADS = 16
A_GROUPS = 2
A_HPG = A_HEADS // A_GROUPS
A_DK = 192
A_DV = 128
CMP_BLK = 64
N_SEL = 16
WINDOW = 512
B_HEADS = 16
B_GROUPS = 2
B_HPG = B_HEADS // B_GROUPS
B_DH = 128
IDX_HEADS = 8
IDX_DIM = 64
IDX_TOPK = 256
C_HEADS = 32
C_GROUPS = 8
C_HPG = C_HEADS // C_GROUPS
C_DH = 128
MOBA_BLK = 256
MOBA_TOP = 3
MOBA_QCHUNK = 32
N_MEM = 256
X_HEADS = 4
X_DH = 128
D_FF = 14336
N_EXPERTS = 8
TOP_K = 2
MOE_MAX_ROWS = 512
N_BUCKETS = 32
REL_EXACT = 16
REL_MAX_DIST = 1024
Q_BLOCK = 128
EPS = 1e-6
NEG = -1e30
FORCE = 1e9
EVEN_SPLITS = (A_HEADS * A_DK, 3 * A_GROUPS * A_DK, 3 * A_GROUPS * A_DV, 3 * A_HEADS,
               B_HEADS * B_DH, B_GROUPS * B_DH, B_GROUPS * B_DH, IDX_HEADS * IDX_DIM, IDX_DIM, IDX_HEADS)
ODD_SPLITS = (C_HEADS * C_DH, C_GROUPS * C_DH, C_GROUPS * C_DH)


def _rmsnorm_body(x_ref, g_ref, o_ref):
    x = x_ref[...]
    y = x * lax.rsqrt(jnp.mean(x * x, axis=-1, keepdims=True) + EPS)
    o_ref[...] = (y * g_ref[...]).astype(o_ref.dtype)


def rms_norm(x, g, out_dtype=None):
    out_dtype = out_dtype or x.dtype
    shape = x.shape
    d = shape[-1]
    x2 = x.reshape(-1, d)
    rows = x2.shape[0]
    tr = min(rows, 256)
    out = pl.pallas_call(
        _rmsnorm_body,
        grid=(rows // tr,),
        in_specs=[pl.BlockSpec((tr, d), lambda i: (i, 0)),
                  pl.BlockSpec((1, d), lambda i: (0, 0))],
        out_specs=pl.BlockSpec((tr, d), lambda i: (i, 0)),
        out_shape=jax.ShapeDtypeStruct((rows, d), out_dtype),
    )(x2, g.reshape(1, d).astype(jnp.float32))
    return out.reshape(shape)


TQ = 256
TK = 256
N_OFF = -(-(REL_MAX_DIST + TK - 1) // TK) + 1
LANES = 128
VMEM_LIMIT = 48 * 1024 * 1024
_NT = (((1,), (1,)), ((), ()))
BF16 = jnp.bfloat16
F32 = jnp.float32


def bias_by_distance(tab, n_dist):
    return tab[t5_bucket(jnp.arange(n_dist))].T.astype(F32)


def toeplitz_bias_tiles(tab):
    H = tab.shape[1]
    bd = bias_by_distance(tab, N_OFF * TK + TQ)
    epad = jnp.concatenate([jnp.broadcast_to(bd[:, :1], (H, TK - 1)), bd], axis=1)
    w = TQ + TK - 1
    rows = []
    for o in range(N_OFF):
        erev = epad[:, o * TK: o * TK + w][:, ::-1]
        z = jnp.concatenate([erev, erev[:, :1]], axis=1)
        rows.append(jnp.roll(z, -(TQ - 1), axis=1))
    x = jnp.stack(rows, axis=1)
    y = jnp.tile(x, (1, 1, TQ))[:, :, : TQ * w].reshape(H, N_OFF, TQ, w)
    return y[..., :TK]


HEADS_PER_STEP = 4
PREFIX_STEP = 1


def _row_attn_body(mode, hs, dv, scale, n_tiles, *refs):
    if mode == 'window':
        q_ref, k_ref, v_ref, b_ref, o_ref = refs
        x_ref = None
    else:
        q_ref, k_ref, v_ref, b_ref, x_ref, o_ref = refs
    qi = pl.program_id(3)
    w = n_tiles * TK

    def heads(n, t0, k, v, mask_of):
        for h in range(hs):
            mask = mask_of(h, n)
            bias = jnp.concatenate([b_ref[h, jnp.clip(qi - (t0 + j), 0, N_OFF - 1)] for j in range(n)], axis=1)
            s = lax.dot_general(q_ref[0, 0, h].astype(BF16), k, _NT, preferred_element_type=F32) * scale + bias
            s = jnp.where(mask, s, NEG)
            p = jnp.where(mask, jnp.exp(s - jnp.max(s, axis=1, keepdims=True)), 0.0)
            l = jnp.sum(p, axis=1, keepdims=True)
            o = jnp.dot(p.astype(BF16), v, preferred_element_type=F32)
            o_ref[0, :, h * dv:(h + 1) * dv] = jnp.where(l > 0.0, o / jnp.where(l > 0.0, l, 1.0), 0.0)

    if mode == 'window':
        t0 = jnp.maximum(qi - (n_tiles - 1), 0)
        c0 = pl.multiple_of(t0 * TK, TK)
        k = k_ref[0, 0, pl.ds(c0, w), :].astype(BF16)
        v = v_ref[0, 0, pl.ds(c0, w), :].astype(BF16)
        dist = (qi * TQ + lax.broadcasted_iota(jnp.int32, (TQ, w), 0)) - (c0 + lax.broadcasted_iota(jnp.int32, (TQ, w), 1))
        shared_mask = (dist >= 0) & (dist < WINDOW)
        heads(n_tiles, t0, k, v, lambda h, n: shared_mask)
        return

    def causal_prefix(n):
        k = k_ref[0, 0, :n * TK, :].astype(BF16)
        v = v_ref[0, 0, :n * TK, :].astype(BF16)
        if mode == 'key':
            shared_mask = x_ref[0, 0, :, :n * TK] > 0
            heads(n, 0, k, v, lambda h, n: shared_mask)
        else:
            causal = jnp.where(lax.broadcasted_iota(jnp.int32, (TQ, TK), 0)
                               >= lax.broadcasted_iota(jnp.int32, (TQ, TK), 1), 1.0, 0.0)

            def moba_mask(h, n):
                flags = x_ref[0, h]
                return jnp.concatenate(
                    [jnp.where(qi == j, causal, jnp.broadcast_to(flags[:, j:j + 1], (TQ, TK))) for j in range(n)],
                    axis=1) > 0.5
            heads(n, 0, k, v, moba_mask)

    prefixes = sorted({min(n_tiles, p) for p in range(PREFIX_STEP, n_tiles + PREFIX_STEP, PREFIX_STEP)})
    lo = 0
    for n in prefixes:
        pl.when((qi + 1 > lo) & (qi + 1 <= n))(functools.partial(causal_prefix, n))
        lo = n


def block_attention(mode, q, k, v, bias_tiles, extra, scale):
    B, G, HPG, T, dk = q.shape
    dv = v.shape[-1]
    hs = min(HEADS_PER_STEP, HPG)
    nh = HPG // hs
    n_tiles = min(WINDOW // TK + 1, T // TK) if mode == 'window' else T // TK
    assert HPG % hs == 0 and T % TK == 0
    in_specs = [
        pl.BlockSpec((1, 1, hs, TQ, dk), lambda b, g, hh, qi: (b, g, hh, qi, 0)),
        pl.BlockSpec((1, 1, T, dk), lambda b, g, hh, qi: (b, g, 0, 0)),
        pl.BlockSpec((1, 1, T, dv), lambda b, g, hh, qi: (b, g, 0, 0)),
        pl.BlockSpec((hs, N_OFF, TQ, TK), lambda b, g, hh, qi: (g * nh + hh, 0, 0, 0)),
    ]
    args = [q, k, v, bias_tiles]
    if mode == 'key':
        gm = extra.shape[1]
        in_specs.append(pl.BlockSpec((1, 1, TQ, T), lambda b, g, hh, qi: (b, g if gm > 1 else 0, qi, 0)))
        args.append(extra)
    elif mode == 'moba':
        in_specs.append(pl.BlockSpec((1, hs, TQ, extra.shape[-1]), lambda b, g, hh, qi: (b, g * nh + hh, qi, 0)))
        args.append(extra)
    return pl.pallas_call(
        functools.partial(_row_attn_body, mode, hs, dv, scale, n_tiles),
        grid=(B, G, nh, T // TQ),
        in_specs=in_specs,
        out_specs=pl.BlockSpec((1, TQ, hs * dv), lambda b, g, hh, qi: (b, qi, g * nh + hh)),
        out_shape=jax.ShapeDtypeStruct((B, T, G * HPG * dv), F32),
        compiler_params=pltpu.CompilerParams(
            dimension_semantics=("parallel", "parallel", "parallel", "parallel"), vmem_limit_bytes=VMEM_LIMIT),
        name=f"attn_{mode}",
    )(*args)


def _nsa_cmp_body(q_ref, kc_ref, vc_ref, o_ref, mask_ref):
    qi = pl.program_id(2)
    nc = kc_ref.shape[2]
    T = mask_ref.shape[-1]
    kc = kc_ref[0, 0].astype(BF16)
    vc = vc_ref[0, 0].astype(BF16)
    t = qi * TQ + lax.broadcasted_iota(jnp.int32, (TQ, nc), 0)
    j = lax.broadcasted_iota(jnp.int32, (TQ, nc), 1)
    vis = (j + 1) * CMP_BLK <= t + 1
    imp = jnp.zeros((TQ, nc), F32)
    for h in range(A_HPG):
        q = q_ref[0, 0, h].astype(BF16)
        s = lax.dot_general(q, kc, _NT, preferred_element_type=F32) * (A_DK ** -0.5)
        s = jnp.where(vis, s, NEG)
        e = jnp.where(vis, jnp.exp(s - jnp.max(s, axis=1, keepdims=True)), 0.0)
        l = jnp.sum(e, axis=1, keepdims=True)
        p = jnp.where(l > 0.0, e / jnp.where(l > 0.0, l, 1.0), 0.0)
        o_ref[0, :, h * A_DV:(h + 1) * A_DV] = jnp.dot(p.astype(BF16), vc, preferred_element_type=F32)
        imp = imp + p
    own = t // CMP_BLK
    forced = (j == 0) | (j == own) | (j == own - 1)
    score = jnp.where(forced, FORCE, imp)
    score = jnp.where(j <= own, score, NEG)
    rank = jnp.zeros((TQ, nc), F32)
    for i in range(nc):
        si = score[:, i:i + 1]
        rank = rank + jnp.where((si > score) | ((si == score) & (i < j)), 1.0, 0.0)
    sel = jnp.where((rank < float(N_SEL)) & (j <= own), 1.0, 0.0).astype(BF16)
    expand = jnp.where(lax.broadcasted_iota(jnp.int32, (nc, T), 1) // CMP_BLK
                       == lax.broadcasted_iota(jnp.int32, (nc, T), 0), 1.0, 0.0).astype(BF16)
    keys = jnp.dot(sel, expand, preferred_element_type=F32)
    causal = (qi * TQ + lax.broadcasted_iota(jnp.int32, (TQ, T), 0)) >= lax.broadcasted_iota(jnp.int32, (TQ, T), 1)
    mask_ref[0, 0] = jnp.where((keys > 0.5) & causal, 1.0, 0.0).astype(BF16)


def nsa_cmp_select(q, kc, vc):
    B, G, HPG, T, dk = q.shape
    nc = kc.shape[2]
    return pl.pallas_call(
        _nsa_cmp_body,
        grid=(B, G, T // TQ),
        in_specs=[pl.BlockSpec((1, 1, HPG, TQ, dk), lambda b, g, qi: (b, g, 0, qi, 0)),
                  pl.BlockSpec((1, 1, nc, dk), lambda b, g, qi: (b, g, 0, 0)),
                  pl.BlockSpec((1, 1, nc, A_DV), lambda b, g, qi: (b, g, 0, 0))],
        out_specs=[pl.BlockSpec((1, TQ, HPG * A_DV), lambda b, g, qi: (b, qi, g)),
                   pl.BlockSpec((1, 1, TQ, T), lambda b, g, qi: (b, g, qi, 0))],
        out_shape=[jax.ShapeDtypeStruct((B, T, G * HPG * A_DV), F32),
                   jax.ShapeDtypeStruct((B, G, T, T), BF16)],
        compiler_params=pltpu.CompilerParams(
            dimension_semantics=("parallel", "parallel", "parallel"), vmem_limit_bytes=VMEM_LIMIT),
        name="nsa_cmp_select",
    )(q, kc, vc)


def _count(cond):
    return jnp.sum(jnp.where(cond, 1.0, 0.0), axis=1, keepdims=True)


def _dsa_select_body(topk, qi_ref, ki_ref, w_ref, mask_ref):
    qt = pl.program_id(1)
    T = ki_ref.shape[1]
    kidx = ki_ref[0].astype(BF16)
    w = w_ref[0] * (IDX_HEADS ** -0.5)
    score = jnp.zeros((TQ, T), F32)
    for h in range(IDX_HEADS):
        d = lax.dot_general(qi_ref[0, h].astype(BF16), kidx, _NT, preferred_element_type=F32) * (IDX_DIM ** -0.5)
        score = score + jnp.maximum(d, 0.0) * w[:, h:h + 1]
    t = qt * TQ + lax.broadcasted_iota(jnp.int32, (TQ, T), 0)
    s = lax.broadcasted_iota(jnp.int32, (TQ, T), 1)
    causal = s <= t
    score = jnp.where(causal, score, NEG)
    bits = pltpu.bitcast(score, jnp.int32)
    key = jnp.where(bits < 0, bits ^ jnp.int32(0x7FFFFFFF), bits)
    int_min = jnp.int32(-2 ** 31)

    def value_step(i, lo):
        cand = lo + jnp.left_shift(jnp.int32(1), 31 - i)
        return jnp.where(_count(key >= cand) >= float(topk), cand, lo)
    thr = lax.fori_loop(0, 32, value_step, jnp.full((TQ, 1), int_min, jnp.int32))
    above = key > thr
    tied = key == thr
    need = float(topk) - _count(above)
    n_bits = max(1, (T - 1).bit_length())

    def index_step(i, m):
        cand = m + jnp.left_shift(jnp.int32(1), n_bits - 1 - i)
        return jnp.where(_count(tied & (s < cand)) < need, cand, m)
    last = lax.fori_loop(0, n_bits, index_step, jnp.zeros((TQ, 1), jnp.int32))
    sel = (above | (tied & (s <= last))) & causal
    mask_ref[0, 0] = jnp.where(sel, 1.0, 0.0).astype(BF16)


def dsa_select(qi, ki, wi, topk):
    B, H, T, d = qi.shape
    return pl.pallas_call(
        functools.partial(_dsa_select_body, topk),
        grid=(B, T // TQ),
        in_specs=[pl.BlockSpec((1, H, TQ, d), lambda b, qt: (b, 0, qt, 0)),
                  pl.BlockSpec((1, T, d), lambda b, qt: (b, 0, 0)),
                  pl.BlockSpec((1, TQ, H), lambda b, qt: (b, qt, 0))],
        out_specs=pl.BlockSpec((1, 1, TQ, T), lambda b, qt: (b, 0, qt, 0)),
        out_shape=jax.ShapeDtypeStruct((B, 1, T, T), BF16),
        compiler_params=pltpu.CompilerParams(
            dimension_semantics=("parallel", "parallel"), vmem_limit_bytes=VMEM_LIMIT),
        name="dsa_select",
    )(qi, ki, wi)


def _moba_select_body(q_ref, k_ref, f_ref):
    T = k_ref.shape[2]
    nb = T // MOBA_BLK
    row = lax.broadcasted_iota(jnp.int32, (nb, k_ref.shape[3]), 0)
    kmean = jnp.zeros((nb, k_ref.shape[3]), F32)
    for b in range(nb):
        blk_sum = jnp.sum(k_ref[0, 0, b * MOBA_BLK:(b + 1) * MOBA_BLK, :], axis=0, keepdims=True)
        kmean = jnp.where(row == b, blk_sum * (1.0 / MOBA_BLK), kmean)
    kmean = kmean.astype(BF16)
    j = lax.broadcasted_iota(jnp.int32, (nb, T), 0)
    past = j < lax.broadcasted_iota(jnp.int32, (nb, T), 1) // MOBA_BLK
    for h in range(C_HPG):
        s = lax.dot_general(kmean, q_ref[0, 0, h].astype(BF16), _NT, preferred_element_type=F32)
        s = jnp.where(past, s, NEG)
        rank = jnp.zeros((nb, T), F32)
        for i in range(nb):
            si = s[i:i + 1, :]
            rank = rank + jnp.where((si > s) | ((si == s) & (i < j)), 1.0, 0.0)
        f_ref[0, h] = jnp.where((rank < float(MOBA_TOP)) & past, 1.0, 0.0)


def moba_select(q, k):
    B, G, HPG, T, dh = q.shape
    nb = T // MOBA_BLK
    return pl.pallas_call(
        _moba_select_body,
        grid=(B, G),
        in_specs=[pl.BlockSpec((1, 1, HPG, T, dh), lambda b, g: (b, g, 0, 0, 0)),
                  pl.BlockSpec((1, 1, T, dh), lambda b, g: (b, g, 0, 0))],
        out_specs=pl.BlockSpec((1, HPG, nb, T), lambda b, g: (b, g, 0, 0)),
        out_shape=jax.ShapeDtypeStruct((B, G * HPG, nb, T), F32),
        compiler_params=pltpu.CompilerParams(
            dimension_semantics=("parallel", "parallel"), vmem_limit_bytes=VMEM_LIMIT),
        name="moba_select",
    )(q, k)


MM_VMEM_LIMIT = 56 * 1024 * 1024
SUB_ROWS = 256


def _gmm_body(n_w, nk, nsb_max, sb, has_resid, te_ref, ns_ref, src_ref, *refs):
    x_ref = refs[0]
    w_refs = refs[1:1 + n_w]
    pos = 1 + n_w
    r_ref = refs[pos] if has_resid else None
    pos += int(has_resid)
    o_ref = refs[pos]
    acc_refs = refs[pos + 1:pos + 1 + n_w]
    i = pl.program_id(0)
    k = pl.program_id(2)
    n_sb = ns_ref[i]

    @pl.when(k == 0)
    def _():
        for a_ref in acc_refs:
            a_ref[...] = jnp.zeros(a_ref.shape, F32)

    for c in range(1, nsb_max + 1):
        @pl.when(n_sb == c)
        def _(c=c):
            xs = x_ref[:c * sb, :].astype(BF16)
            for w_ref, a_ref in zip(w_refs, acc_refs):
                a_ref[:c * sb, :] += jnp.dot(xs, w_ref[0].astype(BF16), preferred_element_type=F32)

    @pl.when(k == nk - 1)
    def _():
        if n_w == 2:
            val = jax.nn.silu(acc_refs[0][...]) * acc_refs[1][...]
        else:
            val = acc_refs[0][...]
        if has_resid:
            val = val + r_ref[...]
        o_ref[...] = val.astype(o_ref.dtype)


def grouped_matmul(x, ws, tables, tm, tn, tk, out_dtype, resid=None, layer=0):
    P, K = x.shape
    N = ws[0].shape[-1]
    tk = min(tk, K)
    tn = min(tn, N)
    assert P % tm == 0 and K % tk == 0
    sb = min(SUB_ROWS, tm)
    assert tm % sb == 0
    n_w = len(ws)
    ni, nj, nk = P // tm, pl.cdiv(N, tn), K // tk

    def x_idx(i, j, k, te, ns, src):
        return (src[i], jnp.where(ns[i] > 0, k, nk - 1))

    def w_idx(i, j, k, te, ns, src):
        act = ns[i] > 0
        return (layer, te[i], jnp.where(act, k, nk - 1), jnp.where(act, j, nj - 1))

    def o_idx(i, j, k, te, ns, src):
        return (i, j)

    in_specs = [pl.BlockSpec((tm, tk), x_idx)] + [pl.BlockSpec((None, 1, tk, tn), w_idx)] * n_w
    args = [x] + list(ws)
    if resid is not None:
        in_specs.append(pl.BlockSpec((tm, tn), o_idx))
        args.append(resid)
    return pl.pallas_call(
        functools.partial(_gmm_body, n_w, nk, tm // sb, sb, resid is not None),
        grid_spec=pltpu.PrefetchScalarGridSpec(
            num_scalar_prefetch=3, grid=(ni, nj, nk), in_specs=in_specs,
            out_specs=pl.BlockSpec((tm, tn), o_idx),
            scratch_shapes=[pltpu.VMEM((tm, tn), F32)] * n_w),
        out_shape=jax.ShapeDtypeStruct((P, N), out_dtype),
        compiler_params=pltpu.CompilerParams(
            dimension_semantics=("parallel", "parallel", "arbitrary"), vmem_limit_bytes=MM_VMEM_LIMIT),
        name=f"gmm{n_w}_{tm}x{tn}x{tk}",
    )(*tables, *args)


def _dense_tables(rows, tm):
    n = rows // tm
    return (jnp.zeros((n,), jnp.int32), jnp.full((n,), tm // min(SUB_ROWS, tm), jnp.int32),
            jnp.arange(n, dtype=jnp.int32))


def _row_tile(rows):
    for tm in (2048, 1024, 512, 256):
        if rows % tm == 0:
            return tm
    return rows


def matmul(x, w, out_dtype=F32, resid=None, tn=1024, tk=512, layer=None):
    lead = x.shape[:-1]
    x2 = x.reshape(-1, x.shape[-1])
    rows = x2.shape[0]
    tm = _row_tile(rows)
    r2 = None if resid is None else resid.reshape(rows, -1)
    w4 = w[None, None] if layer is None else w[:, None]
    out = grouped_matmul(x2, [w4], _dense_tables(rows, tm), tm, tn, tk, out_dtype, r2, layer or 0)
    return out.reshape(*lead, w.shape[-1])


def swiglu_up(x, w1, w3, tables, tm, layer, tf=512, tk=1024):
    return grouped_matmul(x, [w1, w3], tables, tm, tf, tk, BF16, layer=layer)


PAGES_PER_STEP = 8
REMOVED = -3e38


def bias_by_position(tab, past_len):
    bd = bias_by_distance(tab, REL_MAX_DIST + 1)
    H = bd.shape[0]
    near = bd[:, 1:REL_MAX_DIST + 1][:, ::-1]
    far = jnp.broadcast_to(bd[:, REL_MAX_DIST:], (H, past_len - REL_MAX_DIST))
    return jnp.concatenate([far, near], axis=1), bd[:, :1]


def _page_specs(lanes, lane_block, pages_of):
    def spec(r):
        return pl.BlockSpec((None, PAGE_SIZE, lanes), lambda *a: (pages_of(r)(*a), 0, lane_block))
    return [spec(r) for r in range(PAGES_PER_STEP)]


def _chunk_page(r):
    return lambda b, c, pt, *_: pt[b, c * PAGES_PER_STEP + r]


def _topk_rows(score, k):
    R, L = score.shape
    jf = lax.broadcasted_iota(jnp.int32, (R, L), 1).astype(F32)
    slot = lax.broadcasted_iota(jnp.int32, (R, LANES), 1)
    idx = jnp.zeros((R, LANES), F32)
    val = jnp.zeros((R, LANES), F32)
    for n in range(k):
        m = jnp.max(score, axis=1, keepdims=True)
        i = jnp.min(jnp.where(score == m, jf, 1e9), axis=1, keepdims=True)
        idx = jnp.where(slot == n, i, idx)
        val = jnp.where(slot == n, jnp.where(m > 0.5 * NEG, 1.0, 0.0), val)
        score = jnp.where(jf == i, REMOVED, score)
    return idx, val


def _gather_cmp_body(width, native, pt_ref, *refs):
    pages = refs[:PAGES_PER_STEP]
    pe_ref, o_ref = refs[PAGES_PER_STEP:]
    for r in range(PAGES_PER_STEP):
        for g in range(A_GROUPS):
            rows = pages[r][:, g, :] if native else pages[r][:, g * width:(g + 1) * width]
            o_ref[0, g, r * PAGE_SIZE:(r + 1) * PAGE_SIZE, :] = rows + pe_ref[...]


def gather_compress_rows(cache, page_table, pe, width, layer=None):
    B, n_pages = page_table.shape
    pe2 = jnp.tile(pe, (PAGE_SIZE // CMP_BLK, 1))
    if layer is None:
        page_specs = _page_specs(A_GROUPS * width, 0, _chunk_page)
    else:
        page_specs = [pl.BlockSpec((None, None, PAGE_SIZE, None, A_GROUPS, width),
                                   lambda b, c, pt, r=r: (layer, pt[b, c * PAGES_PER_STEP + r], 0, 0, 0, 0))
                      for r in range(PAGES_PER_STEP)]
    return pl.pallas_call(
        functools.partial(_gather_cmp_body, width, layer is not None),
        grid_spec=pltpu.PrefetchScalarGridSpec(
            num_scalar_prefetch=1, grid=(B, n_pages // PAGES_PER_STEP),
            in_specs=page_specs
            + [pl.BlockSpec((PAGE_SIZE, width), lambda b, c, pt: (0, 0))],
            out_specs=pl.BlockSpec((1, A_GROUPS, PAGES_PER_STEP * PAGE_SIZE, width), lambda b, c, pt: (b, 0, c, 0))),
        out_shape=jax.ShapeDtypeStruct((B, A_GROUPS, n_pages * PAGE_SIZE, width), F32),
        compiler_params=pltpu.CompilerParams(dimension_semantics=("parallel", "parallel"),
                                             vmem_limit_bytes=VMEM_LIMIT),
        name="gather_compress_rows",
    )(page_table, *([cache] * PAGES_PER_STEP), pe2)


def _nsa_sample_a_body(q_ref, kc_ref, vc_ref, wk_ref, wv_ref, kn_ref, vn_ref, bw_ref, oc_ref, ow_ref, idx_ref, val_ref):
    scale = A_DK ** -0.5
    q = q_ref[0, 0]
    qb = q.astype(BF16)
    nc = kc_ref.shape[2]
    s = lax.dot_general(qb, kc_ref[0, 0].astype(BF16), _NT, preferred_element_type=F32) * scale
    e = jnp.exp(s - jnp.max(s, axis=1, keepdims=True))
    p = e / jnp.sum(e, axis=1, keepdims=True)
    oc_ref[0, 0] = jnp.dot(p.astype(BF16), vc_ref[0, 0].astype(BF16), preferred_element_type=F32)
    imp = jnp.concatenate([jnp.sum(p, axis=0, keepdims=True), jnp.zeros((1, LANES), F32)], axis=1)
    j = lax.broadcasted_iota(jnp.int32, imp.shape, 1)
    own = nc
    forced = (j == 0) | (j == own) | (j == own - 1)
    score = jnp.where(forced, FORCE, imp)
    score = jnp.where(j <= own, score, REMOVED)
    idx, val = _topk_rows(score, N_SEL)
    idx_ref[0, 0] = idx.astype(jnp.int32)
    val_ref[0, 0] = val.astype(jnp.int32)
    wb = wk_ref.shape[2]
    sw = lax.dot_general(qb, wk_ref[0, 0].astype(BF16), _NT, preferred_element_type=F32) * scale + bw_ref[0][:, :wb]
    dist = wb - lax.broadcasted_iota(jnp.int32, sw.shape, 1)
    in_win = dist < WINDOW
    sn = jnp.sum(q * kn_ref[0, 0], axis=1, keepdims=True) * scale + bw_ref[0][:, wb:wb + 1]
    m = jnp.maximum(jnp.max(jnp.where(in_win, sw, NEG), axis=1, keepdims=True), sn)
    ew = jnp.where(in_win, jnp.exp(sw - m), 0.0)
    en = jnp.exp(sn - m)
    l = jnp.sum(ew, axis=1, keepdims=True) + en
    ow_ref[0, 0] = (jnp.dot(ew.astype(BF16), wv_ref[0, 0].astype(BF16), preferred_element_type=F32)
                    + en * vn_ref[0, 0]) / l


def _nsa_sample_b_body(nc, idx_ref, val_ref, pt_ref, q_ref, ka_ref, kb_ref, va_ref, vb_ref, ba_ref, bb_ref,
                       kn_ref, vn_ref, b0_ref, o_ref, m_ref, l_ref, acc_ref):
    b = pl.program_id(0)
    n = pl.program_id(1)
    scale = A_DK ** -0.5

    @pl.when(n == 0)
    def _():
        m_ref[...] = jnp.full(m_ref.shape, NEG, F32)
        l_ref[...] = jnp.zeros(l_ref.shape, F32)
        acc_ref[...] = jnp.zeros(acc_ref.shape, F32)

    for g, (k_ref, v_ref, bias_ref) in enumerate(((ka_ref, va_ref, ba_ref), (kb_ref, vb_ref, bb_ref))):
        slot = (b * A_GROUPS + g) * N_SEL + n
        cached = (val_ref[slot] > 0) & (idx_ref[slot] < nc)

        @pl.when(cached)
        def _(g=g, k_ref=k_ref, v_ref=v_ref, bias_ref=bias_ref):
            lo_k = (A_GROUPS + g) * A_DK
            k = k_ref[:, lo_k:lo_k + A_DK].astype(BF16)
            v = v_ref[:, g, :].astype(BF16)
            s = lax.dot_general(q_ref[0, g].astype(BF16), k, _NT, preferred_element_type=F32) * scale + bias_ref[0]
            m_prev = m_ref[g][:, :1]
            m_new = jnp.maximum(m_prev, jnp.max(s, axis=1, keepdims=True))
            p = jnp.exp(s - m_new)
            alpha = jnp.exp(m_prev - m_new)
            l_ref[g] = jnp.broadcast_to(alpha * l_ref[g][:, :1] + jnp.sum(p, axis=1, keepdims=True), l_ref.shape[1:])
            acc_ref[g] = alpha * acc_ref[g] + jnp.dot(p.astype(BF16), v, preferred_element_type=F32)
            m_ref[g] = jnp.broadcast_to(m_new, m_ref.shape[1:])

    @pl.when(n == N_SEL - 1)
    def _():
        for g in range(A_GROUPS):
            sn = jnp.sum(q_ref[0, g] * kn_ref[0, g], axis=1, keepdims=True) * scale + b0_ref[g]
            m_prev = m_ref[g][:, :1]
            m_new = jnp.maximum(m_prev, sn)
            alpha = jnp.exp(m_prev - m_new)
            en = jnp.exp(sn - m_new)
            l = alpha * l_ref[g][:, :1] + en
            o_ref[0, g] = (alpha * acc_ref[g] + en * vn_ref[0, g]) / l


def nsa_decode(q, k3, v3, gates, cache_k, cache_v, layer, win_k, win_v, page_table,
               pe_k, pe_v, wk1, wk2, wv1, wv2, tab_a):
    B, S, G, HPG, DK = q.shape
    n_pool = cache_k.shape[0]
    n_pages = page_table.shape[1]
    P = n_pages * PAGE_SIZE
    nc = P // CMP_BLK
    assert S == 1 and G == A_GROUPS and P % CMP_BLK == 0 and (P + S) // CMP_BLK == nc and win_k.shape[1] == WINDOW
    ck = cache_k.reshape(n_pool, PAGE_SIZE, 2 * G * DK)
    rows_k = gather_compress_rows(ck, page_table, pe_k, DK).reshape(B, G, nc, CMP_BLK * DK)
    rows_v = gather_compress_rows(cache_v, page_table, pe_v, A_DV, layer).reshape(B, G, nc, CMP_BLK * A_DV)
    kc = matmul(jax.nn.silu(matmul(rows_k, wk1)), wk2)
    vc = matmul(jax.nn.silu(matmul(rows_v, wv1)), wv2)
    qh = q.reshape(B, G, HPG, DK)
    brev, b0 = bias_by_position(tab_a, P)
    bw = jnp.concatenate([brev[:, P - WINDOW:], b0, jnp.zeros((A_HEADS, LANES - 1), F32)], axis=1)
    bw = bw.reshape(G, HPG, WINDOW + LANES)
    new_k = jnp.moveaxis(k3[:, 0], 1, 2)
    new_v = jnp.moveaxis(v3[:, 0], 1, 2)
    per_bg = lambda *shape: pl.BlockSpec((1, 1) + shape, lambda b, g: (b, g) + (0,) * len(shape))
    o_c, o_w, idx, val = pl.pallas_call(
        _nsa_sample_a_body,
        grid=(B, G),
        in_specs=[per_bg(HPG, DK), per_bg(nc, DK), per_bg(nc, A_DV), per_bg(WINDOW, DK), per_bg(WINDOW, A_DV),
                  per_bg(1, DK), per_bg(1, A_DV),
                  pl.BlockSpec((1, HPG, WINDOW + LANES), lambda b, g: (g, 0, 0))],
        out_specs=[per_bg(HPG, A_DV), per_bg(HPG, A_DV), per_bg(1, LANES), per_bg(1, LANES)],
        out_shape=[jax.ShapeDtypeStruct((B, G, HPG, A_DV), F32)] * 2
        + [jax.ShapeDtypeStruct((B, G, 1, LANES), jnp.int32)] * 2,
        compiler_params=pltpu.CompilerParams(dimension_semantics=("parallel", "parallel"),
                                             vmem_limit_bytes=VMEM_LIMIT),
        name="nsa_sample_cmp_win",
    )(qh, kc, vc, jnp.moveaxis(win_k, 1, 2), jnp.moveaxis(win_v, 1, 2), new_k[:, :, 2:3], new_v[:, :, 2:3], bw)
    idx_flat = idx[:, :, 0, :N_SEL].reshape(-1)
    val_flat = val[:, :, 0, :N_SEL].reshape(-1)
    bsel = brev.reshape(G, HPG, nc, CMP_BLK).transpose(0, 2, 1, 3)

    def blk(g):
        return lambda b, n, ix, vl, pt: jnp.minimum(ix[(b * G + g) * N_SEL + n], nc - 1)

    def kv_spec(g, lanes):
        half = PAGE_SIZE // CMP_BLK
        return pl.BlockSpec((None, CMP_BLK, lanes),
                            lambda b, n, ix, vl, pt: (pt[b, blk(g)(b, n, ix, vl, pt) // half],
                                                       blk(g)(b, n, ix, vl, pt) % half, 0))

    def v_spec(g):
        half = PAGE_SIZE // CMP_BLK
        return pl.BlockSpec((None, None, CMP_BLK, None, G, A_DV),
                            lambda b, n, ix, vl, pt: (layer, pt[b, blk(g)(b, n, ix, vl, pt) // half],
                                                       blk(g)(b, n, ix, vl, pt) % half, 1, 0, 0))

    def bias_spec(g):
        return pl.BlockSpec((None, 1, HPG, CMP_BLK), lambda b, n, ix, vl, pt: (g, blk(g)(b, n, ix, vl, pt), 0, 0))
    whole = lambda *shape: pl.BlockSpec((1,) + shape, lambda b, n, ix, vl, pt: (b,) + (0,) * len(shape))
    o_s = pl.pallas_call(
        functools.partial(_nsa_sample_b_body, nc),
        grid_spec=pltpu.PrefetchScalarGridSpec(
            num_scalar_prefetch=3, grid=(B, N_SEL),
            in_specs=[whole(G, HPG, DK), kv_spec(0, 2 * G * DK), kv_spec(1, 2 * G * DK),
                      v_spec(0), v_spec(1), bias_spec(0), bias_spec(1),
                      whole(G, 1, DK), whole(G, 1, A_DV),
                      pl.BlockSpec((G, HPG, 1), lambda b, n, ix, vl, pt: (0, 0, 0))],
            out_specs=whole(G, HPG, A_DV),
            scratch_shapes=[pltpu.VMEM((G, HPG, LANES), F32), pltpu.VMEM((G, HPG, LANES), F32),
                            pltpu.VMEM((G, HPG, A_DV), F32)]),
        out_shape=jax.ShapeDtypeStruct((B, G, HPG, A_DV), F32),
        compiler_params=pltpu.CompilerParams(dimension_semantics=("parallel", "arbitrary"),
                                             vmem_limit_bytes=VMEM_LIMIT),
        name="nsa_sample_selected",
    )(idx_flat, val_flat, page_table, qh, ck, ck, cache_v, cache_v, bsel, bsel, new_k[:, :, 1:2], new_v[:, :, 1:2],
      b0.reshape(G, HPG, 1))
    shp = (B, S, G, HPG, A_DV)
    return nsa_combine(gates, o_c.reshape(shp), o_s.reshape(shp), o_w.reshape(shp))


def _dsa_scores_body(pt_ref, *refs):
    pages = refs[:PAGES_PER_STEP]
    qi_ref, w_ref, o_ref = refs[PAGES_PER_STEP:]
    qi = qi_ref[0].astype(BF16)
    w = w_ref[0] * (IDX_HEADS ** -0.5)
    for r in range(PAGES_PER_STEP):
        d = lax.dot_general(qi, pages[r][...].astype(BF16), _NT, preferred_element_type=F32) * (IDX_DIM ** -0.5)
        o_ref[0, :, r * PAGE_SIZE:(r + 1) * PAGE_SIZE] = jnp.sum(jnp.maximum(d, 0.0) * w, axis=0, keepdims=True)


def _dsa_sample_select_body(topk, s_ref, qi_ref, w_ref, kn_ref, m_ref):
    B, P = s_ref.shape
    w = w_ref[...] * (IDX_HEADS ** -0.5)
    dn = jnp.sum(qi_ref[...] * kn_ref[...], axis=2) * (IDX_DIM ** -0.5)
    s_new = jnp.sum(jnp.maximum(dn, 0.0) * w, axis=1, keepdims=True)

    def order_key(x):
        bits = pltpu.bitcast(x, jnp.int32)
        return jnp.where(bits < 0, bits ^ jnp.int32(0x7FFFFFFF), bits)
    key = order_key(s_ref[...])
    key_new = order_key(jnp.broadcast_to(s_new, (B, LANES)))[:, :1]
    pos = lax.broadcasted_iota(jnp.int32, (B, P), 1)

    def count(cond, cond_new):
        return _count(cond) + jnp.where(cond_new, 1.0, 0.0)

    def value_step(i, lo):
        cand = lo + jnp.left_shift(jnp.int32(1), 31 - i)
        return jnp.where(count(key >= cand, key_new >= cand) >= float(topk), cand, lo)
    thr = lax.fori_loop(0, 32, value_step, jnp.full((B, 1), jnp.int32(-2 ** 31), jnp.int32))
    need = float(topk) - count(key > thr, key_new > thr)
    tied = key == thr
    n_bits = max(1, (P - 1).bit_length())

    def index_step(i, m):
        cand = m + jnp.left_shift(jnp.int32(1), n_bits - 1 - i)
        return jnp.where(_count(tied & (pos < cand)) < need, cand, m)
    last = lax.fori_loop(0, n_bits, index_step, jnp.zeros((B, 1), jnp.int32))
    sel = (key > thr) | (tied & (pos <= last))
    taken = _count(sel)
    sel_new = (key_new > thr) | ((key_new == thr) & (taken < float(topk)))
    m_ref[:, :P] = jnp.where(sel, 1.0, 0.0)
    lane = lax.broadcasted_iota(jnp.int32, (B, LANES), 1)
    m_ref[:, P:] = jnp.where((lane == 0) & sel_new, 1.0, 0.0)


def _dsa_sample_attend_body(pt_ref, *refs):
    n = PAGES_PER_STEP
    kp, vp = refs[:n], refs[n:2 * n]
    q_ref, mask_ref, bias_ref, mnew_ref, kn_ref, vn_ref, b0_ref, o_ref, m_ref, l_ref, acc_ref = refs[2 * n:]
    c = pl.program_id(1)
    scale = B_DH ** -0.5

    @pl.when(c == 0)
    def _():
        m_ref[...] = jnp.full(m_ref.shape, NEG, F32)
        l_ref[...] = jnp.zeros(l_ref.shape, F32)
        acc_ref[...] = jnp.zeros(acc_ref.shape, F32)

    mask = mask_ref[0] > 0.5
    for g in range(B_GROUPS):
        k = jnp.concatenate([kp[r][:, g, :] for r in range(n)], axis=0).astype(BF16)
        v = jnp.concatenate([vp[r][:, g, :] for r in range(n)], axis=0).astype(BF16)
        s = lax.dot_general(q_ref[0, g].astype(BF16), k, _NT, preferred_element_type=F32) * scale + bias_ref[g]
        s = jnp.where(mask, s, NEG)
        m_prev = m_ref[g][:, :1]
        m_new = jnp.maximum(m_prev, jnp.max(s, axis=1, keepdims=True))
        p = jnp.where(mask, jnp.exp(s - m_new), 0.0)
        alpha = jnp.exp(m_prev - m_new)
        l_ref[g] = jnp.broadcast_to(alpha * l_ref[g][:, :1] + jnp.sum(p, axis=1, keepdims=True), l_ref.shape[1:])
        acc_ref[g] = alpha * acc_ref[g] + jnp.dot(p.astype(BF16), v, preferred_element_type=F32)
        m_ref[g] = jnp.broadcast_to(m_new, m_ref.shape[1:])

    @pl.when(c == pl.num_programs(1) - 1)
    def _():
        new_on = mnew_ref[0][:, :1] > 0.5
        for g in range(B_GROUPS):
            sn = jnp.sum(q_ref[0, g] * kn_ref[0, g], axis=1, keepdims=True) * scale + b0_ref[g]
            sn = jnp.where(new_on, sn, NEG)
            m_prev = m_ref[g][:, :1]
            m_new = jnp.maximum(m_prev, sn)
            alpha = jnp.exp(m_prev - m_new)
            en = jnp.where(new_on, jnp.exp(sn - m_new), 0.0)
            l = alpha * l_ref[g][:, :1] + en
            o_ref[0, g] = (alpha * acc_ref[g] + en * vn_ref[0, g]) / l


def dsa_decode(q, k, v, qi, ki, wi, cache_k, cache_v, cache_kidx, layer, page_table, tab_b):
    B, S, G, HPG, DH = q.shape
    n_pages = page_table.shape[1]
    P = n_pages * PAGE_SIZE
    n_chunks = n_pages // PAGES_PER_STEP
    chunk = PAGES_PER_STEP * PAGE_SIZE
    assert S == 1 and n_pages % PAGES_PER_STEP == 0
    topk = min(IDX_TOPK, (P + S) // 4)
    params = pltpu.CompilerParams(dimension_semantics=("parallel", "arbitrary"), vmem_limit_bytes=VMEM_LIMIT)

    def pool_specs(*page_shape):
        return [pl.BlockSpec((None, None) + page_shape,
                             lambda b, c, pt, r=r: (layer, pt[b, c * PAGES_PER_STEP + r]) + (0,) * len(page_shape))
                for r in range(PAGES_PER_STEP)]
    scores = pl.pallas_call(
        _dsa_scores_body,
        grid_spec=pltpu.PrefetchScalarGridSpec(
            num_scalar_prefetch=1, grid=(B, n_chunks),
            in_specs=pool_specs(PAGE_SIZE, IDX_DIM)
            + [pl.BlockSpec((1, IDX_HEADS, IDX_DIM), lambda b, c, pt: (b, 0, 0)),
               pl.BlockSpec((1, IDX_HEADS, 1), lambda b, c, pt: (b, 0, 0))],
            out_specs=pl.BlockSpec((1, 1, chunk), lambda b, c, pt: (b, 0, c))),
        out_shape=jax.ShapeDtypeStruct((B, 1, P), F32),
        compiler_params=params, name="dsa_sample_scores",
    )(page_table, *([cache_kidx] * PAGES_PER_STEP), qi[:, 0], wi[:, 0, :, None])
    sel = pl.pallas_call(
        functools.partial(_dsa_sample_select_body, topk),
        out_shape=jax.ShapeDtypeStruct((B, P + LANES), F32),
        compiler_params=pltpu.CompilerParams(vmem_limit_bytes=VMEM_LIMIT), name="dsa_sample_select",
    )(scores.reshape(B, P), qi[:, 0], wi[:, 0], ki)
    sel = sel.reshape(B, 1, P + LANES)
    brev, b0 = bias_by_position(tab_b, P)
    whole = lambda *shape: pl.BlockSpec((1,) + shape, lambda b, c, pt: (b,) + (0,) * len(shape))
    o = pl.pallas_call(
        _dsa_sample_attend_body,
        grid_spec=pltpu.PrefetchScalarGridSpec(
            num_scalar_prefetch=1, grid=(B, n_chunks),
            in_specs=pool_specs(PAGE_SIZE, G, DH) + pool_specs(PAGE_SIZE, G, DH)
            + [whole(G, HPG, DH),
               pl.BlockSpec((1, 1, chunk), lambda b, c, pt: (b, 0, c)),
               pl.BlockSpec((G, HPG, chunk), lambda b, c, pt: (0, 0, c)),
               pl.BlockSpec((1, 1, LANES), lambda b, c, pt: (b, 0, P // LANES)),
               whole(G, 1, DH), whole(G, 1, DH),
               pl.BlockSpec((G, HPG, 1), lambda b, c, pt: (0, 0, 0))],
            out_specs=whole(G, HPG, DH),
            scratch_shapes=[pltpu.VMEM((G, HPG, LANES), F32), pltpu.VMEM((G, HPG, LANES), F32),
                            pltpu.VMEM((G, HPG, DH), F32)]),
        out_shape=jax.ShapeDtypeStruct((B, G, HPG, DH), F32),
        compiler_params=params, name="dsa_sample_attend",
    )(page_table, *([cache_k] * PAGES_PER_STEP), *([cache_v] * PAGES_PER_STEP), q[:, 0], sel,
      brev.reshape(G, HPG, P), sel,
      jnp.moveaxis(k, 1, 2), jnp.moveaxis(v, 1, 2), b0.reshape(G, HPG, 1))
    return o.reshape(B, S, G * HPG * DH)


def _moba_kmean_body(pt_ref, p0_ref, p1_ref, o_ref):
    o_ref[0] = (jnp.sum(p0_ref[...], axis=0) + jnp.sum(p1_ref[...], axis=0)) * (1.0 / MOBA_BLK)


def _moba_sample_pick_body(nf, q_ref, km_ref, idx_ref, val_ref):
    B, H, DH = q_ref.shape
    nfp = km_ref.shape[1]
    group = lax.broadcasted_iota(jnp.int32, (H, nfp), 0) // C_HPG
    lane = lax.broadcasted_iota(jnp.int32, (H, nfp), 1)
    for b in range(B):
        qb = q_ref[b].astype(BF16)
        s = jnp.full((H, nfp), REMOVED, F32)
        for g in range(C_GROUPS):
            kg = km_ref[b, :, g, :].astype(BF16)
            sg = lax.dot_general(qb, kg, _NT, preferred_element_type=F32)
            s = jnp.where((group == g) & (lane < nf), sg, s)
        idx, val = _topk_rows(s, MOBA_TOP)
        idx_ref[b] = idx.astype(jnp.int32)
        val_ref[b] = val.astype(jnp.int32)


def _moba_sample_attend_body(idx_ref, val_ref, pt_ref, q_ref, k0_ref, k1_ref, v0_ref, v1_ref, bias_ref,
                             kn_ref, vn_ref, b0_ref, o_ref, m_ref, l_ref, acc_ref):
    b, h, r = pl.program_id(0), pl.program_id(1), pl.program_id(2)
    scale = C_DH ** -0.5
    q = q_ref[0, 0]
    group = h // C_HPG

    @pl.when(r == 0)
    def _():
        m_ref[...] = jnp.full(m_ref.shape, NEG, F32)
        l_ref[...] = jnp.zeros(l_ref.shape, F32)
        acc_ref[...] = jnp.zeros(acc_ref.shape, F32)

    def attend(g):
        k = jnp.concatenate([k0_ref[:, g, :], k1_ref[:, g, :]], axis=0).astype(BF16)
        v = jnp.concatenate([v0_ref[:, g, :], v1_ref[:, g, :]], axis=0).astype(BF16)
        s = lax.dot_general(q.astype(BF16), k, _NT, preferred_element_type=F32) * scale + bias_ref[0]
        m_prev = m_ref[:, :1]
        m_new = jnp.maximum(m_prev, jnp.max(s, axis=1, keepdims=True))
        p = jnp.exp(s - m_new)
        alpha = jnp.exp(m_prev - m_new)
        l_ref[...] = jnp.broadcast_to(alpha * l_ref[:, :1] + jnp.sum(p, axis=1, keepdims=True), l_ref.shape)
        acc_ref[...] = alpha * acc_ref[...] + jnp.dot(p.astype(BF16), v, preferred_element_type=F32)
        m_ref[...] = jnp.broadcast_to(m_new, m_ref.shape)

    chosen = val_ref[(b * C_HEADS + h) * MOBA_TOP + r] > 0
    for g in range(C_GROUPS):
        pl.when(chosen & (group == g))(functools.partial(attend, g))

    @pl.when(r == MOBA_TOP - 1)
    def _():
        sn = jnp.sum(q * kn_ref[0, 0], axis=1, keepdims=True) * scale + b0_ref[0]
        m_prev = m_ref[:, :1]
        m_new = jnp.maximum(m_prev, sn)
        alpha = jnp.exp(m_prev - m_new)
        en = jnp.exp(sn - m_new)
        o_ref[0, 0] = (alpha * acc_ref[...] + en * vn_ref[0, 0]) / (alpha * l_ref[:, :1] + en)


def moba_decode(q, k, v, cache_k, cache_v, layer, page_table, tab_c):
    B, S, H, DH = q.shape
    n_pages = page_table.shape[1]
    P = n_pages * PAGE_SIZE
    ppb = MOBA_BLK // PAGE_SIZE
    nf = (P + S) // MOBA_BLK
    assert S == 1 and ppb == 2 and P % MOBA_BLK == 0 and nf == P // MOBA_BLK and nf >= MOBA_TOP
    page_block = (None, None, PAGE_SIZE, C_GROUPS, DH)
    kmean = pl.pallas_call(
        _moba_kmean_body,
        grid_spec=pltpu.PrefetchScalarGridSpec(
            num_scalar_prefetch=1, grid=(B, nf),
            in_specs=[pl.BlockSpec(page_block, lambda b, j, pt: (layer, pt[b, ppb * j], 0, 0, 0)),
                      pl.BlockSpec(page_block, lambda b, j, pt: (layer, pt[b, ppb * j + 1], 0, 0, 0))],
            out_specs=pl.BlockSpec((1, C_GROUPS, DH), lambda b, j, pt: (b * nf + j, 0, 0))),
        out_shape=jax.ShapeDtypeStruct((B * nf, C_GROUPS, DH), F32),
        compiler_params=pltpu.CompilerParams(dimension_semantics=("parallel", "parallel"),
                                             vmem_limit_bytes=VMEM_LIMIT),
        name="moba_sample_kmean",
    )(page_table, cache_k, cache_k).reshape(B, nf, C_GROUPS, DH)
    idx, val = pl.pallas_call(
        functools.partial(_moba_sample_pick_body, nf),
        out_shape=[jax.ShapeDtypeStruct((B, H, LANES), jnp.int32)] * 2,
        compiler_params=pltpu.CompilerParams(vmem_limit_bytes=VMEM_LIMIT), name="moba_sample_pick",
    )(q[:, 0], jnp.pad(kmean, ((0, 0), (0, -nf % LANES), (0, 0), (0, 0))))
    idx_flat = idx[:, :, :MOBA_TOP].reshape(-1)
    val_flat = val[:, :, :MOBA_TOP].reshape(-1)
    brev, b0 = bias_by_position(tab_c, P)

    def blk(b, h, r, ix):
        return jnp.minimum(ix[(b * H + h) * MOBA_TOP + r], nf - 1)

    def page_spec(which):
        return pl.BlockSpec(page_block,
                            lambda b, h, r, ix, vl, pt: (layer, pt[b, ppb * blk(b, h, r, ix) + which], 0, 0, 0))
    per_head = lambda rows, d: pl.BlockSpec((1, 1, rows, d), lambda b, h, r, ix, vl, pt: (b, h, 0, 0))
    per_group = lambda d: pl.BlockSpec((1, 1, 1, d), lambda b, h, r, ix, vl, pt: (b, h // C_HPG, 0, 0))
    o = pl.pallas_call(
        _moba_sample_attend_body,
        grid_spec=pltpu.PrefetchScalarGridSpec(
            num_scalar_prefetch=3, grid=(B, H, MOBA_TOP),
            in_specs=[per_head(1, DH), page_spec(0), page_spec(1), page_spec(0), page_spec(1),
                      pl.BlockSpec((1, 1, MOBA_BLK), lambda b, h, r, ix, vl, pt: (h * nf + blk(b, h, r, ix), 0, 0)),
                      per_group(DH), per_group(DH),
                      pl.BlockSpec((1, 1, 1), lambda b, h, r, ix, vl, pt: (h, 0, 0))],
            out_specs=per_head(1, DH),
            scratch_shapes=[pltpu.VMEM((1, LANES), F32), pltpu.VMEM((1, LANES), F32), pltpu.VMEM((1, DH), F32)]),
        out_shape=jax.ShapeDtypeStruct((B, H, 1, DH), F32),
        compiler_params=pltpu.CompilerParams(dimension_semantics=("parallel", "parallel", "arbitrary"),
                                             vmem_limit_bytes=VMEM_LIMIT),
        name="moba_sample_attend",
    )(idx_flat, val_flat, page_table, q[:, 0, :, None, :], cache_k, cache_k, cache_v, cache_v,
      brev.reshape(H * nf, 1, MOBA_BLK),
      jnp.moveaxis(k, 1, 2), jnp.moveaxis(v, 1, 2), b0.reshape(H, 1, 1))
    return o.reshape(B, S, H * DH)


def split_cols(z, sizes):
    cuts = [int(c) for c in np.cumsum(sizes)[:-1]]
    return jnp.split(z, cuts, axis=-1)


def t5_bucket(dist):
    n = jnp.maximum(dist, 0)
    nf = jnp.maximum(n, 1).astype(jnp.float32)
    large = REL_EXACT + (jnp.log(nf / REL_EXACT) / math.log(REL_MAX_DIST / REL_EXACT)
                         * (N_BUCKETS - REL_EXACT)).astype(jnp.int32)
    return jnp.where(n < REL_EXACT, n, jnp.minimum(large, N_BUCKETS - 1))


def nsa_compress(rows, pe, w1, w2):
    B, L, G, d = rows.shape
    nc = L // CMP_BLK
    blk = rows[:, : nc * CMP_BLK].reshape(B, nc, CMP_BLK, G, d) + pe[None, None, :, None, :]
    flat = blk.transpose(0, 1, 3, 2, 4).reshape(B, nc, G, CMP_BLK * d)
    return matmul(jax.nn.silu(matmul(flat, w1)), w2)


def nsa_combine(gates, o_c, o_s, o_w):
    g = gates[..., None].astype(o_c.dtype)
    o = g[:, :, 0] * o_c + g[:, :, 1] * o_s + g[:, :, 2] * o_w
    return o.reshape(o.shape[0], o.shape[1], -1)


def _heads_first(a):
    return jnp.moveaxis(a, 1, -2)


def nsa_prompt(q, k3, v3, gates, pe_k, pe_v, wk1, wk2, wv1, wv2, tab_a):
    B, T, G, HPG, DK = q.shape
    assert T % TQ == 0 and TQ == TK and (T // CMP_BLK) * CMP_BLK == T
    kc = nsa_compress(k3[:, :, 0], pe_k, wk1, wk2)
    vc = nsa_compress(v3[:, :, 0], pe_v, wv1, wv2)
    qh = _heads_first(q)
    o_c, sel_mask = nsa_cmp_select(qh, _heads_first(kc), _heads_first(vc))
    tiles = toeplitz_bias_tiles(tab_a)
    scale = A_DK ** -0.5
    o_s = block_attention('key', qh, _heads_first(k3[:, :, 1]), _heads_first(v3[:, :, 1]), tiles, sel_mask, scale)
    o_w = block_attention('window', qh, _heads_first(k3[:, :, 2]), _heads_first(v3[:, :, 2]), tiles, None, scale)
    shp = (B, T, G, HPG, A_DV)
    return nsa_combine(gates, o_c.reshape(shp), o_s.reshape(shp), o_w.reshape(shp))


def dsa_prompt(q, k, v, qi, ki, wi, tab_b):
    B, T = q.shape[:2]
    assert T % TQ == 0 and TQ == TK
    sel_mask = dsa_select(_heads_first(qi), ki, wi, min(IDX_TOPK, T // 4))
    return block_attention('key', _heads_first(q), _heads_first(k), _heads_first(v),
                           toeplitz_bias_tiles(tab_b), sel_mask, B_DH ** -0.5)


def moba_prompt(q, k, v, tab_c):
    B, T, H, DH = q.shape
    assert T % MOBA_BLK == 0 and TQ == MOBA_BLK and TK == MOBA_BLK
    qh = _heads_first(q.reshape(B, T, C_GROUPS, C_HPG, DH))
    kh, vh = _heads_first(k), _heads_first(v)
    flags = moba_select(qh, kh).transpose(0, 1, 3, 2)
    return block_attention('moba', qh, kh, vh, toeplitz_bias_tiles(tab_c), flags, DH ** -0.5)


def mem_kv(mem, g, wk, wv, layer):
    m = rms_norm(mem, g, BF16)
    B = mem.shape[0]
    return (matmul(m, wk, layer=layer).reshape(B, N_MEM, X_HEADS, X_DH),
            matmul(m, wv, layer=layer).reshape(B, N_MEM, X_HEADS, X_DH))


def cross_attend(x, h, mk, mv, wq, wo, layer):
    B, T = h.shape[:2]
    q = matmul(h, wq, layer=layer).reshape(B, T, X_HEADS, X_DH)
    logits = jnp.einsum('bthd,bmhd->bthm', q, mk, preferred_element_type=jnp.float32) * X_DH ** -0.5
    p = jax.nn.softmax(logits, axis=-1)
    o = jnp.einsum('bthm,bmhd->bthd', p.astype(mv.dtype), mv).reshape(B, T, X_HEADS * X_DH)
    return matmul(o, wo, resid=x, layer=layer)


def dense_swiglu(x, h, w1, w3, w2, layer):
    rows = h.shape[0]
    tm = _row_tile(rows)
    tables = _dense_tables(rows, tm)
    g = swiglu_up(h, w1[:, None], w3[:, None], tables, tm, layer)
    return grouped_matmul(g, [w2[:, None]], tables, tm, 1024, 512, F32, resid=x, layer=layer)


MOE_TM = 2304


def moe_tables(top_e):
    A = top_e.size
    n_tiles = -(-(A + N_EXPERTS * (MOE_TM - 1)) // MOE_TM)
    e_flat = top_e.reshape(A).astype(jnp.int32)
    order = jnp.argsort(e_flat).astype(jnp.int32)
    counts = jnp.bincount(e_flat, length=N_EXPERTS).astype(jnp.int32)
    starts = jnp.cumsum(counts) - counts
    tiles_per = (counts + MOE_TM - 1) // MOE_TM
    tile_end = jnp.cumsum(tiles_per)
    tile_start = tile_end - tiles_per
    e_sorted = e_flat[order]
    prow_sorted = tile_start[e_sorted] * MOE_TM + (jnp.arange(A, dtype=jnp.int32) - starts[e_sorted])
    row_token = (jnp.arange(n_tiles * MOE_TM, dtype=jnp.int32) % (A // TOP_K)).at[prow_sorted].set(order // TOP_K)
    prow_of_assign = jnp.zeros((A,), jnp.int32).at[order].set(prow_sorted)
    ti = jnp.arange(n_tiles, dtype=jnp.int32)
    n_active = tile_end[-1]
    last = n_active - 1
    src = jnp.minimum(ti, last)
    te = jnp.minimum(jnp.searchsorted(tile_end, src, side='right').astype(jnp.int32), N_EXPERTS - 1)
    rows_in = jnp.clip(counts[te] - (src - tile_start[te]) * MOE_TM, 0, MOE_TM)
    nsb = jnp.where(ti < n_active, (rows_in + SUB_ROWS - 1) // SUB_ROWS, 0).astype(jnp.int32)
    return (te, nsb, src), row_token, prow_of_assign


def moe_swiglu(xs, hs, w_router, b_router, w1, w3, w2, layer):
    wr = jnp.pad(w_router, ((0, 0), (0, LANES - N_EXPERTS)))
    logits = jnp.concatenate([matmul(h, wr)[:, :N_EXPERTS] for h in hs], axis=0) + b_router.astype(jnp.float32)
    h = jnp.concatenate(hs, axis=0)
    N = h.shape[0]
    top_l, top_e = lax.top_k(logits, TOP_K)
    gate = jax.nn.softmax(top_l, axis=-1)
    tables, row_token, prow_of_assign = moe_tables(top_e)
    g = swiglu_up(h[row_token], w1, w3, tables, MOE_TM, layer)
    y = grouped_matmul(g, [w2], tables, MOE_TM, 1024, 512, F32, layer=layer)
    slots = prow_of_assign.reshape(N, TOP_K)
    mix = sum(y[slots[:, c]] * gate[:, c:c + 1] for c in range(TOP_K))
    outs, lo = [], 0
    for xg in xs:
        outs.append(xg + mix[lo:lo + xg.shape[0]])
        lo += xg.shape[0]
    return outs


def even_split(z):
    B, T = z.shape[:2]
    qa, ka, va, ga, qb, kb, vb, qi, ki, wi = split_cols(z, EVEN_SPLITS)
    return (qa.reshape(B, T, A_GROUPS, A_HPG, A_DK),
            ka.reshape(B, T, 3, A_GROUPS, A_DK),
            va.reshape(B, T, 3, A_GROUPS, A_DV),
            jax.nn.sigmoid(ga.astype(jnp.float32)).reshape(B, T, 3, A_GROUPS, A_HPG),
            qb.reshape(B, T, B_GROUPS, B_HPG, B_DH),
            kb.reshape(B, T, B_GROUPS, B_DH),
            vb.reshape(B, T, B_GROUPS, B_DH),
            qi.reshape(B, T, IDX_HEADS, IDX_DIM), ki, wi)


def odd_split(z):
    B, T = z.shape[:2]
    q, k, v = split_cols(z, ODD_SPLITS)
    return (q.reshape(B, T, C_HEADS, C_DH), k.reshape(B, T, C_GROUPS, C_DH), v.reshape(B, T, C_GROUPS, C_DH))


def kernel(x_prompt, x_sample, mem_prompt, cache_a_k, cache_a_v, state_a_win_k, state_a_win_v,
           cache_b_k, cache_b_v, cache_b_kidx, cache_c_k, cache_c_v, cache_mem_k, cache_mem_v, page_table,
           rel_bias, norm_mix, norm_mem, norm_cross, norm_ffn, norm_final,
           w_cross_q, w_cross_k, w_cross_v, w_cross_o, w_in_even, w_out_even,
           nsa_pe_k, nsa_pe_v, nsa_phi_k1, nsa_phi_k2, nsa_phi_v1, nsa_phi_v2,
           w_ffn1, w_ffn3, w_ffn2, w_in_odd, w_out_odd, w_router, b_router, w_exp1, w_exp3, w_exp2):
    xp, xs = x_prompt, x_sample
    B, T, D = xp.shape
    Bs, S = xs.shape[:2]
    tab_a = rel_bias[:, :A_HEADS]
    tab_b = rel_bias[:, A_HEADS:A_HEADS + B_HEADS]
    tab_c = rel_bias[:, :C_HEADS]
    names = ('a_k_p', 'a_v_p', 'aw_k_p', 'aw_v_p', 'b_k_p', 'b_v_p', 'b_i_p', 'c_k_p', 'c_v_p', 'm_k_p', 'm_v_p',
             'a_k_s', 'a_v_s', 'aw_k_s', 'aw_v_s', 'b_k_s', 'b_v_s', 'b_i_s', 'c_k_s', 'c_v_s')
    new = {n: [] for n in names}
    for layer in range(DEPTH):
        li = layer // 2
        hp = rms_norm(xp, norm_mix[layer], BF16)
        hs = rms_norm(xs, norm_mix[layer], BF16)
        if layer % 2 == 0:
            phi = (nsa_pe_k[li], nsa_pe_v[li], nsa_phi_k1[li], nsa_phi_k2[li], nsa_phi_v1[li], nsa_phi_v2[li])
            qa, ka, va, ga, qb, kb, vb, qi, ki, wi = even_split(matmul(hp, w_in_even, layer=li))
            o_a = nsa_prompt(qa, ka, va, ga, *phi, tab_a)
            o_b = dsa_prompt(qb, kb, vb, qi, ki, wi, tab_b)
            xp = matmul(jnp.concatenate([o_a, o_b], axis=-1), w_out_even, resid=xp, layer=li)
            wk = min(WINDOW, T)
            new['a_k_p'].append(ka[:, :, :2]); new['a_v_p'].append(va[:, :, :2])
            new['aw_k_p'].append(ka[:, T - wk:, 2]); new['aw_v_p'].append(va[:, T - wk:, 2])
            new['b_k_p'].append(kb); new['b_v_p'].append(vb); new['b_i_p'].append(ki)
            qa, ka, va, ga, qb, kb, vb, qi, ki, wi = even_split(matmul(hs, w_in_even, layer=li))
            o_a = nsa_decode(qa, ka, va, ga, cache_a_k[li], cache_a_v, li, state_a_win_k[li], state_a_win_v[li],
                             page_table, *phi, tab_a)
            o_b = dsa_decode(qb, kb, vb, qi, ki, wi, cache_b_k, cache_b_v, cache_b_kidx, li, page_table, tab_b)
            xs = matmul(jnp.concatenate([o_a, o_b], axis=-1), w_out_even, resid=xs, layer=li)
            new['a_k_s'].append(ka[:, :, :2]); new['a_v_s'].append(va[:, :, :2])
            new['aw_k_s'].append(ka[:, :, 2]); new['aw_v_s'].append(va[:, :, 2])
            new['b_k_s'].append(kb); new['b_v_s'].append(vb); new['b_i_s'].append(ki)
        else:
            q, k, v = odd_split(matmul(hp, w_in_odd, layer=li))
            xp = matmul(moba_prompt(q, k, v, tab_c), w_out_odd, resid=xp, layer=li)
            new['c_k_p'].append(k); new['c_v_p'].append(v)
            q, k, v = odd_split(matmul(hs, w_in_odd, layer=li))
            xs = matmul(moba_decode(q, k, v, cache_c_k, cache_c_v, li, page_table, tab_c), w_out_odd,
                        resid=xs, layer=li)
            new['c_k_s'].append(k); new['c_v_s'].append(v)
        mk, mv = mem_kv(mem_prompt, norm_mem[layer], w_cross_k, w_cross_v, layer)
        new['m_k_p'].append(mk); new['m_v_p'].append(mv)
        xp = cross_attend(xp, rms_norm(xp, norm_cross[layer], BF16), mk, mv, w_cross_q, w_cross_o, layer)
        xs = cross_attend(xs, rms_norm(xs, norm_cross[layer], BF16), cache_mem_k[layer], cache_mem_v[layer],
                          w_cross_q, w_cross_o, layer)
        hp = rms_norm(xp, norm_ffn[layer], BF16).reshape(B * T, D)
        hs = rms_norm(xs, norm_ffn[layer], BF16).reshape(Bs * S, D)
        xp2, xs2 = xp.reshape(B * T, D), xs.reshape(Bs * S, D)
        if layer % 2 == 0:
            xp2 = dense_swiglu(xp2, hp, w_ffn1, w_ffn3, w_ffn2, li)
            xs2 = dense_swiglu(xs2, hs, w_ffn1, w_ffn3, w_ffn2, li)
        else:
            xp2, xs2 = moe_swiglu([xp2, xs2], [hp, hs], w_router[li], b_router[li], w_exp1, w_exp3, w_exp2, li)
        xp = xp2.reshape(B, T, D)
        xs = xs2.reshape(Bs, S, D)
    y_prompt = rms_norm(xp, norm_final)
    y_sample = rms_norm(xs, norm_final)
    return (y_prompt, y_sample,
            jnp.stack(new['a_k_p']), jnp.stack(new['a_v_p']), jnp.stack(new['aw_k_p']), jnp.stack(new['aw_v_p']),
            jnp.stack(new['b_k_p']), jnp.stack(new['b_v_p']), jnp.stack(new['b_i_p']),
            jnp.stack(new['c_k_p']), jnp.stack(new['c_v_p']), jnp.stack(new['m_k_p']), jnp.stack(new['m_v_p']),
            jnp.stack(new['a_k_s']), jnp.stack(new['a_v_s']), jnp.stack(new['aw_k_s']), jnp.stack(new['aw_v_s']),
            jnp.stack(new['b_k_s']), jnp.stack(new['b_v_s']), jnp.stack(new['b_i_s']),
            jnp.stack(new['c_k_s']), jnp.stack(new['c_v_s']))
```

```python
import functools
import math
import jax, jax.numpy as jnp
from jax import lax
import numpy as np
from jax.experimental import pallas as pl
from jax.experimental.pallas import tpu as pltpu

D_MODEL = 4096
BATCH = 4
SEQ = 2048
DEPTH = 2
DEC_BATCH = 8
DEC_SEQ = 1
PAST_LEN = 16384
PAGE_SIZE = 128

N_EVEN = (DEPTH + 1) // 2
N_ODD = DEPTH // 2
HEAD_SLOTS = 32
A_HEADS = 16
A_GROUPS = 2
A_HPG = A_HEADS // A_GROUPS
A_DK = 192
A_DV = 128
CMP_BLK = 64
N_SEL = 16
WINDOW = 512
B_HEADS = 16
B_GROUPS = 2
B_HPG = B_HEADS // B_GROUPS
B_DH = 128
IDX_HEADS = 8
IDX_DIM = 64
IDX_TOPK = 256
C_HEADS = 32
C_GROUPS = 8
C_HPG = C_HEADS // C_GROUPS
C_DH = 128
MOBA_BLK = 256
MOBA_TOP = 3
MOBA_QCHUNK = 32
N_MEM = 256
X_HEADS = 4
X_DH = 128
D_FF = 14336
N_EXPERTS = 8
TOP_K = 2
MOE_MAX_ROWS = 512
N_BUCKETS = 32
REL_EXACT = 16
REL_MAX_DIST = 1024
Q_BLOCK = 128
EPS = 1e-6
NEG = -1e30
FORCE = 1e9
EVEN_SPLITS = (A_HEADS * A_DK, 3 * A_GROUPS * A_DK, 3 * A_GROUPS * A_DV, 3 * A_HEADS,
               B_HEADS * B_DH, B_GROUPS * B_DH, B_GROUPS * B_DH, IDX_HEADS * IDX_DIM, IDX_DIM, IDX_HEADS)
ODD_SPLITS = (C_HEADS * C_DH, C_GROUPS * C_DH, C_GROUPS * C_DH)


def _rmsnorm_body(x_ref, g_ref, o_ref):
    x = x_ref[...]
    y = x * lax.rsqrt(jnp.mean(x * x, axis=-1, keepdims=True) + EPS)
    o_ref[...] = (y * g_ref[...]).astype(o_ref.dtype)


def rms_norm(x, g, out_dtype=None):
    out_dtype = out_dtype or x.dtype
    shape = x.shape
    d = shape[-1]
    x2 = x.reshape(-1, d)
    rows = x2.shape[0]
    tr = min(rows, 256)
    out = pl.pallas_call(
        _rmsnorm_body,
        grid=(rows // tr,),
        in_specs=[pl.BlockSpec((tr, d), lambda i: (i, 0)),
                  pl.BlockSpec((1, d), lambda i: (0, 0))],
        out_specs=pl.BlockSpec((tr, d), lambda i: (i, 0)),
        out_shape=jax.ShapeDtypeStruct((rows, d), out_dtype),
    )(x2, g.reshape(1, d).astype(jnp.float32))
    return out.reshape(shape)


TQ = 256
TK = 256
N_OFF = -(-(REL_MAX_DIST + TK - 1) // TK) + 1
LANES = 128
VMEM_LIMIT = 48 * 1024 * 1024
_NT = (((1,), (1,)), ((), ()))
BF16 = jnp.bfloat16
F32 = jnp.float32


def bias_by_distance(tab, n_dist):
    return tab[t5_bucket(jnp.arange(n_dist))].T.astype(F32)


def toeplitz_bias_tiles(tab):
    H = tab.shape[1]
    bd = bias_by_distance(tab, N_OFF * TK + TQ)
    epad = jnp.concatenate([jnp.broadcast_to(bd[:, :1], (H, TK - 1)), bd], axis=1)
    w = TQ + TK - 1
    rows = []
    for o in range(N_OFF):
        erev = epad[:, o * TK: o * TK + w][:, ::-1]
        z = jnp.concatenate([erev, erev[:, :1]], axis=1)
        rows.append(jnp.roll(z, -(TQ - 1), axis=1))
    x = jnp.stack(rows, axis=1)
    y = jnp.tile(x, (1, 1, TQ))[:, :, : TQ * w].reshape(H, N_OFF, TQ, w)
    return y[..., :TK]


HEADS_PER_STEP = 4
PREFIX_STEP = 1


def _row_attn_body(mode, hs, dv, scale, n_tiles, *refs):
    if mode == 'window':
        q_ref, k_ref, v_ref, b_ref, o_ref = refs
        x_ref = None
    else:
        q_ref, k_ref, v_ref, b_ref, x_ref, o_ref = refs
    qi = pl.program_id(3)
    w = n_tiles * TK

    def heads(n, t0, k, v, mask_of):
        for h in range(hs):
            mask = mask_of(h, n)
            bias = jnp.concatenate([b_ref[h, jnp.clip(qi - (t0 + j), 0, N_OFF - 1)] for j in range(n)], axis=1)
            s = lax.dot_general(q_ref[0, 0, h].astype(BF16), k, _NT, preferred_element_type=F32) * scale + bias
            s = jnp.where(mask, s, NEG)
            p = jnp.where(mask, jnp.exp(s - jnp.max(s, axis=1, keepdims=True)), 0.0)
            l = jnp.sum(p, axis=1, keepdims=True)
            o = jnp.dot(p.astype(BF16), v, preferred_element_type=F32)
            o_ref[0, :, h * dv:(h + 1) * dv] = jnp.where(l > 0.0, o / jnp.where(l > 0.0, l, 1.0), 0.0)

    if mode == 'window':
        t0 = jnp.maximum(qi - (n_tiles - 1), 0)
        c0 = pl.multiple_of(t0 * TK, TK)
        k = k_ref[0, 0, pl.ds(c0, w), :].astype(BF16)
        v = v_ref[0, 0, pl.ds(c0, w), :].astype(BF16)
        dist = (qi * TQ + lax.broadcasted_iota(jnp.int32, (TQ, w), 0)) - (c0 + lax.broadcasted_iota(jnp.int32, (TQ, w), 1))
        shared_mask = (dist >= 0) & (dist < WINDOW)
        heads(n_tiles, t0, k, v, lambda h, n: shared_mask)
        return

    def causal_prefix(n):
        k = k_ref[0, 0, :n * TK, :].astype(BF16)
        v = v_ref[0, 0, :n * TK, :].astype(BF16)
        if mode == 'key':
            shared_mask = x_ref[0, 0, :, :n * TK] > 0
            heads(n, 0, k, v, lambda h, n: shared_mask)
        else:
            causal = jnp.where(lax.broadcasted_iota(jnp.int32, (TQ, TK), 0)
                               >= lax.broadcasted_iota(jnp.int32, (TQ, TK), 1), 1.0, 0.0)

            def moba_mask(h, n):
                flags = x_ref[0, h]
                return jnp.concatenate(
                    [jnp.where(qi == j, causal, jnp.broadcast_to(flags[:, j:j + 1], (TQ, TK))) for j in range(n)],
                    axis=1) > 0.5
            heads(n, 0, k, v, moba_mask)

    prefixes = sorted({min(n_tiles, p) for p in range(PREFIX_STEP, n_tiles + PREFIX_STEP, PREFIX_STEP)})
    lo = 0
    for n in prefixes:
        pl.when((qi + 1 > lo) & (qi + 1 <= n))(functools.partial(causal_prefix, n))
        lo = n


def block_attention(mode, q, k, v, bias_tiles, extra, scale):
    B, G, HPG, T, dk = q.shape
    dv = v.shape[-1]
    hs = min(HEADS_PER_STEP, HPG)
    nh = HPG // hs
    n_tiles = min(WINDOW // TK + 1, T // TK) if mode == 'window' else T // TK
    assert HPG % hs == 0 and T % TK == 0
    in_specs = [
        pl.BlockSpec((1, 1, hs, TQ, dk), lambda b, g, hh, qi: (b, g, hh, qi, 0)),
        pl.BlockSpec((1, 1, T, dk), lambda b, g, hh, qi: (b, g, 0, 0)),
        pl.BlockSpec((1, 1, T, dv), lambda b, g, hh, qi: (b, g, 0, 0)),
        pl.BlockSpec((hs, N_OFF, TQ, TK), lambda b, g, hh, qi: (g * nh + hh, 0, 0, 0)),
    ]
    args = [q, k, v, bias_tiles]
    if mode == 'key':
        gm = extra.shape[1]
        in_specs.append(pl.BlockSpec((1, 1, TQ, T), lambda b, g, hh, qi: (b, g if gm > 1 else 0, qi, 0)))
        args.append(extra)
    elif mode == 'moba':
        in_specs.append(pl.BlockSpec((1, hs, TQ, extra.shape[-1]), lambda b, g, hh, qi: (b, g * nh + hh, qi, 0)))
        args.append(extra)
    return pl.pallas_call(
        functools.partial(_row_attn_body, mode, hs, dv, scale, n_tiles),
        grid=(B, G, nh, T // TQ),
        in_specs=in_specs,
        out_specs=pl.BlockSpec((1, TQ, hs * dv), lambda b, g, hh, qi: (b, qi, g * nh + hh)),
        out_shape=jax.ShapeDtypeStruct((B, T, G * HPG * dv), F32),
        compiler_params=pltpu.CompilerParams(
            dimension_semantics=("parallel", "parallel", "parallel", "parallel"), vmem_limit_bytes=VMEM_LIMIT),
        name=f"attn_{mode}",
    )(*args)


def _nsa_cmp_body(q_ref, kc_ref, vc_ref, o_ref, mask_ref):
    qi = pl.program_id(2)
    nc = kc_ref.shape[2]
    T = mask_ref.shape[-1]
    kc = kc_ref[0, 0].astype(BF16)
    vc = vc_ref[0, 0].astype(BF16)
    t = qi * TQ + lax.broadcasted_iota(jnp.int32, (TQ, nc), 0)
    j = lax.broadcasted_iota(jnp.int32, (TQ, nc), 1)
    vis = (j + 1) * CMP_BLK <= t + 1
    imp = jnp.zeros((TQ, nc), F32)
    for h in range(A_HPG):
        q = q_ref[0, 0, h].astype(BF16)
        s = lax.dot_general(q, kc, _NT, preferred_element_type=F32) * (A_DK ** -0.5)
        s = jnp.where(vis, s, NEG)
        e = jnp.where(vis, jnp.exp(s - jnp.max(s, axis=1, keepdims=True)), 0.0)
        l = jnp.sum(e, axis=1, keepdims=True)
        p = jnp.where(l > 0.0, e / jnp.where(l > 0.0, l, 1.0), 0.0)
        o_ref[0, :, h * A_DV:(h + 1) * A_DV] = jnp.dot(p.astype(BF16), vc, preferred_element_type=F32)
        imp = imp + p
    own = t // CMP_BLK
    forced = (j == 0) | (j == own) | (j == own - 1)
    score = jnp.where(forced, FORCE, imp)
    score = jnp.where(j <= own, score, NEG)
    rank = jnp.zeros((TQ, nc), F32)
    for i in range(nc):
        si = score[:, i:i + 1]
        rank = rank + jnp.where((si > score) | ((si == score) & (i < j)), 1.0, 0.0)
    sel = jnp.where((rank < float(N_SEL)) & (j <= own), 1.0, 0.0).astype(BF16)
    expand = jnp.where(lax.broadcasted_iota(jnp.int32, (nc, T), 1) // CMP_BLK
                       == lax.broadcasted_iota(jnp.int32, (nc, T), 0), 1.0, 0.0).astype(BF16)
    keys = jnp.dot(sel, expand, preferred_element_type=F32)
    causal = (qi * TQ + lax.broadcasted_iota(jnp.int32, (TQ, T), 0)) >= lax.broadcasted_iota(jnp.int32, (TQ, T), 1)
    mask_ref[0, 0] = jnp.where((keys > 0.5) & causal, 1.0, 0.0).astype(BF16)


def nsa_cmp_select(q, kc, vc):
    B, G, HPG, T, dk = q.shape
    nc = kc.shape[2]
    return pl.pallas_call(
        _nsa_cmp_body,
        grid=(B, G, T // TQ),
        in_specs=[pl.BlockSpec((1, 1, HPG, TQ, dk), lambda b, g, qi: (b, g, 0, qi, 0)),
                  pl.BlockSpec((1, 1, nc, dk), lambda b, g, qi: (b, g, 0, 0)),
                  pl.BlockSpec((1, 1, nc, A_DV), lambda b, g, qi: (b, g, 0, 0))],
        out_specs=[pl.BlockSpec((1, TQ, HPG * A_DV), lambda b, g, qi: (b, qi, g)),
                   pl.BlockSpec((1, 1, TQ, T), lambda b, g, qi: (b, g, qi, 0))],
        out_shape=[jax.ShapeDtypeStruct((B, T, G * HPG * A_DV), F32),
                   jax.ShapeDtypeStruct((B, G, T, T), BF16)],
        compiler_params=pltpu.CompilerParams(
            dimension_semantics=("parallel", "parallel", "parallel"), vmem_limit_bytes=VMEM_LIMIT),
        name="nsa_cmp_select",
    )(q, kc, vc)


def _count(cond):
    return jnp.sum(jnp.where(cond, 1.0, 0.0), axis=1, keepdims=True)


def _dsa_select_body(topk, qi_ref, ki_ref, w_ref, mask_ref):
    qt = pl.program_id(1)
    T = ki_ref.shape[1]
    kidx = ki_ref[0].astype(BF16)
    w = w_ref[0] * (IDX_HEADS ** -0.5)
    score = jnp.zeros((TQ, T), F32)
    for h in range(IDX_HEADS):
        d = lax.dot_general(qi_ref[0, h].astype(BF16), kidx, _NT, preferred_element_type=F32) * (IDX_DIM ** -0.5)
        score = score + jnp.maximum(d, 0.0) * w[:, h:h + 1]
    t = qt * TQ + lax.broadcasted_iota(jnp.int32, (TQ, T), 0)
    s = lax.broadcasted_iota(jnp.int32, (TQ, T), 1)
    causal = s <= t
    score = jnp.where(causal, score, NEG)
    bits = pltpu.bitcast(score, jnp.int32)
    key = jnp.where(bits < 0, bits ^ jnp.int32(0x7FFFFFFF), bits)
    int_min = jnp.int32(-2 ** 31)

    def value_step(i, lo):
        cand = lo + jnp.left_shift(jnp.int32(1), 31 - i)
        return jnp.where(_count(key >= cand) >= float(topk), cand, lo)
    thr = lax.fori_loop(0, 32, value_step, jnp.full((TQ, 1), int_min, jnp.int32))
    above = key > thr
    tied = key == thr
    need = float(topk) - _count(above)
    n_bits = max(1, (T - 1).bit_length())

    def index_step(i, m):
        cand = m + jnp.left_shift(jnp.int32(1), n_bits - 1 - i)
        return jnp.where(_count(tied & (s < cand)) < need, cand, m)
    last = lax.fori_loop(0, n_bits, index_step, jnp.zeros((TQ, 1), jnp.int32))
    sel = (above | (tied & (s <= last))) & causal
    mask_ref[0, 0] = jnp.where(sel, 1.0, 0.0).astype(BF16)


def dsa_select(qi, ki, wi, topk):
    B, H, T, d = qi.shape
    return pl.pallas_call(
        functools.partial(_dsa_select_body, topk),
        grid=(B, T // TQ),
        in_specs=[pl.BlockSpec((1, H, TQ, d), lambda b, qt: (b, 0, qt, 0)),
                  pl.BlockSpec((1, T, d), lambda b, qt: (b, 0, 0)),
                  pl.BlockSpec((1, TQ, H), lambda b, qt: (b, qt, 0))],
        out_specs=pl.BlockSpec((1, 1, TQ, T), lambda b, qt: (b, 0, qt, 0)),
        out_shape=jax.ShapeDtypeStruct((B, 1, T, T), BF16),
        compiler_params=pltpu.CompilerParams(
            dimension_semantics=("parallel", "parallel"), vmem_limit_bytes=VMEM_LIMIT),
        name="dsa_select",
    )(qi, ki, wi)


def _moba_select_body(q_ref, k_ref, f_ref):
    T = k_ref.shape[2]
    nb = T // MOBA_BLK
    row = lax.broadcasted_iota(jnp.int32, (nb, k_ref.shape[3]), 0)
    kmean = jnp.zeros((nb, k_ref.shape[3]), F32)
    for b in range(nb):
        blk_sum = jnp.sum(k_ref[0, 0, b * MOBA_BLK:(b + 1) * MOBA_BLK, :], axis=0, keepdims=True)
        kmean = jnp.where(row == b, blk_sum * (1.0 / MOBA_BLK), kmean)
    kmean = kmean.astype(BF16)
    j = lax.broadcasted_iota(jnp.int32, (nb, T), 0)
    past = j < lax.broadcasted_iota(jnp.int32, (nb, T), 1) // MOBA_BLK
    for h in range(C_HPG):
        s = lax.dot_general(kmean, q_ref[0, 0, h].astype(BF16), _NT, preferred_element_type=F32)
        s = jnp.where(past, s, NEG)
        rank = jnp.zeros((nb, T), F32)
        for i in range(nb):
            si = s[i:i + 1, :]
            rank = rank + jnp.where((si > s) | ((si == s) & (i < j)), 1.0, 0.0)
        f_ref[0, h] = jnp.where((rank < float(MOBA_TOP)) & past, 1.0, 0.0)


def moba_select(q, k):
    B, G, HPG, T, dh = q.shape
    nb = T // MOBA_BLK
    return pl.pallas_call(
        _moba_select_body,
        grid=(B, G),
        in_specs=[pl.BlockSpec((1, 1, HPG, T, dh), lambda b, g: (b, g, 0, 0, 0)),
                  pl.BlockSpec((1, 1, T, dh), lambda b, g: (b, g, 0, 0))],
        out_specs=pl.BlockSpec((1, HPG, nb, T), lambda b, g: (b, g, 0, 0)),
        out_shape=jax.ShapeDtypeStruct((B, G * HPG, nb, T), F32),
        compiler_params=pltpu.CompilerParams(
            dimension_semantics=("parallel", "parallel"), vmem_limit_bytes=VMEM_LIMIT),
        name="moba_select",
    )(q, k)


MM_VMEM_LIMIT = 56 * 1024 * 1024
SUB_ROWS = 256


def _gmm_body(n_w, nk, nsb_max, sb, has_resid, te_ref, ns_ref, src_ref, *refs):
    x_ref = refs[0]
    w_refs = refs[1:1 + n_w]
    pos = 1 + n_w
    r_ref = refs[pos] if has_resid else None
    pos += int(has_resid)
    o_ref = refs[pos]
    acc_refs = refs[pos + 1:pos + 1 + n_w]
    i = pl.program_id(0)
    k = pl.program_id(2)
    n_sb = ns_ref[i]

    @pl.when(k == 0)
    def _():
        for a_ref in acc_refs:
            a_ref[...] = jnp.zeros(a_ref.shape, F32)

    for c in range(1, nsb_max + 1):
        @pl.when(n_sb == c)
        def _(c=c):
            xs = x_ref[:c * sb, :].astype(BF16)
            for w_ref, a_ref in zip(w_refs, acc_refs):
                a_ref[:c * sb, :] += jnp.dot(xs, w_ref[0].astype(BF16), preferred_element_type=F32)

    @pl.when(k == nk - 1)
    def _():
        if n_w == 2:
            val = jax.nn.silu(acc_refs[0][...]) * acc_refs[1][...]
        else:
            val = acc_refs[0][...]
        if has_resid:
            val = val + r_ref[...]
        o_ref[...] = val.astype(o_ref.dtype)


def grouped_matmul(x, ws, tables, tm, tn, tk, out_dtype, resid=None, layer=0):
    P, K = x.shape
    N = ws[0].shape[-1]
    tk = min(tk, K)
    tn = min(tn, N)
    assert P % tm == 0 and K % tk == 0
    sb = min(SUB_ROWS, tm)
    assert tm % sb == 0
    n_w = len(ws)
    ni, nj, nk = P // tm, pl.cdiv(N, tn), K // tk

    def x_idx(i, j, k, te, ns, src):
        return (src[i], jnp.where(ns[i] > 0, k, nk - 1))

    def w_idx(i, j, k, te, ns, src):
        act = ns[i] > 0
        return (layer, te[i], jnp.where(act, k, nk - 1), jnp.where(act, j, nj - 1))

    def o_idx(i, j, k, te, ns, src):
        return (i, j)

    in_specs = [pl.BlockSpec((tm, tk), x_idx)] + [pl.BlockSpec((None, 1, tk, tn), w_idx)] * n_w
    args = [x] + list(ws)
    if resid is not None:
        in_specs.append(pl.BlockSpec((tm, tn), o_idx))
        args.append(resid)
    return pl.pallas_call(
        functools.partial(_gmm_body, n_w, nk, tm // sb, sb, resid is not None),
        grid_spec=pltpu.PrefetchScalarGridSpec(
            num_scalar_prefetch=3, grid=(ni, nj, nk), in_specs=in_specs,
            out_specs=pl.BlockSpec((tm, tn), o_idx),
            scratch_shapes=[pltpu.VMEM((tm, tn), F32)] * n_w),
        out_shape=jax.ShapeDtypeStruct((P, N), out_dtype),
        compiler_params=pltpu.CompilerParams(
            dimension_semantics=("parallel", "parallel", "arbitrary"), vmem_limit_bytes=MM_VMEM_LIMIT),
        name=f"gmm{n_w}_{tm}x{tn}x{tk}",
    )(*tables, *args)


def _dense_tables(rows, tm):
    n = rows // tm
    return (jnp.zeros((n,), jnp.int32), jnp.full((n,), tm // min(SUB_ROWS, tm), jnp.int32),
            jnp.arange(n, dtype=jnp.int32))


def _row_tile(rows):
    for tm in (2048, 1024, 512, 256):
        if rows % tm == 0:
            return tm
    return rows


def matmul(x, w, out_dtype=F32, resid=None, tn=1024, tk=512, layer=None):
    lead = x.shape[:-1]
    x2 = x.reshape(-1, x.shape[-1])
    rows = x2.shape[0]
    tm = _row_tile(rows)
    r2 = None if resid is None else resid.reshape(rows, -1)
    w4 = w[None, None] if layer is None else w[:, None]
    out = grouped_matmul(x2, [w4], _dense_tables(rows, tm), tm, tn, tk, out_dtype, r2, layer or 0)
    return out.reshape(*lead, w.shape[-1])


def swiglu_up(x, w1, w3, tables, tm, layer, tf=512, tk=1024):
    return grouped_matmul(x, [w1, w3], tables, tm, tf, tk, BF16, layer=layer)


PAGES_PER_STEP = 8
REMOVED = -3e38


def bias_by_position(tab, past_len):
    bd = bias_by_distance(tab, REL_MAX_DIST + 1)
    H = bd.shape[0]
    near = bd[:, 1:REL_MAX_DIST + 1][:, ::-1]
    far = jnp.broadcast_to(bd[:, REL_MAX_DIST:], (H, past_len - REL_MAX_DIST))
    return jnp.concatenate([far, near], axis=1), bd[:, :1]


def _page_specs(lanes, lane_block, pages_of):
    def spec(r):
        return pl.BlockSpec((None, PAGE_SIZE, lanes), lambda *a: (pages_of(r)(*a), 0, lane_block))
    return [spec(r) for r in range(PAGES_PER_STEP)]


def _chunk_page(r):
    return lambda b, c, pt, *_: pt[b, c * PAGES_PER_STEP + r]


def _topk_rows(score, k):
    R, L = score.shape
    jf = lax.broadcasted_iota(jnp.int32, (R, L), 1).astype(F32)
    slot = lax.broadcasted_iota(jnp.int32, (R, LANES), 1)
    idx = jnp.zeros((R, LANES), F32)
    val = jnp.zeros((R, LANES), F32)
    for n in range(k):
        m = jnp.max(score, axis=1, keepdims=True)
        i = jnp.min(jnp.where(score == m, jf, 1e9), axis=1, keepdims=True)
        idx = jnp.where(slot == n, i, idx)
        val = jnp.where(slot == n, jnp.where(m > 0.5 * NEG, 1.0, 0.0), val)
        score = jnp.where(jf == i, REMOVED, score)
    return idx, val


def _gather_cmp_body(width, native, pt_ref, *refs):
    pages = refs[:PAGES_PER_STEP]
    pe_ref, o_ref = refs[PAGES_PER_STEP:]
    for r in range(PAGES_PER_STEP):
        for g in range(A_GROUPS):
            rows = pages[r][:, g, :] if native else pages[r][:, g * width:(g + 1) * width]
            o_ref[0, g, r * PAGE_SIZE:(r + 1) * PAGE_SIZE, :] = rows + pe_ref[...]


def gather_compress_rows(cache, page_table, pe, width, layer=None):
    B, n_pages = page_table.shape
    pe2 = jnp.tile(pe, (PAGE_SIZE // CMP_BLK, 1))
    if layer is None:
        page_specs = _page_specs(A_GROUPS * width, 0, _chunk_page)
    else:
        page_specs = [pl.BlockSpec((None, None, PAGE_SIZE, None, A_GROUPS, width),
                                   lambda b, c, pt, r=r: (layer, pt[b, c * PAGES_PER_STEP + r], 0, 0, 0, 0))
                      for r in range(PAGES_PER_STEP)]
    return pl.pallas_call(
        functools.partial(_gather_cmp_body, width, layer is not None),
        grid_spec=pltpu.PrefetchScalarGridSpec(
            num_scalar_prefetch=1, grid=(B, n_pages // PAGES_PER_STEP),
            in_specs=page_specs
            + [pl.BlockSpec((PAGE_SIZE, width), lambda b, c, pt: (0, 0))],
            out_specs=pl.BlockSpec((1, A_GROUPS, PAGES_PER_STEP * PAGE_SIZE, width), lambda b, c, pt: (b, 0, c, 0))),
        out_shape=jax.ShapeDtypeStruct((B, A_GROUPS, n_pages * PAGE_SIZE, width), F32),
        compiler_params=pltpu.CompilerParams(dimension_semantics=("parallel", "parallel"),
                                             vmem_limit_bytes=VMEM_LIMIT),
        name="gather_compress_rows",
    )(page_table, *([cache] * PAGES_PER_STEP), pe2)


def _nsa_sample_a_body(q_ref, kc_ref, vc_ref, wk_ref, wv_ref, kn_ref, vn_ref, bw_ref, oc_ref, ow_ref, idx_ref, val_ref):
    scale = A_DK ** -0.5
    q = q_ref[0, 0]
    qb = q.astype(BF16)
    nc = kc_ref.shape[2]
    s = lax.dot_general(qb, kc_ref[0, 0].astype(BF16), _NT, preferred_element_type=F32) * scale
    e = jnp.exp(s - jnp.max(s, axis=1, keepdims=True))
    p = e / jnp.sum(e, axis=1, keepdims=True)
    oc_ref[0, 0] = jnp.dot(p.astype(BF16), vc_ref[0, 0].astype(BF16), preferred_element_type=F32)
    imp = jnp.concatenate([jnp.sum(p, axis=0, keepdims=True), jnp.zeros((1, LANES), F32)], axis=1)
    j = lax.broadcasted_iota(jnp.int32, imp.shape, 1)
    own = nc
    forced = (j == 0) | (j == own) | (j == own - 1)
    score = jnp.where(forced, FORCE, imp)
    score = jnp.where(j <= own, score, REMOVED)
    idx, val = _topk_rows(score, N_SEL)
    idx_ref[0, 0] = idx.astype(jnp.int32)
    val_ref[0, 0] = val.astype(jnp.int32)
    wb = wk_ref.shape[2]
    sw = lax.dot_general(qb, wk_ref[0, 0].astype(BF16), _NT, preferred_element_type=F32) * scale + bw_ref[0][:, :wb]
    dist = wb - lax.broadcasted_iota(jnp.int32, sw.shape, 1)
    in_win = dist < WINDOW
    sn = jnp.sum(q * kn_ref[0, 0], axis=1, keepdims=True) * scale + bw_ref[0][:, wb:wb + 1]
    m = jnp.maximum(jnp.max(jnp.where(in_win, sw, NEG), axis=1, keepdims=True), sn)
    ew = jnp.where(in_win, jnp.exp(sw - m), 0.0)
    en = jnp.exp(sn - m)
    l = jnp.sum(ew, axis=1, keepdims=True) + en
    ow_ref[0, 0] = (jnp.dot(ew.astype(BF16), wv_ref[0, 0].astype(BF16), preferred_element_type=F32)
                    + en * vn_ref[0, 0]) / l


def _nsa_sample_b_body(nc, idx_ref, val_ref, pt_ref, q_ref, ka_ref, kb_ref, va_ref, vb_ref, ba_ref, bb_ref,
                       kn_ref, vn_ref, b0_ref, o_ref, m_ref, l_ref, acc_ref):
    b = pl.program_id(0)
    n = pl.program_id(1)
    scale = A_DK ** -0.5

    @pl.when(n == 0)
    def _():
        m_ref[...] = jnp.full(m_ref.shape, NEG, F32)
        l_ref[...] = jnp.zeros(l_ref.shape, F32)
        acc_ref[...] = jnp.zeros(acc_ref.shape, F32)

    for g, (k_ref, v_ref, bias_ref) in enumerate(((ka_ref, va_ref, ba_ref), (kb_ref, vb_ref, bb_ref))):
        slot = (b * A_GROUPS + g) * N_SEL + n
        cached = (val_ref[slot] > 0) & (idx_ref[slot] < nc)

        @pl.when(cached)
        def _(g=g, k_ref=k_ref, v_ref=v_ref, bias_ref=bias_ref):
            lo_k = (A_GROUPS + g) * A_DK
            k = k_ref[:, lo_k:lo_k + A_DK].astype(BF16)
            v = v_ref[:, g, :].astype(BF16)
            s = lax.dot_general(q_ref[0, g].astype(BF16), k, _NT, preferred_element_type=F32) * scale + bias_ref[0]
            m_prev = m_ref[g][:, :1]
            m_new = jnp.maximum(m_prev, jnp.max(s, axis=1, keepdims=True))
            p = jnp.exp(s - m_new)
            alpha = jnp.exp(m_prev - m_new)
            l_ref[g] = jnp.broadcast_to(alpha * l_ref[g][:, :1] + jnp.sum(p, axis=1, keepdims=True), l_ref.shape[1:])
            acc_ref[g] = alpha * acc_ref[g] + jnp.dot(p.astype(BF16), v, preferred_element_type=F32)
            m_ref[g] = jnp.broadcast_to(m_new, m_ref.shape[1:])

    @pl.when(n == N_SEL - 1)
    def _():
        for g in range(A_GROUPS):
            sn = jnp.sum(q_ref[0, g] * kn_ref[0, g], axis=1, keepdims=True) * scale + b0_ref[g]
            m_prev = m_ref[g][:, :1]
            m_new = jnp.maximum(m_prev, sn)
            alpha = jnp.exp(m_prev - m_new)
            en = jnp.exp(sn - m_new)
            l = alpha * l_ref[g][:, :1] + en
            o_ref[0, g] = (alpha * acc_ref[g] + en * vn_ref[0, g]) / l


def nsa_decode(q, k3, v3, gates, cache_k, cache_v, layer, win_k, win_v, page_table,
               pe_k, pe_v, wk1, wk2, wv1, wv2, tab_a):
    B, S, G, HPG, DK = q.shape
    n_pool = cache_k.shape[0]
    n_pages = page_table.shape[1]
    P = n_pages * PAGE_SIZE
    nc = P // CMP_BLK
    assert S == 1 and G == A_GROUPS and P % CMP_BLK == 0 and (P + S) // CMP_BLK == nc and win_k.shape[1] == WINDOW
    ck = cache_k.reshape(n_pool, PAGE_SIZE, 2 * G * DK)
    rows_k = gather_compress_rows(ck, page_table, pe_k, DK).reshape(B, G, nc, CMP_BLK * DK)
    rows_v = gather_compress_rows(cache_v, page_table, pe_v, A_DV, layer).reshape(B, G, nc, CMP_BLK * A_DV)
    kc = matmul(jax.nn.silu(matmul(rows_k, wk1)), wk2)
    vc = matmul(jax.nn.silu(matmul(rows_v, wv1)), wv2)
    qh = q.reshape(B, G, HPG, DK)
    brev, b0 = bias_by_position(tab_a, P)
    bw = jnp.concatenate([brev[:, P - WINDOW:], b0, jnp.zeros((A_HEADS, LANES - 1), F32)], axis=1)
    bw = bw.reshape(G, HPG, WINDOW + LANES)
    new_k = jnp.moveaxis(k3[:, 0], 1, 2)
    new_v = jnp.moveaxis(v3[:, 0], 1, 2)
    per_bg = lambda *shape: pl.BlockSpec((1, 1) + shape, lambda b, g: (b, g) + (0,) * len(shape))
    o_c, o_w, idx, val = pl.pallas_call(
        _nsa_sample_a_body,
        grid=(B, G),
        in_specs=[per_bg(HPG, DK), per_bg(nc, DK), per_bg(nc, A_DV), per_bg(WINDOW, DK), per_bg(WINDOW, A_DV),
                  per_bg(1, DK), per_bg(1, A_DV),
                  pl.BlockSpec((1, HPG, WINDOW + LANES), lambda b, g: (g, 0, 0))],
        out_specs=[per_bg(HPG, A_DV), per_bg(HPG, A_DV), per_bg(1, LANES), per_bg(1, LANES)],
        out_shape=[jax.ShapeDtypeStruct((B, G, HPG, A_DV), F32)] * 2
        + [jax.ShapeDtypeStruct((B, G, 1, LANES), jnp.int32)] * 2,
        compiler_params=pltpu.CompilerParams(dimension_semantics=("parallel", "parallel"),
                                             vmem_limit_bytes=VMEM_LIMIT),
        name="nsa_sample_cmp_win",
    )(qh, kc, vc, jnp.moveaxis(win_k, 1, 2), jnp.moveaxis(win_v, 1, 2), new_k[:, :, 2:3], new_v[:, :, 2:3], bw)
    idx_flat = idx[:, :, 0, :N_SEL].reshape(-1)
    val_flat = val[:, :, 0, :N_SEL].reshape(-1)
    bsel = brev.reshape(G, HPG, nc, CMP_BLK).transpose(0, 2, 1, 3)

    def blk(g):
        return lambda b, n, ix, vl, pt: jnp.minimum(ix[(b * G + g) * N_SEL + n], nc - 1)

    def kv_spec(g, lanes):
        half = PAGE_SIZE // CMP_BLK
        return pl.BlockSpec((None, CMP_BLK, lanes),
                            lambda b, n, ix, vl, pt: (pt[b, blk(g)(b, n, ix, vl, pt) // half],
                                                       blk(g)(b, n, ix, vl, pt) % half, 0))

    def v_spec(g):
        half = PAGE_SIZE // CMP_BLK
        return pl.BlockSpec((None, None, CMP_BLK, None, G, A_DV),
                            lambda b, n, ix, vl, pt: (layer, pt[b, blk(g)(b, n, ix, vl, pt) // half],
                                                       blk(g)(b, n, ix, vl, pt) % half, 1, 0, 0))

    def bias_spec(g):
        return pl.BlockSpec((None, 1, HPG, CMP_BLK), lambda b, n, ix, vl, pt: (g, blk(g)(b, n, ix, vl, pt), 0, 0))
    whole = lambda *shape: pl.BlockSpec((1,) + shape, lambda b, n, ix, vl, pt: (b,) + (0,) * len(shape))
    o_s = pl.pallas_call(
        functools.partial(_nsa_sample_b_body, nc),
        grid_spec=pltpu.PrefetchScalarGridSpec(
            num_scalar_prefetch=3, grid=(B, N_SEL),
            in_specs=[whole(G, HPG, DK), kv_spec(0, 2 * G * DK), kv_spec(1, 2 * G * DK),
                      v_spec(0), v_spec(1), bias_spec(0), bias_spec(1),
                      whole(G, 1, DK), whole(G, 1, A_DV),
                      pl.BlockSpec((G, HPG, 1), lambda b, n, ix, vl, pt: (0, 0, 0))],
            out_specs=whole(G, HPG, A_DV),
            scratch_shapes=[pltpu.VMEM((G, HPG, LANES), F32), pltpu.VMEM((G, HPG, LANES), F32),
                            pltpu.VMEM((G, HPG, A_DV), F32)]),
        out_shape=jax.ShapeDtypeStruct((B, G, HPG, A_DV), F32),
        compiler_params=pltpu.CompilerParams(dimension_semantics=("parallel", "arbitrary"),
                                             vmem_limit_bytes=VMEM_LIMIT),
        name="nsa_sample_selected",
    )(idx_flat, val_flat, page_table, qh, ck, ck, cache_v, cache_v, bsel, bsel, new_k[:, :, 1:2], new_v[:, :, 1:2],
      b0.reshape(G, HPG, 1))
    shp = (B, S, G, HPG, A_DV)
    return nsa_combine(gates, o_c.reshape(shp), o_s.reshape(shp), o_w.reshape(shp))


def _dsa_scores_body(pt_ref, *refs):
    pages = refs[:PAGES_PER_STEP]
    qi_ref, w_ref, o_ref = refs[PAGES_PER_STEP:]
    qi = qi_ref[0].astype(BF16)
    w = w_ref[0] * (IDX_HEADS ** -0.5)
    for r in range(PAGES_PER_STEP):
        d = lax.dot_general(qi, pages[r][...].astype(BF16), _NT, preferred_element_type=F32) * (IDX_DIM ** -0.5)
        o_ref[0, :, r * PAGE_SIZE:(r + 1) * PAGE_SIZE] = jnp.sum(jnp.maximum(d, 0.0) * w, axis=0, keepdims=True)


def _dsa_sample_select_body(topk, s_ref, qi_ref, w_ref, kn_ref, m_ref):
    B, P = s_ref.shape
    w = w_ref[...] * (IDX_HEADS ** -0.5)
    dn = jnp.sum(qi_ref[...] * kn_ref[...], axis=2) * (IDX_DIM ** -0.5)
    s_new = jnp.sum(jnp.maximum(dn, 0.0) * w, axis=1, keepdims=True)

    def order_key(x):
        bits = pltpu.bitcast(x, jnp.int32)
        return jnp.where(bits < 0, bits ^ jnp.int32(0x7FFFFFFF), bits)
    key = order_key(s_ref[...])
    key_new = order_key(jnp.broadcast_to(s_new, (B, LANES)))[:, :1]
    pos = lax.broadcasted_iota(jnp.int32, (B, P), 1)

    def count(cond, cond_new):
        return _count(cond) + jnp.where(cond_new, 1.0, 0.0)

    def value_step(i, lo):
        cand = lo + jnp.left_shift(jnp.int32(1), 31 - i)
        return jnp.where(count(key >= cand, key_new >= cand) >= float(topk), cand, lo)
    thr = lax.fori_loop(0, 32, value_step, jnp.full((B, 1), jnp.int32(-2 ** 31), jnp.int32))
    need = float(topk) - count(key > thr, key_new > thr)
    tied = key == thr
    n_bits = max(1, (P - 1).bit_length())

    def index_step(i, m):
        cand = m + jnp.left_shift(jnp.int32(1), n_bits - 1 - i)
        return jnp.where(_count(tied & (pos < cand)) < need, cand, m)
    last = lax.fori_loop(0, n_bits, index_step, jnp.zeros((B, 1), jnp.int32))
    sel = (key > thr) | (tied & (pos <= last))
    taken = _count(sel)
    sel_new = (key_new > thr) | ((key_new == thr) & (taken < float(topk)))
    m_ref[:, :P] = jnp.where(sel, 1.0, 0.0)
    lane = lax.broadcasted_iota(jnp.int32, (B, LANES), 1)
    m_ref[:, P:] = jnp.where((lane == 0) & sel_new, 1.0, 0.0)


def _dsa_sample_attend_body(pt_ref, *refs):
    n = PAGES_PER_STEP
    kp, vp = refs[:n], refs[n:2 * n]
    q_ref, mask_ref, bias_ref, mnew_ref, kn_ref, vn_ref, b0_ref, o_ref, m_ref, l_ref, acc_ref = refs[2 * n:]
    c = pl.program_id(1)
    scale = B_DH ** -0.5

    @pl.when(c == 0)
    def _():
        m_ref[...] = jnp.full(m_ref.shape, NEG, F32)
        l_ref[...] = jnp.zeros(l_ref.shape, F32)
        acc_ref[...] = jnp.zeros(acc_ref.shape, F32)

    mask = mask_ref[0] > 0.5
    for g in range(B_GROUPS):
        k = jnp.concatenate([kp[r][:, g, :] for r in range(n)], axis=0).astype(BF16)
        v = jnp.concatenate([vp[r][:, g, :] for r in range(n)], axis=0).astype(BF16)
        s = lax.dot_general(q_ref[0, g].astype(BF16), k, _NT, preferred_element_type=F32) * scale + bias_ref[g]
        s = jnp.where(mask, s, NEG)
        m_prev = m_ref[g][:, :1]
        m_new = jnp.maximum(m_prev, jnp.max(s, axis=1, keepdims=True))
        p = jnp.where(mask, jnp.exp(s - m_new), 0.0)
        alpha = jnp.exp(m_prev - m_new)
        l_ref[g] = jnp.broadcast_to(alpha * l_ref[g][:, :1] + jnp.sum(p, axis=1, keepdims=True), l_ref.shape[1:])
        acc_ref[g] = alpha * acc_ref[g] + jnp.dot(p.astype(BF16), v, preferred_element_type=F32)
        m_ref[g] = jnp.broadcast_to(m_new, m_ref.shape[1:])

    @pl.when(c == pl.num_programs(1) - 1)
    def _():
        new_on = mnew_ref[0][:, :1] > 0.5
        for g in range(B_GROUPS):
            sn = jnp.sum(q_ref[0, g] * kn_ref[0, g], axis=1, keepdims=True) * scale + b0_ref[g]
            sn = jnp.where(new_on, sn, NEG)
            m_prev = m_ref[g][:, :1]
            m_new = jnp.maximum(m_prev, sn)
            alpha = jnp.exp(m_prev - m_new)
            en = jnp.where(new_on, jnp.exp(sn - m_new), 0.0)
            l = alpha * l_ref[g][:, :1] + en
            o_ref[0, g] = (alpha * acc_ref[g] + en * vn_ref[0, g]) / l


def dsa_decode(q, k, v, qi, ki, wi, cache_k, cache_v, cache_kidx, layer, page_table, tab_b):
    B, S, G, HPG, DH = q.shape
    n_pages = page_table.shape[1]
    P = n_pages * PAGE_SIZE
    n_chunks = n_pages // PAGES_PER_STEP
    chunk = PAGES_PER_STEP * PAGE_SIZE
    assert S == 1 and n_pages % PAGES_PER_STEP == 0
    topk = min(IDX_TOPK, (P + S) // 4)
    params = pltpu.CompilerParams(dimension_semantics=("parallel", "arbitrary"), vmem_limit_bytes=VMEM_LIMIT)

    def pool_specs(*page_shape):
        return [pl.BlockSpec((None, None) + page_shape,
                             lambda b, c, pt, r=r: (layer, pt[b, c * PAGES_PER_STEP + r]) + (0,) * len(page_shape))
                for r in range(PAGES_PER_STEP)]
    scores = pl.pallas_call(
        _dsa_scores_body,
        grid_spec=pltpu.PrefetchScalarGridSpec(
            num_scalar_prefetch=1, grid=(B, n_chunks),
            in_specs=pool_specs(PAGE_SIZE, IDX_DIM)
            + [pl.BlockSpec((1, IDX_HEADS, IDX_DIM), lambda b, c, pt: (b, 0, 0)),
               pl.BlockSpec((1, IDX_HEADS, 1), lambda b, c, pt: (b, 0, 0))],
            out_specs=pl.BlockSpec((1, 1, chunk), lambda b, c, pt: (b, 0, c))),
        out_shape=jax.ShapeDtypeStruct((B, 1, P), F32),
        compiler_params=params, name="dsa_sample_scores",
    )(page_table, *([cache_kidx] * PAGES_PER_STEP), qi[:, 0], wi[:, 0, :, None])
    sel = pl.pallas_call(
        functools.partial(_dsa_sample_select_body, topk),
        out_shape=jax.ShapeDtypeStruct((B, P + LANES), F32),
        compiler_params=pltpu.CompilerParams(vmem_limit_bytes=VMEM_LIMIT), name="dsa_sample_select",
    )(scores.reshape(B, P), qi[:, 0], wi[:, 0], ki)
    sel = sel.reshape(B, 1, P + LANES)
    brev, b0 = bias_by_position(tab_b, P)
    whole = lambda *shape: pl.BlockSpec((1,) + shape, lambda b, c, pt: (b,) + (0,) * len(shape))
    o = pl.pallas_call(
        _dsa_sample_attend_body,
        grid_spec=pltpu.PrefetchScalarGridSpec(
            num_scalar_prefetch=1, grid=(B, n_chunks),
            in_specs=pool_specs(PAGE_SIZE, G, DH) + pool_specs(PAGE_SIZE, G, DH)
            + [whole(G, HPG, DH),
               pl.BlockSpec((1, 1, chunk), lambda b, c, pt: (b, 0, c)),
               pl.BlockSpec((G, HPG, chunk), lambda b, c, pt: (0, 0, c)),
               pl.BlockSpec((1, 1, LANES), lambda b, c, pt: (b, 0, P // LANES)),
               whole(G, 1, DH), whole(G, 1, DH),
               pl.BlockSpec((G, HPG, 1), lambda b, c, pt: (0, 0, 0))],
            out_specs=whole(G, HPG, DH),
            scratch_shapes=[pltpu.VMEM((G, HPG, LANES), F32), pltpu.VMEM((G, HPG, LANES), F32),
                            pltpu.VMEM((G, HPG, DH), F32)]),
        out_shape=jax.ShapeDtypeStruct((B, G, HPG, DH), F32),
        compiler_params=params, name="dsa_sample_attend",
    )(page_table, *([cache_k] * PAGES_PER_STEP), *([cache_v] * PAGES_PER_STEP), q[:, 0], sel,
      brev.reshape(G, HPG, P), sel,
      jnp.moveaxis(k, 1, 2), jnp.moveaxis(v, 1, 2), b0.reshape(G, HPG, 1))
    return o.reshape(B, S, G * HPG * DH)


def _moba_kmean_body(pt_ref, p0_ref, p1_ref, o_ref):
    o_ref[0] = (jnp.sum(p0_ref[...], axis=0) + jnp.sum(p1_ref[...], axis=0)) * (1.0 / MOBA_BLK)


def _moba_sample_pick_body(nf, q_ref, km_ref, idx_ref, val_ref):
    B, H, DH = q_ref.shape
    nfp = km_ref.shape[1]
    group = lax.broadcasted_iota(jnp.int32, (H, nfp), 0) // C_HPG
    lane = lax.broadcasted_iota(jnp.int32, (H, nfp), 1)
    for b in range(B):
        qb = q_ref[b].astype(BF16)
        s = jnp.full((H, nfp), REMOVED, F32)
        for g in range(C_GROUPS):
            kg = km_ref[b, :, g, :].astype(BF16)
            sg = lax.dot_general(qb, kg, _NT, preferred_element_type=F32)
            s = jnp.where((group == g) & (lane < nf), sg, s)
        idx, val = _topk_rows(s, MOBA_TOP)
        idx_ref[b] = idx.astype(jnp.int32)
        val_ref[b] = val.astype(jnp.int32)


def _moba_sample_attend_body(idx_ref, val_ref, pt_ref, q_ref, *refs):
    n = MOBA_TOP * (MOBA_BLK // PAGE_SIZE)
    kp, vp, bias = refs[:n], refs[n:2 * n], refs[2 * n:2 * n + MOBA_TOP]
    kn_ref, vn_ref, b0_ref, o_ref = refs[2 * n + MOBA_TOP:]
    b, h = pl.program_id(0), pl.program_id(1)
    scale = C_DH ** -0.5
    q = q_ref[0, 0]
    chosen = [val_ref[(b * C_HEADS + h) * MOBA_TOP + r] > 0 for r in range(MOBA_TOP)]

    def attend(g):
        sn = jnp.sum(q * kn_ref[0, 0], axis=1, keepdims=True) * scale + b0_ref[0]
        scores, m = [], sn
        for r in range(MOBA_TOP):
            k = jnp.concatenate([kp[2 * r][:, g, :], kp[2 * r + 1][:, g, :]], axis=0).astype(BF16)
            s = lax.dot_general(q.astype(BF16), k, _NT, preferred_element_type=F32) * scale + bias[r][0]
            s = jnp.where(chosen[r], s, NEG)
            scores.append(s)
            m = jnp.maximum(m, jnp.max(s, axis=1, keepdims=True))
        en = jnp.exp(sn - m)
        l, acc = en, en * vn_ref[0, 0]
        for r in range(MOBA_TOP):
            v = jnp.concatenate([vp[2 * r][:, g, :], vp[2 * r + 1][:, g, :]], axis=0).astype(BF16)
            p = jnp.where(chosen[r], jnp.exp(scores[r] - m), 0.0)
            l = l + jnp.sum(p, axis=1, keepdims=True)
            acc = acc + jnp.dot(p.astype(BF16), v, preferred_element_type=F32)
        o_ref[0, 0] = acc / l

    for g in range(C_GROUPS):
        pl.when(h // C_HPG == g)(functools.partial(attend, g))


def moba_decode(q, k, v, cache_k, cache_v, layer, page_table, tab_c):
    B, S, H, DH = q.shape
    n_pages = page_table.shape[1]
    P = n_pages * PAGE_SIZE
    ppb = MOBA_BLK // PAGE_SIZE
    nf = (P + S) // MOBA_BLK
    assert S == 1 and ppb == 2 and P % MOBA_BLK == 0 and nf == P // MOBA_BLK and nf >= MOBA_TOP
    page_block = (None, None, PAGE_SIZE, C_GROUPS, DH)
    kmean = pl.pallas_call(
        _moba_kmean_body,
        grid_spec=pltpu.PrefetchScalarGridSpec(
            num_scalar_prefetch=1, grid=(B, nf),
            in_specs=[pl.BlockSpec(page_block, lambda b, j, pt: (layer, pt[b, ppb * j], 0, 0, 0)),
                      pl.BlockSpec(page_block, lambda b, j, pt: (layer, pt[b, ppb * j + 1], 0, 0, 0))],
            out_specs=pl.BlockSpec((1, C_GROUPS, DH), lambda b, j, pt: (b * nf + j, 0, 0))),
        out_shape=jax.ShapeDtypeStruct((B * nf, C_GROUPS, DH), F32),
        compiler_params=pltpu.CompilerParams(dimension_semantics=("parallel", "parallel"),
                                             vmem_limit_bytes=VMEM_LIMIT),
        name="moba_sample_kmean",
    )(page_table, cache_k, cache_k).reshape(B, nf, C_GROUPS, DH)
    idx, val = pl.pallas_call(
        functools.partial(_moba_sample_pick_body, nf),
        out_shape=[jax.ShapeDtypeStruct((B, H, LANES), jnp.int32)] * 2,
        compiler_params=pltpu.CompilerParams(vmem_limit_bytes=VMEM_LIMIT), name="moba_sample_pick",
    )(q[:, 0], jnp.pad(kmean, ((0, 0), (0, -nf % LANES), (0, 0), (0, 0))))
    idx_flat = idx[:, :, :MOBA_TOP].reshape(-1)
    val_flat = val[:, :, :MOBA_TOP].reshape(-1)
    brev, b0 = bias_by_position(tab_c, P)

    def blk(b, h, r, ix):
        return jnp.minimum(ix[(b * H + h) * MOBA_TOP + r], nf - 1)

    def page_specs():
        return [pl.BlockSpec(page_block,
                             lambda b, h, ix, vl, pt, r=r, w=w: (layer, pt[b, ppb * blk(b, h, r, ix) + w], 0, 0, 0))
                for r in range(MOBA_TOP) for w in range(ppb)]
    bias_specs = [pl.BlockSpec((1, 1, MOBA_BLK), lambda b, h, ix, vl, pt, r=r: (h * nf + blk(b, h, r, ix), 0, 0))
                  for r in range(MOBA_TOP)]
    per_head = lambda rows, d: pl.BlockSpec((1, 1, rows, d), lambda b, h, ix, vl, pt: (b, h, 0, 0))
    per_group = lambda d: pl.BlockSpec((1, 1, 1, d), lambda b, h, ix, vl, pt: (b, h // C_HPG, 0, 0))
    n_pg = MOBA_TOP * ppb
    brev3 = brev.reshape(H * nf, 1, MOBA_BLK)
    o = pl.pallas_call(
        _moba_sample_attend_body,
        grid_spec=pltpu.PrefetchScalarGridSpec(
            num_scalar_prefetch=3, grid=(B, H),
            in_specs=[per_head(1, DH)] + page_specs() + page_specs() + bias_specs
            + [per_group(DH), per_group(DH), pl.BlockSpec((1, 1, 1), lambda b, h, ix, vl, pt: (h, 0, 0))],
            out_specs=per_head(1, DH)),
        out_shape=jax.ShapeDtypeStruct((B, H, 1, DH), F32),
        compiler_params=pltpu.CompilerParams(dimension_semantics=("parallel", "parallel"),
                                             vmem_limit_bytes=VMEM_LIMIT),
        name="moba_sample_attend",
    )(idx_flat, val_flat, page_table, q[:, 0, :, None, :], *([cache_k] * n_pg), *([cache_v] * n_pg),
      *([brev3] * MOBA_TOP),
      jnp.moveaxis(k, 1, 2), jnp.moveaxis(v, 1, 2), b0.reshape(H, 1, 1))
    return o.reshape(B, S, H * DH)


def split_cols(z, sizes):
    cuts = [int(c) for c in np.cumsum(sizes)[:-1]]
    return jnp.split(z, cuts, axis=-1)


def t5_bucket(dist):
    n = jnp.maximum(dist, 0)
    nf = jnp.maximum(n, 1).astype(jnp.float32)
    large = REL_EXACT + (jnp.log(nf / REL_EXACT) / math.log(REL_MAX_DIST / REL_EXACT)
                         * (N_BUCKETS - REL_EXACT)).astype(jnp.int32)
    return jnp.where(n < REL_EXACT, n, jnp.minimum(large, N_BUCKETS - 1))


def nsa_compress(rows, pe, w1, w2):
    B, L, G, d = rows.shape
    nc = L // CMP_BLK
    blk = rows[:, : nc * CMP_BLK].reshape(B, nc, CMP_BLK, G, d) + pe[None, None, :, None, :]
    flat = blk.transpose(0, 1, 3, 2, 4).reshape(B, nc, G, CMP_BLK * d)
    return matmul(jax.nn.silu(matmul(flat, w1)), w2)


def nsa_combine(gates, o_c, o_s, o_w):
    g = gates[..., None].astype(o_c.dtype)
    o = g[:, :, 0] * o_c + g[:, :, 1] * o_s + g[:, :, 2] * o_w
    return o.reshape(o.shape[0], o.shape[1], -1)


def _heads_first(a):
    return jnp.moveaxis(a, 1, -2)


def nsa_prompt(q, k3, v3, gates, pe_k, pe_v, wk1, wk2, wv1, wv2, tab_a):
    B, T, G, HPG, DK = q.shape
    assert T % TQ == 0 and TQ == TK and (T // CMP_BLK) * CMP_BLK == T
    kc = nsa_compress(k3[:, :, 0], pe_k, wk1, wk2)
    vc = nsa_compress(v3[:, :, 0], pe_v, wv1, wv2)
    qh = _heads_first(q)
    o_c, sel_mask = nsa_cmp_select(qh, _heads_first(kc), _heads_first(vc))
    tiles = toeplitz_bias_tiles(tab_a)
    scale = A_DK ** -0.5
    o_s = block_attention('key', qh, _heads_first(k3[:, :, 1]), _heads_first(v3[:, :, 1]), tiles, sel_mask, scale)
    o_w = block_attention('window', qh, _heads_first(k3[:, :, 2]), _heads_first(v3[:, :, 2]), tiles, None, scale)
    shp = (B, T, G, HPG, A_DV)
    return nsa_combine(gates, o_c.reshape(shp), o_s.reshape(shp), o_w.reshape(shp))


def dsa_prompt(q, k, v, qi, ki, wi, tab_b):
    B, T = q.shape[:2]
    assert T % TQ == 0 and TQ == TK
    sel_mask = dsa_select(_heads_first(qi), ki, wi, min(IDX_TOPK, T // 4))
    return block_attention('key', _heads_first(q), _heads_first(k), _heads_first(v),
                           toeplitz_bias_tiles(tab_b), sel_mask, B_DH ** -0.5)


def moba_prompt(q, k, v, tab_c):
    B, T, H, DH = q.shape
    assert T % MOBA_BLK == 0 and TQ == MOBA_BLK and TK == MOBA_BLK
    qh = _heads_first(q.reshape(B, T, C_GROUPS, C_HPG, DH))
    kh, vh = _heads_first(k), _heads_first(v)
    flags = moba_select(qh, kh).transpose(0, 1, 3, 2)
    return block_attention('moba', qh, kh, vh, toeplitz_bias_tiles(tab_c), flags, DH ** -0.5)


def mem_kv(mem, g, wk, wv, layer):
    m = rms_norm(mem, g, BF16)
    B = mem.shape[0]
    return (matmul(m, wk, layer=layer).reshape(B, N_MEM, X_HEADS, X_DH),
            matmul(m, wv, layer=layer).reshape(B, N_MEM, X_HEADS, X_DH))


def cross_attend(x, h, mk, mv, wq, wo, layer):
    B, T = h.shape[:2]
    q = matmul(h, wq, layer=layer).reshape(B, T, X_HEADS, X_DH)
    logits = jnp.einsum('bthd,bmhd->bthm', q, mk, preferred_element_type=jnp.float32) * X_DH ** -0.5
    p = jax.nn.softmax(logits, axis=-1)
    o = jnp.einsum('bthm,bmhd->bthd', p.astype(mv.dtype), mv).reshape(B, T, X_HEADS * X_DH)
    return matmul(o, wo, resid=x, layer=layer)


def dense_swiglu(x, h, w1, w3, w2, layer):
    rows = h.shape[0]
    tm = _row_tile(rows)
    tables = _dense_tables(rows, tm)
    g = swiglu_up(h, w1[:, None], w3[:, None], tables, tm, layer)
    return grouped_matmul(g, [w2[:, None]], tables, tm, 1024, 512, F32, resid=x, layer=layer)


MOE_TM = 2304


def moe_tables(top_e):
    A = top_e.size
    n_tiles = -(-(A + N_EXPERTS * (MOE_TM - 1)) // MOE_TM)
    e_flat = top_e.reshape(A).astype(jnp.int32)
    order = jnp.argsort(e_flat).astype(jnp.int32)
    counts = jnp.bincount(e_flat, length=N_EXPERTS).astype(jnp.int32)
    starts = jnp.cumsum(counts) - counts
    tiles_per = (counts + MOE_TM - 1) // MOE_TM
    tile_end = jnp.cumsum(tiles_per)
    tile_start = tile_end - tiles_per
    e_sorted = e_flat[order]
    prow_sorted = tile_start[e_sorted] * MOE_TM + (jnp.arange(A, dtype=jnp.int32) - starts[e_sorted])
    row_token = (jnp.arange(n_tiles * MOE_TM, dtype=jnp.int32) % (A // TOP_K)).at[prow_sorted].set(order // TOP_K)
    prow_of_assign = jnp.zeros((A,), jnp.int32).at[order].set(prow_sorted)
    ti = jnp.arange(n_tiles, dtype=jnp.int32)
    n_active = tile_end[-1]
    last = n_active - 1
    src = jnp.minimum(ti, last)
    te = jnp.minimum(jnp.searchsorted(tile_end, src, side='right').astype(jnp.int32), N_EXPERTS - 1)
    rows_in = jnp.clip(counts[te] - (src - tile_start[te]) * MOE_TM, 0, MOE_TM)
    nsb = jnp.where(ti < n_active, (rows_in + SUB_ROWS - 1) // SUB_ROWS, 0).astype(jnp.int32)
    return (te, nsb, src), row_token, prow_of_assign


def moe_swiglu(xs, hs, w_router, b_router, w1, w3, w2, layer):
    wr = jnp.pad(w_router, ((0, 0), (0, LANES - N_EXPERTS)))
    logits = jnp.concatenate([matmul(h, wr)[:, :N_EXPERTS] for h in hs], axis=0) + b_router.astype(jnp.float32)
    h = jnp.concatenate(hs, axis=0)
    N = h.shape[0]
    top_l, top_e = lax.top_k(logits, TOP_K)
    gate = jax.nn.softmax(top_l, axis=-1)
    tables, row_token, prow_of_assign = moe_tables(top_e)
    g = swiglu_up(h[row_token], w1, w3, tables, MOE_TM, layer)
    y = grouped_matmul(g, [w2], tables, MOE_TM, 1024, 512, F32, layer=layer)
    slots = prow_of_assign.reshape(N, TOP_K)
    mix = sum(y[slots[:, c]] * gate[:, c:c + 1] for c in range(TOP_K))
    outs, lo = [], 0
    for xg in xs:
        outs.append(xg + mix[lo:lo + xg.shape[0]])
        lo += xg.shape[0]
    return outs


def even_split(z):
    B, T = z.shape[:2]
    qa, ka, va, ga, qb, kb, vb, qi, ki, wi = split_cols(z, EVEN_SPLITS)
    return (qa.reshape(B, T, A_GROUPS, A_HPG, A_DK),
            ka.reshape(B, T, 3, A_GROUPS, A_DK),
            va.reshape(B, T, 3, A_GROUPS, A_DV),
            jax.nn.sigmoid(ga.astype(jnp.float32)).reshape(B, T, 3, A_GROUPS, A_HPG),
            qb.reshape(B, T, B_GROUPS, B_HPG, B_DH),
            kb.reshape(B, T, B_GROUPS, B_DH),
            vb.reshape(B, T, B_GROUPS, B_DH),
            qi.reshape(B, T, IDX_HEADS, IDX_DIM), ki, wi)


def odd_split(z):
    B, T = z.shape[:2]
    q, k, v = split_cols(z, ODD_SPLITS)
    return (q.reshape(B, T, C_HEADS, C_DH), k.reshape(B, T, C_GROUPS, C_DH), v.reshape(B, T, C_GROUPS, C_DH))


def kernel(x_prompt, x_sample, mem_prompt, cache_a_k, cache_a_v, state_a_win_k, state_a_win_v,
           cache_b_k, cache_b_v, cache_b_kidx, cache_c_k, cache_c_v, cache_mem_k, cache_mem_v, page_table,
           rel_bias, norm_mix, norm_mem, norm_cross, norm_ffn, norm_final,
           w_cross_q, w_cross_k, w_cross_v, w_cross_o, w_in_even, w_out_even,
           nsa_pe_k, nsa_pe_v, nsa_phi_k1, nsa_phi_k2, nsa_phi_v1, nsa_phi_v2,
           w_ffn1, w_ffn3, w_ffn2, w_in_odd, w_out_odd, w_router, b_router, w_exp1, w_exp3, w_exp2):
    xp, xs = x_prompt, x_sample
    B, T, D = xp.shape
    Bs, S = xs.shape[:2]
    tab_a = rel_bias[:, :A_HEADS]
    tab_b = rel_bias[:, A_HEADS:A_HEADS + B_HEADS]
    tab_c = rel_bias[:, :C_HEADS]
    names = ('a_k_p', 'a_v_p', 'aw_k_p', 'aw_v_p', 'b_k_p', 'b_v_p', 'b_i_p', 'c_k_p', 'c_v_p', 'm_k_p', 'm_v_p',
             'a_k_s', 'a_v_s', 'aw_k_s', 'aw_v_s', 'b_k_s', 'b_v_s', 'b_i_s', 'c_k_s', 'c_v_s')
    new = {n: [] for n in names}
    for layer in range(DEPTH):
        li = layer // 2
        hp = rms_norm(xp, norm_mix[layer], BF16)
        hs = rms_norm(xs, norm_mix[layer], BF16)
        if layer % 2 == 0:
            phi = (nsa_pe_k[li], nsa_pe_v[li], nsa_phi_k1[li], nsa_phi_k2[li], nsa_phi_v1[li], nsa_phi_v2[li])
            qa, ka, va, ga, qb, kb, vb, qi, ki, wi = even_split(matmul(hp, w_in_even, layer=li))
            o_a = nsa_prompt(qa, ka, va, ga, *phi, tab_a)
            o_b = dsa_prompt(qb, kb, vb, qi, ki, wi, tab_b)
            xp = matmul(jnp.concatenate([o_a, o_b], axis=-1), w_out_even, resid=xp, layer=li)
            wk = min(WINDOW, T)
            new['a_k_p'].append(ka[:, :, :2]); new['a_v_p'].append(va[:, :, :2])
            new['aw_k_p'].append(ka[:, T - wk:, 2]); new['aw_v_p'].append(va[:, T - wk:, 2])
            new['b_k_p'].append(kb); new['b_v_p'].append(vb); new['b_i_p'].append(ki)
            qa, ka, va, ga, qb, kb, vb, qi, ki, wi = even_split(matmul(hs, w_in_even, layer=li))
            o_a = nsa_decode(qa, ka, va, ga, cache_a_k[li], cache_a_v, li, state_a_win_k[li], state_a_win_v[li],
                             page_table, *phi, tab_a)
            o_b = dsa_decode(qb, kb, vb, qi, ki, wi, cache_b_k, cache_b_v, cache_b_kidx, li, page_table, tab_b)
            xs = matmul(jnp.concatenate([o_a, o_b], axis=-1), w_out_even, resid=xs, layer=li)
            new['a_k_s'].append(ka[:, :, :2]); new['a_v_s'].append(va[:, :, :2])
            new['aw_k_s'].append(ka[:, :, 2]); new['aw_v_s'].append(va[:, :, 2])
            new['b_k_s'].append(kb); new['b_v_s'].append(vb); new['b_i_s'].append(ki)
        else:
            q, k, v = odd_split(matmul(hp, w_in_odd, layer=li))
            xp = matmul(moba_prompt(q, k, v, tab_c), w_out_odd, resid=xp, layer=li)
            new['c_k_p'].append(k); new['c_v_p'].append(v)
            q, k, v = odd_split(matmul(hs, w_in_odd, layer=li))
            xs = matmul(moba_decode(q, k, v, cache_c_k, cache_c_v, li, page_table, tab_c), w_out_odd,
                        resid=xs, layer=li)
            new['c_k_s'].append(k); new['c_v_s'].append(v)
        mk, mv = mem_kv(mem_prompt, norm_mem[layer], w_cross_k, w_cross_v, layer)
        new['m_k_p'].append(mk); new['m_v_p'].append(mv)
        xp = cross_attend(xp, rms_norm(xp, norm_cross[layer], BF16), mk, mv, w_cross_q, w_cross_o, layer)
        xs = cross_attend(xs, rms_norm(xs, norm_cross[layer], BF16), cache_mem_k[layer], cache_mem_v[layer],
                          w_cross_q, w_cross_o, layer)
        hp = rms_norm(xp, norm_ffn[layer], BF16).reshape(B * T, D)
        hs = rms_norm(xs, norm_ffn[layer], BF16).reshape(Bs * S, D)
        xp2, xs2 = xp.reshape(B * T, D), xs.reshape(Bs * S, D)
        if layer % 2 == 0:
            xp2 = dense_swiglu(xp2, hp, w_ffn1, w_ffn3, w_ffn2, li)
            xs2 = dense_swiglu(xs2, hs, w_ffn1, w_ffn3, w_ffn2, li)
        else:
            xp2, xs2 = moe_swiglu([xp2, xs2], [hp, hs], w_router[li], b_router[li], w_exp1, w_exp3, w_exp2, li)
        xp = xp2.reshape(B, T, D)
        xs = xs2.reshape(Bs, S, D)
    y_prompt = rms_norm(xp, norm_final)
    y_sample = rms_norm(xs, norm_final)
    return (y_prompt, y_sample,
            jnp.stack(new['a_k_p']), jnp.stack(new['a_v_p']), jnp.stack(new['aw_k_p']), jnp.stack(new['aw_v_p']),
            jnp.stack(new['b_k_p']), jnp.stack(new['b_v_p']), jnp.stack(new['b_i_p']),
            jnp.stack(new['c_k_p']), jnp.stack(new['c_v_p']), jnp.stack(new['m_k_p']), jnp.stack(new['m_v_p']),
            jnp.stack(new['a_k_s']), jnp.stack(new['a_v_s']), jnp.stack(new['aw_k_s']), jnp.stack(new['aw_v_s']),
            jnp.stack(new['b_k_s']), jnp.stack(new['b_v_s']), jnp.stack(new['b_i_s']),
            jnp.stack(new['c_k_s']), jnp.stack(new['c_v_s']))
```
